```python
import math
import jax, jax.numpy as jnp
from jax import lax
import numpy as np

D_MODEL = 2048
BATCH = 2
SEQ = 4096
DEPTH = 1

HEAD_DIM = 128
MIX_WIDTH = D_MODEL
MOBA_WIDTH = MIX_WIDTH // 2
MOBA_HEADS = MOBA_WIDTH // HEAD_DIM
MOBA_BLOCK = 256
MOBA_TOPK = 3
MOBA_QCHUNK = 32
DIFF_WIDTH = MIX_WIDTH - MOBA_WIDTH
DIFF_V_DIM = 2 * HEAD_DIM
DIFF_HEADS = DIFF_WIDTH // DIFF_V_DIM
DIFF_QK_DIM = HEAD_DIM
DIFF_QK_WIDTH = DIFF_HEADS * 2 * DIFF_QK_DIM
DIFF_QBLOCK = 128
IN_SIZES = (MOBA_WIDTH, MOBA_WIDTH, MOBA_WIDTH, DIFF_QK_WIDTH, DIFF_QK_WIDTH, DIFF_WIDTH)
IN_WIDTH = sum(IN_SIZES)
N_GROUPS = 4
EXPERTS_PER_GROUP = 8
N_EXPERTS = N_GROUPS * EXPERTS_PER_GROUP
EXPERT_TOPK = 2
D_EXPERT = D_MODEL // 4
RMS_EPS = 1e-6
ALIBI_MAX_BIAS = 8.0

kernel_name = "hymba_moba_diffattn_hmoe"


def rms_norm(x, w):
    xf = x.astype(jnp.float32)
    y = xf * lax.rsqrt(jnp.mean(xf * xf, axis=-1, keepdims=True) + RMS_EPS)
    return (y * w.astype(jnp.float32)).astype(x.dtype)


def alibi_slopes(n):
    i = jnp.arange(n, dtype=jnp.float32) + 1.0
    return jnp.exp2(-ALIBI_MAX_BIAS * i / n)


def to_heads(t, n, d):
    b, s, _ = t.shape
    return t.reshape(b, s, n, d).transpose(0, 2, 1, 3)


def moba_attention(q, k, v, slopes):
    B, H, S, Dh = q.shape
    nb = -(-S // MOBA_BLOCK)
    sp = nb * MOBA_BLOCK
    n_top = min(MOBA_TOPK, nb)
    pad = ((0, 0), (0, 0), (0, sp - S), (0, 0))
    kf = jnp.pad(k.astype(jnp.float32), pad)
    vf = jnp.pad(v.astype(jnp.float32), pad)
    kb = kf.reshape(B, H, nb, MOBA_BLOCK, Dh)
    vb = vf.reshape(B, H, nb, MOBA_BLOCK, Dh)
    k_mean = jnp.mean(kb, axis=3)
    qf = q.astype(jnp.float32) * (Dh ** -0.5)
    b_idx = jnp.arange(B)[:, None, None, None]
    h_idx = jnp.arange(H)[None, :, None, None]
    blk = jnp.arange(nb)
    in_blk = jnp.arange(MOBA_BLOCK)
    n_chunks = S // MOBA_QCHUNK

    def chunk(c):
        t0 = c * MOBA_QCHUNK
        qc = lax.dynamic_slice_in_dim(qf, t0, MOBA_QCHUNK, axis=2)
        t = t0 + jnp.arange(MOBA_QCHUNK)
        own = t0 // MOBA_BLOCK
        gate = jnp.einsum('bhqd,bhnd->bhqn', qc, k_mean)
        gate = jnp.where(blk < own, gate, -jnp.inf)
        _, sel = lax.top_k(gate, n_top)
        sel_valid = sel < own
        k_sel = kb[b_idx, h_idx, sel]
        v_sel = vb[b_idx, h_idx, sel]
        pos_sel = sel[..., None] * MOBA_BLOCK + in_blk
        dist_sel = (t[None, None, :, None, None] - pos_sel).astype(jnp.float32)
        s_sel = jnp.einsum('bhqd,bhqjkd->bhqjk', qc, k_sel) - slopes[None, :, None, None, None] * dist_sel
        s_sel = jnp.where(sel_valid[..., None], s_sel, -jnp.inf)
        k_own = lax.dynamic_index_in_dim(kb, own, axis=2, keepdims=False)
        v_own = lax.dynamic_index_in_dim(vb, own, axis=2, keepdims=False)
        pos_own = own * MOBA_BLOCK + in_blk
        dist_own = (t[:, None] - pos_own[None, :]).astype(jnp.float32)
        s_own = jnp.einsum('bhqd,bhkd->bhqk', qc, k_own) - slopes[None, :, None, None] * dist_own
        s_own = jnp.where(pos_own[None, :] <= t[:, None], s_own, -jnp.inf)
        n_sel = n_top * MOBA_BLOCK
        s_all = jnp.concatenate([s_sel.reshape(B, H, MOBA_QCHUNK, n_sel), s_own], axis=-1)
        p = jax.nn.softmax(s_all, axis=-1)
        p_sel = p[..., :n_sel].reshape(B, H, MOBA_QCHUNK, n_top, MOBA_BLOCK)
        p_own = p[..., n_sel:]
        out = (jnp.einsum('bhqjk,bhqjkd->bhqd', p_sel, v_sel)
               + jnp.einsum('bhqk,bhkd->bhqd', p_own, v_own))
        return out.astype(q.dtype)

    outs = lax.map(chunk, jnp.arange(n_chunks))
    return outs.transpose(1, 2, 0, 3, 4).reshape(B, H, S, Dh)


def diff_attention(q1, q2, k1, k2, v, lam, slopes):
    B, H, S, Dq = q1.shape
    scale = Dq ** -0.5
    q1f = q1.astype(jnp.float32) * scale
    q2f = q2.astype(jnp.float32) * scale
    k1f, k2f, vf = k1.astype(jnp.float32), k2.astype(jnp.float32), v.astype(jnp.float32)
    s_pos = jnp.arange(S)
    n_blocks = S // DIFF_QBLOCK

    def block(c):
        t0 = c * DIFF_QBLOCK
        t = t0 + jnp.arange(DIFF_QBLOCK)
        causal = s_pos[None, :] <= t[:, None]
        bias = -slopes[:, None, None] * (t[:, None] - s_pos[None, :]).astype(jnp.float32)

        def probs(qf, kf):
            qc = lax.dynamic_slice_in_dim(qf, t0, DIFF_QBLOCK, axis=2)
            s = jnp.einsum('bhqd,bhkd->bhqk', qc, kf) + bias[None]
            return jax.nn.softmax(jnp.where(causal, s, -jnp.inf), axis=-1)

        a = probs(q1f, k1f) - lam * probs(q2f, k2f)
        return jnp.einsum('bhqk,bhkd->bhqd', a, vf)

    outs = lax.map(block, jnp.arange(n_blocks))
    return outs.transpose(1, 2, 0, 3, 4).reshape(B, H, S, vf.shape[-1])


def hier_moe(h, w_rg, b_rg, w_re, b_re, w_gate, w_up, w_down):
    B, S, D = h.shape
    t = h.reshape(B * S, D)
    tf = t.astype(jnp.float32)
    g_probs = jax.nn.softmax(tf @ w_rg.astype(jnp.float32) + b_rg.astype(jnp.float32), axis=-1)
    g_p, g_top = lax.top_k(g_probs, 1)
    e_logits = (tf @ w_re.astype(jnp.float32) + b_re.astype(jnp.float32)).reshape(-1, N_GROUPS, EXPERTS_PER_GROUP)
    e_in = jnp.take_along_axis(e_logits, g_top[:, :, None], axis=1)[:, 0]
    e_p, e_top = lax.top_k(jax.nn.softmax(e_in, axis=-1), EXPERT_TOPK)
    e_p = e_p / jnp.sum(e_p, axis=-1, keepdims=True)
    weights = g_p * e_p
    expert_ids = g_top * EXPERTS_PER_GROUP + e_top
    combine = jnp.sum(jax.nn.one_hot(expert_ids, N_EXPERTS, dtype=jnp.float32) * weights[..., None], axis=1)
    combine = combine.astype(t.dtype)
    out = jnp.zeros_like(t)
    for g in range(N_GROUPS):
        sl = slice(g * EXPERTS_PER_GROUP, (g + 1) * EXPERTS_PER_GROUP)
        a = jnp.einsum('td,edf->tef', t, w_gate[sl])
        u = jnp.einsum('td,edf->tef', t, w_up[sl])
        act = jax.nn.silu(a) * u * combine[:, sl, None]
        out = out + jnp.einsum('tef,efd->td', act, w_down[sl])
    return out.reshape(B, S, D)


def setup_inputs(seed: int = 0) -> dict:
    key = jax.random.key(seed)
    ks = jax.random.split(key, 20)
    f32 = jnp.float32
    nrm = lambda k, shape, s: jax.random.normal(k, shape, f32) * s
    L = DEPTH
    return {
        "x": jax.random.normal(ks[0], (BATCH, SEQ, D_MODEL), f32),
        "norm_mix_w": 1.0 + nrm(ks[1], (L, D_MODEL), 0.02),
        "w_in": nrm(ks[2], (L, D_MODEL, IN_WIDTH), D_MODEL ** -0.5),
        "lambda_q1": nrm(ks[3], (L, DIFF_QK_DIM), 0.1),
        "lambda_k1": nrm(ks[4], (L, DIFF_QK_DIM), 0.1),
        "lambda_q2": nrm(ks[5], (L, DIFF_QK_DIM), 0.1),
        "lambda_k2": nrm(ks[6], (L, DIFF_QK_DIM), 0.1),
        "diff_subln_w": 1.0 + nrm(ks[7], (L, DIFF_V_DIM), 0.02),
        "w_out": nrm(ks[8], (L, MIX_WIDTH, D_MODEL), MIX_WIDTH ** -0.5),
        "norm_ffn_w": 1.0 + nrm(ks[9], (L, D_MODEL), 0.02),
        "w_router_group": nrm(ks[10], (L, D_MODEL, N_GROUPS), D_MODEL ** -0.5),
        "b_router_group": nrm(ks[11], (L, N_GROUPS), 0.01),
        "w_router_expert": nrm(ks[12], (L, D_MODEL, N_EXPERTS), D_MODEL ** -0.5),
        "b_router_expert": nrm(ks[13], (L, N_EXPERTS), 0.01),
        "w_gate": nrm(ks[14], (L, N_EXPERTS, D_MODEL, D_EXPERT), D_MODEL ** -0.5),
        "w_up": nrm(ks[15], (L, N_EXPERTS, D_MODEL, D_EXPERT), D_MODEL ** -0.5),
        "w_down": nrm(ks[16], (L, N_EXPERTS, D_EXPERT, D_MODEL), D_EXPERT ** -0.5),
        "norm_final_w": 1.0 + nrm(ks[17], (D_MODEL,), 0.02),
    }


def reference(x, norm_mix_w, w_in, lambda_q1, lambda_k1, lambda_q2, lambda_k2, diff_subln_w,
              w_out, norm_ffn_w, w_router_group, b_router_group, w_router_expert, b_router_expert,
              w_gate, w_up, w_down, norm_final_w):
    B, S, _ = x.shape
    slopes_a = alibi_slopes(MOBA_HEADS)
    slopes_b = alibi_slopes(DIFF_HEADS)
    offsets = [int(o) for o in np.cumsum(IN_SIZES)[:-1]]
    for l in range(DEPTH):
        h = rms_norm(x, norm_mix_w[l])
        proj = h @ w_in[l]
        qa, ka, va, qb, kb, vb = jnp.split(proj, offsets, axis=-1)
        out_a = moba_attention(to_heads(qa, MOBA_HEADS, HEAD_DIM), to_heads(ka, MOBA_HEADS, HEAD_DIM),
                               to_heads(va, MOBA_HEADS, HEAD_DIM), slopes_a)
        out_a = out_a.transpose(0, 2, 1, 3).reshape(B, S, MOBA_WIDTH).astype(x.dtype)
        qb = qb.reshape(B, S, DIFF_HEADS, 2, DIFF_QK_DIM).transpose(0, 2, 3, 1, 4)
        kb = kb.reshape(B, S, DIFF_HEADS, 2, DIFF_QK_DIM).transpose(0, 2, 3, 1, 4)
        vbh = to_heads(vb, DIFF_HEADS, DIFF_V_DIM)
        lambda_init = 0.8 - 0.6 * math.exp(-0.3 * l)
        lam = (jnp.exp(jnp.sum(lambda_q1[l].astype(jnp.float32) * lambda_k1[l].astype(jnp.float32)))
               - jnp.exp(jnp.sum(lambda_q2[l].astype(jnp.float32) * lambda_k2[l].astype(jnp.float32)))
               + lambda_init)
        out_b = diff_attention(qb[:, :, 0], qb[:, :, 1], kb[:, :, 0], kb[:, :, 1], vbh, lam, slopes_b)
        out_b = rms_norm(out_b, diff_subln_w[l]) * (1.0 - lambda_init)
        out_b = out_b.transpose(0, 2, 1, 3).reshape(B, S, DIFF_WIDTH).astype(x.dtype)
        mixed = jnp.concatenate([out_a, out_b], axis=-1)
        x = x + mixed @ w_out[l]
        h = rms_norm(x, norm_ffn_w[l])
        x = x + hier_moe(h, w_router_group[l], b_router_group[l], w_router_expert[l], b_router_expert[l],
                         w_gate[l], w_up[l], w_down[l])
    return rms_norm(x, norm_final_w)
```

```python
import functools

import numpy as np
import jax
import jax.numpy as jnp
from jax import lax
from jax.experimental import pallas as pl
from jax.experimental.pallas import tpu as pltpu

F32 = jnp.float32
BF16 = jnp.bfloat16

D_MODEL = 2048
HEAD_DIM = 128
MOBA_HEADS = 8
MOBA_BLOCK = 256
MOBA_TOPK = 3
DIFF_HEADS = 4
DIFF_V_DIM = 256
IN_WIDTH = 6144
N_GROUPS = 4
EXPERTS_PER_GROUP = 8
N_EXPERTS = 32
EXPERT_TOPK = 2
D_EXPERT = 512
RMS_EPS = 1e-6
ALIBI_MAX_BIAS = 8.0
LAMBDA_INIT = 0.8 - 0.6 * float(np.exp(-0.3 * 0))

LANES = 128
QK_SCALE = HEAD_DIM ** -0.5
NEG_BIG = -1e30
POS_SPLIT = 64
ROW_WORD_CHUNKS = D_MODEL // (2 * LANES)

IN_TM, IN_TN = 1024, 1024
ATT_TQ = 256
OUT_TM = 512
MOE_TM = 256
CMB_TT = 256
VMEM_LIMIT = 56 * 1024 * 1024


def _alibi_slopes(n):
    return np.exp2(-ALIBI_MAX_BIAS * (np.arange(n, dtype=np.float64) + 1.0) / n).astype(np.float32)


def _dot_nt(a, b):
    return lax.dot_general(a, b, (((1,), (1,)), ((), ())), preferred_element_type=F32)


def _dot(a, b):
    return jnp.dot(a, b, preferred_element_type=F32)


def _inproj_kernel(x_ref, nw_ref, cs_ref, w_ref, o_ref, h_ref):
    @pl.when(pl.program_id(1) == 0)
    def _():
        x = x_ref[...]
        r = lax.rsqrt(jnp.mean(x * x, axis=-1, keepdims=True) + RMS_EPS)
        h_ref[...] = (x * r * nw_ref[...]).astype(BF16)

    acc = _dot(h_ref[...], w_ref[...])
    o_ref[...] = (acc * cs_ref[...]).astype(BF16)


def _in_proj(x2d, norm_w, w_in_bf16, col_scale):
    t = x2d.shape[0]
    return pl.pallas_call(
        _inproj_kernel,
        grid=(t // IN_TM, IN_WIDTH // IN_TN),
        in_specs=[
            pl.BlockSpec((IN_TM, D_MODEL), lambda i, j: (i, 0)),
            pl.BlockSpec((1, D_MODEL), lambda i, j: (0, 0)),
            pl.BlockSpec((1, IN_TN), lambda i, j: (0, j)),
            pl.BlockSpec((D_MODEL, IN_TN), lambda i, j: (0, j)),
        ],
        out_specs=pl.BlockSpec((IN_TM, IN_TN), lambda i, j: (i, j)),
        out_shape=jax.ShapeDtypeStruct((t, IN_WIDTH), BF16),
        scratch_shapes=[pltpu.VMEM((IN_TM, D_MODEL), BF16)],
        compiler_params=pltpu.CompilerParams(
            dimension_semantics=("arbitrary", "arbitrary"), vmem_limit_bytes=VMEM_LIMIT),
        name="in_proj",
    )(x2d, norm_w, col_scale, w_in_bf16)


def _key_position_features(seq, first_lane):
    pos = lax.broadcasted_iota(jnp.int32, (seq, LANES), 0)
    lane = lax.broadcasted_iota(jnp.int32, (seq, LANES), 1)
    ext = jnp.where(lane == first_lane, (pos // POS_SPLIT).astype(F32), 0.0)
    return jnp.where(lane == first_lane + 1, (pos % POS_SPLIT).astype(F32), ext), pos, lane


def _softmax_init(s, v):
    m = jnp.max(s, axis=-1, keepdims=True)
    p = jnp.exp(s - m)
    return m, jnp.sum(p, axis=-1, keepdims=True), _dot(p.astype(BF16), v)


def _softmax_step(s, v, m, l, acc):
    m_new = jnp.maximum(m, jnp.max(s, axis=-1, keepdims=True))
    alpha = jnp.exp(m - m_new)
    p = jnp.exp(s - m_new)
    return m_new, alpha * l + jnp.sum(p, axis=-1, keepdims=True), alpha * acc + _dot(p.astype(BF16), v)


def _causal(s):
    row = lax.broadcasted_iota(jnp.int32, s.shape, 0)
    col = lax.broadcasted_iota(jnp.int32, s.shape, 1)
    return jnp.where(col <= row, s, -jnp.inf)


def _moba_kernel(slopes_ref, q_ref, k_ref, v_ref, o_ref, kaug_ref, kmh_ref, kml_ref, *, seq):
    nb = seq // MOBA_BLOCK
    h = pl.program_id(1)
    i = pl.program_id(2)
    slope = slopes_ref[h]

    @pl.when(i == 0)
    def _():
        k = k_ref[...]
        kaug_ref[:, 0:HEAD_DIM] = k
        ext, pos, lane = _key_position_features(seq, nb)
        ext = jnp.where(lane == pos // MOBA_BLOCK, 1.0, ext)
        kaug_ref[:, HEAD_DIM:2 * HEAD_DIM] = ext.astype(BF16)
        km = jnp.mean(k.astype(F32).reshape(nb, MOBA_BLOCK, HEAD_DIM), axis=1)
        kmh = km.astype(BF16)
        kml = (km - kmh.astype(F32)).astype(BF16)
        kmh_ref[...] = jnp.zeros(kmh_ref.shape, BF16)
        kml_ref[...] = jnp.zeros(kml_ref.shape, BF16)
        kmh_ref[0:nb, :] = kmh
        kml_ref[0:nb, :] = kml

    q = q_ref[...]
    gate = _dot_nt(q, kmh_ref[...]) + _dot_nt(q, kml_ref[...])
    lane = lax.broadcasted_iota(jnp.int32, (ATT_TQ, LANES), 1)
    lane_f = lane.astype(F32)
    g = jnp.where(lane < i, gate, -jnp.inf)
    sel = jnp.zeros((ATT_TQ, LANES), F32)
    for _ in range(MOBA_TOPK):
        mx = jnp.max(g, axis=-1, keepdims=True)
        idx = jnp.min(jnp.where(g == mx, lane_f, float(LANES)), axis=-1, keepdims=True)
        pick = (lane_f == idx) & (mx > -jnp.inf)
        sel = jnp.where(pick, 1.0, sel)
        g = jnp.where(pick, -jnp.inf, g)

    qe = jnp.where(lane < nb, jnp.where(sel > 0.0, 0.0, NEG_BIG), 0.0)
    qe = jnp.where(lane == i, 0.0, qe)
    qe = jnp.where(lane == nb, POS_SPLIT * slope, qe)
    qe = jnp.where(lane == nb + 1, slope, qe)
    qaug = jnp.concatenate([q, qe.astype(BF16)], axis=1)

    r0 = pl.multiple_of(i * MOBA_BLOCK, MOBA_BLOCK)
    s = _causal(_dot_nt(qaug, kaug_ref[pl.ds(r0, MOBA_BLOCK), :]))
    m, l, acc = _softmax_init(s, v_ref[pl.ds(r0, MOBA_BLOCK), :])

    def body(j, carry):
        rj = pl.multiple_of(j * MOBA_BLOCK, MOBA_BLOCK)
        sj = _dot_nt(qaug, kaug_ref[pl.ds(rj, MOBA_BLOCK), :])
        return _softmax_step(sj, v_ref[pl.ds(rj, MOBA_BLOCK), :], *carry)

    m, l, acc = lax.fori_loop(0, i, body, (m, l, acc))
    o_ref[...] = (acc / l).astype(BF16)


def _moba(proj, slopes, batch, seq):
    nq = seq // ATT_TQ
    q_col, k_col, v_col = 0, MOBA_HEADS, 2 * MOBA_HEADS
    return pl.pallas_call(
        functools.partial(_moba_kernel, seq=seq),
        grid_spec=pltpu.PrefetchScalarGridSpec(
            num_scalar_prefetch=1,
            grid=(batch, MOBA_HEADS, nq),
            in_specs=[
                pl.BlockSpec((ATT_TQ, HEAD_DIM), lambda b, h, i, s: (b * nq + i, q_col + h)),
                pl.BlockSpec((seq, HEAD_DIM), lambda b, h, i, s: (b, k_col + h)),
                pl.BlockSpec((seq, HEAD_DIM), lambda b, h, i, s: (b, v_col + h)),
            ],
            out_specs=pl.BlockSpec((ATT_TQ, HEAD_DIM), lambda b, h, i, s: (b * nq + i, h)),
            scratch_shapes=[
                pltpu.VMEM((seq, 2 * HEAD_DIM), BF16),
                pltpu.VMEM((LANES, HEAD_DIM), BF16),
                pltpu.VMEM((LANES, HEAD_DIM), BF16),
            ],
        ),
        out_shape=jax.ShapeDtypeStruct((batch * seq, MOBA_HEADS * HEAD_DIM), BF16),
        compiler_params=pltpu.CompilerParams(
            dimension_semantics=("arbitrary", "arbitrary", "arbitrary"), vmem_limit_bytes=VMEM_LIMIT),
        name="moba",
    )(slopes, proj, proj, proj)


def _diff_kernel(slopes_ref, lq1_ref, lk1_ref, lq2_ref, lk2_ref, sw_ref,
                 q1_ref, q2_ref, k1_ref, k2_ref, v_ref, o_ref, ka1_ref, ka2_ref, *, seq):
    h = pl.program_id(1)
    i = pl.program_id(2)
    slope = slopes_ref[h]

    @pl.when(i == 0)
    def _():
        ext, _, _ = _key_position_features(seq, 0)
        ext = ext.astype(BF16)
        ka1_ref[:, 0:HEAD_DIM] = k1_ref[...]
        ka1_ref[:, HEAD_DIM:2 * HEAD_DIM] = ext
        ka2_ref[:, 0:HEAD_DIM] = k2_ref[...]
        ka2_ref[:, HEAD_DIM:2 * HEAD_DIM] = ext

    lane = lax.broadcasted_iota(jnp.int32, (ATT_TQ, LANES), 1)
    qe = jnp.where(lane == 0, POS_SPLIT * slope, 0.0)
    qe = jnp.where(lane == 1, slope, qe).astype(BF16)
    qa1 = jnp.concatenate([q1_ref[...], qe], axis=1)
    qa2 = jnp.concatenate([q2_ref[...], qe], axis=1)

    r0 = pl.multiple_of(i * ATT_TQ, ATT_TQ)
    vd = v_ref[pl.ds(r0, ATT_TQ), :]
    c1 = _softmax_init(_causal(_dot_nt(qa1, ka1_ref[pl.ds(r0, ATT_TQ), :])), vd)
    c2 = _softmax_init(_causal(_dot_nt(qa2, ka2_ref[pl.ds(r0, ATT_TQ), :])), vd)

    def body(j, carry):
        a, b = carry
        rj = pl.multiple_of(j * ATT_TQ, ATT_TQ)
        vj = v_ref[pl.ds(rj, ATT_TQ), :]
        a = _softmax_step(_dot_nt(qa1, ka1_ref[pl.ds(rj, ATT_TQ), :]), vj, *a)
        b = _softmax_step(_dot_nt(qa2, ka2_ref[pl.ds(rj, ATT_TQ), :]), vj, *b)
        return a, b

    (_, l1, acc1), (_, l2, acc2) = lax.fori_loop(0, i, body, (c1, c2))

    lam = (jnp.exp(jnp.sum(lq1_ref[...] * lk1_ref[...], axis=-1, keepdims=True))
           - jnp.exp(jnp.sum(lq2_ref[...] * lk2_ref[...], axis=-1, keepdims=True))
           + LAMBDA_INIT)
    o = acc1 / l1 - lam * (acc2 / l2)
    y = o * lax.rsqrt(jnp.mean(o * o, axis=-1, keepdims=True) + RMS_EPS)
    o_ref[...] = ((y * sw_ref[...]) * (1.0 - LAMBDA_INIT)).astype(BF16)


def _diff(proj, slopes, lq1, lk1, lq2, lk2, subln_w, batch, seq):
    nq = seq // ATT_TQ
    q_col = 3 * MOBA_HEADS
    k_col = q_col + 2 * DIFF_HEADS
    v_col = (k_col + 2 * DIFF_HEADS) // 2
    vec = lambda n: pl.BlockSpec((1, n), lambda b, h, i, s: (0, 0))
    return pl.pallas_call(
        functools.partial(_diff_kernel, seq=seq),
        grid_spec=pltpu.PrefetchScalarGridSpec(
            num_scalar_prefetch=1,
            grid=(batch, DIFF_HEADS, nq),
            in_specs=[
                vec(HEAD_DIM), vec(HEAD_DIM), vec(HEAD_DIM), vec(HEAD_DIM), vec(DIFF_V_DIM),
                pl.BlockSpec((ATT_TQ, HEAD_DIM), lambda b, h, i, s: (b * nq + i, q_col + 2 * h)),
                pl.BlockSpec((ATT_TQ, HEAD_DIM), lambda b, h, i, s: (b * nq + i, q_col + 2 * h + 1)),
                pl.BlockSpec((seq, HEAD_DIM), lambda b, h, i, s: (b, k_col + 2 * h)),
                pl.BlockSpec((seq, HEAD_DIM), lambda b, h, i, s: (b, k_col + 2 * h + 1)),
                pl.BlockSpec((seq, DIFF_V_DIM), lambda b, h, i, s: (b, v_col + h)),
            ],
            out_specs=pl.BlockSpec((ATT_TQ, DIFF_V_DIM), lambda b, h, i, s: (b * nq + i, h)),
            scratch_shapes=[
                pltpu.VMEM((seq, 2 * HEAD_DIM), BF16),
                pltpu.VMEM((seq, 2 * HEAD_DIM), BF16),
            ],
        ),
        out_shape=jax.ShapeDtypeStruct((batch * seq, DIFF_HEADS * DIFF_V_DIM), BF16),
        compiler_params=pltpu.CompilerParams(
            dimension_semantics=("arbitrary", "arbitrary", "arbitrary"), vmem_limit_bytes=VMEM_LIMIT),
        name="diff_attn",
    )(slopes, lq1, lk1, lq2, lk2, subln_w, proj, proj, proj, proj, proj)


def _outproj_kernel(x_ref, oa_ref, ob_ref, wa_ref, wb_ref, nw_ref, wrh_ref, wrl_ref, br_ref,
                    x2_ref, hp_ref, lg_ref):
    x2 = x_ref[...] + _dot(oa_ref[...], wa_ref[...]) + _dot(ob_ref[...], wb_ref[...])
    x2_ref[...] = x2
    r = lax.rsqrt(jnp.mean(x2 * x2, axis=-1, keepdims=True) + RMS_EPS)
    h = x2 * r * nw_ref[...]
    hh = h.astype(BF16)
    hl = (h - hh.astype(F32)).astype(BF16)
    lg_ref[...] = (_dot(hh, wrh_ref[...]) + _dot(hl, wrh_ref[...]) + _dot(hh, wrl_ref[...])
                   + br_ref[...])
    bits = pltpu.bitcast(hh.astype(F32), jnp.uint32)
    for c in range(ROW_WORD_CHUNKS):
        lo = bits[:, 2 * c * LANES:(2 * c + 1) * LANES]
        hi = bits[:, (2 * c + 1) * LANES:(2 * c + 2) * LANES]
        hp_ref[pl.ds(c, OUT_TM, stride=ROW_WORD_CHUNKS), :] = (lo >> 16) | (hi & jnp.uint32(0xFFFF0000))


def _out_proj(x2d, out_a, out_b, wa, wb, norm_w, wr_hi, wr_lo, b_r):
    t = x2d.shape[0]
    half = D_MODEL // 2
    row = lambda n: pl.BlockSpec((OUT_TM, n), lambda i: (i, 0))
    full = lambda r, c: pl.BlockSpec((r, c), lambda i: (0, 0))
    return pl.pallas_call(
        _outproj_kernel,
        grid=(t // OUT_TM,),
        in_specs=[row(D_MODEL), row(half), row(half), full(half, D_MODEL), full(half, D_MODEL),
                  full(1, D_MODEL), full(D_MODEL, LANES), full(D_MODEL, LANES), full(1, LANES)],
        out_specs=[row(D_MODEL), pl.BlockSpec((OUT_TM * ROW_WORD_CHUNKS, LANES), lambda i: (i, 0)), row(LANES)],
        out_shape=[jax.ShapeDtypeStruct((t, D_MODEL), F32),
                   jax.ShapeDtypeStruct((t * ROW_WORD_CHUNKS, LANES), jnp.uint32),
                   jax.ShapeDtypeStruct((t, LANES), F32)],
        compiler_params=pltpu.CompilerParams(
            dimension_semantics=("arbitrary",), vmem_limit_bytes=VMEM_LIMIT),
        name="out_proj",
    )(x2d, out_a, out_b, wa, wb, norm_w, wr_hi, wr_lo, b_r)


GATHER_UNROLL = 8


def _dispatch_kernel(src_ref, nt_ref, hp_ref, xs_ref):
    t = pl.program_id(0)

    @pl.when(t < nt_ref[0])
    def _():
        def one(r, _):
            xs_ref[r] = hp_ref[src_ref[t * MOE_TM + r]]
            return 0
        lax.fori_loop(0, MOE_TM, one, 0, unroll=GATHER_UNROLL)

    @pl.when(t >= nt_ref[0])
    def _():
        xs_ref[...] = jnp.zeros(xs_ref.shape, xs_ref.dtype)


def _dispatch(src, n_tiles, hp, npad):
    n_tokens = hp.shape[0] // ROW_WORD_CHUNKS
    hp3 = hp.reshape(n_tokens, ROW_WORD_CHUNKS, LANES)
    xs = pl.pallas_call(
        _dispatch_kernel,
        grid_spec=pltpu.PrefetchScalarGridSpec(
            num_scalar_prefetch=2,
            grid=(npad // MOE_TM,),
            in_specs=[pl.BlockSpec(memory_space=pltpu.VMEM)],
            out_specs=pl.BlockSpec((MOE_TM, ROW_WORD_CHUNKS, LANES), lambda t, s, n: (t, 0, 0)),
        ),
        out_shape=jax.ShapeDtypeStruct((npad, ROW_WORD_CHUNKS, LANES), hp.dtype),
        compiler_params=pltpu.CompilerParams(
            dimension_semantics=("arbitrary",), vmem_limit_bytes=VMEM_LIMIT),
        name="dispatch",
    )(src, n_tiles, hp3)
    return xs.reshape(npad * ROW_WORD_CHUNKS, LANES)


def _experts_kernel(te_ref, nt_ref, xs_ref, rw_ref, wg_ref, wu_ref, wd_ref, y_ref,
                    wgb_ref, wub_ref, wdb_ref, x_ref):
    t = pl.program_id(0)

    @pl.when(t < nt_ref[0])
    def _():
        e = te_ref[t]
        prev = te_ref[jnp.maximum(t - 1, 0)]

        @pl.when((t == 0) | (e != prev))
        def _():
            wgb_ref[...] = wg_ref[0].astype(BF16)
            wub_ref[...] = wu_ref[0].astype(BF16)
            wdb_ref[...] = wd_ref[0].astype(BF16)

        for c in range(ROW_WORD_CHUNKS):
            u32 = xs_ref[pl.ds(c, MOE_TM, stride=ROW_WORD_CHUNKS), :]
            x_ref[:, 2 * c * LANES:(2 * c + 1) * LANES] = pltpu.bitcast(u32 << 16, F32).astype(BF16)
            x_ref[:, (2 * c + 1) * LANES:(2 * c + 2) * LANES] = (
                pltpu.bitcast(u32 & jnp.uint32(0xFFFF0000), F32).astype(BF16))
        x = x_ref[...]
        a = _dot(x, wgb_ref[...])
        u = _dot(x, wub_ref[...])
        act = (a * jax.nn.sigmoid(a)) * u * rw_ref[...]
        y_ref[...] = _dot(act.astype(BF16), wdb_ref[...])

    @pl.when(t >= nt_ref[0])
    def _():
        y_ref[...] = jnp.zeros(y_ref.shape, y_ref.dtype)


def _experts(tile_expert, n_tiles, xs, roww, w_gate, w_up, w_down):
    npad = xs.shape[0] // ROW_WORD_CHUNKS
    last = lambda t, nt: jnp.minimum(t, jnp.maximum(nt[0] - 1, 0))
    return pl.pallas_call(
        _experts_kernel,
        grid_spec=pltpu.PrefetchScalarGridSpec(
            num_scalar_prefetch=2,
            grid=(npad // MOE_TM,),
            in_specs=[
                pl.BlockSpec((MOE_TM * ROW_WORD_CHUNKS, LANES), lambda t, te, nt: (last(t, nt), 0)),
                pl.BlockSpec((MOE_TM, 1), lambda t, te, nt: (last(t, nt), 0)),
                pl.BlockSpec((1, D_MODEL, D_EXPERT), lambda t, te, nt: (te[last(t, nt)], 0, 0)),
                pl.BlockSpec((1, D_MODEL, D_EXPERT), lambda t, te, nt: (te[last(t, nt)], 0, 0)),
                pl.BlockSpec((1, D_EXPERT, D_MODEL), lambda t, te, nt: (te[last(t, nt)], 0, 0)),
            ],
            out_specs=pl.BlockSpec((MOE_TM, D_MODEL), lambda t, te, nt: (t, 0)),
            scratch_shapes=[
                pltpu.VMEM((D_MODEL, D_EXPERT), BF16),
                pltpu.VMEM((D_MODEL, D_EXPERT), BF16),
                pltpu.VMEM((D_EXPERT, D_MODEL), BF16),
                pltpu.VMEM((MOE_TM, D_MODEL), BF16),
            ],
        ),
        out_shape=jax.ShapeDtypeStruct((npad, D_MODEL), F32),
        compiler_params=pltpu.CompilerParams(
            dimension_semantics=("arbitrary",), vmem_limit_bytes=VMEM_LIMIT),
        name="experts",
    )(tile_expert, n_tiles, xs, roww, w_gate, w_up, w_down)


def _combine_kernel(pos_ref, x2_ref, nw_ref, y_ref, o_ref, buf_ref, sem):
    i = pl.program_id(0)

    def row_copy(p, k, r):
        return pltpu.make_async_copy(y_ref.at[pl.ds(p, 1), :], buf_ref.at[k, pl.ds(r, 1), :], sem)

    def issue(r, _):
        a = (i * CMB_TT + r) * EXPERT_TOPK
        for k in range(EXPERT_TOPK):
            row_copy(pos_ref[a + k], k, r).start()
        return 0

    lax.fori_loop(0, CMB_TT, issue, 0)

    def drain(r, _):
        for k in range(EXPERT_TOPK):
            row_copy(0, k, 0).wait()
        return 0

    lax.fori_loop(0, CMB_TT, drain, 0)

    x3 = x2_ref[...] + (buf_ref[0] + buf_ref[1])
    r = lax.rsqrt(jnp.mean(x3 * x3, axis=-1, keepdims=True) + RMS_EPS)
    o_ref[...] = x3 * r * nw_ref[...]


def _combine(pos, x2, norm_w, y):
    t = x2.shape[0]
    return pl.pallas_call(
        _combine_kernel,
        grid_spec=pltpu.PrefetchScalarGridSpec(
            num_scalar_prefetch=1,
            grid=(t // CMB_TT,),
            in_specs=[
                pl.BlockSpec((CMB_TT, D_MODEL), lambda i, p: (i, 0)),
                pl.BlockSpec((1, D_MODEL), lambda i, p: (0, 0)),
                pl.BlockSpec(memory_space=pl.ANY),
            ],
            out_specs=pl.BlockSpec((CMB_TT, D_MODEL), lambda i, p: (i, 0)),
            scratch_shapes=[
                pltpu.VMEM((EXPERT_TOPK, CMB_TT, D_MODEL), F32),
                pltpu.SemaphoreType.DMA(()),
            ],
        ),
        out_shape=jax.ShapeDtypeStruct((t, D_MODEL), F32),
        compiler_params=pltpu.CompilerParams(
            dimension_semantics=("arbitrary",), vmem_limit_bytes=VMEM_LIMIT),
        name="combine",
    )(pos, x2, norm_w, y)


def _route(logits, n_tokens):
    g_logits = logits[:, :N_GROUPS]
    e_logits = logits[:, N_GROUPS:N_GROUPS + N_EXPERTS].reshape(n_tokens, N_GROUPS, EXPERTS_PER_GROUP)
    g_p, g_top = lax.top_k(jax.nn.softmax(g_logits, axis=-1), 1)
    e_in = jnp.take_along_axis(e_logits, g_top[:, :, None], axis=1)[:, 0]
    e_p, e_top = lax.top_k(jax.nn.softmax(e_in, axis=-1), EXPERT_TOPK)
    e_p = e_p / jnp.sum(e_p, axis=-1, keepdims=True)
    weights = (g_p * e_p).reshape(-1)
    expert = (g_top * EXPERTS_PER_GROUP + e_top).reshape(-1).astype(jnp.int32)

    n_asg = n_tokens * EXPERT_TOPK
    npad = n_asg + N_EXPERTS * MOE_TM
    onehot = (expert[:, None] == jnp.arange(N_EXPERTS, dtype=jnp.int32)[None, :]).astype(jnp.int32)
    csum = jnp.cumsum(onehot, axis=0)
    rank = jnp.take_along_axis(csum, expert[:, None], axis=1)[:, 0] - 1
    counts = csum[-1]
    padded = ((counts + MOE_TM - 1) // MOE_TM) * MOE_TM
    ends = jnp.cumsum(padded)
    pos = (ends - padded)[expert] + rank
    n_tiles = ends[-1] // MOE_TM
    tile_start = jnp.arange(npad // MOE_TM, dtype=jnp.int32) * MOE_TM
    tile_expert = jnp.minimum(jnp.searchsorted(ends, tile_start, side="right"), N_EXPERTS - 1)
    src = jnp.zeros((npad,), jnp.int32).at[pos].set(jnp.arange(n_asg, dtype=jnp.int32) // EXPERT_TOPK)
    roww = jnp.zeros((npad,), F32).at[pos].set(weights)
    return (pos.astype(jnp.int32), src, roww[:, None], tile_expert.astype(jnp.int32),
            n_tiles.astype(jnp.int32).reshape(1), npad)


def kernel(x, norm_mix_w, w_in, lambda_q1, lambda_k1, lambda_q2, lambda_k2, diff_subln_w, w_out, norm_ffn_w,
           w_router_group, b_router_group, w_router_expert, b_router_expert, w_gate, w_up, w_down, norm_final_w):
    batch, seq, _ = x.shape
    assert seq % MOBA_BLOCK == 0 and seq // MOBA_BLOCK <= LANES - 2
    assert w_in.shape[0] == 1, "single-layer block"
    n_tokens = batch * seq
    x2d = x.reshape(n_tokens, D_MODEL)

    slopes_a = _alibi_slopes(MOBA_HEADS)
    slopes_b = _alibi_slopes(DIFF_HEADS)
    for s in (slopes_a, slopes_b):
        assert np.all(np.log2(s) == np.round(np.log2(s))), "ALiBi slopes must be powers of two"

    col = np.arange(IN_WIDTH)
    is_q = (col < MOBA_HEADS * HEAD_DIM) | ((col >= 3 * MOBA_HEADS * HEAD_DIM) & (col < 4 * MOBA_HEADS * HEAD_DIM))
    col_scale = jnp.asarray(np.where(is_q, QK_SCALE, 1.0).astype(np.float32))[None, :]

    proj = _in_proj(x2d, norm_mix_w[0][None, :], w_in[0].astype(BF16), col_scale)
    out_a = _moba(proj, jnp.asarray(slopes_a), batch, seq)
    out_b = _diff(proj, jnp.asarray(slopes_b), lambda_q1, lambda_k1, lambda_q2, lambda_k2, diff_subln_w, batch, seq)

    half = D_MODEL // 2
    w_o = w_out[0].astype(BF16)
    w_r = jnp.concatenate([w_router_group[0], w_router_expert[0]], axis=1)
    w_r = jnp.pad(w_r, ((0, 0), (0, LANES - w_r.shape[1])))
    wr_hi = w_r.astype(BF16)
    wr_lo = (w_r - wr_hi.astype(F32)).astype(BF16)
    b_r = jnp.pad(jnp.concatenate([b_router_group[0], b_router_expert[0]]), (0, LANES - N_GROUPS - N_EXPERTS))[None, :]
    x2, hp, logits = _out_proj(x2d, out_a, out_b, w_o[:half], w_o[half:], norm_ffn_w[0][None, :], wr_hi, wr_lo, b_r)

    pos, src, roww, tile_expert, n_tiles, npad = _route(logits, n_tokens)
    xs = _dispatch(src, n_tiles, hp, npad)
    y = _experts(tile_expert, n_tiles, xs, roww, w_gate[0], w_up[0], w_down[0])
    out = _combine(pos, x2, norm_final_w[None, :], y)
    return out.reshape(batch, seq, D_MODEL)
```

```python
import functools

import ml_dtypes
import numpy as np
import jax
import jax.numpy as jnp
from jax import lax
from jax.experimental import pallas as pl
from jax.experimental.pallas import tpu as pltpu

F32 = jnp.float32
BF16 = jnp.bfloat16

D_MODEL = 2048
HEAD_DIM = 128
MOBA_HEADS = 8
MOBA_BLOCK = 256
MOBA_TOPK = 3
DIFF_HEADS = 4
DIFF_V_DIM = 256
IN_WIDTH = 6144
N_GROUPS = 4
EXPERTS_PER_GROUP = 8
N_EXPERTS = 32
EXPERT_TOPK = 2
D_EXPERT = 512
RMS_EPS = 1e-6
ALIBI_MAX_BIAS = 8.0
LAMBDA_INIT = 0.8 - 0.6 * float(np.exp(-0.3 * 0))

LANES = 128
QK_SCALE = HEAD_DIM ** -0.5
LOG2E = float(np.log2(np.e))
NEG_BIG = -1e30
M_INIT = -1e38
POS_SPLIT = 64
SLOPE_PIECES = 3
ROW_WORD_CHUNKS = D_MODEL // (2 * LANES)

IN_TM, IN_TN = 1024, 1024
ATT_TQ = 512
ATT_TK = 512
HEADS_PER_STEP = 2
KV_UNROLL = 2
OUT_TM = 512
MOE_TM = 256
CMB_TT = 256
VMEM_LIMIT = 56 * 1024 * 1024


def _alibi_slope_pieces(n):
    rem = np.exp2(-ALIBI_MAX_BIAS * (np.arange(n, dtype=np.float64) + 1.0) / n) * LOG2E
    pieces = []
    for _ in range(SLOPE_PIECES):
        p = rem.astype(ml_dtypes.bfloat16).astype(np.float64)
        pieces.append(p)
        rem = rem - p
    return np.stack(pieces, axis=1).reshape(-1).astype(np.float32)


def _dot_nt(a, b):
    return lax.dot_general(a, b, (((1,), (1,)), ((), ())), preferred_element_type=F32)


def _dot(a, b):
    return jnp.dot(a, b, preferred_element_type=F32)


def _inproj_kernel(x_ref, nw_ref, cs_ref, w_ref, o_ref, h_ref):
    @pl.when(pl.program_id(1) == 0)
    def _():
        x = x_ref[...]
        r = lax.rsqrt(jnp.mean(x * x, axis=-1, keepdims=True) + RMS_EPS)
        h_ref[...] = (x * r * nw_ref[...]).astype(BF16)

    acc = _dot(h_ref[...], w_ref[...])
    o_ref[...] = (acc * cs_ref[...]).astype(BF16)


def _in_proj(x2d, norm_w, w_in_bf16, col_scale):
    t = x2d.shape[0]
    return pl.pallas_call(
        _inproj_kernel,
        grid=(t // IN_TM, IN_WIDTH // IN_TN),
        in_specs=[
            pl.BlockSpec((IN_TM, D_MODEL), lambda i, j: (i, 0)),
            pl.BlockSpec((1, D_MODEL), lambda i, j: (0, 0)),
            pl.BlockSpec((1, IN_TN), lambda i, j: (0, j)),
            pl.BlockSpec((D_MODEL, IN_TN), lambda i, j: (0, j)),
        ],
        out_specs=pl.BlockSpec((IN_TM, IN_TN), lambda i, j: (i, j)),
        out_shape=jax.ShapeDtypeStruct((t, IN_WIDTH), BF16),
        scratch_shapes=[pltpu.VMEM((IN_TM, D_MODEL), BF16)],
        compiler_params=pltpu.CompilerParams(
            dimension_semantics=("arbitrary", "arbitrary"), vmem_limit_bytes=VMEM_LIMIT),
        name="in_proj",
    )(x2d, norm_w, col_scale, w_in_bf16)


def _key_position_features(seq, first_lane):
    pos = lax.broadcasted_iota(jnp.int32, (seq, LANES), 0)
    lane = lax.broadcasted_iota(jnp.int32, (seq, LANES), 1)
    hi = (lane >= first_lane) & (lane < first_lane + SLOPE_PIECES)
    lo = (lane >= first_lane + SLOPE_PIECES) & (lane < first_lane + 2 * SLOPE_PIECES)
    ext = jnp.where(hi, (pos // POS_SPLIT).astype(F32), 0.0)
    return jnp.where(lo, (pos % POS_SPLIT).astype(F32), ext), pos, lane


def _slope_feature_rows(feat_ref, head, width):
    row = lax.broadcasted_iota(jnp.int32, (8, width), 0)
    out = jnp.zeros((8, width), F32)
    for k in range(SLOPE_PIECES):
        c = feat_ref[head * SLOPE_PIECES + k]
        out = jnp.where(row == k, POS_SPLIT * c, out)
        out = jnp.where(row == SLOPE_PIECES + k, c, out)
    return out


def _causal_tile(s):
    row = lax.broadcasted_iota(jnp.int32, s.shape, 0)
    col = lax.broadcasted_iota(jnp.int32, s.shape, 1)
    return jnp.where(col <= row, s, -jnp.inf)


def _online_step(s, m):
    m_new = jnp.maximum(m, jnp.max(s, axis=-1, keepdims=True))
    return m_new, jnp.exp2(m - m_new), jnp.exp2(s - m_new)


def _unrolled_loop(n, body, init):
    def trip(t, carry):
        for u in range(KV_UNROLL):
            carry = body(t * KV_UNROLL + u, carry)
        return carry

    carry = lax.fori_loop(0, n // KV_UNROLL, trip, init)
    return lax.fori_loop((n // KV_UNROLL) * KV_UNROLL, n, body, carry)


def _moba_kernel(feat_ref, q_ref, k_ref, v_ref, o_ref, kaug_ref, vaug_ref, kmh_ref, kml_ref, qaug_ref, *, seq):
    nb = seq // MOBA_BLOCK
    hp = pl.program_id(1)
    i = pl.program_id(2)

    @pl.when(i == 0)
    def _():
        ext, pos, lane = _key_position_features(seq, nb)
        ext = jnp.where(lane == pos // MOBA_BLOCK, 1.0, ext).astype(BF16)
        ones_col = jnp.where(lane == 0, 1.0, 0.0).astype(BF16)
        for g in range(HEADS_PER_STEP):
            k = k_ref[:, g * HEAD_DIM:(g + 1) * HEAD_DIM]
            kaug_ref[g, :, 0:HEAD_DIM] = k
            kaug_ref[g, :, HEAD_DIM:2 * HEAD_DIM] = ext
            vaug_ref[g, :, 0:HEAD_DIM] = v_ref[:, g * HEAD_DIM:(g + 1) * HEAD_DIM]
            vaug_ref[g, :, HEAD_DIM:2 * HEAD_DIM] = ones_col
            km = jnp.mean(k.astype(F32).reshape(nb, MOBA_BLOCK, HEAD_DIM), axis=1)
            kmh = km.astype(BF16)
            kmh_ref[g] = jnp.zeros((LANES, HEAD_DIM), BF16)
            kml_ref[g] = jnp.zeros((LANES, HEAD_DIM), BF16)
            kmh_ref[g, 0:nb, :] = kmh
            kml_ref[g, 0:nb, :] = (km - kmh.astype(F32)).astype(BF16)

    blk = lax.broadcasted_iota(jnp.int32, (nb, ATT_TQ), 0)
    col = lax.broadcasted_iota(jnp.int32, (nb, ATT_TQ), 1)
    own = i * (ATT_TQ // MOBA_BLOCK) + col // MOBA_BLOCK
    blk_f = blk.astype(F32)
    for g in range(HEADS_PER_STEP):
        q = q_ref[:, g * HEAD_DIM:(g + 1) * HEAD_DIM]
        gate = (_dot_nt(kmh_ref[g], q) + _dot_nt(kml_ref[g], q))[0:nb, :]
        gate = jnp.where(blk < own, gate, -jnp.inf)
        sel = jnp.zeros((nb, ATT_TQ), F32)
        for _ in range(MOBA_TOPK):
            mx = jnp.max(gate, axis=0, keepdims=True)
            idx = jnp.min(jnp.where(gate == mx, blk_f, float(LANES)), axis=0, keepdims=True)
            pick = (blk_f == idx) & (mx > -jnp.inf)
            sel = jnp.where(pick, 1.0, sel)
            gate = jnp.where(pick, -jnp.inf, gate)
        bias = jnp.where((sel > 0.0) | (blk == own), 0.0, NEG_BIG)
        feat = jnp.concatenate(
            [bias, _slope_feature_rows(feat_ref, hp * HEADS_PER_STEP + g, ATT_TQ),
             jnp.zeros((LANES - nb - 8, ATT_TQ), F32)], axis=0)
        qaug_ref[g, :, 0:HEAD_DIM] = q
        qaug_ref[g, :, HEAD_DIM:2 * HEAD_DIM] = feat.T.astype(BF16)

    def tile_step(g, r, state, causal):
        m, acc = state
        s = _dot_nt(qaug_ref[g], kaug_ref[g, pl.ds(r, ATT_TK), :])
        if causal:
            s = _causal_tile(s)
        m, alpha, p = _online_step(s, m)
        return m, alpha * acc + _dot(p.astype(BF16), vaug_ref[g, pl.ds(r, ATT_TK), :])

    def body(j, states):
        r = pl.multiple_of(j * ATT_TK, ATT_TK)
        return tuple(tile_step(g, r, states[g], False) for g in range(HEADS_PER_STEP))

    init = tuple((jnp.full((ATT_TQ, 1), M_INIT, F32), jnp.zeros((ATT_TQ, 2 * HEAD_DIM), F32))
                 for _ in range(HEADS_PER_STEP))
    states = _unrolled_loop(i, body, init)
    r0 = pl.multiple_of(i * ATT_TQ, ATT_TQ)
    for g in range(HEADS_PER_STEP):
        _, acc = tile_step(g, r0, states[g], True)
        o_ref[:, g * HEAD_DIM:(g + 1) * HEAD_DIM] = (
            acc[:, 0:HEAD_DIM] / acc[:, HEAD_DIM:HEAD_DIM + 1]).astype(BF16)


def _moba(proj, feats, batch, seq):
    nq = seq // ATT_TQ
    width = HEADS_PER_STEP * HEAD_DIM
    q_col, k_col, v_col = 0, MOBA_HEADS // HEADS_PER_STEP, 2 * MOBA_HEADS // HEADS_PER_STEP
    return pl.pallas_call(
        functools.partial(_moba_kernel, seq=seq),
        grid_spec=pltpu.PrefetchScalarGridSpec(
            num_scalar_prefetch=1,
            grid=(batch, MOBA_HEADS // HEADS_PER_STEP, nq),
            in_specs=[
                pl.BlockSpec((ATT_TQ, width), lambda b, h, i, s: (b * nq + i, q_col + h)),
                pl.BlockSpec((seq, width), lambda b, h, i, s: (b, k_col + h)),
                pl.BlockSpec((seq, width), lambda b, h, i, s: (b, v_col + h)),
            ],
            out_specs=pl.BlockSpec((ATT_TQ, width), lambda b, h, i, s: (b * nq + i, h)),
            scratch_shapes=[
                pltpu.VMEM((HEADS_PER_STEP, seq, 2 * HEAD_DIM), BF16),
                pltpu.VMEM((HEADS_PER_STEP, seq, 2 * HEAD_DIM), BF16),
                pltpu.VMEM((HEADS_PER_STEP, LANES, HEAD_DIM), BF16),
                pltpu.VMEM((HEADS_PER_STEP, LANES, HEAD_DIM), BF16),
                pltpu.VMEM((HEADS_PER_STEP, ATT_TQ, 2 * HEAD_DIM), BF16),
            ],
        ),
        out_shape=jax.ShapeDtypeStruct((batch * seq, MOBA_HEADS * HEAD_DIM), BF16),
        compiler_params=pltpu.CompilerParams(
            dimension_semantics=("arbitrary", "arbitrary", "arbitrary"), vmem_limit_bytes=VMEM_LIMIT),
        name="moba",
    )(feats, proj, proj, proj)


def _diff_kernel(feat_ref, lq1_ref, lk1_ref, lq2_ref, lk2_ref, sw_ref,
                 q_ref, k_ref, v_ref, o_ref, kaug_ref, qaug_ref, *, seq):
    h = pl.program_id(1)
    i = pl.program_id(2)

    @pl.when(i == 0)
    def _():
        ext = _key_position_features(seq, 0)[0].astype(BF16)
        for c in range(2):
            kaug_ref[c, :, 0:HEAD_DIM] = k_ref[:, c * HEAD_DIM:(c + 1) * HEAD_DIM]
            kaug_ref[c, :, HEAD_DIM:2 * HEAD_DIM] = ext

    feat = jnp.concatenate([_slope_feature_rows(feat_ref, h, ATT_TQ),
                            jnp.zeros((LANES - 8, ATT_TQ), F32)], axis=0).T.astype(BF16)
    for c in range(2):
        qaug_ref[c, :, 0:HEAD_DIM] = q_ref[:, c * HEAD_DIM:(c + 1) * HEAD_DIM]
        qaug_ref[c, :, HEAD_DIM:2 * HEAD_DIM] = feat

    def tile_step(c, r, v, state, causal):
        m, l, acc = state
        s = _dot_nt(qaug_ref[c], kaug_ref[c, pl.ds(r, ATT_TK), :])
        if causal:
            s = _causal_tile(s)
        m, alpha, p = _online_step(s, m)
        return (m, alpha * l + jnp.sum(p, axis=-1, keepdims=True),
                alpha * acc + _dot(p.astype(BF16), v))

    def body(j, states):
        r = pl.multiple_of(j * ATT_TK, ATT_TK)
        v = v_ref[pl.ds(r, ATT_TK), :]
        return tuple(tile_step(c, r, v, states[c], False) for c in range(2))

    init = tuple((jnp.full((ATT_TQ, 1), M_INIT, F32), jnp.zeros((ATT_TQ, 1), F32),
                  jnp.zeros((ATT_TQ, DIFF_V_DIM), F32)) for _ in range(2))
    states = _unrolled_loop(i, body, init)
    r0 = pl.multiple_of(i * ATT_TQ, ATT_TQ)
    v0 = v_ref[pl.ds(r0, ATT_TK), :]
    (_, l1, acc1), (_, l2, acc2) = (tile_step(c, r0, v0, states[c], True) for c in range(2))

    lam = (jnp.exp(jnp.sum(lq1_ref[...] * lk1_ref[...], axis=-1, keepdims=True))
           - jnp.exp(jnp.sum(lq2_ref[...] * lk2_ref[...], axis=-1, keepdims=True))
           + LAMBDA_INIT)
    o = acc1 / l1 - lam * (acc2 / l2)
    y = o * lax.rsqrt(jnp.mean(o * o, axis=-1, keepdims=True) + RMS_EPS)
    o_ref[...] = ((y * sw_ref[...]) * (1.0 - LAMBDA_INIT)).astype(BF16)


def _diff(proj, feats, lq1, lk1, lq2, lk2, subln_w, batch, seq):
    nq = seq // ATT_TQ
    width = 2 * HEAD_DIM
    q_col = 3 * MOBA_HEADS * HEAD_DIM // width
    k_col = q_col + DIFF_HEADS
    v_col = k_col + DIFF_HEADS
    vec = lambda n: pl.BlockSpec((1, n), lambda b, h, i, s: (0, 0))
    return pl.pallas_call(
        functools.partial(_diff_kernel, seq=seq),
        grid_spec=pltpu.PrefetchScalarGridSpec(
            num_scalar_prefetch=1,
            grid=(batch, DIFF_HEADS, nq),
            in_specs=[
                vec(HEAD_DIM), vec(HEAD_DIM), vec(HEAD_DIM), vec(HEAD_DIM), vec(DIFF_V_DIM),
                pl.BlockSpec((ATT_TQ, width), lambda b, h, i, s: (b * nq + i, q_col + h)),
                pl.BlockSpec((seq, width), lambda b, h, i, s: (b, k_col + h)),
                pl.BlockSpec((seq, DIFF_V_DIM), lambda b, h, i, s: (b, v_col + h)),
            ],
            out_specs=pl.BlockSpec((ATT_TQ, DIFF_V_DIM), lambda b, h, i, s: (b * nq + i, h)),
            scratch_shapes=[
                pltpu.VMEM((2, seq, 2 * HEAD_DIM), BF16),
                pltpu.VMEM((2, ATT_TQ, 2 * HEAD_DIM), BF16),
            ],
        ),
        out_shape=jax.ShapeDtypeStruct((batch * seq, DIFF_HEADS * DIFF_V_DIM), BF16),
        compiler_params=pltpu.CompilerParams(
            dimension_semantics=("arbitrary", "arbitrary", "arbitrary"), vmem_limit_bytes=VMEM_LIMIT),
        name="diff_attn",
    )(feats, lq1, lk1, lq2, lk2, subln_w, proj, proj, proj)


def _outproj_kernel(x_ref, oa_ref, ob_ref, wa_ref, wb_ref, nw_ref, wrh_ref, wrl_ref, br_ref,
                    x2_ref, hp_ref, lg_ref):
    x2 = x_ref[...] + _dot(oa_ref[...], wa_ref[...]) + _dot(ob_ref[...], wb_ref[...])
    x2_ref[...] = x2
    r = lax.rsqrt(jnp.mean(x2 * x2, axis=-1, keepdims=True) + RMS_EPS)
    h = x2 * r * nw_ref[...]
    hh = h.astype(BF16)
    hl = (h - hh.astype(F32)).astype(BF16)
    lg_ref[...] = (_dot(hh, wrh_ref[...]) + _dot(hl, wrh_ref[...]) + _dot(hh, wrl_ref[...])
                   + br_ref[...])
    bits = pltpu.bitcast(hh.astype(F32), jnp.uint32)
    for c in range(ROW_WORD_CHUNKS):
        lo = bits[:, 2 * c * LANES:(2 * c + 1) * LANES]
        hi = bits[:, (2 * c + 1) * LANES:(2 * c + 2) * LANES]
        hp_ref[pl.ds(c, OUT_TM, stride=ROW_WORD_CHUNKS), :] = (lo >> 16) | (hi & jnp.uint32(0xFFFF0000))


def _out_proj(x2d, out_a, out_b, wa, wb, norm_w, wr_hi, wr_lo, b_r):
    t = x2d.shape[0]
    half = D_MODEL // 2
    row = lambda n: pl.BlockSpec((OUT_TM, n), lambda i: (i, 0))
    full = lambda r, c: pl.BlockSpec((r, c), lambda i: (0, 0))
    return pl.pallas_call(
        _outproj_kernel,
        grid=(t // OUT_TM,),
        in_specs=[row(D_MODEL), row(half), row(half), full(half, D_MODEL), full(half, D_MODEL),
                  full(1, D_MODEL), full(D_MODEL, LANES), full(D_MODEL, LANES), full(1, LANES)],
        out_specs=[row(D_MODEL), pl.BlockSpec((OUT_TM * ROW_WORD_CHUNKS, LANES), lambda i: (i, 0)), row(LANES)],
        out_shape=[jax.ShapeDtypeStruct((t, D_MODEL), F32),
                   jax.ShapeDtypeStruct((t * ROW_WORD_CHUNKS, LANES), jnp.uint32),
                   jax.ShapeDtypeStruct((t, LANES), F32)],
        compiler_params=pltpu.CompilerParams(
            dimension_semantics=("arbitrary",), vmem_limit_bytes=VMEM_LIMIT),
        name="out_proj",
    )(x2d, out_a, out_b, wa, wb, norm_w, wr_hi, wr_lo, b_r)


GATHER_UNROLL = 8


def _dispatch_kernel(src_ref, nt_ref, hp_ref, xs_ref):
    t = pl.program_id(0)

    @pl.when(t < nt_ref[0])
    def _():
        def one(r, _):
            xs_ref[r] = hp_ref[src_ref[t * MOE_TM + r]]
            return 0
        lax.fori_loop(0, MOE_TM, one, 0, unroll=GATHER_UNROLL)

    @pl.when(t >= nt_ref[0])
    def _():
        xs_ref[...] = jnp.zeros(xs_ref.shape, xs_ref.dtype)


def _dispatch(src, n_tiles, hp, npad):
    n_tokens = hp.shape[0] // ROW_WORD_CHUNKS
    hp3 = hp.reshape(n_tokens, ROW_WORD_CHUNKS, LANES)
    xs = pl.pallas_call(
        _dispatch_kernel,
        grid_spec=pltpu.PrefetchScalarGridSpec(
            num_scalar_prefetch=2,
            grid=(npad // MOE_TM,),
            in_specs=[pl.BlockSpec(memory_space=pltpu.VMEM)],
            out_specs=pl.BlockSpec((MOE_TM, ROW_WORD_CHUNKS, LANES), lambda t, s, n: (t, 0, 0)),
        ),
        out_shape=jax.ShapeDtypeStruct((npad, ROW_WORD_CHUNKS, LANES), hp.dtype),
        compiler_params=pltpu.CompilerParams(
            dimension_semantics=("arbitrary",), vmem_limit_bytes=VMEM_LIMIT),
        name="dispatch",
    )(src, n_tiles, hp3)
    return xs.reshape(npad * ROW_WORD_CHUNKS, LANES)


def _experts_kernel(te_ref, nt_ref, xs_ref, rw_ref, wg_ref, wu_ref, wd_ref, y_ref,
                    wgb_ref, wub_ref, wdb_ref, x_ref):
    t = pl.program_id(0)

    @pl.when(t < nt_ref[0])
    def _():
        e = te_ref[t]
        prev = te_ref[jnp.maximum(t - 1, 0)]

        @pl.when((t == 0) | (e != prev))
        def _():
            wgb_ref[...] = wg_ref[0].astype(BF16)
            wub_ref[...] = wu_ref[0].astype(BF16)
            wdb_ref[...] = wd_ref[0].astype(BF16)

        for c in range(ROW_WORD_CHUNKS):
            u32 = xs_ref[pl.ds(c, MOE_TM, stride=ROW_WORD_CHUNKS), :]
            x_ref[:, 2 * c * LANES:(2 * c + 1) * LANES] = pltpu.bitcast(u32 << 16, F32).astype(BF16)
            x_ref[:, (2 * c + 1) * LANES:(2 * c + 2) * LANES] = (
                pltpu.bitcast(u32 & jnp.uint32(0xFFFF0000), F32).astype(BF16))
        x = x_ref[...]
        a = _dot(x, wgb_ref[...])
        u = _dot(x, wub_ref[...])
        act = (a * jax.nn.sigmoid(a)) * u * rw_ref[...]
        y_ref[...] = _dot(act.astype(BF16), wdb_ref[...])

    @pl.when(t >= nt_ref[0])
    def _():
        y_ref[...] = jnp.zeros(y_ref.shape, y_ref.dtype)


def _experts(tile_expert, n_tiles, xs, roww, w_gate, w_up, w_down):
    npad = xs.shape[0] // ROW_WORD_CHUNKS
    last = lambda t, nt: jnp.minimum(t, jnp.maximum(nt[0] - 1, 0))
    return pl.pallas_call(
        _experts_kernel,
        grid_spec=pltpu.PrefetchScalarGridSpec(
            num_scalar_prefetch=2,
            grid=(npad // MOE_TM,),
            in_specs=[
                pl.BlockSpec((MOE_TM * ROW_WORD_CHUNKS, LANES), lambda t, te, nt: (last(t, nt), 0)),
                pl.BlockSpec((MOE_TM, 1), lambda t, te, nt: (last(t, nt), 0)),
                pl.BlockSpec((1, D_MODEL, D_EXPERT), lambda t, te, nt: (te[last(t, nt)], 0, 0)),
                pl.BlockSpec((1, D_MODEL, D_EXPERT), lambda t, te, nt: (te[last(t, nt)], 0, 0)),
                pl.BlockSpec((1, D_EXPERT, D_MODEL), lambda t, te, nt: (te[last(t, nt)], 0, 0)),
            ],
            out_specs=pl.BlockSpec((MOE_TM, D_MODEL), lambda t, te, nt: (t, 0)),
            scratch_shapes=[
                pltpu.VMEM((D_MODEL, D_EXPERT), BF16),
                pltpu.VMEM((D_MODEL, D_EXPERT), BF16),
                pltpu.VMEM((D_EXPERT, D_MODEL), BF16),
                pltpu.VMEM((MOE_TM, D_MODEL), BF16),
            ],
        ),
        out_shape=jax.ShapeDtypeStruct((npad, D_MODEL), F32),
        compiler_params=pltpu.CompilerParams(
            dimension_semantics=("arbitrary",), vmem_limit_bytes=VMEM_LIMIT),
        name="experts",
    )(tile_expert, n_tiles, xs, roww, w_gate, w_up, w_down)


def _combine_kernel(pos_ref, x2_ref, nw_ref, y_ref, o_ref, buf_ref, sem):
    i = pl.program_id(0)

    def row_copy(p, k, r):
        return pltpu.make_async_copy(y_ref.at[pl.ds(p, 1), :], buf_ref.at[k, pl.ds(r, 1), :], sem)

    def issue(r, _):
        a = (i * CMB_TT + r) * EXPERT_TOPK
        for k in range(EXPERT_TOPK):
            row_copy(pos_ref[a + k], k, r).start()
        return 0

    lax.fori_loop(0, CMB_TT, issue, 0)

    def drain(r, _):
        for k in range(EXPERT_TOPK):
            row_copy(0, k, 0).wait()
        return 0

    lax.fori_loop(0, CMB_TT, drain, 0)

    x3 = x2_ref[...] + (buf_ref[0] + buf_ref[1])
    r = lax.rsqrt(jnp.mean(x3 * x3, axis=-1, keepdims=True) + RMS_EPS)
    o_ref[...] = x3 * r * nw_ref[...]


def _combine(pos, x2, norm_w, y):
    t = x2.shape[0]
    return pl.pallas_call(
        _combine_kernel,
        grid_spec=pltpu.PrefetchScalarGridSpec(
            num_scalar_prefetch=1,
            grid=(t // CMB_TT,),
            in_specs=[
                pl.BlockSpec((CMB_TT, D_MODEL), lambda i, p: (i, 0)),
                pl.BlockSpec((1, D_MODEL), lambda i, p: (0, 0)),
                pl.BlockSpec(memory_space=pl.ANY),
            ],
            out_specs=pl.BlockSpec((CMB_TT, D_MODEL), lambda i, p: (i, 0)),
            scratch_shapes=[
                pltpu.VMEM((EXPERT_TOPK, CMB_TT, D_MODEL), F32),
                pltpu.SemaphoreType.DMA(()),
            ],
        ),
        out_shape=jax.ShapeDtypeStruct((t, D_MODEL), F32),
        compiler_params=pltpu.CompilerParams(
            dimension_semantics=("arbitrary",), vmem_limit_bytes=VMEM_LIMIT),
        name="combine",
    )(pos, x2, norm_w, y)


def _route(logits, n_tokens):
    g_logits = logits[:, :N_GROUPS]
    e_logits = logits[:, N_GROUPS:N_GROUPS + N_EXPERTS].reshape(n_tokens, N_GROUPS, EXPERTS_PER_GROUP)
    g_p, g_top = lax.top_k(jax.nn.softmax(g_logits, axis=-1), 1)
    e_in = jnp.take_along_axis(e_logits, g_top[:, :, None], axis=1)[:, 0]
    e_p, e_top = lax.top_k(jax.nn.softmax(e_in, axis=-1), EXPERT_TOPK)
    e_p = e_p / jnp.sum(e_p, axis=-1, keepdims=True)
    weights = (g_p * e_p).reshape(-1)
    expert = (g_top * EXPERTS_PER_GROUP + e_top).reshape(-1).astype(jnp.int32)

    n_asg = n_tokens * EXPERT_TOPK
    npad = n_asg + N_EXPERTS * MOE_TM
    onehot = (expert[:, None] == jnp.arange(N_EXPERTS, dtype=jnp.int32)[None, :]).astype(jnp.int32)
    csum = jnp.cumsum(onehot, axis=0)
    rank = jnp.take_along_axis(csum, expert[:, None], axis=1)[:, 0] - 1
    counts = csum[-1]
    padded = ((counts + MOE_TM - 1) // MOE_TM) * MOE_TM
    ends = jnp.cumsum(padded)
    pos = (ends - padded)[expert] + rank
    n_tiles = ends[-1] // MOE_TM
    tile_start = jnp.arange(npad // MOE_TM, dtype=jnp.int32) * MOE_TM
    tile_expert = jnp.minimum(jnp.sum(ends[None, :] <= tile_start[:, None], axis=1), N_EXPERTS - 1)
    src = jnp.zeros((npad,), jnp.int32).at[pos].set(jnp.arange(n_asg, dtype=jnp.int32) // EXPERT_TOPK)
    roww = jnp.zeros((npad,), F32).at[pos].set(weights)
    return (pos.astype(jnp.int32), src, roww[:, None], tile_expert.astype(jnp.int32),
            n_tiles.astype(jnp.int32).reshape(1), npad)


def kernel(x, norm_mix_w, w_in, lambda_q1, lambda_k1, lambda_q2, lambda_k2, diff_subln_w, w_out, norm_ffn_w,
           w_router_group, b_router_group, w_router_expert, b_router_expert, w_gate, w_up, w_down, norm_final_w):
    batch, seq, _ = x.shape
    assert seq % ATT_TQ == 0 and seq // MOBA_BLOCK + 8 <= LANES and ATT_TK == ATT_TQ
    assert w_in.shape[0] == 1, "single-layer block"
    n_tokens = batch * seq
    x2d = x.reshape(n_tokens, D_MODEL)

    col = np.arange(IN_WIDTH)
    is_q = (col < MOBA_HEADS * HEAD_DIM) | ((col >= 3 * MOBA_HEADS * HEAD_DIM) & (col < 4 * MOBA_HEADS * HEAD_DIM))
    col_scale = jnp.asarray(np.where(is_q, QK_SCALE * LOG2E, 1.0).astype(np.float32))[None, :]

    proj = _in_proj(x2d, norm_mix_w[0][None, :], w_in[0].astype(BF16), col_scale)
    out_a = _moba(proj, jnp.asarray(_alibi_slope_pieces(MOBA_HEADS)), batch, seq)
    out_b = _diff(proj, jnp.asarray(_alibi_slope_pieces(DIFF_HEADS)), lambda_q1, lambda_k1, lambda_q2, lambda_k2,
                  diff_subln_w, batch, seq)

    half = D_MODEL // 2
    w_o = w_out[0].astype(BF16)
    w_r = jnp.concatenate([w_router_group[0], w_router_expert[0]], axis=1)
    w_r = jnp.pad(w_r, ((0, 0), (0, LANES - w_r.shape[1])))
    wr_hi = w_r.astype(BF16)
    wr_lo = (w_r - wr_hi.astype(F32)).astype(BF16)
    b_r = jnp.pad(jnp.concatenate([b_router_group[0], b_router_expert[0]]), (0, LANES - N_GROUPS - N_EXPERTS))[None, :]
    x2, hp, logits = _out_proj(x2d, out_a, out_b, w_o[:half], w_o[half:], norm_ffn_w[0][None, :], wr_hi, wr_lo, b_r)

    pos, src, roww, tile_expert, n_tiles, npad = _route(logits, n_tokens)
    xs = _dispatch(src, n_tiles, hp, npad)
    y = _experts(tile_expert, n_tiles, xs, roww, w_gate[0], w_up[0], w_down[0])
    out = _combine(pos, x2, norm_final_w[None, :], y)
    return out.reshape(batch, seq, D_MODEL)
```

```python
import functools

import ml_dtypes
import numpy as np
import jax
import jax.numpy as jnp
from jax import lax
from jax.experimental import pallas as pl
from jax.experimental.pallas import tpu as pltpu

F32 = jnp.float32
BF16 = jnp.bfloat16

D_MODEL = 2048
HEAD_DIM = 128
MOBA_HEADS = 8
MOBA_BLOCK = 256
MOBA_TOPK = 3
DIFF_HEADS = 4
DIFF_V_DIM = 256
IN_WIDTH = 6144
N_GROUPS = 4
EXPERTS_PER_GROUP = 8
N_EXPERTS = 32
EXPERT_TOPK = 2
D_EXPERT = 512
RMS_EPS = 1e-6
ALIBI_MAX_BIAS = 8.0
LAMBDA_INIT = 0.8 - 0.6 * float(np.exp(-0.3 * 0))

LANES = 128
QK_SCALE = HEAD_DIM ** -0.5
LOG2E = float(np.log2(np.e))
NEG_BIG = -1e30
M_INIT = -1e38
POS_SPLIT = 64
SLOPE_PIECES = 3
ROW_WORD_CHUNKS = D_MODEL // (2 * LANES)
META_X1, META_X2, META_W1, META_W2, META_RANK1, META_RANK2 = range(6)

IN_TM, IN_TN = 1024, 1024
ATT_TQ = 512
ATT_TK = 512
HEADS_PER_STEP = 2
KV_UNROLL = 2
OUT_TM = 512
MOE_TM = 256
CMB_TT = 256
VMEM_LIMIT = 56 * 1024 * 1024


def _alibi_slope_pieces(n):
    rem = np.exp2(-ALIBI_MAX_BIAS * (np.arange(n, dtype=np.float64) + 1.0) / n) * LOG2E
    pieces = []
    for _ in range(SLOPE_PIECES):
        p = rem.astype(ml_dtypes.bfloat16).astype(np.float64)
        pieces.append(p)
        rem = rem - p
    return np.stack(pieces, axis=1).reshape(-1).astype(np.float32)


def _dot_nt(a, b):
    return lax.dot_general(a, b, (((1,), (1,)), ((), ())), preferred_element_type=F32)


def _dot(a, b):
    return jnp.dot(a, b, preferred_element_type=F32)


def _inproj_kernel(x_ref, nw_ref, cs_ref, w_ref, o_ref, h_ref):
    @pl.when(pl.program_id(1) == 0)
    def _():
        x = x_ref[...]
        r = lax.rsqrt(jnp.mean(x * x, axis=-1, keepdims=True) + RMS_EPS)
        h_ref[...] = (x * r * nw_ref[...]).astype(BF16)

    acc = _dot(h_ref[...], w_ref[...])
    o_ref[...] = (acc * cs_ref[...]).astype(BF16)


def _in_proj(x2d, norm_w, w_in_bf16, col_scale):
    t = x2d.shape[0]
    return pl.pallas_call(
        _inproj_kernel,
        grid=(t // IN_TM, IN_WIDTH // IN_TN),
        in_specs=[
            pl.BlockSpec((IN_TM, D_MODEL), lambda i, j: (i, 0)),
            pl.BlockSpec((1, D_MODEL), lambda i, j: (0, 0)),
            pl.BlockSpec((1, IN_TN), lambda i, j: (0, j)),
            pl.BlockSpec((D_MODEL, IN_TN), lambda i, j: (0, j)),
        ],
        out_specs=pl.BlockSpec((IN_TM, IN_TN), lambda i, j: (i, j)),
        out_shape=jax.ShapeDtypeStruct((t, IN_WIDTH), BF16),
        scratch_shapes=[pltpu.VMEM((IN_TM, D_MODEL), BF16)],
        compiler_params=pltpu.CompilerParams(
            dimension_semantics=("arbitrary", "arbitrary"), vmem_limit_bytes=VMEM_LIMIT),
        name="in_proj",
    )(x2d, norm_w, col_scale, w_in_bf16)


def _key_position_features(seq, first_lane):
    pos = lax.broadcasted_iota(jnp.int32, (seq, LANES), 0)
    lane = lax.broadcasted_iota(jnp.int32, (seq, LANES), 1)
    hi = (lane >= first_lane) & (lane < first_lane + SLOPE_PIECES)
    lo = (lane >= first_lane + SLOPE_PIECES) & (lane < first_lane + 2 * SLOPE_PIECES)
    ext = jnp.where(hi, (pos // POS_SPLIT).astype(F32), 0.0)
    return jnp.where(lo, (pos % POS_SPLIT).astype(F32), ext), pos, lane


def _slope_feature_rows(feat_ref, head, width):
    row = lax.broadcasted_iota(jnp.int32, (8, width), 0)
    out = jnp.zeros((8, width), F32)
    for k in range(SLOPE_PIECES):
        c = feat_ref[head * SLOPE_PIECES + k]
        out = jnp.where(row == k, POS_SPLIT * c, out)
        out = jnp.where(row == SLOPE_PIECES + k, c, out)
    return out


def _causal_tile(s):
    row = lax.broadcasted_iota(jnp.int32, s.shape, 0)
    col = lax.broadcasted_iota(jnp.int32, s.shape, 1)
    return jnp.where(col <= row, s, -jnp.inf)


def _online_step(s, m):
    m_new = jnp.maximum(m, jnp.max(s, axis=-1, keepdims=True))
    return m_new, jnp.exp2(m - m_new), jnp.exp2(s - m_new)


def _unrolled_loop(n, body, init):
    def trip(t, carry):
        for u in range(KV_UNROLL):
            carry = body(t * KV_UNROLL + u, carry)
        return carry

    carry = lax.fori_loop(0, n // KV_UNROLL, trip, init)
    return lax.fori_loop((n // KV_UNROLL) * KV_UNROLL, n, body, carry)


def _moba_kernel(feat_ref, q_ref, k_ref, v_ref, o_ref, kaug_ref, vaug_ref, kmh_ref, kml_ref, qaug_ref, *, seq):
    nb = seq // MOBA_BLOCK
    hp = pl.program_id(1)
    i = pl.program_id(2)

    @pl.when(i == 0)
    def _():
        ext, pos, lane = _key_position_features(seq, nb)
        ext = jnp.where(lane == pos // MOBA_BLOCK, 1.0, ext).astype(BF16)
        ones_col = jnp.where(lane == 0, 1.0, 0.0).astype(BF16)
        for g in range(HEADS_PER_STEP):
            k = k_ref[:, g * HEAD_DIM:(g + 1) * HEAD_DIM]
            kaug_ref[g, :, 0:HEAD_DIM] = k
            kaug_ref[g, :, HEAD_DIM:2 * HEAD_DIM] = ext
            vaug_ref[g, :, 0:HEAD_DIM] = v_ref[:, g * HEAD_DIM:(g + 1) * HEAD_DIM]
            vaug_ref[g, :, HEAD_DIM:2 * HEAD_DIM] = ones_col
            km = jnp.mean(k.astype(F32).reshape(nb, MOBA_BLOCK, HEAD_DIM), axis=1)
            kmh = km.astype(BF16)
            kmh_ref[g] = jnp.zeros((LANES, HEAD_DIM), BF16)
            kml_ref[g] = jnp.zeros((LANES, HEAD_DIM), BF16)
            kmh_ref[g, 0:nb, :] = kmh
            kml_ref[g, 0:nb, :] = (km - kmh.astype(F32)).astype(BF16)

    blk = lax.broadcasted_iota(jnp.int32, (nb, ATT_TQ), 0)
    col = lax.broadcasted_iota(jnp.int32, (nb, ATT_TQ), 1)
    own = i * (ATT_TQ // MOBA_BLOCK) + col // MOBA_BLOCK
    blk_f = blk.astype(F32)
    for g in range(HEADS_PER_STEP):
        q = q_ref[:, g * HEAD_DIM:(g + 1) * HEAD_DIM]
        gate = (_dot_nt(kmh_ref[g], q) + _dot_nt(kml_ref[g], q))[0:nb, :]
        gate = jnp.where(blk < own, gate, -jnp.inf)
        sel = jnp.zeros((nb, ATT_TQ), F32)
        for _ in range(MOBA_TOPK):
            mx = jnp.max(gate, axis=0, keepdims=True)
            idx = jnp.min(jnp.where(gate == mx, blk_f, float(LANES)), axis=0, keepdims=True)
            pick = (blk_f == idx) & (mx > -jnp.inf)
            sel = jnp.where(pick, 1.0, sel)
            gate = jnp.where(pick, -jnp.inf, gate)
        bias = jnp.where((sel > 0.0) | (blk == own), 0.0, NEG_BIG)
        feat = jnp.concatenate(
            [bias, _slope_feature_rows(feat_ref, hp * HEADS_PER_STEP + g, ATT_TQ),
             jnp.zeros((LANES - nb - 8, ATT_TQ), F32)], axis=0)
        qaug_ref[g, :, 0:HEAD_DIM] = q
        qaug_ref[g, :, HEAD_DIM:2 * HEAD_DIM] = feat.T.astype(BF16)

    def tile_step(g, r, state, causal):
        m, acc = state
        s = _dot_nt(qaug_ref[g], kaug_ref[g, pl.ds(r, ATT_TK), :])
        if causal:
            s = _causal_tile(s)
        m, alpha, p = _online_step(s, m)
        return m, alpha * acc + _dot(p.astype(BF16), vaug_ref[g, pl.ds(r, ATT_TK), :])

    def body(j, states):
        r = pl.multiple_of(j * ATT_TK, ATT_TK)
        return tuple(tile_step(g, r, states[g], False) for g in range(HEADS_PER_STEP))

    init = tuple((jnp.full((ATT_TQ, 1), M_INIT, F32), jnp.zeros((ATT_TQ, 2 * HEAD_DIM), F32))
                 for _ in range(HEADS_PER_STEP))
    states = _unrolled_loop(i, body, init)
    r0 = pl.multiple_of(i * ATT_TQ, ATT_TQ)
    for g in range(HEADS_PER_STEP):
        _, acc = tile_step(g, r0, states[g], True)
        o_ref[:, g * HEAD_DIM:(g + 1) * HEAD_DIM] = (
            acc[:, 0:HEAD_DIM] / acc[:, HEAD_DIM:HEAD_DIM + 1]).astype(BF16)


def _moba(proj, feats, batch, seq):
    nq = seq // ATT_TQ
    width = HEADS_PER_STEP * HEAD_DIM
    q_col, k_col, v_col = 0, MOBA_HEADS // HEADS_PER_STEP, 2 * MOBA_HEADS // HEADS_PER_STEP
    return pl.pallas_call(
        functools.partial(_moba_kernel, seq=seq),
        grid_spec=pltpu.PrefetchScalarGridSpec(
            num_scalar_prefetch=1,
            grid=(batch, MOBA_HEADS // HEADS_PER_STEP, nq),
            in_specs=[
                pl.BlockSpec((ATT_TQ, width), lambda b, h, i, s: (b * nq + i, q_col + h)),
                pl.BlockSpec((seq, width), lambda b, h, i, s: (b, k_col + h)),
                pl.BlockSpec((seq, width), lambda b, h, i, s: (b, v_col + h)),
            ],
            out_specs=pl.BlockSpec((ATT_TQ, width), lambda b, h, i, s: (b * nq + i, h)),
            scratch_shapes=[
                pltpu.VMEM((HEADS_PER_STEP, seq, 2 * HEAD_DIM), BF16),
                pltpu.VMEM((HEADS_PER_STEP, seq, 2 * HEAD_DIM), BF16),
                pltpu.VMEM((HEADS_PER_STEP, LANES, HEAD_DIM), BF16),
                pltpu.VMEM((HEADS_PER_STEP, LANES, HEAD_DIM), BF16),
                pltpu.VMEM((HEADS_PER_STEP, ATT_TQ, 2 * HEAD_DIM), BF16),
            ],
        ),
        out_shape=jax.ShapeDtypeStruct((batch * seq, MOBA_HEADS * HEAD_DIM), BF16),
        compiler_params=pltpu.CompilerParams(
            dimension_semantics=("arbitrary", "arbitrary", "arbitrary"), vmem_limit_bytes=VMEM_LIMIT),
        name="moba",
    )(feats, proj, proj, proj)


def _diff_kernel(feat_ref, lq1_ref, lk1_ref, lq2_ref, lk2_ref, sw_ref,
                 q_ref, k_ref, v_ref, o_ref, kaug_ref, qaug_ref, *, seq):
    h = pl.program_id(1)
    i = pl.program_id(2)

    @pl.when(i == 0)
    def _():
        ext = _key_position_features(seq, 0)[0].astype(BF16)
        for c in range(2):
            kaug_ref[c, :, 0:HEAD_DIM] = k_ref[:, c * HEAD_DIM:(c + 1) * HEAD_DIM]
            kaug_ref[c, :, HEAD_DIM:2 * HEAD_DIM] = ext

    feat = jnp.concatenate([_slope_feature_rows(feat_ref, h, ATT_TQ),
                            jnp.zeros((LANES - 8, ATT_TQ), F32)], axis=0).T.astype(BF16)
    for c in range(2):
        qaug_ref[c, :, 0:HEAD_DIM] = q_ref[:, c * HEAD_DIM:(c + 1) * HEAD_DIM]
        qaug_ref[c, :, HEAD_DIM:2 * HEAD_DIM] = feat

    def tile_step(c, r, v, state, causal):
        m, l, acc = state
        s = _dot_nt(qaug_ref[c], kaug_ref[c, pl.ds(r, ATT_TK), :])
        if causal:
            s = _causal_tile(s)
        m, alpha, p = _online_step(s, m)
        return (m, alpha * l + jnp.sum(p, axis=-1, keepdims=True),
                alpha * acc + _dot(p.astype(BF16), v))

    def body(j, states):
        r = pl.multiple_of(j * ATT_TK, ATT_TK)
        v = v_ref[pl.ds(r, ATT_TK), :]
        return tuple(tile_step(c, r, v, states[c], False) for c in range(2))

    init = tuple((jnp.full((ATT_TQ, 1), M_INIT, F32), jnp.zeros((ATT_TQ, 1), F32),
                  jnp.zeros((ATT_TQ, DIFF_V_DIM), F32)) for _ in range(2))
    states = _unrolled_loop(i, body, init)
    r0 = pl.multiple_of(i * ATT_TQ, ATT_TQ)
    v0 = v_ref[pl.ds(r0, ATT_TK), :]
    (_, l1, acc1), (_, l2, acc2) = (tile_step(c, r0, v0, states[c], True) for c in range(2))

    lam = (jnp.exp(jnp.sum(lq1_ref[...] * lk1_ref[...], axis=-1, keepdims=True))
           - jnp.exp(jnp.sum(lq2_ref[...] * lk2_ref[...], axis=-1, keepdims=True))
           + LAMBDA_INIT)
    o = acc1 / l1 - lam * (acc2 / l2)
    y = o * lax.rsqrt(jnp.mean(o * o, axis=-1, keepdims=True) + RMS_EPS)
    o_ref[...] = ((y * sw_ref[...]) * (1.0 - LAMBDA_INIT)).astype(BF16)


def _diff(proj, feats, lq1, lk1, lq2, lk2, subln_w, batch, seq):
    nq = seq // ATT_TQ
    width = 2 * HEAD_DIM
    q_col = 3 * MOBA_HEADS * HEAD_DIM // width
    k_col = q_col + DIFF_HEADS
    v_col = k_col + DIFF_HEADS
    vec = lambda n: pl.BlockSpec((1, n), lambda b, h, i, s: (0, 0))
    return pl.pallas_call(
        functools.partial(_diff_kernel, seq=seq),
        grid_spec=pltpu.PrefetchScalarGridSpec(
            num_scalar_prefetch=1,
            grid=(batch, DIFF_HEADS, nq),
            in_specs=[
                vec(HEAD_DIM), vec(HEAD_DIM), vec(HEAD_DIM), vec(HEAD_DIM), vec(DIFF_V_DIM),
                pl.BlockSpec((ATT_TQ, width), lambda b, h, i, s: (b * nq + i, q_col + h)),
                pl.BlockSpec((seq, width), lambda b, h, i, s: (b, k_col + h)),
                pl.BlockSpec((seq, DIFF_V_DIM), lambda b, h, i, s: (b, v_col + h)),
            ],
            out_specs=pl.BlockSpec((ATT_TQ, DIFF_V_DIM), lambda b, h, i, s: (b * nq + i, h)),
            scratch_shapes=[
                pltpu.VMEM((2, seq, 2 * HEAD_DIM), BF16),
                pltpu.VMEM((2, ATT_TQ, 2 * HEAD_DIM), BF16),
            ],
        ),
        out_shape=jax.ShapeDtypeStruct((batch * seq, DIFF_HEADS * DIFF_V_DIM), BF16),
        compiler_params=pltpu.CompilerParams(
            dimension_semantics=("arbitrary", "arbitrary", "arbitrary"), vmem_limit_bytes=VMEM_LIMIT),
        name="diff_attn",
    )(feats, lq1, lk1, lq2, lk2, subln_w, proj, proj, proj)


def _route_tile(lg, run, tri):
    lane = lax.broadcasted_iota(jnp.int32, lg.shape, 1).astype(F32)
    rmax = lambda v: jnp.max(v, axis=-1, keepdims=True)
    first = lambda v, m: jnp.min(jnp.where(v == m, lane, float(LANES)), axis=-1, keepdims=True)

    gl = jnp.where(lane < N_GROUPS, lg, -jnp.inf)
    gmax = rmax(gl)
    group = first(gl, gmax)
    p_group = 1.0 / jnp.sum(jnp.exp(gl - gmax), axis=-1, keepdims=True)

    lo = N_GROUPS + EXPERTS_PER_GROUP * group
    el = jnp.where((lane >= lo) & (lane < lo + EXPERTS_PER_GROUP), lg, -jnp.inf)
    e1 = rmax(el)
    lane1 = first(el, e1)
    el = jnp.where(lane == lane1, -jnp.inf, el)
    e2 = rmax(el)
    lane2 = first(el, e2)
    t2 = jnp.exp(e2 - e1)
    w1 = p_group / (1.0 + t2)
    w2 = p_group * t2 / (1.0 + t2)
    x1 = lane1 - N_GROUPS
    x2 = lane2 - N_GROUPS

    hit1 = lane == x1
    hit2 = lane == x2
    onehot = jnp.where(hit1 | hit2, 1.0, 0.0)
    before = _dot(tri, onehot.astype(BF16)) + run
    rank1 = jnp.sum(jnp.where(hit1, before, 0.0), axis=-1, keepdims=True)
    rank2 = jnp.sum(jnp.where(hit2, before, 0.0), axis=-1, keepdims=True)

    meta = jnp.zeros(lg.shape, F32)
    for k, v in ((META_X1, x1), (META_X2, x2), (META_W1, w1), (META_W2, w2),
                 (META_RANK1, rank1), (META_RANK2, rank2)):
        meta = jnp.where(lane == k, v, meta)
    return meta, run + jnp.sum(onehot, axis=0, keepdims=True)


def _outproj_kernel(x_ref, oa_ref, ob_ref, wa_ref, wb_ref, nw_ref, wrh_ref, wrl_ref, br_ref,
                    x2_ref, hp_ref, meta_ref, cnt_ref, run_ref, tri_ref):
    @pl.when(pl.program_id(0) == 0)
    def _():
        row = lax.broadcasted_iota(jnp.int32, tri_ref.shape, 0)
        col = lax.broadcasted_iota(jnp.int32, tri_ref.shape, 1)
        tri_ref[...] = jnp.where(col < row, 1.0, 0.0).astype(BF16)
        run_ref[...] = jnp.zeros(run_ref.shape, F32)

    x2 = x_ref[...] + _dot(oa_ref[...], wa_ref[...]) + _dot(ob_ref[...], wb_ref[...])
    x2_ref[...] = x2
    r = lax.rsqrt(jnp.mean(x2 * x2, axis=-1, keepdims=True) + RMS_EPS)
    h = x2 * r * nw_ref[...]
    hh = h.astype(BF16)
    hl = (h - hh.astype(F32)).astype(BF16)
    lg = (_dot(hh, wrh_ref[...]) + _dot(hl, wrh_ref[...]) + _dot(hh, wrl_ref[...]) + br_ref[...])
    meta, run = _route_tile(lg, run_ref[...], tri_ref[...])
    meta_ref[...] = meta
    run_ref[...] = run
    cnt_ref[...] = run
    bits = pltpu.bitcast(hh.astype(F32), jnp.uint32)
    for c in range(ROW_WORD_CHUNKS):
        lo = bits[:, 2 * c * LANES:(2 * c + 1) * LANES]
        hi = bits[:, (2 * c + 1) * LANES:(2 * c + 2) * LANES]
        hp_ref[pl.ds(c, OUT_TM, stride=ROW_WORD_CHUNKS), :] = (lo >> 16) | (hi & jnp.uint32(0xFFFF0000))


def _out_proj(x2d, out_a, out_b, wa, wb, norm_w, wr_hi, wr_lo, b_r):
    t = x2d.shape[0]
    half = D_MODEL // 2
    row = lambda n: pl.BlockSpec((OUT_TM, n), lambda i: (i, 0))
    full = lambda r, c: pl.BlockSpec((r, c), lambda i: (0, 0))
    return pl.pallas_call(
        _outproj_kernel,
        grid=(t // OUT_TM,),
        in_specs=[row(D_MODEL), row(half), row(half), full(half, D_MODEL), full(half, D_MODEL),
                  full(1, D_MODEL), full(D_MODEL, LANES), full(D_MODEL, LANES), full(1, LANES)],
        out_specs=[row(D_MODEL), pl.BlockSpec((OUT_TM * ROW_WORD_CHUNKS, LANES), lambda i: (i, 0)), row(LANES),
                   full(1, LANES)],
        out_shape=[jax.ShapeDtypeStruct((t, D_MODEL), F32),
                   jax.ShapeDtypeStruct((t * ROW_WORD_CHUNKS, LANES), jnp.uint32),
                   jax.ShapeDtypeStruct((t, LANES), F32),
                   jax.ShapeDtypeStruct((1, LANES), F32)],
        scratch_shapes=[pltpu.VMEM((1, LANES), F32), pltpu.VMEM((OUT_TM, OUT_TM), BF16)],
        compiler_params=pltpu.CompilerParams(
            dimension_semantics=("arbitrary",), vmem_limit_bytes=VMEM_LIMIT),
        name="out_proj",
    )(x2d, out_a, out_b, wa, wb, norm_w, wr_hi, wr_lo, b_r)


GATHER_UNROLL = 8


def _dispatch_kernel(pos_ref, nt_ref, hp_ref, meta_ref, xs_ref, rm_ref, src_ref):
    t = pl.program_id(0)

    @pl.when(t == 0)
    def _():
        def clear(p, _):
            src_ref[p] = 0
            return 0
        lax.fori_loop(0, src_ref.shape[0], clear, 0, unroll=GATHER_UNROLL)

        def place(a, _):
            src_ref[pos_ref[a]] = a // EXPERT_TOPK
            return 0
        lax.fori_loop(0, pos_ref.shape[0], place, 0, unroll=GATHER_UNROLL)

    @pl.when(t < nt_ref[0])
    def _():
        def one(r, _):
            tok = src_ref[t * MOE_TM + r]
            xs_ref[r] = hp_ref[tok]
            rm_ref[pl.ds(r, 1), :] = meta_ref[pl.ds(tok, 1), :]
            return 0
        lax.fori_loop(0, MOE_TM, one, 0, unroll=GATHER_UNROLL)

    @pl.when(t >= nt_ref[0])
    def _():
        xs_ref[...] = jnp.zeros(xs_ref.shape, xs_ref.dtype)
        rm_ref[...] = jnp.zeros(rm_ref.shape, rm_ref.dtype)


def _dispatch(pos, n_tiles, hp, meta, npad):
    n_tokens = hp.shape[0] // ROW_WORD_CHUNKS
    hp3 = hp.reshape(n_tokens, ROW_WORD_CHUNKS, LANES)
    xs, rmeta = pl.pallas_call(
        _dispatch_kernel,
        grid_spec=pltpu.PrefetchScalarGridSpec(
            num_scalar_prefetch=2,
            grid=(npad // MOE_TM,),
            in_specs=[pl.BlockSpec(memory_space=pltpu.VMEM), pl.BlockSpec(memory_space=pltpu.VMEM)],
            out_specs=[pl.BlockSpec((MOE_TM, ROW_WORD_CHUNKS, LANES), lambda t, s, n: (t, 0, 0)),
                       pl.BlockSpec((MOE_TM, LANES), lambda t, s, n: (t, 0))],
            scratch_shapes=[pltpu.SMEM((npad,), jnp.int32)],
        ),
        out_shape=[jax.ShapeDtypeStruct((npad, ROW_WORD_CHUNKS, LANES), hp.dtype),
                   jax.ShapeDtypeStruct((npad, LANES), F32)],
        compiler_params=pltpu.CompilerParams(
            dimension_semantics=("arbitrary",), vmem_limit_bytes=VMEM_LIMIT),
        name="dispatch",
    )(pos, n_tiles, hp3, meta)
    return xs.reshape(npad * ROW_WORD_CHUNKS, LANES), rmeta


def _experts_kernel(te_ref, nt_ref, xs_ref, rm_ref, wg_ref, wu_ref, wd_ref, y_ref,
                    wgb_ref, wub_ref, wdb_ref, x_ref):
    t = pl.program_id(0)

    @pl.when(t < nt_ref[0])
    def _():
        e = te_ref[t]
        prev = te_ref[jnp.maximum(t - 1, 0)]

        @pl.when((t == 0) | (e != prev))
        def _():
            wgb_ref[...] = wg_ref[0].astype(BF16)
            wub_ref[...] = wu_ref[0].astype(BF16)
            wdb_ref[...] = wd_ref[0].astype(BF16)

        for c in range(ROW_WORD_CHUNKS):
            u32 = xs_ref[pl.ds(c, MOE_TM, stride=ROW_WORD_CHUNKS), :]
            x_ref[:, 2 * c * LANES:(2 * c + 1) * LANES] = pltpu.bitcast(u32 << 16, F32).astype(BF16)
            x_ref[:, (2 * c + 1) * LANES:(2 * c + 2) * LANES] = (
                pltpu.bitcast(u32 & jnp.uint32(0xFFFF0000), F32).astype(BF16))
        x = x_ref[...]
        a = _dot(x, wgb_ref[...])
        u = _dot(x, wub_ref[...])
        rm = rm_ref[...]
        gate = jnp.where(rm[:, META_X1:META_X1 + 1] == e.astype(F32),
                         rm[:, META_W1:META_W1 + 1], rm[:, META_W2:META_W2 + 1])
        act = (a * jax.nn.sigmoid(a)) * u * gate
        y_ref[...] = _dot(act.astype(BF16), wdb_ref[...])

    @pl.when(t >= nt_ref[0])
    def _():
        y_ref[...] = jnp.zeros(y_ref.shape, y_ref.dtype)


def _experts(tile_expert, n_tiles, xs, rmeta, w_gate, w_up, w_down):
    npad = xs.shape[0] // ROW_WORD_CHUNKS
    last = lambda t, nt: jnp.minimum(t, jnp.maximum(nt[0] - 1, 0))
    return pl.pallas_call(
        _experts_kernel,
        grid_spec=pltpu.PrefetchScalarGridSpec(
            num_scalar_prefetch=2,
            grid=(npad // MOE_TM,),
            in_specs=[
                pl.BlockSpec((MOE_TM * ROW_WORD_CHUNKS, LANES), lambda t, te, nt: (last(t, nt), 0)),
                pl.BlockSpec((MOE_TM, LANES), lambda t, te, nt: (last(t, nt), 0)),
                pl.BlockSpec((1, D_MODEL, D_EXPERT), lambda t, te, nt: (te[last(t, nt)], 0, 0)),
                pl.BlockSpec((1, D_MODEL, D_EXPERT), lambda t, te, nt: (te[last(t, nt)], 0, 0)),
                pl.BlockSpec((1, D_EXPERT, D_MODEL), lambda t, te, nt: (te[last(t, nt)], 0, 0)),
            ],
            out_specs=pl.BlockSpec((MOE_TM, D_MODEL), lambda t, te, nt: (t, 0)),
            scratch_shapes=[
                pltpu.VMEM((D_MODEL, D_EXPERT), BF16),
                pltpu.VMEM((D_MODEL, D_EXPERT), BF16),
                pltpu.VMEM((D_EXPERT, D_MODEL), BF16),
                pltpu.VMEM((MOE_TM, D_MODEL), BF16),
            ],
        ),
        out_shape=jax.ShapeDtypeStruct((npad, D_MODEL), F32),
        compiler_params=pltpu.CompilerParams(
            dimension_semantics=("arbitrary",), vmem_limit_bytes=VMEM_LIMIT),
        name="experts",
    )(tile_expert, n_tiles, xs, rmeta, w_gate, w_up, w_down)


def _combine_kernel(pos_ref, x2_ref, nw_ref, y_ref, o_ref, buf_ref, sem):
    i = pl.program_id(0)

    def row_copy(p, k, r):
        return pltpu.make_async_copy(y_ref.at[pl.ds(p, 1), :], buf_ref.at[k, pl.ds(r, 1), :], sem)

    def issue(r, _):
        a = (i * CMB_TT + r) * EXPERT_TOPK
        for k in range(EXPERT_TOPK):
            row_copy(pos_ref[a + k], k, r).start()
        return 0

    lax.fori_loop(0, CMB_TT, issue, 0)

    def drain(r, _):
        for k in range(EXPERT_TOPK):
            row_copy(0, k, 0).wait()
        return 0

    lax.fori_loop(0, CMB_TT, drain, 0)

    x3 = x2_ref[...] + (buf_ref[0] + buf_ref[1])
    r = lax.rsqrt(jnp.mean(x3 * x3, axis=-1, keepdims=True) + RMS_EPS)
    o_ref[...] = x3 * r * nw_ref[...]


def _combine(pos, x2, norm_w, y):
    t = x2.shape[0]
    return pl.pallas_call(
        _combine_kernel,
        grid_spec=pltpu.PrefetchScalarGridSpec(
            num_scalar_prefetch=1,
            grid=(t // CMB_TT,),
            in_specs=[
                pl.BlockSpec((CMB_TT, D_MODEL), lambda i, p: (i, 0)),
                pl.BlockSpec((1, D_MODEL), lambda i, p: (0, 0)),
                pl.BlockSpec(memory_space=pl.ANY),
            ],
            out_specs=pl.BlockSpec((CMB_TT, D_MODEL), lambda i, p: (i, 0)),
            scratch_shapes=[
                pltpu.VMEM((EXPERT_TOPK, CMB_TT, D_MODEL), F32),
                pltpu.SemaphoreType.DMA(()),
            ],
        ),
        out_shape=jax.ShapeDtypeStruct((t, D_MODEL), F32),
        compiler_params=pltpu.CompilerParams(
            dimension_semantics=("arbitrary",), vmem_limit_bytes=VMEM_LIMIT),
        name="combine",
    )(pos, x2, norm_w, y)


def _layout(meta, counts, n_tokens):
    npad = n_tokens * EXPERT_TOPK + N_EXPERTS * MOE_TM
    counts = counts[0, :N_EXPERTS].astype(jnp.int32)
    padded = ((counts + MOE_TM - 1) // MOE_TM) * MOE_TM
    ends = jnp.cumsum(padded)
    starts = ends - padded
    expert = meta[:, META_X1:META_X2 + 1].astype(jnp.int32)
    rank = meta[:, META_RANK1:META_RANK2 + 1].astype(jnp.int32)
    hit = expert[:, :, None] == jnp.arange(N_EXPERTS, dtype=jnp.int32)
    pos = jnp.sum(jnp.where(hit, starts, 0), axis=-1) + rank
    tile_start = jnp.arange(npad // MOE_TM, dtype=jnp.int32) * MOE_TM
    tile_expert = jnp.minimum(jnp.sum(ends[None, :] <= tile_start[:, None], axis=1), N_EXPERTS - 1)
    return (pos.reshape(-1).astype(jnp.int32), tile_expert.astype(jnp.int32),
            (ends[-1] // MOE_TM).astype(jnp.int32).reshape(1), npad)


def kernel(x, norm_mix_w, w_in, lambda_q1, lambda_k1, lambda_q2, lambda_k2, diff_subln_w, w_out, norm_ffn_w,
           w_router_group, b_router_group, w_router_expert, b_router_expert, w_gate, w_up, w_down, norm_final_w):
    batch, seq, _ = x.shape
    assert seq % ATT_TQ == 0 and seq // MOBA_BLOCK + 8 <= LANES and ATT_TK == ATT_TQ
    assert w_in.shape[0] == 1, "single-layer block"
    n_tokens = batch * seq
    x2d = x.reshape(n_tokens, D_MODEL)

    col = np.arange(IN_WIDTH)
    is_q = (col < MOBA_HEADS * HEAD_DIM) | ((col >= 3 * MOBA_HEADS * HEAD_DIM) & (col < 4 * MOBA_HEADS * HEAD_DIM))
    col_scale = jnp.asarray(np.where(is_q, QK_SCALE * LOG2E, 1.0).astype(np.float32))[None, :]

    proj = _in_proj(x2d, norm_mix_w[0][None, :], w_in[0].astype(BF16), col_scale)
    out_a = _moba(proj, jnp.asarray(_alibi_slope_pieces(MOBA_HEADS)), batch, seq)
    out_b = _diff(proj, jnp.asarray(_alibi_slope_pieces(DIFF_HEADS)), lambda_q1, lambda_k1, lambda_q2, lambda_k2,
                  diff_subln_w, batch, seq)

    half = D_MODEL // 2
    w_o = w_out[0].astype(BF16)
    w_r = jnp.concatenate([w_router_group[0], w_router_expert[0]], axis=1)
    w_r = jnp.pad(w_r, ((0, 0), (0, LANES - w_r.shape[1])))
    wr_hi = w_r.astype(BF16)
    wr_lo = (w_r - wr_hi.astype(F32)).astype(BF16)
    b_r = jnp.pad(jnp.concatenate([b_router_group[0], b_router_expert[0]]), (0, LANES - N_GROUPS - N_EXPERTS))[None, :]
    x2, hp, meta, counts = _out_proj(x2d, out_a, out_b, w_o[:half], w_o[half:], norm_ffn_w[0][None, :],
                                     wr_hi, wr_lo, b_r)

    pos, tile_expert, n_tiles, npad = _layout(meta, counts, n_tokens)
    xs, rmeta = _dispatch(pos, n_tiles, hp, meta, npad)
    y = _experts(tile_expert, n_tiles, xs, rmeta, w_gate[0], w_up[0], w_down[0])
    out = _combine(pos, x2, norm_final_w[None, :], y)
    return out.reshape(batch, seq, D_MODEL)
```

```python
import functools

import ml_dtypes
import numpy as np
import jax
import jax.numpy as jnp
from jax import lax
from jax.experimental import pallas as pl
from jax.experimental.pallas import tpu as pltpu

F32 = jnp.float32
BF16 = jnp.bfloat16

D_MODEL = 2048
HEAD_DIM = 128
MOBA_HEADS = 8
MOBA_BLOCK = 256
MOBA_TOPK = 3
DIFF_HEADS = 4
DIFF_V_DIM = 256
IN_WIDTH = 6144
N_GROUPS = 4
EXPERTS_PER_GROUP = 8
N_EXPERTS = 32
EXPERT_TOPK = 2
D_EXPERT = 512
RMS_EPS = 1e-6
ALIBI_MAX_BIAS = 8.0
LAMBDA_INIT = 0.8 - 0.6 * float(np.exp(-0.3 * 0))

LANES = 128
QK_SCALE = HEAD_DIM ** -0.5
LOG2E = float(np.log2(np.e))
NEG_BIG = -1e30
M_INIT = -1e38
POS_SPLIT = 64
SLOPE_PIECES = 3
ROW_WORD_CHUNKS = D_MODEL // (2 * LANES)
META_X1, META_X2, META_W1, META_W2, META_RANK1, META_RANK2 = range(6)

IN_TM, IN_TN = 1024, 1024
ATT_TQ = 512
ATT_TK = 512
HEADS_PER_STEP = 2
KV_UNROLL = 2
OUT_TM = 512
MOE_TM = 256
CMB_TT = 256
VMEM_LIMIT = 56 * 1024 * 1024


def _alibi_slope_pieces(n):
    rem = np.exp2(-ALIBI_MAX_BIAS * (np.arange(n, dtype=np.float64) + 1.0) / n) * LOG2E
    pieces = []
    for _ in range(SLOPE_PIECES):
        p = rem.astype(ml_dtypes.bfloat16).astype(np.float64)
        pieces.append(p)
        rem = rem - p
    return np.stack(pieces, axis=1).reshape(-1).astype(np.float32)


def _dot_nt(a, b):
    return lax.dot_general(a, b, (((1,), (1,)), ((), ())), preferred_element_type=F32)


def _dot(a, b):
    return jnp.dot(a, b, preferred_element_type=F32)


def _inproj_kernel(x_ref, nw_ref, cs_ref, w_ref, o_ref, h_ref):
    @pl.when(pl.program_id(1) == 0)
    def _():
        x = x_ref[...]
        r = lax.rsqrt(jnp.mean(x * x, axis=-1, keepdims=True) + RMS_EPS)
        h_ref[...] = (x * r * nw_ref[...]).astype(BF16)

    acc = _dot(h_ref[...], w_ref[...])
    o_ref[...] = (acc * cs_ref[...]).astype(BF16)


def _in_proj(x2d, norm_w, w_in_bf16, col_scale):
    t = x2d.shape[0]
    return pl.pallas_call(
        _inproj_kernel,
        grid=(t // IN_TM, IN_WIDTH // IN_TN),
        in_specs=[
            pl.BlockSpec((IN_TM, D_MODEL), lambda i, j: (i, 0)),
            pl.BlockSpec((1, D_MODEL), lambda i, j: (0, 0)),
            pl.BlockSpec((1, IN_TN), lambda i, j: (0, j)),
            pl.BlockSpec((D_MODEL, IN_TN), lambda i, j: (0, j)),
        ],
        out_specs=pl.BlockSpec((IN_TM, IN_TN), lambda i, j: (i, j)),
        out_shape=jax.ShapeDtypeStruct((t, IN_WIDTH), BF16),
        scratch_shapes=[pltpu.VMEM((IN_TM, D_MODEL), BF16)],
        compiler_params=pltpu.CompilerParams(
            dimension_semantics=("arbitrary", "arbitrary"), vmem_limit_bytes=VMEM_LIMIT),
        name="in_proj",
    )(x2d, norm_w, col_scale, w_in_bf16)


def _key_position_features(seq, first_lane):
    pos = lax.broadcasted_iota(jnp.int32, (seq, LANES), 0)
    lane = lax.broadcasted_iota(jnp.int32, (seq, LANES), 1)
    hi = (lane >= first_lane) & (lane < first_lane + SLOPE_PIECES)
    lo = (lane >= first_lane + SLOPE_PIECES) & (lane < first_lane + 2 * SLOPE_PIECES)
    ext = jnp.where(hi, (pos // POS_SPLIT).astype(F32), 0.0)
    return jnp.where(lo, (pos % POS_SPLIT).astype(F32), ext), pos, lane


def _slope_feature_rows(feat_ref, head, width):
    row = lax.broadcasted_iota(jnp.int32, (8, width), 0)
    out = jnp.zeros((8, width), F32)
    for k in range(SLOPE_PIECES):
        c = feat_ref[head * SLOPE_PIECES + k]
        out = jnp.where(row == k, POS_SPLIT * c, out)
        out = jnp.where(row == SLOPE_PIECES + k, c, out)
    return out


def _causal_tile(s):
    row = lax.broadcasted_iota(jnp.int32, s.shape, 0)
    col = lax.broadcasted_iota(jnp.int32, s.shape, 1)
    return jnp.where(col <= row, s, -jnp.inf)


def _online_step(s, m):
    m_new = jnp.maximum(m, jnp.max(s, axis=-1, keepdims=True))
    return m_new, jnp.exp2(m - m_new), jnp.exp2(s - m_new)


def _unrolled_loop(n, body, init):
    def trip(t, carry):
        for u in range(KV_UNROLL):
            carry = body(t * KV_UNROLL + u, carry)
        return carry

    carry = lax.fori_loop(0, n // KV_UNROLL, trip, init)
    return lax.fori_loop((n // KV_UNROLL) * KV_UNROLL, n, body, carry)


def _moba_kernel(feat_ref, q_ref, k_ref, v_ref, o_ref, kaug_ref, vaug_ref, kmh_ref, kml_ref, qaug_ref, *, seq):
    nb = seq // MOBA_BLOCK
    hp = pl.program_id(1)
    i = pl.program_id(2)

    @pl.when(i == 0)
    def _():
        ext, pos, lane = _key_position_features(seq, nb)
        ext = jnp.where(lane == pos // MOBA_BLOCK, 1.0, ext).astype(BF16)
        ones_col = jnp.where(lane == 0, 1.0, 0.0).astype(BF16)
        for g in range(HEADS_PER_STEP):
            k = k_ref[:, g * HEAD_DIM:(g + 1) * HEAD_DIM]
            kaug_ref[g, :, 0:HEAD_DIM] = k
            kaug_ref[g, :, HEAD_DIM:2 * HEAD_DIM] = ext
            vaug_ref[g, :, 0:HEAD_DIM] = v_ref[:, g * HEAD_DIM:(g + 1) * HEAD_DIM]
            vaug_ref[g, :, HEAD_DIM:2 * HEAD_DIM] = ones_col
            km = jnp.mean(k.astype(F32).reshape(nb, MOBA_BLOCK, HEAD_DIM), axis=1)
            kmh = km.astype(BF16)
            kmh_ref[g] = jnp.zeros((LANES, HEAD_DIM), BF16)
            kml_ref[g] = jnp.zeros((LANES, HEAD_DIM), BF16)
            kmh_ref[g, 0:nb, :] = kmh
            kml_ref[g, 0:nb, :] = (km - kmh.astype(F32)).astype(BF16)

    blk = lax.broadcasted_iota(jnp.int32, (nb, ATT_TQ), 0)
    col = lax.broadcasted_iota(jnp.int32, (nb, ATT_TQ), 1)
    own = i * (ATT_TQ // MOBA_BLOCK) + col // MOBA_BLOCK
    blk_f = blk.astype(F32)
    for g in range(HEADS_PER_STEP):
        q = q_ref[:, g * HEAD_DIM:(g + 1) * HEAD_DIM]
        gate = (_dot_nt(kmh_ref[g], q) + _dot_nt(kml_ref[g], q))[0:nb, :]
        gate = jnp.where(blk < own, gate, -jnp.inf)
        sel = jnp.zeros((nb, ATT_TQ), F32)
        for _ in range(MOBA_TOPK):
            mx = jnp.max(gate, axis=0, keepdims=True)
            idx = jnp.min(jnp.where(gate == mx, blk_f, float(LANES)), axis=0, keepdims=True)
            pick = (blk_f == idx) & (mx > -jnp.inf)
            sel = jnp.where(pick, 1.0, sel)
            gate = jnp.where(pick, -jnp.inf, gate)
        bias = jnp.where((sel > 0.0) | (blk == own), 0.0, NEG_BIG)
        feat = jnp.concatenate(
            [bias, _slope_feature_rows(feat_ref, hp * HEADS_PER_STEP + g, ATT_TQ),
             jnp.zeros((LANES - nb - 8, ATT_TQ), F32)], axis=0)
        qaug_ref[g, :, 0:HEAD_DIM] = q
        qaug_ref[g, :, HEAD_DIM:2 * HEAD_DIM] = feat.T.astype(BF16)

    def tile_step(g, r, state, causal):
        m, acc = state
        s = _dot_nt(qaug_ref[g], kaug_ref[g, pl.ds(r, ATT_TK), :])
        if causal:
            s = _causal_tile(s)
        m, alpha, p = _online_step(s, m)
        return m, alpha * acc + _dot(p.astype(BF16), vaug_ref[g, pl.ds(r, ATT_TK), :])

    def body(j, states):
        r = pl.multiple_of(j * ATT_TK, ATT_TK)
        return tuple(tile_step(g, r, states[g], False) for g in range(HEADS_PER_STEP))

    init = tuple((jnp.full((ATT_TQ, 1), M_INIT, F32), jnp.zeros((ATT_TQ, 2 * HEAD_DIM), F32))
                 for _ in range(HEADS_PER_STEP))
    states = _unrolled_loop(i, body, init)
    r0 = pl.multiple_of(i * ATT_TQ, ATT_TQ)
    for g in range(HEADS_PER_STEP):
        _, acc = tile_step(g, r0, states[g], True)
        o_ref[:, g * HEAD_DIM:(g + 1) * HEAD_DIM] = (
            acc[:, 0:HEAD_DIM] / acc[:, HEAD_DIM:HEAD_DIM + 1]).astype(BF16)


def _moba(proj, feats, batch, seq):
    nq = seq // ATT_TQ
    width = HEADS_PER_STEP * HEAD_DIM
    q_col, k_col, v_col = 0, MOBA_HEADS // HEADS_PER_STEP, 2 * MOBA_HEADS // HEADS_PER_STEP
    return pl.pallas_call(
        functools.partial(_moba_kernel, seq=seq),
        grid_spec=pltpu.PrefetchScalarGridSpec(
            num_scalar_prefetch=1,
            grid=(batch, MOBA_HEADS // HEADS_PER_STEP, nq),
            in_specs=[
                pl.BlockSpec((ATT_TQ, width), lambda b, h, i, s: (b * nq + i, q_col + h)),
                pl.BlockSpec((seq, width), lambda b, h, i, s: (b, k_col + h)),
                pl.BlockSpec((seq, width), lambda b, h, i, s: (b, v_col + h)),
            ],
            out_specs=pl.BlockSpec((ATT_TQ, width), lambda b, h, i, s: (b * nq + i, h)),
            scratch_shapes=[
                pltpu.VMEM((HEADS_PER_STEP, seq, 2 * HEAD_DIM), BF16),
                pltpu.VMEM((HEADS_PER_STEP, seq, 2 * HEAD_DIM), BF16),
                pltpu.VMEM((HEADS_PER_STEP, LANES, HEAD_DIM), BF16),
                pltpu.VMEM((HEADS_PER_STEP, LANES, HEAD_DIM), BF16),
                pltpu.VMEM((HEADS_PER_STEP, ATT_TQ, 2 * HEAD_DIM), BF16),
            ],
        ),
        out_shape=jax.ShapeDtypeStruct((batch * seq, MOBA_HEADS * HEAD_DIM), BF16),
        compiler_params=pltpu.CompilerParams(
            dimension_semantics=("arbitrary", "arbitrary", "arbitrary"), vmem_limit_bytes=VMEM_LIMIT),
        name="moba",
    )(feats, proj, proj, proj)


def _diff_kernel(feat_ref, lq1_ref, lk1_ref, lq2_ref, lk2_ref, sw_ref,
                 q_ref, k_ref, v_ref, o_ref, kaug_ref, qaug_ref, *, seq):
    h = pl.program_id(1)
    i = pl.program_id(2)

    @pl.when(i == 0)
    def _():
        ext = _key_position_features(seq, 0)[0].astype(BF16)
        for c in range(2):
            kaug_ref[c, :, 0:HEAD_DIM] = k_ref[:, c * HEAD_DIM:(c + 1) * HEAD_DIM]
            kaug_ref[c, :, HEAD_DIM:2 * HEAD_DIM] = ext

    feat = jnp.concatenate([_slope_feature_rows(feat_ref, h, ATT_TQ),
                            jnp.zeros((LANES - 8, ATT_TQ), F32)], axis=0).T.astype(BF16)
    for c in range(2):
        qaug_ref[c, :, 0:HEAD_DIM] = q_ref[:, c * HEAD_DIM:(c + 1) * HEAD_DIM]
        qaug_ref[c, :, HEAD_DIM:2 * HEAD_DIM] = feat

    def tile_step(c, r, v, state, causal):
        m, l, acc = state
        s = _dot_nt(qaug_ref[c], kaug_ref[c, pl.ds(r, ATT_TK), :])
        if causal:
            s = _causal_tile(s)
        m, alpha, p = _online_step(s, m)
        return (m, alpha * l + jnp.sum(p, axis=-1, keepdims=True),
                alpha * acc + _dot(p.astype(BF16), v))

    def body(j, states):
        r = pl.multiple_of(j * ATT_TK, ATT_TK)
        v = v_ref[pl.ds(r, ATT_TK), :]
        return tuple(tile_step(c, r, v, states[c], False) for c in range(2))

    init = tuple((jnp.full((ATT_TQ, 1), M_INIT, F32), jnp.zeros((ATT_TQ, 1), F32),
                  jnp.zeros((ATT_TQ, DIFF_V_DIM), F32)) for _ in range(2))
    states = _unrolled_loop(i, body, init)
    r0 = pl.multiple_of(i * ATT_TQ, ATT_TQ)
    v0 = v_ref[pl.ds(r0, ATT_TK), :]
    (_, l1, acc1), (_, l2, acc2) = (tile_step(c, r0, v0, states[c], True) for c in range(2))

    lam = (jnp.exp(jnp.sum(lq1_ref[...] * lk1_ref[...], axis=-1, keepdims=True))
           - jnp.exp(jnp.sum(lq2_ref[...] * lk2_ref[...], axis=-1, keepdims=True))
           + LAMBDA_INIT)
    o = acc1 / l1 - lam * (acc2 / l2)
    y = o * lax.rsqrt(jnp.mean(o * o, axis=-1, keepdims=True) + RMS_EPS)
    o_ref[...] = ((y * sw_ref[...]) * (1.0 - LAMBDA_INIT)).astype(BF16)


def _diff(proj, feats, lq1, lk1, lq2, lk2, subln_w, batch, seq):
    nq = seq // ATT_TQ
    width = 2 * HEAD_DIM
    q_col = 3 * MOBA_HEADS * HEAD_DIM // width
    k_col = q_col + DIFF_HEADS
    v_col = k_col + DIFF_HEADS
    vec = lambda n: pl.BlockSpec((1, n), lambda b, h, i, s: (0, 0))
    return pl.pallas_call(
        functools.partial(_diff_kernel, seq=seq),
        grid_spec=pltpu.PrefetchScalarGridSpec(
            num_scalar_prefetch=1,
            grid=(batch, DIFF_HEADS, nq),
            in_specs=[
                vec(HEAD_DIM), vec(HEAD_DIM), vec(HEAD_DIM), vec(HEAD_DIM), vec(DIFF_V_DIM),
                pl.BlockSpec((ATT_TQ, width), lambda b, h, i, s: (b * nq + i, q_col + h)),
                pl.BlockSpec((seq, width), lambda b, h, i, s: (b, k_col + h)),
                pl.BlockSpec((seq, DIFF_V_DIM), lambda b, h, i, s: (b, v_col + h)),
            ],
            out_specs=pl.BlockSpec((ATT_TQ, DIFF_V_DIM), lambda b, h, i, s: (b * nq + i, h)),
            scratch_shapes=[
                pltpu.VMEM((2, seq, 2 * HEAD_DIM), BF16),
                pltpu.VMEM((2, ATT_TQ, 2 * HEAD_DIM), BF16),
            ],
        ),
        out_shape=jax.ShapeDtypeStruct((batch * seq, DIFF_HEADS * DIFF_V_DIM), BF16),
        compiler_params=pltpu.CompilerParams(
            dimension_semantics=("arbitrary", "arbitrary", "arbitrary"), vmem_limit_bytes=VMEM_LIMIT),
        name="diff_attn",
    )(feats, lq1, lk1, lq2, lk2, subln_w, proj, proj, proj)


def _route_tile(lg, run, tri):
    lane = lax.broadcasted_iota(jnp.int32, lg.shape, 1).astype(F32)
    rmax = lambda v: jnp.max(v, axis=-1, keepdims=True)
    first = lambda v, m: jnp.min(jnp.where(v == m, lane, float(LANES)), axis=-1, keepdims=True)

    gl = jnp.where(lane < N_GROUPS, lg, -jnp.inf)
    gmax = rmax(gl)
    group = first(gl, gmax)
    p_group = 1.0 / jnp.sum(jnp.exp(gl - gmax), axis=-1, keepdims=True)

    lo = N_GROUPS + EXPERTS_PER_GROUP * group
    el = jnp.where((lane >= lo) & (lane < lo + EXPERTS_PER_GROUP), lg, -jnp.inf)
    e1 = rmax(el)
    lane1 = first(el, e1)
    el = jnp.where(lane == lane1, -jnp.inf, el)
    e2 = rmax(el)
    lane2 = first(el, e2)
    t2 = jnp.exp(e2 - e1)
    w1 = p_group / (1.0 + t2)
    w2 = p_group * t2 / (1.0 + t2)
    x1 = lane1 - N_GROUPS
    x2 = lane2 - N_GROUPS

    hit1 = lane == x1
    hit2 = lane == x2
    onehot = jnp.where(hit1 | hit2, 1.0, 0.0)
    before = _dot(tri, onehot.astype(BF16)) + run
    rank1 = jnp.sum(jnp.where(hit1, before, 0.0), axis=-1, keepdims=True)
    rank2 = jnp.sum(jnp.where(hit2, before, 0.0), axis=-1, keepdims=True)

    meta = jnp.zeros(lg.shape, F32)
    for k, v in ((META_X1, x1), (META_X2, x2), (META_W1, w1), (META_W2, w2),
                 (META_RANK1, rank1), (META_RANK2, rank2)):
        meta = jnp.where(lane == k, v, meta)
    return meta, run + jnp.sum(onehot, axis=0, keepdims=True)


def _outproj_kernel(x_ref, oa_ref, ob_ref, wa_ref, wb_ref, nw_ref, wrh_ref, wrl_ref, br_ref,
                    x2_ref, hp_ref, meta_ref, cnt_ref, run_ref, tri_ref):
    @pl.when(pl.program_id(0) == 0)
    def _():
        row = lax.broadcasted_iota(jnp.int32, tri_ref.shape, 0)
        col = lax.broadcasted_iota(jnp.int32, tri_ref.shape, 1)
        tri_ref[...] = jnp.where(col < row, 1.0, 0.0).astype(BF16)
        run_ref[...] = jnp.zeros(run_ref.shape, F32)

    x2 = x_ref[...] + _dot(oa_ref[...], wa_ref[...]) + _dot(ob_ref[...], wb_ref[...])
    x2_ref[...] = x2
    r = lax.rsqrt(jnp.mean(x2 * x2, axis=-1, keepdims=True) + RMS_EPS)
    h = x2 * r * nw_ref[...]
    hh = h.astype(BF16)
    hl = (h - hh.astype(F32)).astype(BF16)
    lg = (_dot(hh, wrh_ref[...]) + _dot(hl, wrh_ref[...]) + _dot(hh, wrl_ref[...]) + br_ref[...])
    meta, run = _route_tile(lg, run_ref[...], tri_ref[...])
    meta_ref[...] = meta
    run_ref[...] = run
    cnt_ref[...] = run
    bits = pltpu.bitcast(hh.astype(F32), jnp.uint32)
    for c in range(ROW_WORD_CHUNKS):
        lo = bits[:, 2 * c * LANES:(2 * c + 1) * LANES]
        hi = bits[:, (2 * c + 1) * LANES:(2 * c + 2) * LANES]
        hp_ref[pl.ds(c, OUT_TM, stride=ROW_WORD_CHUNKS), :] = (lo >> 16) | (hi & jnp.uint32(0xFFFF0000))


def _out_proj(x2d, out_a, out_b, wa, wb, norm_w, wr_hi, wr_lo, b_r):
    t = x2d.shape[0]
    half = D_MODEL // 2
    row = lambda n: pl.BlockSpec((OUT_TM, n), lambda i: (i, 0))
    full = lambda r, c: pl.BlockSpec((r, c), lambda i: (0, 0))
    return pl.pallas_call(
        _outproj_kernel,
        grid=(t // OUT_TM,),
        in_specs=[row(D_MODEL), row(half), row(half), full(half, D_MODEL), full(half, D_MODEL),
                  full(1, D_MODEL), full(D_MODEL, LANES), full(D_MODEL, LANES), full(1, LANES)],
        out_specs=[row(D_MODEL), pl.BlockSpec((OUT_TM * ROW_WORD_CHUNKS, LANES), lambda i: (i, 0)), row(LANES),
                   full(1, LANES)],
        out_shape=[jax.ShapeDtypeStruct((t, D_MODEL), F32),
                   jax.ShapeDtypeStruct((t * ROW_WORD_CHUNKS, LANES), jnp.uint32),
                   jax.ShapeDtypeStruct((t, LANES), F32),
                   jax.ShapeDtypeStruct((1, LANES), F32)],
        scratch_shapes=[pltpu.VMEM((1, LANES), F32), pltpu.VMEM((OUT_TM, OUT_TM), BF16)],
        compiler_params=pltpu.CompilerParams(
            dimension_semantics=("arbitrary",), vmem_limit_bytes=VMEM_LIMIT),
        name="out_proj",
    )(x2d, out_a, out_b, wa, wb, norm_w, wr_hi, wr_lo, b_r)


GATHER_UNROLL = 16
CLEAR_UNROLL = 32


def _dispatch_kernel(pos_ref, nt_ref, hp_ref, xs_ref, src_ref):
    t = pl.program_id(0)

    @pl.when(t == 0)
    def _():
        def clear(p, _):
            src_ref[p] = 0
            return 0
        lax.fori_loop(0, src_ref.shape[0], clear, 0, unroll=CLEAR_UNROLL)

        def place(c, _):
            rows = [pos_ref[c * GATHER_UNROLL + u] for u in range(GATHER_UNROLL)]
            for u in range(GATHER_UNROLL):
                src_ref[rows[u]] = c * (GATHER_UNROLL // EXPERT_TOPK) + u // EXPERT_TOPK
            return 0
        lax.fori_loop(0, pos_ref.shape[0] // GATHER_UNROLL, place, 0)

    @pl.when(t < nt_ref[0])
    def _():
        def gather(c, _):
            toks = [src_ref[t * MOE_TM + c * GATHER_UNROLL + u] for u in range(GATHER_UNROLL)]
            for u in range(GATHER_UNROLL):
                xs_ref[c * GATHER_UNROLL + u] = hp_ref[toks[u]]
            return 0
        lax.fori_loop(0, MOE_TM // GATHER_UNROLL, gather, 0)

    @pl.when(t >= nt_ref[0])
    def _():
        xs_ref[...] = jnp.zeros(xs_ref.shape, xs_ref.dtype)


def _dispatch(pos, n_tiles, hp, npad):
    n_tokens = hp.shape[0] // ROW_WORD_CHUNKS
    hp3 = hp.reshape(n_tokens, ROW_WORD_CHUNKS, LANES)
    xs = pl.pallas_call(
        _dispatch_kernel,
        grid_spec=pltpu.PrefetchScalarGridSpec(
            num_scalar_prefetch=2,
            grid=(npad // MOE_TM,),
            in_specs=[pl.BlockSpec(memory_space=pltpu.VMEM)],
            out_specs=pl.BlockSpec((MOE_TM, ROW_WORD_CHUNKS, LANES), lambda t, s, n: (t, 0, 0)),
            scratch_shapes=[pltpu.SMEM((npad,), jnp.int32)],
        ),
        out_shape=jax.ShapeDtypeStruct((npad, ROW_WORD_CHUNKS, LANES), hp.dtype),
        compiler_params=pltpu.CompilerParams(
            dimension_semantics=("arbitrary",), vmem_limit_bytes=VMEM_LIMIT),
        name="dispatch",
    )(pos, n_tiles, hp3)
    return xs.reshape(npad * ROW_WORD_CHUNKS, LANES)


def _experts_kernel(te_ref, nt_ref, xs_ref, wg_ref, wu_ref, wd_ref, y_ref,
                    wgb_ref, wub_ref, wdb_ref, x_ref):
    t = pl.program_id(0)

    @pl.when(t < nt_ref[0])
    def _():
        e = te_ref[t]
        prev = te_ref[jnp.maximum(t - 1, 0)]

        @pl.when((t == 0) | (e != prev))
        def _():
            wgb_ref[...] = wg_ref[0].astype(BF16)
            wub_ref[...] = wu_ref[0].astype(BF16)
            wdb_ref[...] = wd_ref[0].astype(BF16)

        for c in range(ROW_WORD_CHUNKS):
            u32 = xs_ref[pl.ds(c, MOE_TM, stride=ROW_WORD_CHUNKS), :]
            x_ref[:, 2 * c * LANES:(2 * c + 1) * LANES] = pltpu.bitcast(u32 << 16, F32).astype(BF16)
            x_ref[:, (2 * c + 1) * LANES:(2 * c + 2) * LANES] = (
                pltpu.bitcast(u32 & jnp.uint32(0xFFFF0000), F32).astype(BF16))
        x = x_ref[...]
        a = _dot(x, wgb_ref[...])
        u = _dot(x, wub_ref[...])
        act = (a * jax.nn.sigmoid(a)) * u
        y_ref[...] = _dot(act.astype(BF16), wdb_ref[...])

    @pl.when(t >= nt_ref[0])
    def _():
        y_ref[...] = jnp.zeros(y_ref.shape, y_ref.dtype)


def _experts(tile_expert, n_tiles, xs, w_gate, w_up, w_down):
    npad = xs.shape[0] // ROW_WORD_CHUNKS
    last = lambda t, nt: jnp.minimum(t, jnp.maximum(nt[0] - 1, 0))
    return pl.pallas_call(
        _experts_kernel,
        grid_spec=pltpu.PrefetchScalarGridSpec(
            num_scalar_prefetch=2,
            grid=(npad // MOE_TM,),
            in_specs=[
                pl.BlockSpec((MOE_TM * ROW_WORD_CHUNKS, LANES), lambda t, te, nt: (last(t, nt), 0)),
                pl.BlockSpec((1, D_MODEL, D_EXPERT), lambda t, te, nt: (te[last(t, nt)], 0, 0)),
                pl.BlockSpec((1, D_MODEL, D_EXPERT), lambda t, te, nt: (te[last(t, nt)], 0, 0)),
                pl.BlockSpec((1, D_EXPERT, D_MODEL), lambda t, te, nt: (te[last(t, nt)], 0, 0)),
            ],
            out_specs=pl.BlockSpec((MOE_TM, D_MODEL), lambda t, te, nt: (t, 0)),
            scratch_shapes=[
                pltpu.VMEM((D_MODEL, D_EXPERT), BF16),
                pltpu.VMEM((D_MODEL, D_EXPERT), BF16),
                pltpu.VMEM((D_EXPERT, D_MODEL), BF16),
                pltpu.VMEM((MOE_TM, D_MODEL), BF16),
            ],
        ),
        out_shape=jax.ShapeDtypeStruct((npad, D_MODEL), F32),
        compiler_params=pltpu.CompilerParams(
            dimension_semantics=("arbitrary",), vmem_limit_bytes=VMEM_LIMIT),
        name="experts",
    )(tile_expert, n_tiles, xs, w_gate, w_up, w_down)


CMB_UNROLL = 8
CMB_SLOTS = 2


def _combine_kernel(pos_ref, x2_ref, meta_ref, nw_ref, y_ref, o_ref, buf_ref, sem):
    i = pl.program_id(0)

    def row_copy(p, slot, k, r):
        return pltpu.make_async_copy(y_ref.at[pl.ds(p, 1), :], buf_ref.at[slot, k, pl.ds(r, 1), :],
                                     sem.at[slot])

    def issue(tile, slot):
        def trip(c, _):
            a0 = (tile * CMB_TT + c * CMB_UNROLL) * EXPERT_TOPK
            rows = [pos_ref[a0 + u] for u in range(CMB_UNROLL * EXPERT_TOPK)]
            for u in range(CMB_UNROLL * EXPERT_TOPK):
                row_copy(rows[u], slot, u % EXPERT_TOPK, c * CMB_UNROLL + u // EXPERT_TOPK).start(priority=u % 2)
            return 0
        lax.fori_loop(0, CMB_TT // CMB_UNROLL, trip, 0)

    @pl.when(i == 0)
    def _():
        issue(0, 0)

    @pl.when(i + 1 < pl.num_programs(0))
    def _():
        issue(i + 1, (i + 1) % CMB_SLOTS)

    slot = i % CMB_SLOTS
    for k in range(EXPERT_TOPK):
        pltpu.make_async_copy(y_ref.at[pl.ds(0, CMB_TT), :], buf_ref.at[slot, k], sem.at[slot]).wait()

    meta = meta_ref[...]
    x3 = (x2_ref[...] + meta[:, META_W1:META_W1 + 1] * buf_ref[slot, 0]
          + meta[:, META_W2:META_W2 + 1] * buf_ref[slot, 1])
    r = lax.rsqrt(jnp.mean(x3 * x3, axis=-1, keepdims=True) + RMS_EPS)
    o_ref[...] = x3 * r * nw_ref[...]


def _combine(pos, x2, meta, norm_w, y):
    t = x2.shape[0]
    return pl.pallas_call(
        _combine_kernel,
        grid_spec=pltpu.PrefetchScalarGridSpec(
            num_scalar_prefetch=1,
            grid=(t // CMB_TT,),
            in_specs=[
                pl.BlockSpec((CMB_TT, D_MODEL), lambda i, p: (i, 0)),
                pl.BlockSpec((CMB_TT, LANES), lambda i, p: (i, 0)),
                pl.BlockSpec((1, D_MODEL), lambda i, p: (0, 0)),
                pl.BlockSpec(memory_space=pl.ANY),
            ],
            out_specs=pl.BlockSpec((CMB_TT, D_MODEL), lambda i, p: (i, 0)),
            scratch_shapes=[
                pltpu.VMEM((CMB_SLOTS, EXPERT_TOPK, CMB_TT, D_MODEL), F32),
                pltpu.SemaphoreType.DMA((CMB_SLOTS,)),
            ],
        ),
        out_shape=jax.ShapeDtypeStruct((t, D_MODEL), F32),
        compiler_params=pltpu.CompilerParams(
            dimension_semantics=("arbitrary",), vmem_limit_bytes=VMEM_LIMIT),
        name="combine",
    )(pos, x2, meta, norm_w, y)


def _layout(meta, counts, n_tokens):
    npad = n_tokens * EXPERT_TOPK + N_EXPERTS * MOE_TM
    counts = counts[0, :N_EXPERTS].astype(jnp.int32)
    padded = ((counts + MOE_TM - 1) // MOE_TM) * MOE_TM
    ends = jnp.cumsum(padded)
    starts = ends - padded
    expert = meta[:, META_X1:META_X2 + 1].astype(jnp.int32)
    rank = meta[:, META_RANK1:META_RANK2 + 1].astype(jnp.int32)
    hit = expert[:, :, None] == jnp.arange(N_EXPERTS, dtype=jnp.int32)
    pos = jnp.sum(jnp.where(hit, starts, 0), axis=-1) + rank
    tile_start = jnp.arange(npad // MOE_TM, dtype=jnp.int32) * MOE_TM
    tile_expert = jnp.minimum(jnp.sum(ends[None, :] <= tile_start[:, None], axis=1), N_EXPERTS - 1)
    return (pos.reshape(-1).astype(jnp.int32), tile_expert.astype(jnp.int32),
            (ends[-1] // MOE_TM).astype(jnp.int32).reshape(1), npad)


def kernel(x, norm_mix_w, w_in, lambda_q1, lambda_k1, lambda_q2, lambda_k2, diff_subln_w, w_out, norm_ffn_w,
           w_router_group, b_router_group, w_router_expert, b_router_expert, w_gate, w_up, w_down, norm_final_w):
    batch, seq, _ = x.shape
    assert seq % ATT_TQ == 0 and seq // MOBA_BLOCK + 8 <= LANES and ATT_TK == ATT_TQ
    assert w_in.shape[0] == 1, "single-layer block"
    n_tokens = batch * seq
    x2d = x.reshape(n_tokens, D_MODEL)

    col = np.arange(IN_WIDTH)
    is_q = (col < MOBA_HEADS * HEAD_DIM) | ((col >= 3 * MOBA_HEADS * HEAD_DIM) & (col < 4 * MOBA_HEADS * HEAD_DIM))
    col_scale = jnp.asarray(np.where(is_q, QK_SCALE * LOG2E, 1.0).astype(np.float32))[None, :]

    proj = _in_proj(x2d, norm_mix_w[0][None, :], w_in[0].astype(BF16), col_scale)
    out_a = _moba(proj, jnp.asarray(_alibi_slope_pieces(MOBA_HEADS)), batch, seq)
    out_b = _diff(proj, jnp.asarray(_alibi_slope_pieces(DIFF_HEADS)), lambda_q1, lambda_k1, lambda_q2, lambda_k2,
                  diff_subln_w, batch, seq)

    half = D_MODEL // 2
    w_o = w_out[0].astype(BF16)
    w_r = jnp.concatenate([w_router_group[0], w_router_expert[0]], axis=1)
    w_r = jnp.pad(w_r, ((0, 0), (0, LANES - w_r.shape[1])))
    wr_hi = w_r.astype(BF16)
    wr_lo = (w_r - wr_hi.astype(F32)).astype(BF16)
    b_r = jnp.pad(jnp.concatenate([b_router_group[0], b_router_expert[0]]), (0, LANES - N_GROUPS - N_EXPERTS))[None, :]
    x2, hp, meta, counts = _out_proj(x2d, out_a, out_b, w_o[:half], w_o[half:], norm_ffn_w[0][None, :],
                                     wr_hi, wr_lo, b_r)

    pos, tile_expert, n_tiles, npad = _layout(meta, counts, n_tokens)
    xs = _dispatch(pos, n_tiles, hp, npad)
    y = _experts(tile_expert, n_tiles, xs, w_gate[0], w_up[0], w_down[0])
    out = _combine(pos, x2, meta, norm_final_w[None, :], y)
    return out.reshape(batch, seq, D_MODEL)
```

```python
import functools

import ml_dtypes
import numpy as np
import jax
import jax.numpy as jnp
from jax import lax
from jax.experimental import pallas as pl
from jax.experimental.pallas import tpu as pltpu

F32 = jnp.float32
BF16 = jnp.bfloat16

D_MODEL = 2048
HEAD_DIM = 128
MOBA_HEADS = 8
MOBA_BLOCK = 256
MOBA_TOPK = 3
DIFF_HEADS = 4
DIFF_V_DIM = 256
IN_WIDTH = 6144
N_GROUPS = 4
EXPERTS_PER_GROUP = 8
N_EXPERTS = 32
EXPERT_TOPK = 2
D_EXPERT = 512
RMS_EPS = 1e-6
ALIBI_MAX_BIAS = 8.0
LAMBDA_INIT = 0.8 - 0.6 * float(np.exp(-0.3 * 0))

LANES = 128
QK_SCALE = HEAD_DIM ** -0.5
LOG2E = float(np.log2(np.e))
NEG_BIG = -1e30
M_INIT = -1e38
POS_SPLIT = 64
SLOPE_PIECES = 3
ROW_WORD_CHUNKS = D_MODEL // (2 * LANES)
META_X1, META_X2, META_W1, META_W2, META_RANK1, META_RANK2 = range(6)

IN_TM, IN_TN = 1024, 1024
ATT_TQ = 1024
ATT_TK = 512
KV_UNROLL = ATT_TQ // ATT_TK
HEADS_PER_STEP = 2
OUT_TM = 512
MOE_TM = 256
CMB_TT = 256
VMEM_LIMIT = 56 * 1024 * 1024


def _alibi_slope_pieces(n):
    rem = np.exp2(-ALIBI_MAX_BIAS * (np.arange(n, dtype=np.float64) + 1.0) / n) * LOG2E
    pieces = []
    for _ in range(SLOPE_PIECES):
        p = rem.astype(ml_dtypes.bfloat16).astype(np.float64)
        pieces.append(p)
        rem = rem - p
    return np.stack(pieces, axis=1).reshape(-1).astype(np.float32)


def _dot_nt(a, b):
    return lax.dot_general(a, b, (((1,), (1,)), ((), ())), preferred_element_type=F32)


def _dot(a, b):
    return jnp.dot(a, b, preferred_element_type=F32)


def _inproj_kernel(x_ref, nw_ref, cs_ref, w_ref, o_ref, h_ref):
    @pl.when(pl.program_id(1) == 0)
    def _():
        x = x_ref[...]
        r = lax.rsqrt(jnp.mean(x * x, axis=-1, keepdims=True) + RMS_EPS)
        h_ref[...] = (x * r * nw_ref[...]).astype(BF16)

    acc = _dot(h_ref[...], w_ref[...])
    o_ref[...] = (acc * cs_ref[...]).astype(BF16)


def _in_proj(x2d, norm_w, w_in_bf16, col_scale):
    t = x2d.shape[0]
    return pl.pallas_call(
        _inproj_kernel,
        grid=(t // IN_TM, IN_WIDTH // IN_TN),
        in_specs=[
            pl.BlockSpec((IN_TM, D_MODEL), lambda i, j: (i, 0)),
            pl.BlockSpec((1, D_MODEL), lambda i, j: (0, 0)),
            pl.BlockSpec((1, IN_TN), lambda i, j: (0, j)),
            pl.BlockSpec((D_MODEL, IN_TN), lambda i, j: (0, j)),
        ],
        out_specs=pl.BlockSpec((IN_TM, IN_TN), lambda i, j: (i, j)),
        out_shape=jax.ShapeDtypeStruct((t, IN_WIDTH), BF16),
        scratch_shapes=[pltpu.VMEM((IN_TM, D_MODEL), BF16)],
        compiler_params=pltpu.CompilerParams(
            dimension_semantics=("arbitrary", "arbitrary"), vmem_limit_bytes=VMEM_LIMIT),
        name="in_proj",
    )(x2d, norm_w, col_scale, w_in_bf16)


def _key_position_features(seq, first_lane):
    pos = lax.broadcasted_iota(jnp.int32, (seq, LANES), 0)
    lane = lax.broadcasted_iota(jnp.int32, (seq, LANES), 1)
    hi = (lane >= first_lane) & (lane < first_lane + SLOPE_PIECES)
    lo = (lane >= first_lane + SLOPE_PIECES) & (lane < first_lane + 2 * SLOPE_PIECES)
    ext = jnp.where(hi, (pos // POS_SPLIT).astype(F32), 0.0)
    return jnp.where(lo, (pos % POS_SPLIT).astype(F32), ext), pos, lane


def _slope_feature_rows(feat_ref, head, width):
    row = lax.broadcasted_iota(jnp.int32, (8, width), 0)
    out = jnp.zeros((8, width), F32)
    for k in range(SLOPE_PIECES):
        c = feat_ref[head * SLOPE_PIECES + k]
        out = jnp.where(row == k, POS_SPLIT * c, out)
        out = jnp.where(row == SLOPE_PIECES + k, c, out)
    return out


def _causal_tile(s):
    row = lax.broadcasted_iota(jnp.int32, s.shape, 0)
    col = lax.broadcasted_iota(jnp.int32, s.shape, 1)
    return jnp.where(col <= row, s, -jnp.inf)


def _online_step(s, m):
    m_new = jnp.maximum(m, jnp.max(s, axis=-1, keepdims=True))
    return m_new, jnp.exp2(m - m_new), jnp.exp2(s - m_new)


def _past_tiles_loop(i, body, init):
    def trip(t, carry):
        for u in range(KV_UNROLL):
            carry = body(t * KV_UNROLL + u, carry)
        return carry

    return lax.fori_loop(0, i, trip, init)


def _rows_from(state, start):
    return state if start == 0 else tuple(a[start:] for a in state)


def _rows_replace(state, start, tail):
    if start == 0:
        return tuple(tail)
    return tuple(jnp.concatenate([a[:start], b], axis=0) for a, b in zip(state, tail))


def _moba_kernel(feat_ref, q_ref, k_ref, v_ref, o_ref, kaug_ref, vaug_ref, kmh_ref, kml_ref, qaug_ref, *, seq):
    nb = seq // MOBA_BLOCK
    hp = pl.program_id(1)
    i = pl.program_id(2)

    @pl.when(i == 0)
    def _():
        ext, pos, lane = _key_position_features(seq, nb)
        ext = jnp.where(lane == pos // MOBA_BLOCK, 1.0, ext).astype(BF16)
        ones_col = jnp.where(lane == 0, 1.0, 0.0).astype(BF16)
        for g in range(HEADS_PER_STEP):
            k = k_ref[:, g * HEAD_DIM:(g + 1) * HEAD_DIM]
            kaug_ref[g, :, 0:HEAD_DIM] = k
            kaug_ref[g, :, HEAD_DIM:2 * HEAD_DIM] = ext
            vaug_ref[g, :, 0:HEAD_DIM] = v_ref[:, g * HEAD_DIM:(g + 1) * HEAD_DIM]
            vaug_ref[g, :, HEAD_DIM:2 * HEAD_DIM] = ones_col
            km = jnp.mean(k.astype(F32).reshape(nb, MOBA_BLOCK, HEAD_DIM), axis=1)
            kmh = km.astype(BF16)
            kmh_ref[g] = jnp.zeros((LANES, HEAD_DIM), BF16)
            kml_ref[g] = jnp.zeros((LANES, HEAD_DIM), BF16)
            kmh_ref[g, 0:nb, :] = kmh
            kml_ref[g, 0:nb, :] = (km - kmh.astype(F32)).astype(BF16)

    blk = lax.broadcasted_iota(jnp.int32, (nb, ATT_TQ), 0)
    col = lax.broadcasted_iota(jnp.int32, (nb, ATT_TQ), 1)
    own = i * (ATT_TQ // MOBA_BLOCK) + col // MOBA_BLOCK
    blk_f = blk.astype(F32)
    for g in range(HEADS_PER_STEP):
        q = q_ref[:, g * HEAD_DIM:(g + 1) * HEAD_DIM]
        gate = (_dot_nt(kmh_ref[g], q) + _dot_nt(kml_ref[g], q))[0:nb, :]
        gate = jnp.where(blk < own, gate, -jnp.inf)
        sel = jnp.zeros((nb, ATT_TQ), F32)
        for _ in range(MOBA_TOPK):
            mx = jnp.max(gate, axis=0, keepdims=True)
            idx = jnp.min(jnp.where(gate == mx, blk_f, float(LANES)), axis=0, keepdims=True)
            pick = (blk_f == idx) & (mx > -jnp.inf)
            sel = jnp.where(pick, 1.0, sel)
            gate = jnp.where(pick, -jnp.inf, gate)
        bias = jnp.where((sel > 0.0) | (blk == own), 0.0, NEG_BIG)
        feat = jnp.concatenate(
            [bias, _slope_feature_rows(feat_ref, hp * HEADS_PER_STEP + g, ATT_TQ),
             jnp.zeros((LANES - nb - 8, ATT_TQ), F32)], axis=0)
        qaug_ref[g, :, 0:HEAD_DIM] = q
        qaug_ref[g, :, HEAD_DIM:2 * HEAD_DIM] = feat.T.astype(BF16)

    def tile_step(g, r, state, row0=0, causal=False):
        m, acc = state
        s = _dot_nt(qaug_ref[g, row0:, :], kaug_ref[g, pl.ds(r, ATT_TK), :])
        if causal:
            s = _causal_tile(s)
        m, alpha, p = _online_step(s, m)
        return m, alpha * acc + _dot(p.astype(BF16), vaug_ref[g, pl.ds(r, ATT_TK), :])

    def body(j, states):
        r = pl.multiple_of(j * ATT_TK, ATT_TK)
        return tuple(tile_step(g, r, states[g]) for g in range(HEADS_PER_STEP))

    init = tuple((jnp.full((ATT_TQ, 1), M_INIT, F32), jnp.zeros((ATT_TQ, 2 * HEAD_DIM), F32))
                 for _ in range(HEADS_PER_STEP))
    states = list(_past_tiles_loop(i, body, init))
    for d in range(KV_UNROLL):
        r = pl.multiple_of(i * ATT_TQ + d * ATT_TK, ATT_TK)
        for g in range(HEADS_PER_STEP):
            tail = tile_step(g, r, _rows_from(states[g], d * ATT_TK), row0=d * ATT_TK, causal=True)
            states[g] = _rows_replace(states[g], d * ATT_TK, tail)
    for g in range(HEADS_PER_STEP):
        acc = states[g][1]
        o_ref[:, g * HEAD_DIM:(g + 1) * HEAD_DIM] = (
            acc[:, 0:HEAD_DIM] / acc[:, HEAD_DIM:HEAD_DIM + 1]).astype(BF16)


def _moba(proj, feats, batch, seq):
    nq = seq // ATT_TQ
    width = HEADS_PER_STEP * HEAD_DIM
    q_col, k_col, v_col = 0, MOBA_HEADS // HEADS_PER_STEP, 2 * MOBA_HEADS // HEADS_PER_STEP
    return pl.pallas_call(
        functools.partial(_moba_kernel, seq=seq),
        grid_spec=pltpu.PrefetchScalarGridSpec(
            num_scalar_prefetch=1,
            grid=(batch, MOBA_HEADS // HEADS_PER_STEP, nq),
            in_specs=[
                pl.BlockSpec((ATT_TQ, width), lambda b, h, i, s: (b * nq + i, q_col + h)),
                pl.BlockSpec((seq, width), lambda b, h, i, s: (b, k_col + h)),
                pl.BlockSpec((seq, width), lambda b, h, i, s: (b, v_col + h)),
            ],
            out_specs=pl.BlockSpec((ATT_TQ, width), lambda b, h, i, s: (b * nq + i, h)),
            scratch_shapes=[
                pltpu.VMEM((HEADS_PER_STEP, seq, 2 * HEAD_DIM), BF16),
                pltpu.VMEM((HEADS_PER_STEP, seq, 2 * HEAD_DIM), BF16),
                pltpu.VMEM((HEADS_PER_STEP, LANES, HEAD_DIM), BF16),
                pltpu.VMEM((HEADS_PER_STEP, LANES, HEAD_DIM), BF16),
                pltpu.VMEM((HEADS_PER_STEP, ATT_TQ, 2 * HEAD_DIM), BF16),
            ],
        ),
        out_shape=jax.ShapeDtypeStruct((batch * seq, MOBA_HEADS * HEAD_DIM), BF16),
        compiler_params=pltpu.CompilerParams(
            dimension_semantics=("arbitrary", "arbitrary", "arbitrary"), vmem_limit_bytes=VMEM_LIMIT),
        name="moba",
    )(feats, proj, proj, proj)


def _diff_kernel(feat_ref, lq1_ref, lk1_ref, lq2_ref, lk2_ref, sw_ref,
                 q_ref, k_ref, v_ref, o_ref, kaug_ref, qaug_ref, *, seq):
    h = pl.program_id(1)
    i = pl.program_id(2)

    @pl.when(i == 0)
    def _():
        ext = _key_position_features(seq, 0)[0].astype(BF16)
        for c in range(2):
            kaug_ref[c, :, 0:HEAD_DIM] = k_ref[:, c * HEAD_DIM:(c + 1) * HEAD_DIM]
            kaug_ref[c, :, HEAD_DIM:2 * HEAD_DIM] = ext

    feat = jnp.concatenate([_slope_feature_rows(feat_ref, h, ATT_TQ),
                            jnp.zeros((LANES - 8, ATT_TQ), F32)], axis=0).T.astype(BF16)
    for c in range(2):
        qaug_ref[c, :, 0:HEAD_DIM] = q_ref[:, c * HEAD_DIM:(c + 1) * HEAD_DIM]
        qaug_ref[c, :, HEAD_DIM:2 * HEAD_DIM] = feat

    def tile_step(c, r, v, state, row0=0, causal=False):
        m, l, acc = state
        s = _dot_nt(qaug_ref[c, row0:, :], kaug_ref[c, pl.ds(r, ATT_TK), :])
        if causal:
            s = _causal_tile(s)
        m, alpha, p = _online_step(s, m)
        return (m, alpha * l + jnp.sum(p, axis=-1, keepdims=True),
                alpha * acc + _dot(p.astype(BF16), v))

    def body(j, states):
        r = pl.multiple_of(j * ATT_TK, ATT_TK)
        v = v_ref[pl.ds(r, ATT_TK), :]
        return tuple(tile_step(c, r, v, states[c]) for c in range(2))

    init = tuple((jnp.full((ATT_TQ, 1), M_INIT, F32), jnp.zeros((ATT_TQ, 1), F32),
                  jnp.zeros((ATT_TQ, DIFF_V_DIM), F32)) for _ in range(2))
    states = list(_past_tiles_loop(i, body, init))
    for d in range(KV_UNROLL):
        r = pl.multiple_of(i * ATT_TQ + d * ATT_TK, ATT_TK)
        v = v_ref[pl.ds(r, ATT_TK), :]
        for c in range(2):
            tail = tile_step(c, r, v, _rows_from(states[c], d * ATT_TK), row0=d * ATT_TK, causal=True)
            states[c] = _rows_replace(states[c], d * ATT_TK, tail)
    (_, l1, acc1), (_, l2, acc2) = states

    lam = (jnp.exp(jnp.sum(lq1_ref[...] * lk1_ref[...], axis=-1, keepdims=True))
           - jnp.exp(jnp.sum(lq2_ref[...] * lk2_ref[...], axis=-1, keepdims=True))
           + LAMBDA_INIT)
    o = acc1 / l1 - lam * (acc2 / l2)
    y = o * lax.rsqrt(jnp.mean(o * o, axis=-1, keepdims=True) + RMS_EPS)
    o_ref[...] = ((y * sw_ref[...]) * (1.0 - LAMBDA_INIT)).astype(BF16)


def _diff(proj, feats, lq1, lk1, lq2, lk2, subln_w, batch, seq):
    nq = seq // ATT_TQ
    width = 2 * HEAD_DIM
    q_col = 3 * MOBA_HEADS * HEAD_DIM // width
    k_col = q_col + DIFF_HEADS
    v_col = k_col + DIFF_HEADS
    vec = lambda n: pl.BlockSpec((1, n), lambda b, h, i, s: (0, 0))
    return pl.pallas_call(
        functools.partial(_diff_kernel, seq=seq),
        grid_spec=pltpu.PrefetchScalarGridSpec(
            num_scalar_prefetch=1,
            grid=(batch, DIFF_HEADS, nq),
            in_specs=[
                vec(HEAD_DIM), vec(HEAD_DIM), vec(HEAD_DIM), vec(HEAD_DIM), vec(DIFF_V_DIM),
                pl.BlockSpec((ATT_TQ, width), lambda b, h, i, s: (b * nq + i, q_col + h)),
                pl.BlockSpec((seq, width), lambda b, h, i, s: (b, k_col + h)),
                pl.BlockSpec((seq, DIFF_V_DIM), lambda b, h, i, s: (b, v_col + h)),
            ],
            out_specs=pl.BlockSpec((ATT_TQ, DIFF_V_DIM), lambda b, h, i, s: (b * nq + i, h)),
            scratch_shapes=[
                pltpu.VMEM((2, seq, 2 * HEAD_DIM), BF16),
                pltpu.VMEM((2, ATT_TQ, 2 * HEAD_DIM), BF16),
            ],
        ),
        out_shape=jax.ShapeDtypeStruct((batch * seq, DIFF_HEADS * DIFF_V_DIM), BF16),
        compiler_params=pltpu.CompilerParams(
            dimension_semantics=("arbitrary", "arbitrary", "arbitrary"), vmem_limit_bytes=VMEM_LIMIT),
        name="diff_attn",
    )(feats, lq1, lk1, lq2, lk2, subln_w, proj, proj, proj)


def _route_tile(lg, run, tri):
    lane = lax.broadcasted_iota(jnp.int32, lg.shape, 1).astype(F32)
    rmax = lambda v: jnp.max(v, axis=-1, keepdims=True)
    first = lambda v, m: jnp.min(jnp.where(v == m, lane, float(LANES)), axis=-1, keepdims=True)

    gl = jnp.where(lane < N_GROUPS, lg, -jnp.inf)
    gmax = rmax(gl)
    group = first(gl, gmax)
    p_group = 1.0 / jnp.sum(jnp.exp(gl - gmax), axis=-1, keepdims=True)

    lo = N_GROUPS + EXPERTS_PER_GROUP * group
    el = jnp.where((lane >= lo) & (lane < lo + EXPERTS_PER_GROUP), lg, -jnp.inf)
    e1 = rmax(el)
    lane1 = first(el, e1)
    el = jnp.where(lane == lane1, -jnp.inf, el)
    e2 = rmax(el)
    lane2 = first(el, e2)
    t2 = jnp.exp(e2 - e1)
    w1 = p_group / (1.0 + t2)
    w2 = p_group * t2 / (1.0 + t2)
    x1 = lane1 - N_GROUPS
    x2 = lane2 - N_GROUPS

    hit1 = lane == x1
    hit2 = lane == x2
    onehot = jnp.where(hit1 | hit2, 1.0, 0.0)
    before = _dot(tri, onehot.astype(BF16)) + run
    rank1 = jnp.sum(jnp.where(hit1, before, 0.0), axis=-1, keepdims=True)
    rank2 = jnp.sum(jnp.where(hit2, before, 0.0), axis=-1, keepdims=True)

    meta = jnp.zeros(lg.shape, F32)
    for k, v in ((META_X1, x1), (META_X2, x2), (META_W1, w1), (META_W2, w2),
                 (META_RANK1, rank1), (META_RANK2, rank2)):
        meta = jnp.where(lane == k, v, meta)
    return meta, run + jnp.sum(onehot, axis=0, keepdims=True)


def _outproj_kernel(x_ref, oa_ref, ob_ref, wa_ref, wb_ref, nw_ref, wrh_ref, wrl_ref, br_ref,
                    x2_ref, hp_ref, meta_ref, cnt_ref, run_ref, tri_ref):
    @pl.when(pl.program_id(0) == 0)
    def _():
        row = lax.broadcasted_iota(jnp.int32, tri_ref.shape, 0)
        col = lax.broadcasted_iota(jnp.int32, tri_ref.shape, 1)
        tri_ref[...] = jnp.where(col < row, 1.0, 0.0).astype(BF16)
        run_ref[...] = jnp.zeros(run_ref.shape, F32)

    x2 = x_ref[...] + _dot(oa_ref[...], wa_ref[...]) + _dot(ob_ref[...], wb_ref[...])
    x2_ref[...] = x2
    r = lax.rsqrt(jnp.mean(x2 * x2, axis=-1, keepdims=True) + RMS_EPS)
    h = x2 * r * nw_ref[...]
    hh = h.astype(BF16)
    hl = (h - hh.astype(F32)).astype(BF16)
    lg = (_dot(hh, wrh_ref[...]) + _dot(hl, wrh_ref[...]) + _dot(hh, wrl_ref[...]) + br_ref[...])
    meta, run = _route_tile(lg, run_ref[...], tri_ref[...])
    meta_ref[...] = meta
    run_ref[...] = run
    cnt_ref[...] = run
    bits = pltpu.bitcast(hh.astype(F32), jnp.uint32)
    for c in range(ROW_WORD_CHUNKS):
        lo = bits[:, 2 * c * LANES:(2 * c + 1) * LANES]
        hi = bits[:, (2 * c + 1) * LANES:(2 * c + 2) * LANES]
        hp_ref[pl.ds(c, OUT_TM, stride=ROW_WORD_CHUNKS), :] = (lo >> 16) | (hi & jnp.uint32(0xFFFF0000))


def _out_proj(x2d, out_a, out_b, wa, wb, norm_w, wr_hi, wr_lo, b_r):
    t = x2d.shape[0]
    half = D_MODEL // 2
    row = lambda n: pl.BlockSpec((OUT_TM, n), lambda i: (i, 0))
    full = lambda r, c: pl.BlockSpec((r, c), lambda i: (0, 0))
    return pl.pallas_call(
        _outproj_kernel,
        grid=(t // OUT_TM,),
        in_specs=[row(D_MODEL), row(half), row(half), full(half, D_MODEL), full(half, D_MODEL),
                  full(1, D_MODEL), full(D_MODEL, LANES), full(D_MODEL, LANES), full(1, LANES)],
        out_specs=[row(D_MODEL), pl.BlockSpec((OUT_TM * ROW_WORD_CHUNKS, LANES), lambda i: (i, 0)), row(LANES),
                   full(1, LANES)],
        out_shape=[jax.ShapeDtypeStruct((t, D_MODEL), F32),
                   jax.ShapeDtypeStruct((t * ROW_WORD_CHUNKS, LANES), jnp.uint32),
                   jax.ShapeDtypeStruct((t, LANES), F32),
                   jax.ShapeDtypeStruct((1, LANES), F32)],
        scratch_shapes=[pltpu.VMEM((1, LANES), F32), pltpu.VMEM((OUT_TM, OUT_TM), BF16)],
        compiler_params=pltpu.CompilerParams(
            dimension_semantics=("arbitrary",), vmem_limit_bytes=VMEM_LIMIT),
        name="out_proj",
    )(x2d, out_a, out_b, wa, wb, norm_w, wr_hi, wr_lo, b_r)


GATHER_UNROLL = 16
CLEAR_UNROLL = 32


def _dispatch_kernel(pos_ref, nt_ref, hp_ref, xs_ref, src_ref):
    t = pl.program_id(0)

    @pl.when(t == 0)
    def _():
        def clear(p, _):
            src_ref[p] = 0
            return 0
        lax.fori_loop(0, src_ref.shape[0], clear, 0, unroll=CLEAR_UNROLL)

        def place(c, _):
            rows = [pos_ref[c * GATHER_UNROLL + u] for u in range(GATHER_UNROLL)]
            for u in range(GATHER_UNROLL):
                src_ref[rows[u]] = c * (GATHER_UNROLL // EXPERT_TOPK) + u // EXPERT_TOPK
            return 0
        lax.fori_loop(0, pos_ref.shape[0] // GATHER_UNROLL, place, 0)

    @pl.when(t < nt_ref[0])
    def _():
        def gather(c, _):
            toks = [src_ref[t * MOE_TM + c * GATHER_UNROLL + u] for u in range(GATHER_UNROLL)]
            for u in range(GATHER_UNROLL):
                xs_ref[c * GATHER_UNROLL + u] = hp_ref[toks[u]]
            return 0
        lax.fori_loop(0, MOE_TM // GATHER_UNROLL, gather, 0)

    @pl.when(t >= nt_ref[0])
    def _():
        xs_ref[...] = jnp.zeros(xs_ref.shape, xs_ref.dtype)


def _dispatch(pos, n_tiles, hp, npad):
    n_tokens = hp.shape[0] // ROW_WORD_CHUNKS
    hp3 = hp.reshape(n_tokens, ROW_WORD_CHUNKS, LANES)
    xs = pl.pallas_call(
        _dispatch_kernel,
        grid_spec=pltpu.PrefetchScalarGridSpec(
            num_scalar_prefetch=2,
            grid=(npad // MOE_TM,),
            in_specs=[pl.BlockSpec(memory_space=pltpu.VMEM)],
            out_specs=pl.BlockSpec((MOE_TM, ROW_WORD_CHUNKS, LANES), lambda t, s, n: (t, 0, 0)),
            scratch_shapes=[pltpu.SMEM((npad,), jnp.int32)],
        ),
        out_shape=jax.ShapeDtypeStruct((npad, ROW_WORD_CHUNKS, LANES), hp.dtype),
        compiler_params=pltpu.CompilerParams(
            dimension_semantics=("arbitrary",), vmem_limit_bytes=VMEM_LIMIT),
        name="dispatch",
    )(pos, n_tiles, hp3)
    return xs.reshape(npad * ROW_WORD_CHUNKS, LANES)


def _experts_kernel(te_ref, nt_ref, xs_ref, wg_ref, wu_ref, wd_ref, y_ref,
                    wgb_ref, wub_ref, wdb_ref, x_ref):
    t = pl.program_id(0)

    @pl.when(t < nt_ref[0])
    def _():
        e = te_ref[t]
        prev = te_ref[jnp.maximum(t - 1, 0)]

        @pl.when((t == 0) | (e != prev))
        def _():
            wgb_ref[...] = wg_ref[0].astype(BF16)
            wub_ref[...] = wu_ref[0].astype(BF16)
            wdb_ref[...] = wd_ref[0].astype(BF16)

        for c in range(ROW_WORD_CHUNKS):
            u32 = xs_ref[pl.ds(c, MOE_TM, stride=ROW_WORD_CHUNKS), :]
            x_ref[:, 2 * c * LANES:(2 * c + 1) * LANES] = pltpu.bitcast(u32 << 16, F32).astype(BF16)
            x_ref[:, (2 * c + 1) * LANES:(2 * c + 2) * LANES] = (
                pltpu.bitcast(u32 & jnp.uint32(0xFFFF0000), F32).astype(BF16))
        x = x_ref[...]
        a = _dot(x, wgb_ref[...])
        u = _dot(x, wub_ref[...])
        act = (a * jax.nn.sigmoid(a)) * u
        y_ref[...] = _dot(act.astype(BF16), wdb_ref[...])

    @pl.when(t >= nt_ref[0])
    def _():
        y_ref[...] = jnp.zeros(y_ref.shape, y_ref.dtype)


def _experts(tile_expert, n_tiles, xs, w_gate, w_up, w_down):
    npad = xs.shape[0] // ROW_WORD_CHUNKS
    last = lambda t, nt: jnp.minimum(t, jnp.maximum(nt[0] - 1, 0))
    return pl.pallas_call(
        _experts_kernel,
        grid_spec=pltpu.PrefetchScalarGridSpec(
            num_scalar_prefetch=2,
            grid=(npad // MOE_TM,),
            in_specs=[
                pl.BlockSpec((MOE_TM * ROW_WORD_CHUNKS, LANES), lambda t, te, nt: (last(t, nt), 0)),
                pl.BlockSpec((1, D_MODEL, D_EXPERT), lambda t, te, nt: (te[last(t, nt)], 0, 0)),
                pl.BlockSpec((1, D_MODEL, D_EXPERT), lambda t, te, nt: (te[last(t, nt)], 0, 0)),
                pl.BlockSpec((1, D_EXPERT, D_MODEL), lambda t, te, nt: (te[last(t, nt)], 0, 0)),
            ],
            out_specs=pl.BlockSpec((MOE_TM, D_MODEL), lambda t, te, nt: (t, 0)),
            scratch_shapes=[
                pltpu.VMEM((D_MODEL, D_EXPERT), BF16),
                pltpu.VMEM((D_MODEL, D_EXPERT), BF16),
                pltpu.VMEM((D_EXPERT, D_MODEL), BF16),
                pltpu.VMEM((MOE_TM, D_MODEL), BF16),
            ],
        ),
        out_shape=jax.ShapeDtypeStruct((npad, D_MODEL), F32),
        compiler_params=pltpu.CompilerParams(
            dimension_semantics=("arbitrary",), vmem_limit_bytes=VMEM_LIMIT),
        name="experts",
    )(tile_expert, n_tiles, xs, w_gate, w_up, w_down)


CMB_UNROLL = 8
CMB_SLOTS = 2


def _combine_kernel(pos_ref, x2_ref, meta_ref, nw_ref, y_ref, o_ref, buf_ref, sem):
    i = pl.program_id(0)

    def row_copy(p, slot, k, r):
        return pltpu.make_async_copy(y_ref.at[pl.ds(p, 1), :], buf_ref.at[slot, k, pl.ds(r, 1), :],
                                     sem.at[slot])

    def issue(tile, slot):
        def trip(c, _):
            a0 = (tile * CMB_TT + c * CMB_UNROLL) * EXPERT_TOPK
            rows = [pos_ref[a0 + u] for u in range(CMB_UNROLL * EXPERT_TOPK)]
            for u in range(CMB_UNROLL * EXPERT_TOPK):
                row_copy(rows[u], slot, u % EXPERT_TOPK, c * CMB_UNROLL + u // EXPERT_TOPK).start(priority=u % 2)
            return 0
        lax.fori_loop(0, CMB_TT // CMB_UNROLL, trip, 0)

    @pl.when(i == 0)
    def _():
        issue(0, 0)

    @pl.when(i + 1 < pl.num_programs(0))
    def _():
        issue(i + 1, (i + 1) % CMB_SLOTS)

    slot = i % CMB_SLOTS
    for k in range(EXPERT_TOPK):
        pltpu.make_async_copy(y_ref.at[pl.ds(0, CMB_TT), :], buf_ref.at[slot, k], sem.at[slot]).wait()

    meta = meta_ref[...]
    x3 = (x2_ref[...] + meta[:, META_W1:META_W1 + 1] * buf_ref[slot, 0]
          + meta[:, META_W2:META_W2 + 1] * buf_ref[slot, 1])
    r = lax.rsqrt(jnp.mean(x3 * x3, axis=-1, keepdims=True) + RMS_EPS)
    o_ref[...] = x3 * r * nw_ref[...]


def _combine(pos, x2, meta, norm_w, y):
    t = x2.shape[0]
    return pl.pallas_call(
        _combine_kernel,
        grid_spec=pltpu.PrefetchScalarGridSpec(
            num_scalar_prefetch=1,
            grid=(t // CMB_TT,),
            in_specs=[
                pl.BlockSpec((CMB_TT, D_MODEL), lambda i, p: (i, 0)),
                pl.BlockSpec((CMB_TT, LANES), lambda i, p: (i, 0)),
                pl.BlockSpec((1, D_MODEL), lambda i, p: (0, 0)),
                pl.BlockSpec(memory_space=pl.ANY),
            ],
            out_specs=pl.BlockSpec((CMB_TT, D_MODEL), lambda i, p: (i, 0)),
            scratch_shapes=[
                pltpu.VMEM((CMB_SLOTS, EXPERT_TOPK, CMB_TT, D_MODEL), F32),
                pltpu.SemaphoreType.DMA((CMB_SLOTS,)),
            ],
        ),
        out_shape=jax.ShapeDtypeStruct((t, D_MODEL), F32),
        compiler_params=pltpu.CompilerParams(
            dimension_semantics=("arbitrary",), vmem_limit_bytes=VMEM_LIMIT),
        name="combine",
    )(pos, x2, meta, norm_w, y)


def _layout(meta, counts, n_tokens):
    npad = n_tokens * EXPERT_TOPK + N_EXPERTS * MOE_TM
    counts = counts[0, :N_EXPERTS].astype(jnp.int32)
    padded = ((counts + MOE_TM - 1) // MOE_TM) * MOE_TM
    ends = jnp.cumsum(padded)
    starts = ends - padded
    expert = meta[:, META_X1:META_X2 + 1].astype(jnp.int32)
    rank = meta[:, META_RANK1:META_RANK2 + 1].astype(jnp.int32)
    hit = expert[:, :, None] == jnp.arange(N_EXPERTS, dtype=jnp.int32)
    pos = jnp.sum(jnp.where(hit, starts, 0), axis=-1) + rank
    tile_start = jnp.arange(npad // MOE_TM, dtype=jnp.int32) * MOE_TM
    tile_expert = jnp.minimum(jnp.sum(ends[None, :] <= tile_start[:, None], axis=1), N_EXPERTS - 1)
    return (pos.reshape(-1).astype(jnp.int32), tile_expert.astype(jnp.int32),
            (ends[-1] // MOE_TM).astype(jnp.int32).reshape(1), npad)


def kernel(x, norm_mix_w, w_in, lambda_q1, lambda_k1, lambda_q2, lambda_k2, diff_subln_w, w_out, norm_ffn_w,
           w_router_group, b_router_group, w_router_expert, b_router_expert, w_gate, w_up, w_down, norm_final_w):
    batch, seq, _ = x.shape
    assert seq % ATT_TQ == 0 and seq // MOBA_BLOCK + 8 <= LANES
    assert w_in.shape[0] == 1, "single-layer block"
    n_tokens = batch * seq
    x2d = x.reshape(n_tokens, D_MODEL)

    col = np.arange(IN_WIDTH)
    is_q = (col < MOBA_HEADS * HEAD_DIM) | ((col >= 3 * MOBA_HEADS * HEAD_DIM) & (col < 4 * MOBA_HEADS * HEAD_DIM))
    col_scale = jnp.asarray(np.where(is_q, QK_SCALE * LOG2E, 1.0).astype(np.float32))[None, :]

    proj = _in_proj(x2d, norm_mix_w[0][None, :], w_in[0].astype(BF16), col_scale)
    out_a = _moba(proj, jnp.asarray(_alibi_slope_pieces(MOBA_HEADS)), batch, seq)
    out_b = _diff(proj, jnp.asarray(_alibi_slope_pieces(DIFF_HEADS)), lambda_q1, lambda_k1, lambda_q2, lambda_k2,
                  diff_subln_w, batch, seq)

    half = D_MODEL // 2
    w_o = w_out[0].astype(BF16)
    w_r = jnp.concatenate([w_router_group[0], w_router_expert[0]], axis=1)
    w_r = jnp.pad(w_r, ((0, 0), (0, LANES - w_r.shape[1])))
    wr_hi = w_r.astype(BF16)
    wr_lo = (w_r - wr_hi.astype(F32)).astype(BF16)
    b_r = jnp.pad(jnp.concatenate([b_router_group[0], b_router_expert[0]]), (0, LANES - N_GROUPS - N_EXPERTS))[None, :]
    x2, hp, meta, counts = _out_proj(x2d, out_a, out_b, w_o[:half], w_o[half:], norm_ffn_w[0][None, :],
                                     wr_hi, wr_lo, b_r)

    pos, tile_expert, n_tiles, npad = _layout(meta, counts, n_tokens)
    xs = _dispatch(pos, n_tiles, hp, npad)
    y = _experts(tile_expert, n_tiles, xs, w_gate[0], w_up[0], w_down[0])
    out = _combine(pos, x2, meta, norm_final_w[None, :], y)
    return out.reshape(batch, seq, D_MODEL)
```

```python
import functools

import ml_dtypes
import numpy as np
import jax
import jax.numpy as jnp
from jax import lax
from jax.experimental import pallas as pl
from jax.experimental.pallas import tpu as pltpu

F32 = jnp.float32
BF16 = jnp.bfloat16

D_MODEL = 2048
HEAD_DIM = 128
MOBA_HEADS = 8
MOBA_BLOCK = 256
MOBA_TOPK = 3
DIFF_HEADS = 4
DIFF_V_DIM = 256
IN_WIDTH = 6144
N_GROUPS = 4
EXPERTS_PER_GROUP = 8
N_EXPERTS = 32
EXPERT_TOPK = 2
D_EXPERT = 512
RMS_EPS = 1e-6
ALIBI_MAX_BIAS = 8.0
LAMBDA_INIT = 0.8 - 0.6 * float(np.exp(-0.3 * 0))

LANES = 128
QK_SCALE = HEAD_DIM ** -0.5
LOG2E = float(np.log2(np.e))
NEG_BIG = -1e30
M_INIT = -1e38
POS_SPLIT = 64
SLOPE_PIECES = 3
ROW_WORD_CHUNKS = D_MODEL // (2 * LANES)
META_X1, META_X2, META_W1, META_W2, META_RANK1, META_RANK2 = range(6)

IN_TM, IN_TN = 1024, 1024
ATT_TQ = 1024
ATT_TK = 512
KV_UNROLL = ATT_TQ // ATT_TK
HEADS_PER_STEP = 2
OUT_TM = 512
MOE_TM = 256
CMB_TT = 256
VMEM_LIMIT = 56 * 1024 * 1024


def _alibi_slope_pieces(n):
    rem = np.exp2(-ALIBI_MAX_BIAS * (np.arange(n, dtype=np.float64) + 1.0) / n) * LOG2E
    pieces = []
    for _ in range(SLOPE_PIECES):
        p = rem.astype(ml_dtypes.bfloat16).astype(np.float64)
        pieces.append(p)
        rem = rem - p
    return np.stack(pieces, axis=1).reshape(-1).astype(np.float32)


def _dot_nt(a, b):
    return lax.dot_general(a, b, (((1,), (1,)), ((), ())), preferred_element_type=F32)


def _dot(a, b):
    return jnp.dot(a, b, preferred_element_type=F32)


def _inproj_kernel(x_ref, nw_ref, cs_ref, w_ref, o_ref, h_ref):
    @pl.when(pl.program_id(1) == 0)
    def _():
        x = x_ref[...]
        r = lax.rsqrt(jnp.mean(x * x, axis=-1, keepdims=True) + RMS_EPS)
        h_ref[...] = (x * r * nw_ref[...]).astype(BF16)

    acc = _dot(h_ref[...], w_ref[...])
    o_ref[...] = (acc * cs_ref[...]).astype(BF16)


def _in_proj(x2d, norm_w, w_in_bf16, col_scale):
    t = x2d.shape[0]
    return pl.pallas_call(
        _inproj_kernel,
        grid=(t // IN_TM, IN_WIDTH // IN_TN),
        in_specs=[
            pl.BlockSpec((IN_TM, D_MODEL), lambda i, j: (i, 0)),
            pl.BlockSpec((1, D_MODEL), lambda i, j: (0, 0)),
            pl.BlockSpec((1, IN_TN), lambda i, j: (0, j)),
            pl.BlockSpec((D_MODEL, IN_TN), lambda i, j: (0, j)),
        ],
        out_specs=pl.BlockSpec((IN_TM, IN_TN), lambda i, j: (i, j)),
        out_shape=jax.ShapeDtypeStruct((t, IN_WIDTH), BF16),
        scratch_shapes=[pltpu.VMEM((IN_TM, D_MODEL), BF16)],
        compiler_params=pltpu.CompilerParams(
            dimension_semantics=("arbitrary", "arbitrary"), vmem_limit_bytes=VMEM_LIMIT),
        name="in_proj",
    )(x2d, norm_w, col_scale, w_in_bf16)


def _key_position_features(seq, first_lane):
    pos = lax.broadcasted_iota(jnp.int32, (seq, LANES), 0)
    lane = lax.broadcasted_iota(jnp.int32, (seq, LANES), 1)
    hi = (lane >= first_lane) & (lane < first_lane + SLOPE_PIECES)
    lo = (lane >= first_lane + SLOPE_PIECES) & (lane < first_lane + 2 * SLOPE_PIECES)
    ext = jnp.where(hi, (pos // POS_SPLIT).astype(F32), 0.0)
    return jnp.where(lo, (pos % POS_SPLIT).astype(F32), ext), pos, lane


def _slope_feature_rows(feat_ref, head, width):
    row = lax.broadcasted_iota(jnp.int32, (8, width), 0)
    out = jnp.zeros((8, width), F32)
    for k in range(SLOPE_PIECES):
        c = feat_ref[head * SLOPE_PIECES + k]
        out = jnp.where(row == k, POS_SPLIT * c, out)
        out = jnp.where(row == SLOPE_PIECES + k, c, out)
    return out


def _causal_tile(s):
    row = lax.broadcasted_iota(jnp.int32, s.shape, 0)
    col = lax.broadcasted_iota(jnp.int32, s.shape, 1)
    return jnp.where(col <= row, s, -jnp.inf)


def _online_step(s, m):
    m_new = jnp.maximum(m, jnp.max(s, axis=-1, keepdims=True))
    return m_new, jnp.exp2(m - m_new), jnp.exp2(s - m_new)


def _past_tiles_loop(i, body, init):
    def trip(t, carry):
        for u in range(KV_UNROLL):
            carry = body(t * KV_UNROLL + u, carry)
        return carry

    return lax.fori_loop(0, i, trip, init)


def _rows_from(state, start):
    return state if start == 0 else tuple(a[start:] for a in state)


def _rows_replace(state, start, tail):
    if start == 0:
        return tuple(tail)
    return tuple(jnp.concatenate([a[:start], b], axis=0) for a, b in zip(state, tail))


def _moba_kernel(feat_ref, q_ref, k_ref, v_ref, o_ref, kaug_ref, vaug_ref, kmh_ref, kml_ref, qaug_ref, *, seq):
    nb = seq // MOBA_BLOCK
    hp = pl.program_id(1)
    i = pl.program_id(2)

    @pl.when(i == 0)
    def _():
        ext, pos, lane = _key_position_features(seq, nb)
        ext = jnp.where(lane == pos // MOBA_BLOCK, 1.0, ext).astype(BF16)
        ones_col = jnp.where(lane == 0, 1.0, 0.0).astype(BF16)
        for g in range(HEADS_PER_STEP):
            k = k_ref[:, g * HEAD_DIM:(g + 1) * HEAD_DIM]
            kaug_ref[g, :, 0:HEAD_DIM] = k
            kaug_ref[g, :, HEAD_DIM:2 * HEAD_DIM] = ext
            vaug_ref[g, :, 0:HEAD_DIM] = v_ref[:, g * HEAD_DIM:(g + 1) * HEAD_DIM]
            vaug_ref[g, :, HEAD_DIM:2 * HEAD_DIM] = ones_col
            km = jnp.mean(k.astype(F32).reshape(nb, MOBA_BLOCK, HEAD_DIM), axis=1)
            kmh = km.astype(BF16)
            kmh_ref[g] = jnp.zeros((LANES, HEAD_DIM), BF16)
            kml_ref[g] = jnp.zeros((LANES, HEAD_DIM), BF16)
            kmh_ref[g, 0:nb, :] = kmh
            kml_ref[g, 0:nb, :] = (km - kmh.astype(F32)).astype(BF16)

    blk = lax.broadcasted_iota(jnp.int32, (nb, ATT_TQ), 0)
    col = lax.broadcasted_iota(jnp.int32, (nb, ATT_TQ), 1)
    own = i * (ATT_TQ // MOBA_BLOCK) + col // MOBA_BLOCK
    blk_f = blk.astype(F32)
    for g in range(HEADS_PER_STEP):
        q = q_ref[:, g * HEAD_DIM:(g + 1) * HEAD_DIM]
        gate = (_dot_nt(kmh_ref[g], q) + _dot_nt(kml_ref[g], q))[0:nb, :]
        gate = jnp.where(blk < own, gate, -jnp.inf)
        sel = jnp.zeros((nb, ATT_TQ), F32)
        for _ in range(MOBA_TOPK):
            mx = jnp.max(gate, axis=0, keepdims=True)
            idx = jnp.min(jnp.where(gate == mx, blk_f, float(LANES)), axis=0, keepdims=True)
            pick = (blk_f == idx) & (mx > -jnp.inf)
            sel = jnp.where(pick, 1.0, sel)
            gate = jnp.where(pick, -jnp.inf, gate)
        bias = jnp.where((sel > 0.0) | (blk == own), 0.0, NEG_BIG)
        feat = jnp.concatenate(
            [bias, _slope_feature_rows(feat_ref, hp * HEADS_PER_STEP + g, ATT_TQ),
             jnp.zeros((LANES - nb - 8, ATT_TQ), F32)], axis=0)
        qaug_ref[g, :, 0:HEAD_DIM] = q
        qaug_ref[g, :, HEAD_DIM:2 * HEAD_DIM] = feat.T.astype(BF16)

    def tile_step(g, r, state, row0=0, causal=False):
        m, acc = state
        s = _dot_nt(qaug_ref[g, row0:, :], kaug_ref[g, pl.ds(r, ATT_TK), :])
        if causal:
            s = _causal_tile(s)
        m, alpha, p = _online_step(s, m)
        return m, alpha * acc + _dot(p.astype(BF16), vaug_ref[g, pl.ds(r, ATT_TK), :])

    def body(j, states):
        r = pl.multiple_of(j * ATT_TK, ATT_TK)
        return tuple(tile_step(g, r, states[g]) for g in range(HEADS_PER_STEP))

    init = tuple((jnp.full((ATT_TQ, 1), M_INIT, F32), jnp.zeros((ATT_TQ, 2 * HEAD_DIM), F32))
                 for _ in range(HEADS_PER_STEP))
    states = list(_past_tiles_loop(i, body, init))
    for d in range(KV_UNROLL):
        r = pl.multiple_of(i * ATT_TQ + d * ATT_TK, ATT_TK)
        for g in range(HEADS_PER_STEP):
            tail = tile_step(g, r, _rows_from(states[g], d * ATT_TK), row0=d * ATT_TK, causal=True)
            states[g] = _rows_replace(states[g], d * ATT_TK, tail)
    for g in range(HEADS_PER_STEP):
        acc = states[g][1]
        o_ref[:, g * HEAD_DIM:(g + 1) * HEAD_DIM] = (
            acc[:, 0:HEAD_DIM] / acc[:, HEAD_DIM:HEAD_DIM + 1]).astype(BF16)


def _moba(proj, feats, batch, seq):
    nq = seq // ATT_TQ
    width = HEADS_PER_STEP * HEAD_DIM
    q_col, k_col, v_col = 0, MOBA_HEADS // HEADS_PER_STEP, 2 * MOBA_HEADS // HEADS_PER_STEP
    return pl.pallas_call(
        functools.partial(_moba_kernel, seq=seq),
        grid_spec=pltpu.PrefetchScalarGridSpec(
            num_scalar_prefetch=1,
            grid=(batch, MOBA_HEADS // HEADS_PER_STEP, nq),
            in_specs=[
                pl.BlockSpec((ATT_TQ, width), lambda b, h, i, s: (b * nq + i, q_col + h)),
                pl.BlockSpec((seq, width), lambda b, h, i, s: (b, k_col + h)),
                pl.BlockSpec((seq, width), lambda b, h, i, s: (b, v_col + h)),
            ],
            out_specs=pl.BlockSpec((ATT_TQ, width), lambda b, h, i, s: (b * nq + i, h)),
            scratch_shapes=[
                pltpu.VMEM((HEADS_PER_STEP, seq, 2 * HEAD_DIM), BF16),
                pltpu.VMEM((HEADS_PER_STEP, seq, 2 * HEAD_DIM), BF16),
                pltpu.VMEM((HEADS_PER_STEP, LANES, HEAD_DIM), BF16),
                pltpu.VMEM((HEADS_PER_STEP, LANES, HEAD_DIM), BF16),
                pltpu.VMEM((HEADS_PER_STEP, ATT_TQ, 2 * HEAD_DIM), BF16),
            ],
        ),
        out_shape=jax.ShapeDtypeStruct((batch * seq, MOBA_HEADS * HEAD_DIM), BF16),
        compiler_params=pltpu.CompilerParams(
            dimension_semantics=("arbitrary", "arbitrary", "arbitrary"), vmem_limit_bytes=VMEM_LIMIT),
        name="moba",
    )(feats, proj, proj, proj)


def _diff_kernel(feat_ref, lq1_ref, lk1_ref, lq2_ref, lk2_ref, sw_ref,
                 q_ref, k_ref, v_ref, o_ref, kaug_ref, qaug_ref, *, seq):
    h = pl.program_id(1)
    i = pl.program_id(2)

    @pl.when(i == 0)
    def _():
        ext = _key_position_features(seq, 0)[0].astype(BF16)
        for c in range(2):
            kaug_ref[c, :, 0:HEAD_DIM] = k_ref[:, c * HEAD_DIM:(c + 1) * HEAD_DIM]
            kaug_ref[c, :, HEAD_DIM:2 * HEAD_DIM] = ext

    feat = jnp.concatenate([_slope_feature_rows(feat_ref, h, ATT_TQ),
                            jnp.zeros((LANES - 8, ATT_TQ), F32)], axis=0).T.astype(BF16)
    for c in range(2):
        qaug_ref[c, :, 0:HEAD_DIM] = q_ref[:, c * HEAD_DIM:(c + 1) * HEAD_DIM]
        qaug_ref[c, :, HEAD_DIM:2 * HEAD_DIM] = feat

    def tile_step(c, r, v, state, row0=0, causal=False):
        m, l, acc = state
        s = _dot_nt(qaug_ref[c, row0:, :], kaug_ref[c, pl.ds(r, ATT_TK), :])
        if causal:
            s = _causal_tile(s)
        m, alpha, p = _online_step(s, m)
        return (m, alpha * l + jnp.sum(p, axis=-1, keepdims=True),
                alpha * acc + _dot(p.astype(BF16), v))

    def body(j, states):
        r = pl.multiple_of(j * ATT_TK, ATT_TK)
        v = v_ref[pl.ds(r, ATT_TK), :]
        return tuple(tile_step(c, r, v, states[c]) for c in range(2))

    init = tuple((jnp.full((ATT_TQ, 1), M_INIT, F32), jnp.zeros((ATT_TQ, 1), F32),
                  jnp.zeros((ATT_TQ, DIFF_V_DIM), F32)) for _ in range(2))
    states = list(_past_tiles_loop(i, body, init))
    for d in range(KV_UNROLL):
        r = pl.multiple_of(i * ATT_TQ + d * ATT_TK, ATT_TK)
        v = v_ref[pl.ds(r, ATT_TK), :]
        for c in range(2):
            tail = tile_step(c, r, v, _rows_from(states[c], d * ATT_TK), row0=d * ATT_TK, causal=True)
            states[c] = _rows_replace(states[c], d * ATT_TK, tail)
    (_, l1, acc1), (_, l2, acc2) = states

    lam = (jnp.exp(jnp.sum(lq1_ref[...] * lk1_ref[...], axis=-1, keepdims=True))
           - jnp.exp(jnp.sum(lq2_ref[...] * lk2_ref[...], axis=-1, keepdims=True))
           + LAMBDA_INIT)
    o = acc1 / l1 - lam * (acc2 / l2)
    y = o * lax.rsqrt(jnp.mean(o * o, axis=-1, keepdims=True) + RMS_EPS)
    o_ref[...] = ((y * sw_ref[...]) * (1.0 - LAMBDA_INIT)).astype(BF16)


def _diff(proj, feats, lq1, lk1, lq2, lk2, subln_w, batch, seq):
    nq = seq // ATT_TQ
    width = 2 * HEAD_DIM
    q_col = 3 * MOBA_HEADS * HEAD_DIM // width
    k_col = q_col + DIFF_HEADS
    v_col = k_col + DIFF_HEADS
    vec = lambda n: pl.BlockSpec((1, n), lambda b, h, i, s: (0, 0))
    return pl.pallas_call(
        functools.partial(_diff_kernel, seq=seq),
        grid_spec=pltpu.PrefetchScalarGridSpec(
            num_scalar_prefetch=1,
            grid=(batch, DIFF_HEADS, nq),
            in_specs=[
                vec(HEAD_DIM), vec(HEAD_DIM), vec(HEAD_DIM), vec(HEAD_DIM), vec(DIFF_V_DIM),
                pl.BlockSpec((ATT_TQ, width), lambda b, h, i, s: (b * nq + i, q_col + h)),
                pl.BlockSpec((seq, width), lambda b, h, i, s: (b, k_col + h)),
                pl.BlockSpec((seq, DIFF_V_DIM), lambda b, h, i, s: (b, v_col + h)),
            ],
            out_specs=pl.BlockSpec((ATT_TQ, DIFF_V_DIM), lambda b, h, i, s: (b * nq + i, h)),
            scratch_shapes=[
                pltpu.VMEM((2, seq, 2 * HEAD_DIM), BF16),
                pltpu.VMEM((2, ATT_TQ, 2 * HEAD_DIM), BF16),
            ],
        ),
        out_shape=jax.ShapeDtypeStruct((batch * seq, DIFF_HEADS * DIFF_V_DIM), BF16),
        compiler_params=pltpu.CompilerParams(
            dimension_semantics=("arbitrary", "arbitrary", "arbitrary"), vmem_limit_bytes=VMEM_LIMIT),
        name="diff_attn",
    )(feats, lq1, lk1, lq2, lk2, subln_w, proj, proj, proj)


def _route_tile(lg, run, tri):
    lane = lax.broadcasted_iota(jnp.int32, lg.shape, 1).astype(F32)
    rmax = lambda v: jnp.max(v, axis=-1, keepdims=True)
    first = lambda v, m: jnp.min(jnp.where(v == m, lane, float(LANES)), axis=-1, keepdims=True)

    gl = jnp.where(lane < N_GROUPS, lg, -jnp.inf)
    gmax = rmax(gl)
    group = first(gl, gmax)
    p_group = 1.0 / jnp.sum(jnp.exp(gl - gmax), axis=-1, keepdims=True)

    lo = N_GROUPS + EXPERTS_PER_GROUP * group
    el = jnp.where((lane >= lo) & (lane < lo + EXPERTS_PER_GROUP), lg, -jnp.inf)
    e1 = rmax(el)
    lane1 = first(el, e1)
    el = jnp.where(lane == lane1, -jnp.inf, el)
    e2 = rmax(el)
    lane2 = first(el, e2)
    t2 = jnp.exp(e2 - e1)
    w1 = p_group / (1.0 + t2)
    w2 = p_group * t2 / (1.0 + t2)
    x1 = lane1 - N_GROUPS
    x2 = lane2 - N_GROUPS

    hit1 = lane == x1
    hit2 = lane == x2
    onehot = jnp.where(hit1 | hit2, 1.0, 0.0)
    before = _dot(tri, onehot.astype(BF16)) + run
    rank1 = jnp.sum(jnp.where(hit1, before, 0.0), axis=-1, keepdims=True)
    rank2 = jnp.sum(jnp.where(hit2, before, 0.0), axis=-1, keepdims=True)

    meta = jnp.zeros(lg.shape, F32)
    for k, v in ((META_X1, x1), (META_X2, x2), (META_W1, w1), (META_W2, w2),
                 (META_RANK1, rank1), (META_RANK2, rank2)):
        meta = jnp.where(lane == k, v, meta)
    return meta, run + jnp.sum(onehot, axis=0, keepdims=True)


def _outproj_kernel(x_ref, oa_ref, ob_ref, wa_ref, wb_ref, nw_ref, wrh_ref, wrl_ref, br_ref,
                    x2_ref, hp_ref, meta_ref, cnt_ref, run_ref, tri_ref):
    @pl.when(pl.program_id(0) == 0)
    def _():
        row = lax.broadcasted_iota(jnp.int32, tri_ref.shape, 0)
        col = lax.broadcasted_iota(jnp.int32, tri_ref.shape, 1)
        tri_ref[...] = jnp.where(col < row, 1.0, 0.0).astype(BF16)
        run_ref[...] = jnp.zeros(run_ref.shape, F32)

    x2 = x_ref[...] + _dot(oa_ref[...], wa_ref[...]) + _dot(ob_ref[...], wb_ref[...])
    x2_ref[...] = x2
    r = lax.rsqrt(jnp.mean(x2 * x2, axis=-1, keepdims=True) + RMS_EPS)
    h = x2 * r * nw_ref[...]
    hh = h.astype(BF16)
    hl = (h - hh.astype(F32)).astype(BF16)
    lg = (_dot(hh, wrh_ref[...]) + _dot(hl, wrh_ref[...]) + _dot(hh, wrl_ref[...]) + br_ref[...])
    meta, run = _route_tile(lg, run_ref[...], tri_ref[...])
    meta_ref[...] = meta
    run_ref[...] = run
    cnt_ref[...] = run
    bits = pltpu.bitcast(hh.astype(F32), jnp.uint32)
    for c in range(ROW_WORD_CHUNKS):
        lo = bits[:, 2 * c * LANES:(2 * c + 1) * LANES]
        hi = bits[:, (2 * c + 1) * LANES:(2 * c + 2) * LANES]
        hp_ref[pl.ds(c, OUT_TM, stride=ROW_WORD_CHUNKS), :] = (lo >> 16) | (hi & jnp.uint32(0xFFFF0000))


def _out_proj(x2d, out_a, out_b, wa, wb, norm_w, wr_hi, wr_lo, b_r):
    t = x2d.shape[0]
    half = D_MODEL // 2
    row = lambda n: pl.BlockSpec((OUT_TM, n), lambda i: (i, 0))
    full = lambda r, c: pl.BlockSpec((r, c), lambda i: (0, 0))
    return pl.pallas_call(
        _outproj_kernel,
        grid=(t // OUT_TM,),
        in_specs=[row(D_MODEL), row(half), row(half), full(half, D_MODEL), full(half, D_MODEL),
                  full(1, D_MODEL), full(D_MODEL, LANES), full(D_MODEL, LANES), full(1, LANES)],
        out_specs=[row(D_MODEL), pl.BlockSpec((OUT_TM * ROW_WORD_CHUNKS, LANES), lambda i: (i, 0)), row(LANES),
                   full(1, LANES)],
        out_shape=[jax.ShapeDtypeStruct((t, D_MODEL), F32),
                   jax.ShapeDtypeStruct((t * ROW_WORD_CHUNKS, LANES), jnp.uint32),
                   jax.ShapeDtypeStruct((t, LANES), F32),
                   jax.ShapeDtypeStruct((1, LANES), F32)],
        scratch_shapes=[pltpu.VMEM((1, LANES), F32), pltpu.VMEM((OUT_TM, OUT_TM), BF16)],
        compiler_params=pltpu.CompilerParams(
            dimension_semantics=("arbitrary",), vmem_limit_bytes=VMEM_LIMIT),
        name="out_proj",
    )(x2d, out_a, out_b, wa, wb, norm_w, wr_hi, wr_lo, b_r)


GATHER_UNROLL = 16
CLEAR_UNROLL = 32


def _dispatch_kernel(pos_ref, nt_ref, hp_ref, xs_ref, src_ref):
    t = pl.program_id(0)

    @pl.when(t == 0)
    def _():
        def clear(p, _):
            src_ref[p] = 0
            return 0
        lax.fori_loop(0, src_ref.shape[0], clear, 0, unroll=CLEAR_UNROLL)

        def place(c, _):
            rows = [pos_ref[c * GATHER_UNROLL + u] for u in range(GATHER_UNROLL)]
            for u in range(GATHER_UNROLL):
                src_ref[rows[u]] = c * (GATHER_UNROLL // EXPERT_TOPK) + u // EXPERT_TOPK
            return 0
        lax.fori_loop(0, pos_ref.shape[0] // GATHER_UNROLL, place, 0)

    @pl.when(t < nt_ref[0])
    def _():
        def gather(c, _):
            toks = [src_ref[t * MOE_TM + c * GATHER_UNROLL + u] for u in range(GATHER_UNROLL)]
            for u in range(GATHER_UNROLL):
                xs_ref[c * GATHER_UNROLL + u] = hp_ref[toks[u]]
            return 0
        lax.fori_loop(0, MOE_TM // GATHER_UNROLL, gather, 0)

    @pl.when(t >= nt_ref[0])
    def _():
        xs_ref[...] = jnp.zeros(xs_ref.shape, xs_ref.dtype)


def _dispatch(pos, n_tiles, hp, npad):
    n_tokens = hp.shape[0] // ROW_WORD_CHUNKS
    hp3 = hp.reshape(n_tokens, ROW_WORD_CHUNKS, LANES)
    xs = pl.pallas_call(
        _dispatch_kernel,
        grid_spec=pltpu.PrefetchScalarGridSpec(
            num_scalar_prefetch=2,
            grid=(npad // MOE_TM,),
            in_specs=[pl.BlockSpec(memory_space=pltpu.VMEM)],
            out_specs=pl.BlockSpec((MOE_TM, ROW_WORD_CHUNKS, LANES), lambda t, s, n: (t, 0, 0)),
            scratch_shapes=[pltpu.SMEM((npad,), jnp.int32)],
        ),
        out_shape=jax.ShapeDtypeStruct((npad, ROW_WORD_CHUNKS, LANES), hp.dtype),
        compiler_params=pltpu.CompilerParams(
            dimension_semantics=("arbitrary",), vmem_limit_bytes=VMEM_LIMIT),
        name="dispatch",
    )(pos, n_tiles, hp3)
    return xs.reshape(npad * ROW_WORD_CHUNKS, LANES)


W_SLOTS = 2


def _experts_kernel(tk_ref, eseq_ref, nt_ref, xs_ref, wg_hbm, wu_hbm, wd_hbm, y_ref,
                    wg_buf, wu_buf, wd_buf, wgb_ref, wub_ref, wdb_ref, x_ref, sem):
    t = pl.program_id(0)
    n_used = nt_ref[1]
    hbm_bufs = ((wg_hbm, wg_buf), (wu_hbm, wu_buf), (wd_hbm, wd_buf))

    def weight_copies(k, slot):
        e = eseq_ref[k]
        return [pltpu.make_async_copy(hbm.at[e], buf.at[slot], sem.at[slot, n])
                for n, (hbm, buf) in enumerate(hbm_bufs)]

    @pl.when(t < nt_ref[0])
    def _():
        k = tk_ref[t]

        @pl.when((t == 0) | (k != tk_ref[jnp.maximum(t - 1, 0)]))
        def _():
            @pl.when(k == 0)
            def _():
                for s in range(W_SLOTS):
                    @pl.when(s < n_used)
                    def _():
                        for c in weight_copies(s, s):
                            c.start()

            slot = k % W_SLOTS
            for c in weight_copies(k, slot):
                c.wait()
            wgb_ref[...] = wg_buf[slot].astype(BF16)
            wub_ref[...] = wu_buf[slot].astype(BF16)
            wdb_ref[...] = wd_buf[slot].astype(BF16)

            @pl.when(k + W_SLOTS < n_used)
            def _():
                for c in weight_copies(k + W_SLOTS, slot):
                    c.start()

        for c in range(ROW_WORD_CHUNKS):
            u32 = xs_ref[pl.ds(c, MOE_TM, stride=ROW_WORD_CHUNKS), :]
            x_ref[:, 2 * c * LANES:(2 * c + 1) * LANES] = pltpu.bitcast(u32 << 16, F32).astype(BF16)
            x_ref[:, (2 * c + 1) * LANES:(2 * c + 2) * LANES] = (
                pltpu.bitcast(u32 & jnp.uint32(0xFFFF0000), F32).astype(BF16))
        x = x_ref[...]
        a = _dot(x, wgb_ref[...])
        u = _dot(x, wub_ref[...])
        act = (a * jax.nn.sigmoid(a)) * u
        y_ref[...] = _dot(act.astype(BF16), wdb_ref[...])

    @pl.when(t >= nt_ref[0])
    def _():
        y_ref[...] = jnp.zeros(y_ref.shape, y_ref.dtype)


def _experts(tile_k, expert_seq, n_used, xs, w_gate, w_up, w_down):
    npad = xs.shape[0] // ROW_WORD_CHUNKS
    last = lambda t, nt: jnp.minimum(t, jnp.maximum(nt[0] - 1, 0))
    hbm = pl.BlockSpec(memory_space=pl.ANY)
    return pl.pallas_call(
        _experts_kernel,
        grid_spec=pltpu.PrefetchScalarGridSpec(
            num_scalar_prefetch=3,
            grid=(npad // MOE_TM,),
            in_specs=[
                pl.BlockSpec((MOE_TM * ROW_WORD_CHUNKS, LANES), lambda t, tk, es, nt: (last(t, nt), 0)),
                hbm, hbm, hbm,
            ],
            out_specs=pl.BlockSpec((MOE_TM, D_MODEL), lambda t, tk, es, nt: (t, 0)),
            scratch_shapes=[
                pltpu.VMEM((W_SLOTS, D_MODEL, D_EXPERT), F32),
                pltpu.VMEM((W_SLOTS, D_MODEL, D_EXPERT), F32),
                pltpu.VMEM((W_SLOTS, D_EXPERT, D_MODEL), F32),
                pltpu.VMEM((D_MODEL, D_EXPERT), BF16),
                pltpu.VMEM((D_MODEL, D_EXPERT), BF16),
                pltpu.VMEM((D_EXPERT, D_MODEL), BF16),
                pltpu.VMEM((MOE_TM, D_MODEL), BF16),
                pltpu.SemaphoreType.DMA((W_SLOTS, 3)),
            ],
        ),
        out_shape=jax.ShapeDtypeStruct((npad, D_MODEL), F32),
        compiler_params=pltpu.CompilerParams(
            dimension_semantics=("arbitrary",), vmem_limit_bytes=VMEM_LIMIT),
        name="experts",
    )(tile_k, expert_seq, n_used, xs, w_gate, w_up, w_down)


CMB_UNROLL = 8
CMB_SLOTS = 2


def _combine_kernel(pos_ref, x2_ref, meta_ref, nw_ref, y_ref, o_ref, buf_ref, sem):
    i = pl.program_id(0)

    def row_copy(p, slot, k, r):
        return pltpu.make_async_copy(y_ref.at[pl.ds(p, 1), :], buf_ref.at[slot, k, pl.ds(r, 1), :],
                                     sem.at[slot])

    def issue(tile, slot):
        def trip(c, _):
            a0 = (tile * CMB_TT + c * CMB_UNROLL) * EXPERT_TOPK
            rows = [pos_ref[a0 + u] for u in range(CMB_UNROLL * EXPERT_TOPK)]
            for u in range(CMB_UNROLL * EXPERT_TOPK):
                row_copy(rows[u], slot, u % EXPERT_TOPK, c * CMB_UNROLL + u // EXPERT_TOPK).start(priority=u % 2)
            return 0
        lax.fori_loop(0, CMB_TT // CMB_UNROLL, trip, 0)

    @pl.when(i == 0)
    def _():
        issue(0, 0)

    @pl.when(i + 1 < pl.num_programs(0))
    def _():
        issue(i + 1, (i + 1) % CMB_SLOTS)

    slot = i % CMB_SLOTS
    for k in range(EXPERT_TOPK):
        pltpu.make_async_copy(y_ref.at[pl.ds(0, CMB_TT), :], buf_ref.at[slot, k], sem.at[slot]).wait()

    meta = meta_ref[...]
    x3 = (x2_ref[...] + meta[:, META_W1:META_W1 + 1] * buf_ref[slot, 0]
          + meta[:, META_W2:META_W2 + 1] * buf_ref[slot, 1])
    r = lax.rsqrt(jnp.mean(x3 * x3, axis=-1, keepdims=True) + RMS_EPS)
    o_ref[...] = x3 * r * nw_ref[...]


def _combine(pos, x2, meta, norm_w, y):
    t = x2.shape[0]
    return pl.pallas_call(
        _combine_kernel,
        grid_spec=pltpu.PrefetchScalarGridSpec(
            num_scalar_prefetch=1,
            grid=(t // CMB_TT,),
            in_specs=[
                pl.BlockSpec((CMB_TT, D_MODEL), lambda i, p: (i, 0)),
                pl.BlockSpec((CMB_TT, LANES), lambda i, p: (i, 0)),
                pl.BlockSpec((1, D_MODEL), lambda i, p: (0, 0)),
                pl.BlockSpec(memory_space=pl.ANY),
            ],
            out_specs=pl.BlockSpec((CMB_TT, D_MODEL), lambda i, p: (i, 0)),
            scratch_shapes=[
                pltpu.VMEM((CMB_SLOTS, EXPERT_TOPK, CMB_TT, D_MODEL), F32),
                pltpu.SemaphoreType.DMA((CMB_SLOTS,)),
            ],
        ),
        out_shape=jax.ShapeDtypeStruct((t, D_MODEL), F32),
        compiler_params=pltpu.CompilerParams(
            dimension_semantics=("arbitrary",), vmem_limit_bytes=VMEM_LIMIT),
        name="combine",
    )(pos, x2, meta, norm_w, y)


def _layout(meta, counts, n_tokens):
    npad = n_tokens * EXPERT_TOPK + N_EXPERTS * MOE_TM
    counts = counts[0, :N_EXPERTS].astype(jnp.int32)
    padded = ((counts + MOE_TM - 1) // MOE_TM) * MOE_TM
    ends = jnp.cumsum(padded)
    starts = ends - padded
    expert = meta[:, META_X1:META_X2 + 1].astype(jnp.int32)
    rank = meta[:, META_RANK1:META_RANK2 + 1].astype(jnp.int32)
    hit = expert[:, :, None] == jnp.arange(N_EXPERTS, dtype=jnp.int32)
    pos = jnp.sum(jnp.where(hit, starts, 0), axis=-1) + rank
    tile_start = jnp.arange(npad // MOE_TM, dtype=jnp.int32) * MOE_TM
    tile_expert = jnp.minimum(jnp.sum(ends[None, :] <= tile_start[:, None], axis=1), N_EXPERTS - 1)
    used = counts > 0
    k_of_expert = jnp.cumsum(used.astype(jnp.int32)) - 1
    ids = jnp.arange(N_EXPERTS, dtype=jnp.int32)
    expert_seq = jnp.sum(jnp.where(used[None, :] & (k_of_expert[None, :] == ids[:, None]), ids[None, :], 0), axis=1)
    tile_k = jnp.sum(jnp.where(tile_expert[:, None] == ids[None, :], k_of_expert[None, :], 0), axis=1)
    n_tiles = (ends[-1] // MOE_TM).astype(jnp.int32)
    n_used = jnp.stack([n_tiles, jnp.sum(used).astype(jnp.int32)])
    return (pos.reshape(-1).astype(jnp.int32), tile_k.astype(jnp.int32), expert_seq.astype(jnp.int32),
            n_tiles.reshape(1), n_used, npad)


def kernel(x, norm_mix_w, w_in, lambda_q1, lambda_k1, lambda_q2, lambda_k2, diff_subln_w, w_out, norm_ffn_w,
           w_router_group, b_router_group, w_router_expert, b_router_expert, w_gate, w_up, w_down, norm_final_w):
    batch, seq, _ = x.shape
    assert seq % ATT_TQ == 0 and seq // MOBA_BLOCK + 8 <= LANES
    assert w_in.shape[0] == 1, "single-layer block"
    n_tokens = batch * seq
    x2d = x.reshape(n_tokens, D_MODEL)

    col = np.arange(IN_WIDTH)
    is_q = (col < MOBA_HEADS * HEAD_DIM) | ((col >= 3 * MOBA_HEADS * HEAD_DIM) & (col < 4 * MOBA_HEADS * HEAD_DIM))
    col_scale = jnp.asarray(np.where(is_q, QK_SCALE * LOG2E, 1.0).astype(np.float32))[None, :]

    proj = _in_proj(x2d, norm_mix_w[0][None, :], w_in[0].astype(BF16), col_scale)
    out_a = _moba(proj, jnp.asarray(_alibi_slope_pieces(MOBA_HEADS)), batch, seq)
    out_b = _diff(proj, jnp.asarray(_alibi_slope_pieces(DIFF_HEADS)), lambda_q1, lambda_k1, lambda_q2, lambda_k2,
                  diff_subln_w, batch, seq)

    half = D_MODEL // 2
    w_o = w_out[0].astype(BF16)
    w_r = jnp.concatenate([w_router_group[0], w_router_expert[0]], axis=1)
    w_r = jnp.pad(w_r, ((0, 0), (0, LANES - w_r.shape[1])))
    wr_hi = w_r.astype(BF16)
    wr_lo = (w_r - wr_hi.astype(F32)).astype(BF16)
    b_r = jnp.pad(jnp.concatenate([b_router_group[0], b_router_expert[0]]), (0, LANES - N_GROUPS - N_EXPERTS))[None, :]
    x2, hp, meta, counts = _out_proj(x2d, out_a, out_b, w_o[:half], w_o[half:], norm_ffn_w[0][None, :],
                                     wr_hi, wr_lo, b_r)

    pos, tile_k, expert_seq, n_tiles, n_used, npad = _layout(meta, counts, n_tokens)
    xs = _dispatch(pos, n_tiles, hp, npad)
    y = _experts(tile_k, expert_seq, n_used, xs, w_gate[0], w_up[0], w_down[0])
    out = _combine(pos, x2, meta, norm_final_w[None, :], y)
    return out.reshape(batch, seq, D_MODEL)
```

```python
import functools

import ml_dtypes
import numpy as np
import jax
import jax.numpy as jnp
from jax import lax
from jax.experimental import pallas as pl
from jax.experimental.pallas import tpu as pltpu

F32 = jnp.float32
BF16 = jnp.bfloat16

D_MODEL = 2048
HEAD_DIM = 128
MOBA_HEADS = 8
MOBA_BLOCK = 256
MOBA_TOPK = 3
DIFF_HEADS = 4
DIFF_V_DIM = 256
IN_WIDTH = 6144
N_GROUPS = 4
EXPERTS_PER_GROUP = 8
N_EXPERTS = 32
EXPERT_TOPK = 2
D_EXPERT = 512
RMS_EPS = 1e-6
ALIBI_MAX_BIAS = 8.0
LAMBDA_INIT = 0.8 - 0.6 * float(np.exp(-0.3 * 0))

LANES = 128
QK_SCALE = HEAD_DIM ** -0.5
LOG2E = float(np.log2(np.e))
NEG_BIG = -1e30
M_INIT = -1e38
POS_SPLIT = 64
SLOPE_PIECES = 3
ROW_WORD_CHUNKS = D_MODEL // (2 * LANES)
META_X1, META_X2, META_W1, META_W2, META_RANK1, META_RANK2 = range(6)

IN_TM, IN_TN = 1024, 1024
ATT_TQ = 1024
ATT_TK = 512
ATT_LOOP_TK = 512
KV_UNROLL = 2
HEADS_PER_STEP = 4
DIFF_HEADS_PER_STEP = 1
OUT_TM = 512
MOE_TM = 256
CMB_TT = 256
VMEM_LIMIT = 56 * 1024 * 1024


def _alibi_slope_pieces(n):
    rem = np.exp2(-ALIBI_MAX_BIAS * (np.arange(n, dtype=np.float64) + 1.0) / n) * LOG2E
    pieces = []
    for _ in range(SLOPE_PIECES):
        p = rem.astype(ml_dtypes.bfloat16).astype(np.float64)
        pieces.append(p)
        rem = rem - p
    return np.stack(pieces, axis=1).reshape(-1).astype(np.float32)


def _dot_nt(a, b):
    return lax.dot_general(a, b, (((1,), (1,)), ((), ())), preferred_element_type=F32)


def _dot(a, b):
    return jnp.dot(a, b, preferred_element_type=F32)


def _inproj_kernel(x_ref, nw_ref, cs_ref, w_ref, o_ref, h_ref):
    @pl.when(pl.program_id(1) == 0)
    def _():
        x = x_ref[...]
        r = lax.rsqrt(jnp.mean(x * x, axis=-1, keepdims=True) + RMS_EPS)
        h_ref[...] = (x * r * nw_ref[...]).astype(BF16)

    acc = _dot(h_ref[...], w_ref[...].astype(BF16))
    o_ref[...] = (acc * cs_ref[...]).astype(BF16)


def _in_proj(x2d, norm_w, w_in, col_scale):
    t = x2d.shape[0]
    return pl.pallas_call(
        _inproj_kernel,
        grid=(t // IN_TM, IN_WIDTH // IN_TN),
        in_specs=[
            pl.BlockSpec((IN_TM, D_MODEL), lambda i, j: (i, 0)),
            pl.BlockSpec((1, D_MODEL), lambda i, j: (0, 0)),
            pl.BlockSpec((1, IN_TN), lambda i, j: (0, j)),
            pl.BlockSpec((D_MODEL, IN_TN), lambda i, j: (0, j)),
        ],
        out_specs=pl.BlockSpec((IN_TM, IN_TN), lambda i, j: (i, j)),
        out_shape=jax.ShapeDtypeStruct((t, IN_WIDTH), BF16),
        scratch_shapes=[pltpu.VMEM((IN_TM, D_MODEL), BF16)],
        compiler_params=pltpu.CompilerParams(
            dimension_semantics=("arbitrary", "arbitrary"), vmem_limit_bytes=VMEM_LIMIT),
        name="in_proj",
    )(x2d, norm_w, col_scale, w_in)


def _key_position_features(seq, first_lane):
    pos = lax.broadcasted_iota(jnp.int32, (seq, LANES), 0)
    lane = lax.broadcasted_iota(jnp.int32, (seq, LANES), 1)
    hi = (lane >= first_lane) & (lane < first_lane + SLOPE_PIECES)
    lo = (lane >= first_lane + SLOPE_PIECES) & (lane < first_lane + 2 * SLOPE_PIECES)
    ext = jnp.where(hi, (pos // POS_SPLIT).astype(F32), 0.0)
    return jnp.where(lo, (pos % POS_SPLIT).astype(F32), ext), pos, lane


def _slope_feature_rows(feat_ref, head, width):
    row = lax.broadcasted_iota(jnp.int32, (8, width), 0)
    out = jnp.zeros((8, width), F32)
    for k in range(SLOPE_PIECES):
        c = feat_ref[head * SLOPE_PIECES + k]
        out = jnp.where(row == k, POS_SPLIT * c, out)
        out = jnp.where(row == SLOPE_PIECES + k, c, out)
    return out


def _causal_tile(s):
    row = lax.broadcasted_iota(jnp.int32, s.shape, 0)
    col = lax.broadcasted_iota(jnp.int32, s.shape, 1)
    return jnp.where(col <= row, s, -jnp.inf)


def _online_step(s, m):
    m_new = jnp.maximum(m, jnp.max(s, axis=-1, keepdims=True))
    return m_new, jnp.exp2(m - m_new), jnp.exp2(s - m_new)


def _past_tiles_loop(i, body, init):
    def trip(t, carry):
        for u in range(KV_UNROLL):
            carry = body(t * KV_UNROLL + u, carry)
        return carry

    return lax.fori_loop(0, i * (ATT_TQ // (ATT_LOOP_TK * KV_UNROLL)), trip, init)


def _rows_from(state, start):
    return state if start == 0 else tuple(a[start:] for a in state)


def _rows_replace(state, start, tail):
    if start == 0:
        return tuple(tail)
    return tuple(jnp.concatenate([a[:start], b], axis=0) for a, b in zip(state, tail))


def _moba_kernel(feat_ref, q_ref, k_ref, v_ref, o_ref, kaug_ref, vaug_ref, kmh_ref, kml_ref, qaug_ref, *, seq):
    nb = seq // MOBA_BLOCK
    hp = pl.program_id(1)
    i = pl.program_id(2)

    @pl.when(i == 0)
    def _():
        ext, pos, lane = _key_position_features(seq, nb)
        ext = jnp.where(lane == pos // MOBA_BLOCK, 1.0, ext).astype(BF16)
        ones_col = jnp.where(lane == 0, 1.0, 0.0).astype(BF16)
        for g in range(HEADS_PER_STEP):
            k = k_ref[:, g * HEAD_DIM:(g + 1) * HEAD_DIM]
            kaug_ref[g, :, 0:HEAD_DIM] = k
            kaug_ref[g, :, HEAD_DIM:2 * HEAD_DIM] = ext
            vaug_ref[g, :, 0:HEAD_DIM] = v_ref[:, g * HEAD_DIM:(g + 1) * HEAD_DIM]
            vaug_ref[g, :, HEAD_DIM:2 * HEAD_DIM] = ones_col
            km = jnp.mean(k.astype(F32).reshape(nb, MOBA_BLOCK, HEAD_DIM), axis=1)
            kmh = km.astype(BF16)
            kmh_ref[g] = jnp.zeros((LANES, HEAD_DIM), BF16)
            kml_ref[g] = jnp.zeros((LANES, HEAD_DIM), BF16)
            kmh_ref[g, 0:nb, :] = kmh
            kml_ref[g, 0:nb, :] = (km - kmh.astype(F32)).astype(BF16)

    blk = lax.broadcasted_iota(jnp.int32, (nb, ATT_TQ), 0)
    col = lax.broadcasted_iota(jnp.int32, (nb, ATT_TQ), 1)
    own = i * (ATT_TQ // MOBA_BLOCK) + col // MOBA_BLOCK
    blk_f = blk.astype(F32)
    for g in range(HEADS_PER_STEP):
        q = q_ref[:, g * HEAD_DIM:(g + 1) * HEAD_DIM]
        gate = (_dot_nt(kmh_ref[g], q) + _dot_nt(kml_ref[g], q))[0:nb, :]
        gate = jnp.where(blk < own, gate, -jnp.inf)
        sel = jnp.zeros((nb, ATT_TQ), F32)
        for _ in range(MOBA_TOPK):
            mx = jnp.max(gate, axis=0, keepdims=True)
            idx = jnp.min(jnp.where(gate == mx, blk_f, float(LANES)), axis=0, keepdims=True)
            pick = (blk_f == idx) & (mx > -jnp.inf)
            sel = jnp.where(pick, 1.0, sel)
            gate = jnp.where(pick, -jnp.inf, gate)
        bias = jnp.where((sel > 0.0) | (blk == own), 0.0, NEG_BIG)
        feat = jnp.concatenate(
            [bias, _slope_feature_rows(feat_ref, hp * HEADS_PER_STEP + g, ATT_TQ),
             jnp.zeros((LANES - nb - 8, ATT_TQ), F32)], axis=0)
        qaug_ref[g, :, 0:HEAD_DIM] = q
        qaug_ref[g, :, HEAD_DIM:2 * HEAD_DIM] = feat.T.astype(BF16)

    def tile_step(g, r, nk, state, row0=0, causal=False):
        m, acc = state
        s = _dot_nt(qaug_ref[g, row0:, :], kaug_ref[g, pl.ds(r, nk), :])
        if causal:
            s = _causal_tile(s)
        m, alpha, p = _online_step(s, m)
        return m, alpha * acc + _dot(p.astype(BF16), vaug_ref[g, pl.ds(r, nk), :])

    def body(j, states):
        r = pl.multiple_of(j * ATT_LOOP_TK, ATT_LOOP_TK)
        return tuple(tile_step(g, r, ATT_LOOP_TK, states[g]) for g in range(HEADS_PER_STEP))

    init = tuple((jnp.full((ATT_TQ, 1), M_INIT, F32), jnp.zeros((ATT_TQ, 2 * HEAD_DIM), F32))
                 for _ in range(HEADS_PER_STEP))
    states = list(_past_tiles_loop(i, body, init))
    for d in range(ATT_TQ // ATT_TK):
        r = pl.multiple_of(i * ATT_TQ + d * ATT_TK, ATT_TK)
        for g in range(HEADS_PER_STEP):
            tail = tile_step(g, r, ATT_TK, _rows_from(states[g], d * ATT_TK), row0=d * ATT_TK, causal=True)
            states[g] = _rows_replace(states[g], d * ATT_TK, tail)
    for g in range(HEADS_PER_STEP):
        acc = states[g][1]
        o_ref[:, g * HEAD_DIM:(g + 1) * HEAD_DIM] = (
            acc[:, 0:HEAD_DIM] / acc[:, HEAD_DIM:HEAD_DIM + 1]).astype(BF16)


def _moba(proj, feats, batch, seq):
    nq = seq // ATT_TQ
    width = HEADS_PER_STEP * HEAD_DIM
    q_col, k_col, v_col = 0, MOBA_HEADS // HEADS_PER_STEP, 2 * MOBA_HEADS // HEADS_PER_STEP
    return pl.pallas_call(
        functools.partial(_moba_kernel, seq=seq),
        grid_spec=pltpu.PrefetchScalarGridSpec(
            num_scalar_prefetch=1,
            grid=(batch, MOBA_HEADS // HEADS_PER_STEP, nq),
            in_specs=[
                pl.BlockSpec((ATT_TQ, width), lambda b, h, i, s: (b * nq + i, q_col + h)),
                pl.BlockSpec((seq, width), lambda b, h, i, s: (b, k_col + h)),
                pl.BlockSpec((seq, width), lambda b, h, i, s: (b, v_col + h)),
            ],
            out_specs=pl.BlockSpec((ATT_TQ, width), lambda b, h, i, s: (b * nq + i, h)),
            scratch_shapes=[
                pltpu.VMEM((HEADS_PER_STEP, seq, 2 * HEAD_DIM), BF16),
                pltpu.VMEM((HEADS_PER_STEP, seq, 2 * HEAD_DIM), BF16),
                pltpu.VMEM((HEADS_PER_STEP, LANES, HEAD_DIM), BF16),
                pltpu.VMEM((HEADS_PER_STEP, LANES, HEAD_DIM), BF16),
                pltpu.VMEM((HEADS_PER_STEP, ATT_TQ, 2 * HEAD_DIM), BF16),
            ],
        ),
        out_shape=jax.ShapeDtypeStruct((batch * seq, MOBA_HEADS * HEAD_DIM), BF16),
        compiler_params=pltpu.CompilerParams(
            dimension_semantics=("arbitrary", "arbitrary", "arbitrary"), vmem_limit_bytes=VMEM_LIMIT),
        name="moba",
    )(feats, proj, proj, proj)


def _diff_kernel(feat_ref, lq1_ref, lk1_ref, lq2_ref, lk2_ref, sw_ref,
                 q_ref, k_ref, v_ref, o_ref, kaug_ref, qaug_ref, *, seq):
    hg = pl.program_id(1)
    i = pl.program_id(2)
    n_chains = 2 * DIFF_HEADS_PER_STEP

    @pl.when(i == 0)
    def _():
        ext = _key_position_features(seq, 0)[0].astype(BF16)
        for n in range(n_chains):
            kaug_ref[n, :, 0:HEAD_DIM] = k_ref[:, n * HEAD_DIM:(n + 1) * HEAD_DIM]
            kaug_ref[n, :, HEAD_DIM:2 * HEAD_DIM] = ext

    for n in range(n_chains):
        if n % 2 == 0:
            head = hg * DIFF_HEADS_PER_STEP + n // 2
            feat = jnp.concatenate([_slope_feature_rows(feat_ref, head, ATT_TQ),
                                    jnp.zeros((LANES - 8, ATT_TQ), F32)], axis=0).T.astype(BF16)
        qaug_ref[n, :, 0:HEAD_DIM] = q_ref[:, n * HEAD_DIM:(n + 1) * HEAD_DIM]
        qaug_ref[n, :, HEAD_DIM:2 * HEAD_DIM] = feat

    def tile_step(n, r, nk, state, row0=0, causal=False):
        m, l, acc = state
        s = _dot_nt(qaug_ref[n, row0:, :], kaug_ref[n, pl.ds(r, nk), :])
        if causal:
            s = _causal_tile(s)
        m, alpha, p = _online_step(s, m)
        v = v_ref[pl.ds(r, nk), (n // 2) * DIFF_V_DIM:(n // 2 + 1) * DIFF_V_DIM]
        return (m, alpha * l + jnp.sum(p, axis=-1, keepdims=True), alpha * acc + _dot(p.astype(BF16), v))

    def body(j, states):
        r = pl.multiple_of(j * ATT_LOOP_TK, ATT_LOOP_TK)
        return tuple(tile_step(n, r, ATT_LOOP_TK, states[n]) for n in range(n_chains))

    init = tuple((jnp.full((ATT_TQ, 1), M_INIT, F32), jnp.zeros((ATT_TQ, 1), F32),
                  jnp.zeros((ATT_TQ, DIFF_V_DIM), F32)) for _ in range(n_chains))
    states = list(_past_tiles_loop(i, body, init))
    for d in range(ATT_TQ // ATT_TK):
        r = pl.multiple_of(i * ATT_TQ + d * ATT_TK, ATT_TK)
        for n in range(n_chains):
            tail = tile_step(n, r, ATT_TK, _rows_from(states[n], d * ATT_TK), row0=d * ATT_TK, causal=True)
            states[n] = _rows_replace(states[n], d * ATT_TK, tail)

    lam = (jnp.exp(jnp.sum(lq1_ref[...] * lk1_ref[...], axis=-1, keepdims=True))
           - jnp.exp(jnp.sum(lq2_ref[...] * lk2_ref[...], axis=-1, keepdims=True))
           + LAMBDA_INIT)
    for hh in range(DIFF_HEADS_PER_STEP):
        (_, l1, acc1), (_, l2, acc2) = states[2 * hh], states[2 * hh + 1]
        o = acc1 / l1 - lam * (acc2 / l2)
        y = o * lax.rsqrt(jnp.mean(o * o, axis=-1, keepdims=True) + RMS_EPS)
        o_ref[:, hh * DIFF_V_DIM:(hh + 1) * DIFF_V_DIM] = ((y * sw_ref[...]) * (1.0 - LAMBDA_INIT)).astype(BF16)


def _diff(proj, feats, lq1, lk1, lq2, lk2, subln_w, batch, seq):
    nq = seq // ATT_TQ
    width = DIFF_HEADS_PER_STEP * 2 * HEAD_DIM
    groups = DIFF_HEADS // DIFF_HEADS_PER_STEP
    q_col = 3 * MOBA_HEADS * HEAD_DIM // width
    k_col = q_col + groups
    v_col = k_col + groups
    vec = lambda n: pl.BlockSpec((1, n), lambda b, h, i, s: (0, 0))
    return pl.pallas_call(
        functools.partial(_diff_kernel, seq=seq),
        grid_spec=pltpu.PrefetchScalarGridSpec(
            num_scalar_prefetch=1,
            grid=(batch, groups, nq),
            in_specs=[
                vec(HEAD_DIM), vec(HEAD_DIM), vec(HEAD_DIM), vec(HEAD_DIM), vec(DIFF_V_DIM),
                pl.BlockSpec((ATT_TQ, width), lambda b, h, i, s: (b * nq + i, q_col + h)),
                pl.BlockSpec((seq, width), lambda b, h, i, s: (b, k_col + h)),
                pl.BlockSpec((seq, width), lambda b, h, i, s: (b, v_col + h)),
            ],
            out_specs=pl.BlockSpec((ATT_TQ, width), lambda b, h, i, s: (b * nq + i, h)),
            scratch_shapes=[
                pltpu.VMEM((2 * DIFF_HEADS_PER_STEP, seq, 2 * HEAD_DIM), BF16),
                pltpu.VMEM((2 * DIFF_HEADS_PER_STEP, ATT_TQ, 2 * HEAD_DIM), BF16),
            ],
        ),
        out_shape=jax.ShapeDtypeStruct((batch * seq, DIFF_HEADS * DIFF_V_DIM), BF16),
        compiler_params=pltpu.CompilerParams(
            dimension_semantics=("arbitrary", "arbitrary", "arbitrary"), vmem_limit_bytes=VMEM_LIMIT),
        name="diff_attn",
    )(feats, lq1, lk1, lq2, lk2, subln_w, proj, proj, proj)


def _route_tile(lg, run, tri):
    lane = lax.broadcasted_iota(jnp.int32, lg.shape, 1).astype(F32)
    rmax = lambda v: jnp.max(v, axis=-1, keepdims=True)
    first = lambda v, m: jnp.min(jnp.where(v == m, lane, float(LANES)), axis=-1, keepdims=True)

    gl = jnp.where(lane < N_GROUPS, lg, -jnp.inf)
    gmax = rmax(gl)
    group = first(gl, gmax)
    p_group = 1.0 / jnp.sum(jnp.exp(gl - gmax), axis=-1, keepdims=True)

    lo = N_GROUPS + EXPERTS_PER_GROUP * group
    el = jnp.where((lane >= lo) & (lane < lo + EXPERTS_PER_GROUP), lg, -jnp.inf)
    e1 = rmax(el)
    lane1 = first(el, e1)
    el = jnp.where(lane == lane1, -jnp.inf, el)
    e2 = rmax(el)
    lane2 = first(el, e2)
    t2 = jnp.exp(e2 - e1)
    w1 = p_group / (1.0 + t2)
    w2 = p_group * t2 / (1.0 + t2)
    x1 = lane1 - N_GROUPS
    x2 = lane2 - N_GROUPS

    hit1 = lane == x1
    hit2 = lane == x2
    onehot = jnp.where(hit1 | hit2, 1.0, 0.0)
    before = _dot(tri, onehot.astype(BF16)) + run
    rank1 = jnp.sum(jnp.where(hit1, before, 0.0), axis=-1, keepdims=True)
    rank2 = jnp.sum(jnp.where(hit2, before, 0.0), axis=-1, keepdims=True)

    meta = jnp.zeros(lg.shape, F32)
    for k, v in ((META_X1, x1), (META_X2, x2), (META_W1, w1), (META_W2, w2),
                 (META_RANK1, rank1), (META_RANK2, rank2)):
        meta = jnp.where(lane == k, v, meta)
    return meta, run + jnp.sum(onehot, axis=0, keepdims=True)


def _outproj_kernel(x_ref, oa_ref, ob_ref, wa_ref, wb_ref, nw_ref, wrh_ref, wrl_ref, br_ref,
                    x2_ref, hp_ref, meta_ref, cnt_ref, run_ref, tri_ref):
    @pl.when(pl.program_id(0) == 0)
    def _():
        row = lax.broadcasted_iota(jnp.int32, tri_ref.shape, 0)
        col = lax.broadcasted_iota(jnp.int32, tri_ref.shape, 1)
        tri_ref[...] = jnp.where(col < row, 1.0, 0.0).astype(BF16)
        run_ref[...] = jnp.zeros(run_ref.shape, F32)

    x2 = x_ref[...] + _dot(oa_ref[...], wa_ref[...]) + _dot(ob_ref[...], wb_ref[...])
    x2_ref[...] = x2
    r = lax.rsqrt(jnp.mean(x2 * x2, axis=-1, keepdims=True) + RMS_EPS)
    h = x2 * r * nw_ref[...]
    hh = h.astype(BF16)
    hl = (h - hh.astype(F32)).astype(BF16)
    lg = (_dot(hh, wrh_ref[...]) + _dot(hl, wrh_ref[...]) + _dot(hh, wrl_ref[...]) + br_ref[...])
    meta, run = _route_tile(lg, run_ref[...], tri_ref[...])
    meta_ref[...] = meta
    run_ref[...] = run
    cnt_ref[...] = run
    bits = pltpu.bitcast(hh.astype(F32), jnp.uint32)
    for c in range(ROW_WORD_CHUNKS):
        lo = bits[:, 2 * c * LANES:(2 * c + 1) * LANES]
        hi = bits[:, (2 * c + 1) * LANES:(2 * c + 2) * LANES]
        hp_ref[pl.ds(c, OUT_TM, stride=ROW_WORD_CHUNKS), :] = (lo >> 16) | (hi & jnp.uint32(0xFFFF0000))


def _out_proj(x2d, out_a, out_b, wa, wb, norm_w, wr_hi, wr_lo, b_r):
    t = x2d.shape[0]
    half = D_MODEL // 2
    row = lambda n: pl.BlockSpec((OUT_TM, n), lambda i: (i, 0))
    full = lambda r, c: pl.BlockSpec((r, c), lambda i: (0, 0))
    return pl.pallas_call(
        _outproj_kernel,
        grid=(t // OUT_TM,),
        in_specs=[row(D_MODEL), row(half), row(half), full(half, D_MODEL), full(half, D_MODEL),
                  full(1, D_MODEL), full(D_MODEL, LANES), full(D_MODEL, LANES), full(1, LANES)],
        out_specs=[row(D_MODEL), pl.BlockSpec((OUT_TM * ROW_WORD_CHUNKS, LANES), lambda i: (i, 0)), row(LANES),
                   full(1, LANES)],
        out_shape=[jax.ShapeDtypeStruct((t, D_MODEL), F32),
                   jax.ShapeDtypeStruct((t * ROW_WORD_CHUNKS, LANES), jnp.uint32),
                   jax.ShapeDtypeStruct((t, LANES), F32),
                   jax.ShapeDtypeStruct((1, LANES), F32)],
        scratch_shapes=[pltpu.VMEM((1, LANES), F32), pltpu.VMEM((OUT_TM, OUT_TM), BF16)],
        compiler_params=pltpu.CompilerParams(
            dimension_semantics=("arbitrary",), vmem_limit_bytes=VMEM_LIMIT),
        name="out_proj",
    )(x2d, out_a, out_b, wa, wb, norm_w, wr_hi, wr_lo, b_r)


GATHER_UNROLL = 16
CLEAR_UNROLL = 32


def _dispatch_kernel(pos_ref, nt_ref, hp_ref, xs_ref, src_ref):
    t = pl.program_id(0)

    @pl.when(t == 0)
    def _():
        def clear(p, _):
            src_ref[p] = 0
            return 0
        lax.fori_loop(0, src_ref.shape[0], clear, 0, unroll=CLEAR_UNROLL)

        def place(c, _):
            rows = [pos_ref[c * GATHER_UNROLL + u] for u in range(GATHER_UNROLL)]
            for u in range(GATHER_UNROLL):
                src_ref[rows[u]] = c * (GATHER_UNROLL // EXPERT_TOPK) + u // EXPERT_TOPK
            return 0
        lax.fori_loop(0, pos_ref.shape[0] // GATHER_UNROLL, place, 0)

    @pl.when(t < nt_ref[0])
    def _():
        def gather(c, _):
            toks = [src_ref[t * MOE_TM + c * GATHER_UNROLL + u] for u in range(GATHER_UNROLL)]
            for u in range(GATHER_UNROLL):
                xs_ref[c * GATHER_UNROLL + u] = hp_ref[toks[u]]
            return 0
        lax.fori_loop(0, MOE_TM // GATHER_UNROLL, gather, 0)

    @pl.when(t >= nt_ref[0])
    def _():
        xs_ref[...] = jnp.zeros(xs_ref.shape, xs_ref.dtype)


def _dispatch(pos, n_tiles, hp, npad):
    n_tokens = hp.shape[0] // ROW_WORD_CHUNKS
    hp3 = hp.reshape(n_tokens, ROW_WORD_CHUNKS, LANES)
    xs = pl.pallas_call(
        _dispatch_kernel,
        grid_spec=pltpu.PrefetchScalarGridSpec(
            num_scalar_prefetch=2,
            grid=(npad // MOE_TM,),
            in_specs=[pl.BlockSpec(memory_space=pltpu.VMEM)],
            out_specs=pl.BlockSpec((MOE_TM, ROW_WORD_CHUNKS, LANES), lambda t, s, n: (t, 0, 0)),
            scratch_shapes=[pltpu.SMEM((npad,), jnp.int32)],
        ),
        out_shape=jax.ShapeDtypeStruct((npad, ROW_WORD_CHUNKS, LANES), hp.dtype),
        compiler_params=pltpu.CompilerParams(
            dimension_semantics=("arbitrary",), vmem_limit_bytes=VMEM_LIMIT),
        name="dispatch",
    )(pos, n_tiles, hp3)
    return xs.reshape(npad * ROW_WORD_CHUNKS, LANES)


W_SLOTS = 2


def _experts_kernel(tk_ref, eseq_ref, nt_ref, xs_ref, wg_hbm, wu_hbm, wd_hbm, y_ref,
                    wg_buf, wu_buf, wd_buf, wgb_ref, wub_ref, wdb_ref, x_ref, sem):
    t = pl.program_id(0)
    n_used = nt_ref[1]
    hbm_bufs = ((wg_hbm, wg_buf), (wu_hbm, wu_buf), (wd_hbm, wd_buf))

    def weight_copies(k, slot):
        e = eseq_ref[k]
        return [pltpu.make_async_copy(hbm.at[e], buf.at[slot], sem.at[slot, n])
                for n, (hbm, buf) in enumerate(hbm_bufs)]

    @pl.when(t < nt_ref[0])
    def _():
        k = tk_ref[t]

        @pl.when((t == 0) | (k != tk_ref[jnp.maximum(t - 1, 0)]))
        def _():
            @pl.when(k == 0)
            def _():
                for s in range(W_SLOTS):
                    @pl.when(s < n_used)
                    def _():
                        for c in weight_copies(s, s):
                            c.start()

            slot = k % W_SLOTS
            for c in weight_copies(k, slot):
                c.wait()
            wgb_ref[...] = wg_buf[slot].astype(BF16)
            wub_ref[...] = wu_buf[slot].astype(BF16)
            wdb_ref[...] = wd_buf[slot].astype(BF16)

            @pl.when(k + W_SLOTS < n_used)
            def _():
                for c in weight_copies(k + W_SLOTS, slot):
                    c.start()

        for c in range(ROW_WORD_CHUNKS):
            u32 = xs_ref[pl.ds(c, MOE_TM, stride=ROW_WORD_CHUNKS), :]
            x_ref[:, 2 * c * LANES:(2 * c + 1) * LANES] = pltpu.bitcast(u32 << 16, F32).astype(BF16)
            x_ref[:, (2 * c + 1) * LANES:(2 * c + 2) * LANES] = (
                pltpu.bitcast(u32 & jnp.uint32(0xFFFF0000), F32).astype(BF16))
        x = x_ref[...]
        a = _dot(x, wgb_ref[...])
        u = _dot(x, wub_ref[...])
        act = (a * jax.nn.sigmoid(a)) * u
        y_ref[...] = _dot(act.astype(BF16), wdb_ref[...])

    @pl.when(t >= nt_ref[0])
    def _():
        y_ref[...] = jnp.zeros(y_ref.shape, y_ref.dtype)


def _experts(tile_k, expert_seq, n_used, xs, w_gate, w_up, w_down):
    npad = xs.shape[0] // ROW_WORD_CHUNKS
    last = lambda t, nt: jnp.minimum(t, jnp.maximum(nt[0] - 1, 0))
    hbm = pl.BlockSpec(memory_space=pl.ANY)
    return pl.pallas_call(
        _experts_kernel,
        grid_spec=pltpu.PrefetchScalarGridSpec(
            num_scalar_prefetch=3,
            grid=(npad // MOE_TM,),
            in_specs=[
                pl.BlockSpec((MOE_TM * ROW_WORD_CHUNKS, LANES), lambda t, tk, es, nt: (last(t, nt), 0)),
                hbm, hbm, hbm,
            ],
            out_specs=pl.BlockSpec((MOE_TM, D_MODEL), lambda t, tk, es, nt: (t, 0)),
            scratch_shapes=[
                pltpu.VMEM((W_SLOTS, D_MODEL, D_EXPERT), F32),
                pltpu.VMEM((W_SLOTS, D_MODEL, D_EXPERT), F32),
                pltpu.VMEM((W_SLOTS, D_EXPERT, D_MODEL), F32),
                pltpu.VMEM((D_MODEL, D_EXPERT), BF16),
                pltpu.VMEM((D_MODEL, D_EXPERT), BF16),
                pltpu.VMEM((D_EXPERT, D_MODEL), BF16),
                pltpu.VMEM((MOE_TM, D_MODEL), BF16),
                pltpu.SemaphoreType.DMA((W_SLOTS, 3)),
            ],
        ),
        out_shape=jax.ShapeDtypeStruct((npad, D_MODEL), F32),
        compiler_params=pltpu.CompilerParams(
            dimension_semantics=("arbitrary",), vmem_limit_bytes=VMEM_LIMIT),
        name="experts",
    )(tile_k, expert_seq, n_used, xs, w_gate, w_up, w_down)


CMB_UNROLL = 8
CMB_SLOTS = 2


def _combine_kernel(pos_ref, x2_ref, meta_ref, nw_ref, y_ref, o_ref, buf_ref, sem):
    i = pl.program_id(0)

    def row_copy(p, slot, k, r):
        return pltpu.make_async_copy(y_ref.at[pl.ds(p, 1), :], buf_ref.at[slot, k, pl.ds(r, 1), :],
                                     sem.at[slot])

    def issue(tile, slot):
        def trip(c, _):
            a0 = (tile * CMB_TT + c * CMB_UNROLL) * EXPERT_TOPK
            rows = [pos_ref[a0 + u] for u in range(CMB_UNROLL * EXPERT_TOPK)]
            for u in range(CMB_UNROLL * EXPERT_TOPK):
                row_copy(rows[u], slot, u % EXPERT_TOPK, c * CMB_UNROLL + u // EXPERT_TOPK).start(priority=u % 2)
            return 0
        lax.fori_loop(0, CMB_TT // CMB_UNROLL, trip, 0)

    @pl.when(i == 0)
    def _():
        issue(0, 0)

    @pl.when(i + 1 < pl.num_programs(0))
    def _():
        issue(i + 1, (i + 1) % CMB_SLOTS)

    slot = i % CMB_SLOTS
    for k in range(EXPERT_TOPK):
        pltpu.make_async_copy(y_ref.at[pl.ds(0, CMB_TT), :], buf_ref.at[slot, k], sem.at[slot]).wait()

    meta = meta_ref[...]
    x3 = (x2_ref[...] + meta[:, META_W1:META_W1 + 1] * buf_ref[slot, 0]
          + meta[:, META_W2:META_W2 + 1] * buf_ref[slot, 1])
    r = lax.rsqrt(jnp.mean(x3 * x3, axis=-1, keepdims=True) + RMS_EPS)
    o_ref[...] = x3 * r * nw_ref[...]


def _combine(pos, x2, meta, norm_w, y):
    t = x2.shape[0]
    return pl.pallas_call(
        _combine_kernel,
        grid_spec=pltpu.PrefetchScalarGridSpec(
            num_scalar_prefetch=1,
            grid=(t // CMB_TT,),
            in_specs=[
                pl.BlockSpec((CMB_TT, D_MODEL), lambda i, p: (i, 0)),
                pl.BlockSpec((CMB_TT, LANES), lambda i, p: (i, 0)),
                pl.BlockSpec((1, D_MODEL), lambda i, p: (0, 0)),
                pl.BlockSpec(memory_space=pl.ANY),
            ],
            out_specs=pl.BlockSpec((CMB_TT, D_MODEL), lambda i, p: (i, 0)),
            scratch_shapes=[
                pltpu.VMEM((CMB_SLOTS, EXPERT_TOPK, CMB_TT, D_MODEL), F32),
                pltpu.SemaphoreType.DMA((CMB_SLOTS,)),
            ],
        ),
        out_shape=jax.ShapeDtypeStruct((t, D_MODEL), F32),
        compiler_params=pltpu.CompilerParams(
            dimension_semantics=("arbitrary",), vmem_limit_bytes=VMEM_LIMIT),
        name="combine",
    )(pos, x2, meta, norm_w, y)


def _layout(meta, counts, n_tokens):
    npad = n_tokens * EXPERT_TOPK + N_EXPERTS * MOE_TM
    counts = counts[0, :N_EXPERTS].astype(jnp.int32)
    padded = ((counts + MOE_TM - 1) // MOE_TM) * MOE_TM
    ends = jnp.cumsum(padded)
    starts = ends - padded
    expert = meta[:, META_X1:META_X2 + 1].astype(jnp.int32)
    rank = meta[:, META_RANK1:META_RANK2 + 1].astype(jnp.int32)
    hit = expert[:, :, None] == jnp.arange(N_EXPERTS, dtype=jnp.int32)
    pos = jnp.sum(jnp.where(hit, starts, 0), axis=-1) + rank
    tile_start = jnp.arange(npad // MOE_TM, dtype=jnp.int32) * MOE_TM
    tile_expert = jnp.minimum(jnp.sum(ends[None, :] <= tile_start[:, None], axis=1), N_EXPERTS - 1)
    used = counts > 0
    k_of_expert = jnp.cumsum(used.astype(jnp.int32)) - 1
    ids = jnp.arange(N_EXPERTS, dtype=jnp.int32)
    expert_seq = jnp.sum(jnp.where(used[None, :] & (k_of_expert[None, :] == ids[:, None]), ids[None, :], 0), axis=1)
    tile_k = jnp.sum(jnp.where(tile_expert[:, None] == ids[None, :], k_of_expert[None, :], 0), axis=1)
    n_tiles = (ends[-1] // MOE_TM).astype(jnp.int32)
    n_used = jnp.stack([n_tiles, jnp.sum(used).astype(jnp.int32)])
    return (pos.reshape(-1).astype(jnp.int32), tile_k.astype(jnp.int32), expert_seq.astype(jnp.int32),
            n_tiles.reshape(1), n_used, npad)


def kernel(x, norm_mix_w, w_in, lambda_q1, lambda_k1, lambda_q2, lambda_k2, diff_subln_w, w_out, norm_ffn_w,
           w_router_group, b_router_group, w_router_expert, b_router_expert, w_gate, w_up, w_down, norm_final_w):
    batch, seq, _ = x.shape
    assert seq % ATT_TQ == 0 and seq // MOBA_BLOCK + 8 <= LANES
    assert w_in.shape[0] == 1, "single-layer block"
    n_tokens = batch * seq
    x2d = x.reshape(n_tokens, D_MODEL)

    col = np.arange(IN_WIDTH)
    is_q = (col < MOBA_HEADS * HEAD_DIM) | ((col >= 3 * MOBA_HEADS * HEAD_DIM) & (col < 4 * MOBA_HEADS * HEAD_DIM))
    col_scale = jnp.asarray(np.where(is_q, QK_SCALE * LOG2E, 1.0).astype(np.float32))[None, :]

    proj = _in_proj(x2d, norm_mix_w[0][None, :], w_in[0], col_scale)
    out_a = _moba(proj, jnp.asarray(_alibi_slope_pieces(MOBA_HEADS)), batch, seq)
    out_b = _diff(proj, jnp.asarray(_alibi_slope_pieces(DIFF_HEADS)), lambda_q1, lambda_k1, lambda_q2, lambda_k2,
                  diff_subln_w, batch, seq)

    half = D_MODEL // 2
    w_o = w_out[0].astype(BF16)
    w_r = jnp.concatenate([w_router_group[0], w_router_expert[0]], axis=1)
    w_r = jnp.pad(w_r, ((0, 0), (0, LANES - w_r.shape[1])))
    wr_hi = w_r.astype(BF16)
    wr_lo = (w_r - wr_hi.astype(F32)).astype(BF16)
    b_r = jnp.pad(jnp.concatenate([b_router_group[0], b_router_expert[0]]), (0, LANES - N_GROUPS - N_EXPERTS))[None, :]
    x2, hp, meta, counts = _out_proj(x2d, out_a, out_b, w_o[:half], w_o[half:], norm_ffn_w[0][None, :],
                                     wr_hi, wr_lo, b_r)

    pos, tile_k, expert_seq, n_tiles, n_used, npad = _layout(meta, counts, n_tokens)
    xs = _dispatch(pos, n_tiles, hp, npad)
    y = _experts(tile_k, expert_seq, n_used, xs, w_gate[0], w_up[0], w_down[0])
    out = _combine(pos, x2, meta, norm_final_w[None, :], y)
    return out.reshape(batch, seq, D_MODEL)
```

```python
import functools

import ml_dtypes
import numpy as np
import jax
import jax.numpy as jnp
from jax import lax
from jax.experimental import pallas as pl
from jax.experimental.pallas import tpu as pltpu

F32 = jnp.float32
BF16 = jnp.bfloat16

D_MODEL = 2048
HEAD_DIM = 128
MOBA_HEADS = 8
MOBA_BLOCK = 256
MOBA_TOPK = 3
DIFF_HEADS = 4
DIFF_V_DIM = 256
IN_WIDTH = 6144
N_GROUPS = 4
EXPERTS_PER_GROUP = 8
N_EXPERTS = 32
EXPERT_TOPK = 2
D_EXPERT = 512
RMS_EPS = 1e-6
ALIBI_MAX_BIAS = 8.0
LAMBDA_INIT = 0.8 - 0.6 * float(np.exp(-0.3 * 0))

LANES = 128
QK_SCALE = HEAD_DIM ** -0.5
LOG2E = float(np.log2(np.e))
NEG_BIG = -1e30
M_INIT = -1e38
POS_SPLIT = 64
SLOPE_PIECES = 3
ROW_WORD_CHUNKS = D_MODEL // (2 * LANES)
META_X1, META_X2, META_W1, META_W2, META_RANK1, META_RANK2 = range(6)

IN_TM, IN_TN = 1024, 1024
ATT_TQ = 1024
ATT_TK = 512
ATT_LOOP_TK = 512
KV_UNROLL = 2
HEADS_PER_STEP = 4
DIFF_HEADS_PER_STEP = 1
OUT_TM = 512
MOE_TM = 256
CMB_TT = 256
VMEM_LIMIT = 56 * 1024 * 1024


def _alibi_slope_pieces(n):
    rem = np.exp2(-ALIBI_MAX_BIAS * (np.arange(n, dtype=np.float64) + 1.0) / n) * LOG2E
    pieces = []
    for _ in range(SLOPE_PIECES):
        p = rem.astype(ml_dtypes.bfloat16).astype(np.float64)
        pieces.append(p)
        rem = rem - p
    return np.stack(pieces, axis=1).reshape(-1).astype(np.float32)


def _dot_nt(a, b):
    return lax.dot_general(a, b, (((1,), (1,)), ((), ())), preferred_element_type=F32)


def _dot(a, b):
    return jnp.dot(a, b, preferred_element_type=F32)


def _inproj_kernel(x_ref, nw_ref, cs_ref, w_ref, o_ref, h_ref):
    @pl.when(pl.program_id(1) == 0)
    def _():
        x = x_ref[...]
        r = lax.rsqrt(jnp.mean(x * x, axis=-1, keepdims=True) + RMS_EPS)
        h_ref[...] = (x * r * nw_ref[...]).astype(BF16)

    acc = _dot(h_ref[...], w_ref[...].astype(BF16))
    o_ref[...] = (acc * cs_ref[...]).astype(BF16)


def _in_proj(x2d, norm_w, w_in, col_scale):
    t = x2d.shape[0]
    return pl.pallas_call(
        _inproj_kernel,
        grid=(t // IN_TM, IN_WIDTH // IN_TN),
        in_specs=[
            pl.BlockSpec((IN_TM, D_MODEL), lambda i, j: (i, 0)),
            pl.BlockSpec((1, D_MODEL), lambda i, j: (0, 0)),
            pl.BlockSpec((1, IN_TN), lambda i, j: (0, j)),
            pl.BlockSpec((D_MODEL, IN_TN), lambda i, j: (0, j)),
        ],
        out_specs=pl.BlockSpec((IN_TM, IN_TN), lambda i, j: (i, j)),
        out_shape=jax.ShapeDtypeStruct((t, IN_WIDTH), BF16),
        scratch_shapes=[pltpu.VMEM((IN_TM, D_MODEL), BF16)],
        compiler_params=pltpu.CompilerParams(
            dimension_semantics=("arbitrary", "arbitrary"), vmem_limit_bytes=VMEM_LIMIT),
        name="in_proj",
    )(x2d, norm_w, col_scale, w_in)


def _key_position_features(seq, first_lane):
    pos = lax.broadcasted_iota(jnp.int32, (seq, LANES), 0)
    lane = lax.broadcasted_iota(jnp.int32, (seq, LANES), 1)
    hi = (lane >= first_lane) & (lane < first_lane + SLOPE_PIECES)
    lo = (lane >= first_lane + SLOPE_PIECES) & (lane < first_lane + 2 * SLOPE_PIECES)
    ext = jnp.where(hi, (pos // POS_SPLIT).astype(F32), 0.0)
    return jnp.where(lo, (pos % POS_SPLIT).astype(F32), ext), pos, lane


def _slope_feature_rows(feat_ref, head, width):
    row = lax.broadcasted_iota(jnp.int32, (8, width), 0)
    out = jnp.zeros((8, width), F32)
    for k in range(SLOPE_PIECES):
        c = feat_ref[head * SLOPE_PIECES + k]
        out = jnp.where(row == k, POS_SPLIT * c, out)
        out = jnp.where(row == SLOPE_PIECES + k, c, out)
    return out


def _causal_tile(s):
    row = lax.broadcasted_iota(jnp.int32, s.shape, 0)
    col = lax.broadcasted_iota(jnp.int32, s.shape, 1)
    return jnp.where(col <= row, s, -jnp.inf)


def _online_step(s, m):
    m_new = jnp.maximum(m, jnp.max(s, axis=-1, keepdims=True))
    return m_new, jnp.exp2(m - m_new), jnp.exp2(s - m_new)


def _past_tiles_loop(i, body, init):
    def trip(t, carry):
        for u in range(KV_UNROLL):
            carry = body(t * KV_UNROLL + u, carry)
        return carry

    return lax.fori_loop(0, i * (ATT_TQ // (ATT_LOOP_TK * KV_UNROLL)), trip, init)


def _rows_from(state, start):
    return state if start == 0 else tuple(a[start:] for a in state)


def _rows_replace(state, start, tail):
    if start == 0:
        return tuple(tail)
    return tuple(jnp.concatenate([a[:start], b], axis=0) for a, b in zip(state, tail))


def _moba_kernel(feat_ref, q_ref, k_ref, v_ref, o_ref, kaug_ref, vaug_ref, kmh_ref, kml_ref, qaug_ref, *, seq):
    nb = seq // MOBA_BLOCK
    hp = pl.program_id(1)
    i = pl.program_id(2)

    @pl.when(i == 0)
    def _():
        ext, pos, lane = _key_position_features(seq, nb)
        ext = jnp.where(lane == pos // MOBA_BLOCK, 1.0, ext).astype(BF16)
        ones_col = jnp.where(lane == 0, 1.0, 0.0).astype(BF16)
        for g in range(HEADS_PER_STEP):
            k = k_ref[:, g * HEAD_DIM:(g + 1) * HEAD_DIM]
            kaug_ref[g, :, 0:HEAD_DIM] = k
            kaug_ref[g, :, HEAD_DIM:2 * HEAD_DIM] = ext
            vaug_ref[g, :, 0:HEAD_DIM] = v_ref[:, g * HEAD_DIM:(g + 1) * HEAD_DIM]
            vaug_ref[g, :, HEAD_DIM:2 * HEAD_DIM] = ones_col
            km = jnp.mean(k.astype(F32).reshape(nb, MOBA_BLOCK, HEAD_DIM), axis=1)
            kmh = km.astype(BF16)
            kmh_ref[g] = jnp.zeros((LANES, HEAD_DIM), BF16)
            kml_ref[g] = jnp.zeros((LANES, HEAD_DIM), BF16)
            kmh_ref[g, 0:nb, :] = kmh
            kml_ref[g, 0:nb, :] = (km - kmh.astype(F32)).astype(BF16)

    blk = lax.broadcasted_iota(jnp.int32, (nb, ATT_TQ), 0)
    col = lax.broadcasted_iota(jnp.int32, (nb, ATT_TQ), 1)
    own = i * (ATT_TQ // MOBA_BLOCK) + col // MOBA_BLOCK
    blk_f = blk.astype(F32)
    for g in range(HEADS_PER_STEP):
        q = q_ref[:, g * HEAD_DIM:(g + 1) * HEAD_DIM]
        gate = (_dot_nt(kmh_ref[g], q) + _dot_nt(kml_ref[g], q))[0:nb, :]
        gate = jnp.where(blk < own, gate, -jnp.inf)
        sel = jnp.zeros((nb, ATT_TQ), F32)
        for _ in range(MOBA_TOPK):
            mx = jnp.max(gate, axis=0, keepdims=True)
            idx = jnp.min(jnp.where(gate == mx, blk_f, float(LANES)), axis=0, keepdims=True)
            pick = (blk_f == idx) & (mx > -jnp.inf)
            sel = jnp.where(pick, 1.0, sel)
            gate = jnp.where(pick, -jnp.inf, gate)
        bias = jnp.where((sel > 0.0) | (blk == own), 0.0, NEG_BIG)
        feat = jnp.concatenate(
            [bias, _slope_feature_rows(feat_ref, hp * HEADS_PER_STEP + g, ATT_TQ),
             jnp.zeros((LANES - nb - 8, ATT_TQ), F32)], axis=0)
        qaug_ref[g, :, 0:HEAD_DIM] = q
        qaug_ref[g, :, HEAD_DIM:2 * HEAD_DIM] = feat.T.astype(BF16)

    def tile_step(g, r, nk, state, row0=0, causal=False):
        m, acc = state
        s = _dot_nt(qaug_ref[g, row0:, :], kaug_ref[g, pl.ds(r, nk), :])
        if causal:
            s = _causal_tile(s)
        m, alpha, p = _online_step(s, m)
        return m, alpha * acc + _dot(p.astype(BF16), vaug_ref[g, pl.ds(r, nk), :])

    def body(j, states):
        r = pl.multiple_of(j * ATT_LOOP_TK, ATT_LOOP_TK)
        return tuple(tile_step(g, r, ATT_LOOP_TK, states[g]) for g in range(HEADS_PER_STEP))

    init = tuple((jnp.full((ATT_TQ, 1), M_INIT, F32), jnp.zeros((ATT_TQ, 2 * HEAD_DIM), F32))
                 for _ in range(HEADS_PER_STEP))
    states = list(_past_tiles_loop(i, body, init))
    for d in range(ATT_TQ // ATT_TK):
        r = pl.multiple_of(i * ATT_TQ + d * ATT_TK, ATT_TK)
        for g in range(HEADS_PER_STEP):
            tail = tile_step(g, r, ATT_TK, _rows_from(states[g], d * ATT_TK), row0=d * ATT_TK, causal=True)
            states[g] = _rows_replace(states[g], d * ATT_TK, tail)
    for g in range(HEADS_PER_STEP):
        acc = states[g][1]
        o_ref[:, g * HEAD_DIM:(g + 1) * HEAD_DIM] = (
            acc[:, 0:HEAD_DIM] / acc[:, HEAD_DIM:HEAD_DIM + 1]).astype(BF16)


def _moba(proj, feats, batch, seq):
    nq = seq // ATT_TQ
    width = HEADS_PER_STEP * HEAD_DIM
    q_col, k_col, v_col = 0, MOBA_HEADS // HEADS_PER_STEP, 2 * MOBA_HEADS // HEADS_PER_STEP
    return pl.pallas_call(
        functools.partial(_moba_kernel, seq=seq),
        grid_spec=pltpu.PrefetchScalarGridSpec(
            num_scalar_prefetch=1,
            grid=(batch, MOBA_HEADS // HEADS_PER_STEP, nq),
            in_specs=[
                pl.BlockSpec((ATT_TQ, width), lambda b, h, i, s: (b * nq + i, q_col + h)),
                pl.BlockSpec((seq, width), lambda b, h, i, s: (b, k_col + h)),
                pl.BlockSpec((seq, width), lambda b, h, i, s: (b, v_col + h)),
            ],
            out_specs=pl.BlockSpec((ATT_TQ, width), lambda b, h, i, s: (b * nq + i, h)),
            scratch_shapes=[
                pltpu.VMEM((HEADS_PER_STEP, seq, 2 * HEAD_DIM), BF16),
                pltpu.VMEM((HEADS_PER_STEP, seq, 2 * HEAD_DIM), BF16),
                pltpu.VMEM((HEADS_PER_STEP, LANES, HEAD_DIM), BF16),
                pltpu.VMEM((HEADS_PER_STEP, LANES, HEAD_DIM), BF16),
                pltpu.VMEM((HEADS_PER_STEP, ATT_TQ, 2 * HEAD_DIM), BF16),
            ],
        ),
        out_shape=jax.ShapeDtypeStruct((batch * seq, MOBA_HEADS * HEAD_DIM), BF16),
        compiler_params=pltpu.CompilerParams(
            dimension_semantics=("arbitrary", "arbitrary", "arbitrary"), vmem_limit_bytes=VMEM_LIMIT),
        name="moba",
    )(feats, proj, proj, proj)


def _diff_kernel(feat_ref, lq1_ref, lk1_ref, lq2_ref, lk2_ref, sw_ref,
                 q_ref, k_ref, v_ref, o_ref, kaug_ref, qaug_ref, *, seq):
    hg = pl.program_id(1)
    i = pl.program_id(2)
    n_chains = 2 * DIFF_HEADS_PER_STEP

    @pl.when(i == 0)
    def _():
        ext = _key_position_features(seq, 0)[0].astype(BF16)
        for n in range(n_chains):
            kaug_ref[n, :, 0:HEAD_DIM] = k_ref[:, n * HEAD_DIM:(n + 1) * HEAD_DIM]
            kaug_ref[n, :, HEAD_DIM:2 * HEAD_DIM] = ext

    for n in range(n_chains):
        if n % 2 == 0:
            head = hg * DIFF_HEADS_PER_STEP + n // 2
            feat = jnp.concatenate([_slope_feature_rows(feat_ref, head, ATT_TQ),
                                    jnp.zeros((LANES - 8, ATT_TQ), F32)], axis=0).T.astype(BF16)
        qaug_ref[n, :, 0:HEAD_DIM] = q_ref[:, n * HEAD_DIM:(n + 1) * HEAD_DIM]
        qaug_ref[n, :, HEAD_DIM:2 * HEAD_DIM] = feat

    def tile_step(n, r, nk, state, row0=0, causal=False):
        m, l, acc = state
        s = _dot_nt(qaug_ref[n, row0:, :], kaug_ref[n, pl.ds(r, nk), :])
        if causal:
            s = _causal_tile(s)
        m, alpha, p = _online_step(s, m)
        v = v_ref[pl.ds(r, nk), (n // 2) * DIFF_V_DIM:(n // 2 + 1) * DIFF_V_DIM]
        return (m, alpha * l + jnp.sum(p, axis=-1, keepdims=True), alpha * acc + _dot(p.astype(BF16), v))

    def body(j, states):
        r = pl.multiple_of(j * ATT_LOOP_TK, ATT_LOOP_TK)
        return tuple(tile_step(n, r, ATT_LOOP_TK, states[n]) for n in range(n_chains))

    init = tuple((jnp.full((ATT_TQ, 1), M_INIT, F32), jnp.zeros((ATT_TQ, 1), F32),
                  jnp.zeros((ATT_TQ, DIFF_V_DIM), F32)) for _ in range(n_chains))
    states = list(_past_tiles_loop(i, body, init))
    for d in range(ATT_TQ // ATT_TK):
        r = pl.multiple_of(i * ATT_TQ + d * ATT_TK, ATT_TK)
        for n in range(n_chains):
            tail = tile_step(n, r, ATT_TK, _rows_from(states[n], d * ATT_TK), row0=d * ATT_TK, causal=True)
            states[n] = _rows_replace(states[n], d * ATT_TK, tail)

    lam = (jnp.exp(jnp.sum(lq1_ref[...] * lk1_ref[...], axis=-1, keepdims=True))
           - jnp.exp(jnp.sum(lq2_ref[...] * lk2_ref[...], axis=-1, keepdims=True))
           + LAMBDA_INIT)
    for hh in range(DIFF_HEADS_PER_STEP):
        (_, l1, acc1), (_, l2, acc2) = states[2 * hh], states[2 * hh + 1]
        o = acc1 / l1 - lam * (acc2 / l2)
        y = o * lax.rsqrt(jnp.mean(o * o, axis=-1, keepdims=True) + RMS_EPS)
        o_ref[:, hh * DIFF_V_DIM:(hh + 1) * DIFF_V_DIM] = ((y * sw_ref[...]) * (1.0 - LAMBDA_INIT)).astype(BF16)


def _diff(proj, feats, lq1, lk1, lq2, lk2, subln_w, batch, seq):
    nq = seq // ATT_TQ
    width = DIFF_HEADS_PER_STEP * 2 * HEAD_DIM
    groups = DIFF_HEADS // DIFF_HEADS_PER_STEP
    q_col = 3 * MOBA_HEADS * HEAD_DIM // width
    k_col = q_col + groups
    v_col = k_col + groups
    vec = lambda n: pl.BlockSpec((1, n), lambda b, h, i, s: (0, 0))
    return pl.pallas_call(
        functools.partial(_diff_kernel, seq=seq),
        grid_spec=pltpu.PrefetchScalarGridSpec(
            num_scalar_prefetch=1,
            grid=(batch, groups, nq),
            in_specs=[
                vec(HEAD_DIM), vec(HEAD_DIM), vec(HEAD_DIM), vec(HEAD_DIM), vec(DIFF_V_DIM),
                pl.BlockSpec((ATT_TQ, width), lambda b, h, i, s: (b * nq + i, q_col + h)),
                pl.BlockSpec((seq, width), lambda b, h, i, s: (b, k_col + h)),
                pl.BlockSpec((seq, width), lambda b, h, i, s: (b, v_col + h)),
            ],
            out_specs=pl.BlockSpec((ATT_TQ, width), lambda b, h, i, s: (b * nq + i, h)),
            scratch_shapes=[
                pltpu.VMEM((2 * DIFF_HEADS_PER_STEP, seq, 2 * HEAD_DIM), BF16),
                pltpu.VMEM((2 * DIFF_HEADS_PER_STEP, ATT_TQ, 2 * HEAD_DIM), BF16),
            ],
        ),
        out_shape=jax.ShapeDtypeStruct((batch * seq, DIFF_HEADS * DIFF_V_DIM), BF16),
        compiler_params=pltpu.CompilerParams(
            dimension_semantics=("arbitrary", "arbitrary", "arbitrary"), vmem_limit_bytes=VMEM_LIMIT),
        name="diff_attn",
    )(feats, lq1, lk1, lq2, lk2, subln_w, proj, proj, proj)


def _route_tile(lg, run, tri):
    lane = lax.broadcasted_iota(jnp.int32, lg.shape, 1).astype(F32)
    rmax = lambda v: jnp.max(v, axis=-1, keepdims=True)
    first = lambda v, m: jnp.min(jnp.where(v == m, lane, float(LANES)), axis=-1, keepdims=True)

    gl = jnp.where(lane < N_GROUPS, lg, -jnp.inf)
    gmax = rmax(gl)
    group = first(gl, gmax)
    p_group = 1.0 / jnp.sum(jnp.exp(gl - gmax), axis=-1, keepdims=True)

    lo = N_GROUPS + EXPERTS_PER_GROUP * group
    el = jnp.where((lane >= lo) & (lane < lo + EXPERTS_PER_GROUP), lg, -jnp.inf)
    e1 = rmax(el)
    lane1 = first(el, e1)
    el = jnp.where(lane == lane1, -jnp.inf, el)
    e2 = rmax(el)
    lane2 = first(el, e2)
    t2 = jnp.exp(e2 - e1)
    w1 = p_group / (1.0 + t2)
    w2 = p_group * t2 / (1.0 + t2)
    x1 = lane1 - N_GROUPS
    x2 = lane2 - N_GROUPS

    hit1 = lane == x1
    hit2 = lane == x2
    onehot = jnp.where(hit1 | hit2, 1.0, 0.0)
    before = _dot(tri, onehot.astype(BF16)) + run
    rank1 = jnp.sum(jnp.where(hit1, before, 0.0), axis=-1, keepdims=True)
    rank2 = jnp.sum(jnp.where(hit2, before, 0.0), axis=-1, keepdims=True)

    meta = jnp.zeros(lg.shape, F32)
    for k, v in ((META_X1, x1), (META_X2, x2), (META_W1, w1), (META_W2, w2),
                 (META_RANK1, rank1), (META_RANK2, rank2)):
        meta = jnp.where(lane == k, v, meta)
    return meta, run + jnp.sum(onehot, axis=0, keepdims=True)


def _outproj_kernel(x_ref, oa_ref, ob_ref, w_ref, nw_ref, wrh_ref, wrl_ref, br_ref,
                    x2_ref, hp_ref, meta_ref, cnt_ref, run_ref, tri_ref):
    @pl.when(pl.program_id(0) == 0)
    def _():
        row = lax.broadcasted_iota(jnp.int32, tri_ref.shape, 0)
        col = lax.broadcasted_iota(jnp.int32, tri_ref.shape, 1)
        tri_ref[...] = jnp.where(col < row, 1.0, 0.0).astype(BF16)
        run_ref[...] = jnp.zeros(run_ref.shape, F32)

    half = D_MODEL // 2
    x2 = x_ref[...] + _dot(oa_ref[...], w_ref[0:half, :]) + _dot(ob_ref[...], w_ref[half:, :])
    x2_ref[...] = x2
    r = lax.rsqrt(jnp.mean(x2 * x2, axis=-1, keepdims=True) + RMS_EPS)
    h = x2 * r * nw_ref[...]
    hh = h.astype(BF16)
    hl = (h - hh.astype(F32)).astype(BF16)
    lg = (_dot(hh, wrh_ref[...]) + _dot(hl, wrh_ref[...]) + _dot(hh, wrl_ref[...]) + br_ref[...])
    meta, run = _route_tile(lg, run_ref[...], tri_ref[...])
    meta_ref[...] = meta
    run_ref[...] = run
    cnt_ref[...] = run
    bits = pltpu.bitcast(hh.astype(F32), jnp.uint32)
    for c in range(ROW_WORD_CHUNKS):
        lo = bits[:, 2 * c * LANES:(2 * c + 1) * LANES]
        hi = bits[:, (2 * c + 1) * LANES:(2 * c + 2) * LANES]
        hp_ref[pl.ds(c, OUT_TM, stride=ROW_WORD_CHUNKS), :] = (lo >> 16) | (hi & jnp.uint32(0xFFFF0000))


def _out_proj(x2d, out_a, out_b, w_o, norm_w, wr_hi, wr_lo, b_r):
    t = x2d.shape[0]
    half = D_MODEL // 2
    row = lambda n: pl.BlockSpec((OUT_TM, n), lambda i: (i, 0))
    full = lambda r, c: pl.BlockSpec((r, c), lambda i: (0, 0))
    return pl.pallas_call(
        _outproj_kernel,
        grid=(t // OUT_TM,),
        in_specs=[row(D_MODEL), row(half), row(half), full(D_MODEL, D_MODEL),
                  full(1, D_MODEL), full(D_MODEL, LANES), full(D_MODEL, LANES), full(1, LANES)],
        out_specs=[row(D_MODEL), pl.BlockSpec((OUT_TM * ROW_WORD_CHUNKS, LANES), lambda i: (i, 0)), row(LANES),
                   full(1, LANES)],
        out_shape=[jax.ShapeDtypeStruct((t, D_MODEL), F32),
                   jax.ShapeDtypeStruct((t * ROW_WORD_CHUNKS, LANES), jnp.uint32),
                   jax.ShapeDtypeStruct((t, LANES), F32),
                   jax.ShapeDtypeStruct((1, LANES), F32)],
        scratch_shapes=[pltpu.VMEM((1, LANES), F32), pltpu.VMEM((OUT_TM, OUT_TM), BF16)],
        compiler_params=pltpu.CompilerParams(
            dimension_semantics=("arbitrary",), vmem_limit_bytes=VMEM_LIMIT),
        name="out_proj",
    )(x2d, out_a, out_b, w_o, norm_w, wr_hi, wr_lo, b_r)


GATHER_UNROLL = 16
CLEAR_UNROLL = 32


def _dispatch_kernel(pos0_ref, pos1_ref, nt_ref, hp_hbm, xs_ref, hp_ref, src_ref, sem):
    t = pl.program_id(0)
    pos_refs = (pos0_ref, pos1_ref)

    @pl.when(t == 0)
    def _():
        load = pltpu.make_async_copy(hp_hbm, hp_ref, sem)
        load.start()

        def clear(p, _):
            src_ref[p] = 0
            return 0
        lax.fori_loop(0, src_ref.shape[0], clear, 0, unroll=CLEAR_UNROLL)

        def place(c, _):
            tok0 = c * (GATHER_UNROLL // EXPERT_TOPK)
            rows = [p[tok0 + u] for u in range(GATHER_UNROLL // EXPERT_TOPK) for p in pos_refs]
            for n, row in enumerate(rows):
                src_ref[row] = tok0 + n // EXPERT_TOPK
            return 0
        lax.fori_loop(0, pos_refs[0].shape[0] * EXPERT_TOPK // GATHER_UNROLL, place, 0)
        load.wait()

    @pl.when(t < nt_ref[0])
    def _():
        def gather(c, _):
            toks = [src_ref[t * MOE_TM + c * GATHER_UNROLL + u] for u in range(GATHER_UNROLL)]
            for u in range(GATHER_UNROLL):
                xs_ref[c * GATHER_UNROLL + u] = hp_ref[toks[u]]
            return 0
        lax.fori_loop(0, MOE_TM // GATHER_UNROLL, gather, 0)

    @pl.when(t >= nt_ref[0])
    def _():
        xs_ref[...] = jnp.zeros(xs_ref.shape, xs_ref.dtype)


def _dispatch(pos, n_tiles, hp, npad):
    n_tokens = hp.shape[0] // ROW_WORD_CHUNKS
    hp3 = hp.reshape(n_tokens, ROW_WORD_CHUNKS, LANES)
    xs = pl.pallas_call(
        _dispatch_kernel,
        grid_spec=pltpu.PrefetchScalarGridSpec(
            num_scalar_prefetch=3,
            grid=(npad // MOE_TM,),
            in_specs=[pl.BlockSpec(memory_space=pl.ANY)],
            out_specs=pl.BlockSpec((MOE_TM, ROW_WORD_CHUNKS, LANES), lambda t, p0, p1, n: (t, 0, 0)),
            scratch_shapes=[pltpu.VMEM(hp3.shape, hp3.dtype), pltpu.SMEM((npad,), jnp.int32),
                            pltpu.SemaphoreType.DMA(())],
        ),
        out_shape=jax.ShapeDtypeStruct((npad, ROW_WORD_CHUNKS, LANES), hp.dtype),
        compiler_params=pltpu.CompilerParams(
            dimension_semantics=("arbitrary",), vmem_limit_bytes=VMEM_LIMIT),
        name="dispatch",
    )(pos[0], pos[1], n_tiles, hp3)
    return xs.reshape(npad * ROW_WORD_CHUNKS, LANES)


W_SLOTS = 2


def _experts_kernel(tk_ref, eseq_ref, nt_ref, xs_ref, wg_hbm, wu_hbm, wd_hbm, y_ref,
                    wg_buf, wu_buf, wd_buf, wgb_ref, wub_ref, wdb_ref, x_ref, sem):
    t = pl.program_id(0)
    n_used = nt_ref[1]
    hbm_bufs = ((wg_hbm, wg_buf), (wu_hbm, wu_buf), (wd_hbm, wd_buf))

    def weight_copies(k, slot):
        e = eseq_ref[k]
        return [pltpu.make_async_copy(hbm.at[e], buf.at[slot], sem.at[slot, n])
                for n, (hbm, buf) in enumerate(hbm_bufs)]

    @pl.when(t < nt_ref[0])
    def _():
        k = tk_ref[t]

        @pl.when((t == 0) | (k != tk_ref[jnp.maximum(t - 1, 0)]))
        def _():
            @pl.when(k == 0)
            def _():
                for s in range(W_SLOTS):
                    @pl.when(s < n_used)
                    def _():
                        for c in weight_copies(s, s):
                            c.start()

            slot = k % W_SLOTS
            for c in weight_copies(k, slot):
                c.wait()
            wgb_ref[...] = wg_buf[slot].astype(BF16)
            wub_ref[...] = wu_buf[slot].astype(BF16)
            wdb_ref[...] = wd_buf[slot].astype(BF16)

            @pl.when(k + W_SLOTS < n_used)
            def _():
                for c in weight_copies(k + W_SLOTS, slot):
                    c.start()

        for c in range(ROW_WORD_CHUNKS):
            u32 = xs_ref[pl.ds(c, MOE_TM, stride=ROW_WORD_CHUNKS), :]
            x_ref[:, 2 * c * LANES:(2 * c + 1) * LANES] = pltpu.bitcast(u32 << 16, F32).astype(BF16)
            x_ref[:, (2 * c + 1) * LANES:(2 * c + 2) * LANES] = (
                pltpu.bitcast(u32 & jnp.uint32(0xFFFF0000), F32).astype(BF16))
        x = x_ref[...]
        a = _dot(x, wgb_ref[...])
        u = _dot(x, wub_ref[...])
        act = (a * jax.nn.sigmoid(a)) * u
        y_ref[...] = _dot(act.astype(BF16), wdb_ref[...])

    @pl.when(t >= nt_ref[0])
    def _():
        y_ref[...] = jnp.zeros(y_ref.shape, y_ref.dtype)


def _experts(tile_k, expert_seq, n_used, xs, w_gate, w_up, w_down):
    npad = xs.shape[0] // ROW_WORD_CHUNKS
    last = lambda t, nt: jnp.minimum(t, jnp.maximum(nt[0] - 1, 0))
    hbm = pl.BlockSpec(memory_space=pl.ANY)
    return pl.pallas_call(
        _experts_kernel,
        grid_spec=pltpu.PrefetchScalarGridSpec(
            num_scalar_prefetch=3,
            grid=(npad // MOE_TM,),
            in_specs=[
                pl.BlockSpec((MOE_TM * ROW_WORD_CHUNKS, LANES), lambda t, tk, es, nt: (last(t, nt), 0)),
                hbm, hbm, hbm,
            ],
            out_specs=pl.BlockSpec((MOE_TM, D_MODEL), lambda t, tk, es, nt: (t, 0)),
            scratch_shapes=[
                pltpu.VMEM((W_SLOTS, D_MODEL, D_EXPERT), F32),
                pltpu.VMEM((W_SLOTS, D_MODEL, D_EXPERT), F32),
                pltpu.VMEM((W_SLOTS, D_EXPERT, D_MODEL), F32),
                pltpu.VMEM((D_MODEL, D_EXPERT), BF16),
                pltpu.VMEM((D_MODEL, D_EXPERT), BF16),
                pltpu.VMEM((D_EXPERT, D_MODEL), BF16),
                pltpu.VMEM((MOE_TM, D_MODEL), BF16),
                pltpu.SemaphoreType.DMA((W_SLOTS, 3)),
            ],
        ),
        out_shape=jax.ShapeDtypeStruct((npad, D_MODEL), F32),
        compiler_params=pltpu.CompilerParams(
            dimension_semantics=("arbitrary",), vmem_limit_bytes=VMEM_LIMIT),
        name="experts",
    )(tile_k, expert_seq, n_used, xs, w_gate, w_up, w_down)


CMB_UNROLL = 8
CMB_SLOTS = 2


def _combine_kernel(pos0_ref, pos1_ref, x2_ref, meta_ref, nw_ref, y_ref, o_ref, buf_ref, sem):
    i = pl.program_id(0)
    pos_refs = (pos0_ref, pos1_ref)

    def row_copy(p, slot, k, r):
        return pltpu.make_async_copy(y_ref.at[pl.ds(p, 1), :], buf_ref.at[slot, k, pl.ds(r, 1), :],
                                     sem.at[slot])

    def issue(tile, slot):
        for c in range(CMB_TT // CMB_UNROLL):
            tok0 = tile * CMB_TT + c * CMB_UNROLL
            rows = [[p[tok0 + u] for u in range(CMB_UNROLL)] for p in pos_refs]
            for u in range(CMB_UNROLL):
                for k in range(EXPERT_TOPK):
                    row_copy(rows[k][u], slot, k, c * CMB_UNROLL + u).start(priority=k % 2)

    @pl.when(i == 0)
    def _():
        issue(0, 0)

    @pl.when(i + 1 < pl.num_programs(0))
    def _():
        issue(i + 1, (i + 1) % CMB_SLOTS)

    slot = i % CMB_SLOTS
    for k in range(EXPERT_TOPK):
        pltpu.make_async_copy(y_ref.at[pl.ds(0, CMB_TT), :], buf_ref.at[slot, k], sem.at[slot]).wait()

    meta = meta_ref[...]
    x3 = (x2_ref[...] + meta[:, META_W1:META_W1 + 1] * buf_ref[slot, 0]
          + meta[:, META_W2:META_W2 + 1] * buf_ref[slot, 1])
    r = lax.rsqrt(jnp.mean(x3 * x3, axis=-1, keepdims=True) + RMS_EPS)
    o_ref[...] = x3 * r * nw_ref[...]


def _combine(pos, x2, meta, norm_w, y):
    t = x2.shape[0]
    return pl.pallas_call(
        _combine_kernel,
        grid_spec=pltpu.PrefetchScalarGridSpec(
            num_scalar_prefetch=2,
            grid=(t // CMB_TT,),
            in_specs=[
                pl.BlockSpec((CMB_TT, D_MODEL), lambda i, p0, p1: (i, 0)),
                pl.BlockSpec((CMB_TT, LANES), lambda i, p0, p1: (i, 0)),
                pl.BlockSpec((1, D_MODEL), lambda i, p0, p1: (0, 0)),
                pl.BlockSpec(memory_space=pl.ANY),
            ],
            out_specs=pl.BlockSpec((CMB_TT, D_MODEL), lambda i, p0, p1: (i, 0)),
            scratch_shapes=[
                pltpu.VMEM((CMB_SLOTS, EXPERT_TOPK, CMB_TT, D_MODEL), F32),
                pltpu.SemaphoreType.DMA((CMB_SLOTS,)),
            ],
        ),
        out_shape=jax.ShapeDtypeStruct((t, D_MODEL), F32),
        compiler_params=pltpu.CompilerParams(
            dimension_semantics=("arbitrary",), vmem_limit_bytes=VMEM_LIMIT),
        name="combine",
    )(pos[0], pos[1], x2, meta, norm_w, y)


def _layout(meta, counts, n_tokens):
    npad = n_tokens * EXPERT_TOPK + N_EXPERTS * MOE_TM
    counts = counts[0, :N_EXPERTS].astype(jnp.int32)
    padded = ((counts + MOE_TM - 1) // MOE_TM) * MOE_TM
    ends = jnp.cumsum(padded)
    starts = ends - padded
    ids = jnp.arange(N_EXPERTS, dtype=jnp.int32)
    pos = []
    for lane_x, lane_rank in ((META_X1, META_RANK1), (META_X2, META_RANK2)):
        hit = meta[:, lane_x].astype(jnp.int32)[:, None] == ids[None, :]
        pos.append(jnp.sum(jnp.where(hit, starts[None, :], 0), axis=-1) + meta[:, lane_rank].astype(jnp.int32))
    tile_start = jnp.arange(npad // MOE_TM, dtype=jnp.int32) * MOE_TM
    tile_expert = jnp.minimum(jnp.sum(ends[None, :] <= tile_start[:, None], axis=1), N_EXPERTS - 1)
    used = counts > 0
    k_of_expert = jnp.cumsum(used.astype(jnp.int32)) - 1
    expert_seq = jnp.sum(jnp.where(used[None, :] & (k_of_expert[None, :] == ids[:, None]), ids[None, :], 0), axis=1)
    tile_k = jnp.sum(jnp.where(tile_expert[:, None] == ids[None, :], k_of_expert[None, :], 0), axis=1)
    n_tiles = (ends[-1] // MOE_TM).astype(jnp.int32)
    n_used = jnp.stack([n_tiles, jnp.sum(used).astype(jnp.int32)])
    return pos, tile_k.astype(jnp.int32), expert_seq.astype(jnp.int32), n_tiles.reshape(1), n_used, npad


def kernel(x, norm_mix_w, w_in, lambda_q1, lambda_k1, lambda_q2, lambda_k2, diff_subln_w, w_out, norm_ffn_w,
           w_router_group, b_router_group, w_router_expert, b_router_expert, w_gate, w_up, w_down, norm_final_w):
    batch, seq, _ = x.shape
    assert seq % ATT_TQ == 0 and seq // MOBA_BLOCK + 8 <= LANES
    assert w_in.shape[0] == 1, "single-layer block"
    n_tokens = batch * seq
    x2d = x.reshape(n_tokens, D_MODEL)

    col = np.arange(IN_WIDTH)
    is_q = (col < MOBA_HEADS * HEAD_DIM) | ((col >= 3 * MOBA_HEADS * HEAD_DIM) & (col < 4 * MOBA_HEADS * HEAD_DIM))
    col_scale = jnp.asarray(np.where(is_q, QK_SCALE * LOG2E, 1.0).astype(np.float32))[None, :]

    proj = _in_proj(x2d, norm_mix_w[0][None, :], w_in[0], col_scale)
    out_a = _moba(proj, jnp.asarray(_alibi_slope_pieces(MOBA_HEADS)), batch, seq)
    out_b = _diff(proj, jnp.asarray(_alibi_slope_pieces(DIFF_HEADS)), lambda_q1, lambda_k1, lambda_q2, lambda_k2,
                  diff_subln_w, batch, seq)

    half = D_MODEL // 2
    w_o = w_out[0].astype(BF16)
    w_r = jnp.concatenate([w_router_group[0], w_router_expert[0]], axis=1)
    w_r = jnp.pad(w_r, ((0, 0), (0, LANES - w_r.shape[1])))
    wr_hi = w_r.astype(BF16)
    wr_lo = (w_r - wr_hi.astype(F32)).astype(BF16)
    b_r = jnp.pad(jnp.concatenate([b_router_group[0], b_router_expert[0]]), (0, LANES - N_GROUPS - N_EXPERTS))[None, :]
    x2, hp, meta, counts = _out_proj(x2d, out_a, out_b, w_o, norm_ffn_w[0][None, :],
                                     wr_hi, wr_lo, b_r)

    pos, tile_k, expert_seq, n_tiles, n_used, npad = _layout(meta, counts, n_tokens)
    xs = _dispatch(pos, n_tiles, hp, npad)
    y = _experts(tile_k, expert_seq, n_used, xs, w_gate[0], w_up[0], w_down[0])
    out = _combine(pos, x2, meta, norm_final_w[None, :], y)
    return out.reshape(batch, seq, D_MODEL)
```

```python
import functools

import ml_dtypes
import numpy as np
import jax
import jax.numpy as jnp
from jax import lax
from jax.experimental import pallas as pl
from jax.experimental.pallas import tpu as pltpu

F32 = jnp.float32
BF16 = jnp.bfloat16

D_MODEL = 2048
HEAD_DIM = 128
MOBA_HEADS = 8
MOBA_BLOCK = 256
MOBA_TOPK = 3
DIFF_HEADS = 4
DIFF_V_DIM = 256
IN_WIDTH = 6144
N_GROUPS = 4
EXPERTS_PER_GROUP = 8
N_EXPERTS = 32
EXPERT_TOPK = 2
D_EXPERT = 512
RMS_EPS = 1e-6
ALIBI_MAX_BIAS = 8.0
LAMBDA_INIT = 0.8 - 0.6 * float(np.exp(-0.3 * 0))

LANES = 128
QK_SCALE = HEAD_DIM ** -0.5
LOG2E = float(np.log2(np.e))
NEG_BIG = -1e30
M_INIT = -1e38
POS_SPLIT = 64
SLOPE_PIECES = 3
ROW_WORD_CHUNKS = D_MODEL // (2 * LANES)
META_X1, META_X2, META_W1, META_W2, META_RANK1, META_RANK2 = range(6)

IN_TM, IN_TN = 1024, 1024
ATT_TQ = 1024
ATT_TK = 512
ATT_LOOP_TK = 512
KV_UNROLL = 2
HEADS_PER_STEP = 4
DIFF_HEADS_PER_STEP = 1
OUT_TM = 512
MOE_TM = 256
CMB_TT = 256
VMEM_LIMIT = 56 * 1024 * 1024


def _alibi_slope_pieces(n):
    rem = np.exp2(-ALIBI_MAX_BIAS * (np.arange(n, dtype=np.float64) + 1.0) / n) * LOG2E
    pieces = []
    for _ in range(SLOPE_PIECES):
        p = rem.astype(ml_dtypes.bfloat16).astype(np.float64)
        pieces.append(p)
        rem = rem - p
    return np.stack(pieces, axis=1).reshape(-1).astype(np.float32)


def _dot_nt(a, b):
    return lax.dot_general(a, b, (((1,), (1,)), ((), ())), preferred_element_type=F32)


def _dot(a, b):
    return jnp.dot(a, b, preferred_element_type=F32)


def _inproj_kernel(x_ref, nw_ref, cs_ref, w_ref, o_ref, h_ref):
    @pl.when(pl.program_id(1) == 0)
    def _():
        x = x_ref[...]
        r = lax.rsqrt(jnp.mean(x * x, axis=-1, keepdims=True) + RMS_EPS)
        h_ref[...] = (x * r * nw_ref[...]).astype(BF16)

    acc = _dot(h_ref[...], w_ref[...].astype(BF16))
    o_ref[...] = (acc * cs_ref[...]).astype(BF16)


def _in_proj(x2d, norm_w, w_in, col_scale):
    t = x2d.shape[0]
    return pl.pallas_call(
        _inproj_kernel,
        grid=(t // IN_TM, IN_WIDTH // IN_TN),
        in_specs=[
            pl.BlockSpec((IN_TM, D_MODEL), lambda i, j: (i, 0)),
            pl.BlockSpec((1, D_MODEL), lambda i, j: (0, 0)),
            pl.BlockSpec((1, IN_TN), lambda i, j: (0, j)),
            pl.BlockSpec((D_MODEL, IN_TN), lambda i, j: (0, j)),
        ],
        out_specs=pl.BlockSpec((IN_TM, IN_TN), lambda i, j: (i, j)),
        out_shape=jax.ShapeDtypeStruct((t, IN_WIDTH), BF16),
        scratch_shapes=[pltpu.VMEM((IN_TM, D_MODEL), BF16)],
        compiler_params=pltpu.CompilerParams(
            dimension_semantics=("arbitrary", "arbitrary"), vmem_limit_bytes=VMEM_LIMIT),
        name="in_proj",
    )(x2d, norm_w, col_scale, w_in)


def _key_position_features(seq, first_lane):
    pos = lax.broadcasted_iota(jnp.int32, (seq, LANES), 0)
    lane = lax.broadcasted_iota(jnp.int32, (seq, LANES), 1)
    hi = (lane >= first_lane) & (lane < first_lane + SLOPE_PIECES)
    lo = (lane >= first_lane + SLOPE_PIECES) & (lane < first_lane + 2 * SLOPE_PIECES)
    ext = jnp.where(hi, (pos // POS_SPLIT).astype(F32), 0.0)
    return jnp.where(lo, (pos % POS_SPLIT).astype(F32), ext), pos, lane


def _slope_feature_rows(feat_ref, head, width):
    row = lax.broadcasted_iota(jnp.int32, (8, width), 0)
    out = jnp.zeros((8, width), F32)
    for k in range(SLOPE_PIECES):
        c = feat_ref[head * SLOPE_PIECES + k]
        out = jnp.where(row == k, POS_SPLIT * c, out)
        out = jnp.where(row == SLOPE_PIECES + k, c, out)
    return out


def _causal_tile(s):
    row = lax.broadcasted_iota(jnp.int32, s.shape, 0)
    col = lax.broadcasted_iota(jnp.int32, s.shape, 1)
    return jnp.where(col <= row, s, -jnp.inf)


def _online_step(s, m):
    m_new = jnp.maximum(m, jnp.max(s, axis=-1, keepdims=True))
    return m_new, jnp.exp2(m - m_new), jnp.exp2(s - m_new)


def _past_tiles_loop(i, body, init):
    def trip(t, carry):
        for u in range(KV_UNROLL):
            carry = body(t * KV_UNROLL + u, carry)
        return carry

    return lax.fori_loop(0, i * (ATT_TQ // (ATT_LOOP_TK * KV_UNROLL)), trip, init)


def _rows_from(state, start):
    return state if start == 0 else tuple(a[start:] for a in state)


def _rows_replace(state, start, tail):
    if start == 0:
        return tuple(tail)
    return tuple(jnp.concatenate([a[:start], b], axis=0) for a, b in zip(state, tail))


def _moba_kernel(feat_ref, q_ref, k_ref, v_ref, o_ref, kaug_ref, vaug_ref, kmh_ref, kml_ref, qaug_ref, *, seq):
    nb = seq // MOBA_BLOCK
    hp = pl.program_id(1)
    i = pl.program_id(2)

    @pl.when(i == 0)
    def _():
        ext, pos, lane = _key_position_features(seq, nb)
        ext = jnp.where(lane == pos // MOBA_BLOCK, 1.0, ext).astype(BF16)
        ones_col = jnp.where(lane == 0, 1.0, 0.0).astype(BF16)
        for g in range(HEADS_PER_STEP):
            k = k_ref[:, g * HEAD_DIM:(g + 1) * HEAD_DIM]
            kaug_ref[g, :, 0:HEAD_DIM] = k
            kaug_ref[g, :, HEAD_DIM:2 * HEAD_DIM] = ext
            vaug_ref[g, :, 0:HEAD_DIM] = v_ref[:, g * HEAD_DIM:(g + 1) * HEAD_DIM]
            vaug_ref[g, :, HEAD_DIM:2 * HEAD_DIM] = ones_col
            km = jnp.mean(k.astype(F32).reshape(nb, MOBA_BLOCK, HEAD_DIM), axis=1)
            kmh = km.astype(BF16)
            kmh_ref[g] = jnp.zeros((LANES, HEAD_DIM), BF16)
            kml_ref[g] = jnp.zeros((LANES, HEAD_DIM), BF16)
            kmh_ref[g, 0:nb, :] = kmh
            kml_ref[g, 0:nb, :] = (km - kmh.astype(F32)).astype(BF16)

    blk = lax.broadcasted_iota(jnp.int32, (nb, ATT_TQ), 0)
    col = lax.broadcasted_iota(jnp.int32, (nb, ATT_TQ), 1)
    own = i * (ATT_TQ // MOBA_BLOCK) + col // MOBA_BLOCK
    blk_f = blk.astype(F32)
    for g in range(HEADS_PER_STEP):
        q = q_ref[:, g * HEAD_DIM:(g + 1) * HEAD_DIM]
        gate = (_dot_nt(kmh_ref[g], q) + _dot_nt(kml_ref[g], q))[0:nb, :]
        gate = jnp.where(blk < own, gate, -jnp.inf)
        sel = jnp.zeros((nb, ATT_TQ), F32)
        for _ in range(MOBA_TOPK):
            mx = jnp.max(gate, axis=0, keepdims=True)
            idx = jnp.min(jnp.where(gate == mx, blk_f, float(LANES)), axis=0, keepdims=True)
            pick = (blk_f == idx) & (mx > -jnp.inf)
            sel = jnp.where(pick, 1.0, sel)
            gate = jnp.where(pick, -jnp.inf, gate)
        bias = jnp.where((sel > 0.0) | (blk == own), 0.0, NEG_BIG)
        feat = jnp.concatenate(
            [bias, _slope_feature_rows(feat_ref, hp * HEADS_PER_STEP + g, ATT_TQ),
             jnp.zeros((LANES - nb - 8, ATT_TQ), F32)], axis=0)
        qaug_ref[g, :, 0:HEAD_DIM] = q
        qaug_ref[g, :, HEAD_DIM:2 * HEAD_DIM] = feat.T.astype(BF16)

    def tile_step(g, r, nk, state, row0=0, causal=False):
        m, acc = state
        s = _dot_nt(qaug_ref[g, row0:, :], kaug_ref[g, pl.ds(r, nk), :])
        if causal:
            s = _causal_tile(s)
        m, alpha, p = _online_step(s, m)
        return m, alpha * acc + _dot(p.astype(BF16), vaug_ref[g, pl.ds(r, nk), :])

    def body(j, states):
        r = pl.multiple_of(j * ATT_LOOP_TK, ATT_LOOP_TK)
        return tuple(tile_step(g, r, ATT_LOOP_TK, states[g]) for g in range(HEADS_PER_STEP))

    init = tuple((jnp.full((ATT_TQ, 1), M_INIT, F32), jnp.zeros((ATT_TQ, 2 * HEAD_DIM), F32))
                 for _ in range(HEADS_PER_STEP))
    states = list(_past_tiles_loop(i, body, init))
    for d in range(ATT_TQ // ATT_TK):
        r = pl.multiple_of(i * ATT_TQ + d * ATT_TK, ATT_TK)
        for g in range(HEADS_PER_STEP):
            tail = tile_step(g, r, ATT_TK, _rows_from(states[g], d * ATT_TK), row0=d * ATT_TK, causal=True)
            states[g] = _rows_replace(states[g], d * ATT_TK, tail)
    for g in range(HEADS_PER_STEP):
        acc = states[g][1]
        o_ref[:, g * HEAD_DIM:(g + 1) * HEAD_DIM] = (
            acc[:, 0:HEAD_DIM] / acc[:, HEAD_DIM:HEAD_DIM + 1]).astype(BF16)


def _moba(proj, feats, batch, seq):
    nq = seq // ATT_TQ
    width = HEADS_PER_STEP * HEAD_DIM
    q_col, k_col, v_col = 0, MOBA_HEADS // HEADS_PER_STEP, 2 * MOBA_HEADS // HEADS_PER_STEP
    return pl.pallas_call(
        functools.partial(_moba_kernel, seq=seq),
        grid_spec=pltpu.PrefetchScalarGridSpec(
            num_scalar_prefetch=1,
            grid=(batch, MOBA_HEADS // HEADS_PER_STEP, nq),
            in_specs=[
                pl.BlockSpec((ATT_TQ, width), lambda b, h, i, s: (b * nq + i, q_col + h)),
                pl.BlockSpec((seq, width), lambda b, h, i, s: (b, k_col + h)),
                pl.BlockSpec((seq, width), lambda b, h, i, s: (b, v_col + h)),
            ],
            out_specs=pl.BlockSpec((ATT_TQ, width), lambda b, h, i, s: (b * nq + i, h)),
            scratch_shapes=[
                pltpu.VMEM((HEADS_PER_STEP, seq, 2 * HEAD_DIM), BF16),
                pltpu.VMEM((HEADS_PER_STEP, seq, 2 * HEAD_DIM), BF16),
                pltpu.VMEM((HEADS_PER_STEP, LANES, HEAD_DIM), BF16),
                pltpu.VMEM((HEADS_PER_STEP, LANES, HEAD_DIM), BF16),
                pltpu.VMEM((HEADS_PER_STEP, ATT_TQ, 2 * HEAD_DIM), BF16),
            ],
        ),
        out_shape=jax.ShapeDtypeStruct((batch * seq, MOBA_HEADS * HEAD_DIM), BF16),
        compiler_params=pltpu.CompilerParams(
            dimension_semantics=("arbitrary", "arbitrary", "arbitrary"), vmem_limit_bytes=VMEM_LIMIT),
        name="moba",
    )(feats, proj, proj, proj)


def _diff_kernel(feat_ref, lq1_ref, lk1_ref, lq2_ref, lk2_ref, sw_ref,
                 q_ref, k_ref, v_ref, o_ref, kaug_ref, qaug_ref, *, seq):
    hg = pl.program_id(1)
    i = pl.program_id(2)
    n_chains = 2 * DIFF_HEADS_PER_STEP

    @pl.when(i == 0)
    def _():
        ext = _key_position_features(seq, 0)[0].astype(BF16)
        for n in range(n_chains):
            kaug_ref[n, :, 0:HEAD_DIM] = k_ref[:, n * HEAD_DIM:(n + 1) * HEAD_DIM]
            kaug_ref[n, :, HEAD_DIM:2 * HEAD_DIM] = ext

    for n in range(n_chains):
        if n % 2 == 0:
            head = hg * DIFF_HEADS_PER_STEP + n // 2
            feat = jnp.concatenate([_slope_feature_rows(feat_ref, head, ATT_TQ),
                                    jnp.zeros((LANES - 8, ATT_TQ), F32)], axis=0).T.astype(BF16)
        qaug_ref[n, :, 0:HEAD_DIM] = q_ref[:, n * HEAD_DIM:(n + 1) * HEAD_DIM]
        qaug_ref[n, :, HEAD_DIM:2 * HEAD_DIM] = feat

    def tile_step(n, r, nk, state, row0=0, causal=False):
        m, l, acc = state
        s = _dot_nt(qaug_ref[n, row0:, :], kaug_ref[n, pl.ds(r, nk), :])
        if causal:
            s = _causal_tile(s)
        m, alpha, p = _online_step(s, m)
        v = v_ref[pl.ds(r, nk), (n // 2) * DIFF_V_DIM:(n // 2 + 1) * DIFF_V_DIM]
        p = p.astype(BF16)
        return (m, alpha * l + jnp.sum(p.astype(F32), axis=-1, keepdims=True), alpha * acc + _dot(p, v))

    def body(j, states):
        r = pl.multiple_of(j * ATT_LOOP_TK, ATT_LOOP_TK)
        return tuple(tile_step(n, r, ATT_LOOP_TK, states[n]) for n in range(n_chains))

    init = tuple((jnp.full((ATT_TQ, 1), M_INIT, F32), jnp.zeros((ATT_TQ, 1), F32),
                  jnp.zeros((ATT_TQ, DIFF_V_DIM), F32)) for _ in range(n_chains))
    states = list(_past_tiles_loop(i, body, init))
    for d in range(ATT_TQ // ATT_TK):
        r = pl.multiple_of(i * ATT_TQ + d * ATT_TK, ATT_TK)
        for n in range(n_chains):
            tail = tile_step(n, r, ATT_TK, _rows_from(states[n], d * ATT_TK), row0=d * ATT_TK, causal=True)
            states[n] = _rows_replace(states[n], d * ATT_TK, tail)

    lam = (jnp.exp(jnp.sum(lq1_ref[...] * lk1_ref[...], axis=-1, keepdims=True))
           - jnp.exp(jnp.sum(lq2_ref[...] * lk2_ref[...], axis=-1, keepdims=True))
           + LAMBDA_INIT)
    for hh in range(DIFF_HEADS_PER_STEP):
        (_, l1, acc1), (_, l2, acc2) = states[2 * hh], states[2 * hh + 1]
        o = acc1 / l1 - lam * (acc2 / l2)
        y = o * lax.rsqrt(jnp.mean(o * o, axis=-1, keepdims=True) + RMS_EPS)
        o_ref[:, hh * DIFF_V_DIM:(hh + 1) * DIFF_V_DIM] = ((y * sw_ref[...]) * (1.0 - LAMBDA_INIT)).astype(BF16)


def _diff(proj, feats, lq1, lk1, lq2, lk2, subln_w, batch, seq):
    nq = seq // ATT_TQ
    width = DIFF_HEADS_PER_STEP * 2 * HEAD_DIM
    groups = DIFF_HEADS // DIFF_HEADS_PER_STEP
    q_col = 3 * MOBA_HEADS * HEAD_DIM // width
    k_col = q_col + groups
    v_col = k_col + groups
    vec = lambda n: pl.BlockSpec((1, n), lambda b, h, i, s: (0, 0))
    return pl.pallas_call(
        functools.partial(_diff_kernel, seq=seq),
        grid_spec=pltpu.PrefetchScalarGridSpec(
            num_scalar_prefetch=1,
            grid=(batch, groups, nq),
            in_specs=[
                vec(HEAD_DIM), vec(HEAD_DIM), vec(HEAD_DIM), vec(HEAD_DIM), vec(DIFF_V_DIM),
                pl.BlockSpec((ATT_TQ, width), lambda b, h, i, s: (b * nq + i, q_col + h)),
                pl.BlockSpec((seq, width), lambda b, h, i, s: (b, k_col + h)),
                pl.BlockSpec((seq, width), lambda b, h, i, s: (b, v_col + h)),
            ],
            out_specs=pl.BlockSpec((ATT_TQ, width), lambda b, h, i, s: (b * nq + i, h)),
            scratch_shapes=[
                pltpu.VMEM((2 * DIFF_HEADS_PER_STEP, seq, 2 * HEAD_DIM), BF16),
                pltpu.VMEM((2 * DIFF_HEADS_PER_STEP, ATT_TQ, 2 * HEAD_DIM), BF16),
            ],
        ),
        out_shape=jax.ShapeDtypeStruct((batch * seq, DIFF_HEADS * DIFF_V_DIM), BF16),
        compiler_params=pltpu.CompilerParams(
            dimension_semantics=("arbitrary", "arbitrary", "arbitrary"), vmem_limit_bytes=VMEM_LIMIT),
        name="diff_attn",
    )(feats, lq1, lk1, lq2, lk2, subln_w, proj, proj, proj)


def _route_tile(lg, run, tri):
    lane = lax.broadcasted_iota(jnp.int32, lg.shape, 1).astype(F32)
    rmax = lambda v: jnp.max(v, axis=-1, keepdims=True)
    first = lambda v, m: jnp.min(jnp.where(v == m, lane, float(LANES)), axis=-1, keepdims=True)

    gl = jnp.where(lane < N_GROUPS, lg, -jnp.inf)
    gmax = rmax(gl)
    group = first(gl, gmax)
    p_group = 1.0 / jnp.sum(jnp.exp(gl - gmax), axis=-1, keepdims=True)

    lo = N_GROUPS + EXPERTS_PER_GROUP * group
    el = jnp.where((lane >= lo) & (lane < lo + EXPERTS_PER_GROUP), lg, -jnp.inf)
    e1 = rmax(el)
    lane1 = first(el, e1)
    el = jnp.where(lane == lane1, -jnp.inf, el)
    e2 = rmax(el)
    lane2 = first(el, e2)
    t2 = jnp.exp(e2 - e1)
    w1 = p_group / (1.0 + t2)
    w2 = p_group * t2 / (1.0 + t2)
    x1 = lane1 - N_GROUPS
    x2 = lane2 - N_GROUPS

    hit1 = lane == x1
    hit2 = lane == x2
    onehot = jnp.where(hit1 | hit2, 1.0, 0.0)
    before = _dot(tri, onehot.astype(BF16)) + run
    rank1 = jnp.sum(jnp.where(hit1, before, 0.0), axis=-1, keepdims=True)
    rank2 = jnp.sum(jnp.where(hit2, before, 0.0), axis=-1, keepdims=True)

    meta = jnp.zeros(lg.shape, F32)
    for k, v in ((META_X1, x1), (META_X2, x2), (META_W1, w1), (META_W2, w2),
                 (META_RANK1, rank1), (META_RANK2, rank2)):
        meta = jnp.where(lane == k, v, meta)
    return meta, run + jnp.sum(onehot, axis=0, keepdims=True)


def _outproj_kernel(x_ref, oa_ref, ob_ref, w_ref, nw_ref, wr_ref, br_ref,
                    x2_ref, hp_ref, meta_ref, cnt_ref, run_ref, tri_ref):
    @pl.when(pl.program_id(0) == 0)
    def _():
        row = lax.broadcasted_iota(jnp.int32, tri_ref.shape, 0)
        col = lax.broadcasted_iota(jnp.int32, tri_ref.shape, 1)
        tri_ref[...] = jnp.where(col < row, 1.0, 0.0).astype(BF16)
        run_ref[...] = jnp.zeros(run_ref.shape, F32)

    half = D_MODEL // 2
    x2 = x_ref[...] + _dot(oa_ref[...], w_ref[0:half, :]) + _dot(ob_ref[...], w_ref[half:, :])
    x2_ref[...] = x2
    r = lax.rsqrt(jnp.mean(x2 * x2, axis=-1, keepdims=True) + RMS_EPS)
    h = x2 * r * nw_ref[...]
    hh = h.astype(BF16)
    hl = (h - hh.astype(F32)).astype(BF16)
    prod = _dot(jnp.concatenate([hh, hl], axis=0), wr_ref[...])
    lg = prod[:OUT_TM, :LANES] + prod[OUT_TM:, :LANES] + prod[:OUT_TM, LANES:] + br_ref[...]
    meta, run = _route_tile(lg, run_ref[...], tri_ref[...])
    meta_ref[...] = meta
    run_ref[...] = run
    cnt_ref[...] = run
    bits = pltpu.bitcast(hh.astype(F32), jnp.uint32)
    for c in range(ROW_WORD_CHUNKS):
        lo = bits[:, 2 * c * LANES:(2 * c + 1) * LANES]
        hi = bits[:, (2 * c + 1) * LANES:(2 * c + 2) * LANES]
        hp_ref[pl.ds(c, OUT_TM, stride=ROW_WORD_CHUNKS), :] = (lo >> 16) | (hi & jnp.uint32(0xFFFF0000))


def _out_proj(x2d, out_a, out_b, w_o, norm_w, w_r, b_r):
    t = x2d.shape[0]
    half = D_MODEL // 2
    row = lambda n: pl.BlockSpec((OUT_TM, n), lambda i: (i, 0))
    full = lambda r, c: pl.BlockSpec((r, c), lambda i: (0, 0))
    return pl.pallas_call(
        _outproj_kernel,
        grid=(t // OUT_TM,),
        in_specs=[row(D_MODEL), row(half), row(half), full(D_MODEL, D_MODEL),
                  full(1, D_MODEL), full(D_MODEL, 2 * LANES), full(1, LANES)],
        out_specs=[row(D_MODEL), pl.BlockSpec((OUT_TM * ROW_WORD_CHUNKS, LANES), lambda i: (i, 0)), row(LANES),
                   full(1, LANES)],
        out_shape=[jax.ShapeDtypeStruct((t, D_MODEL), F32),
                   jax.ShapeDtypeStruct((t * ROW_WORD_CHUNKS, LANES), jnp.uint32),
                   jax.ShapeDtypeStruct((t, LANES), F32),
                   jax.ShapeDtypeStruct((1, LANES), F32)],
        scratch_shapes=[pltpu.VMEM((1, LANES), F32), pltpu.VMEM((OUT_TM, OUT_TM), BF16)],
        compiler_params=pltpu.CompilerParams(
            dimension_semantics=("arbitrary",), vmem_limit_bytes=VMEM_LIMIT),
        name="out_proj",
    )(x2d, out_a, out_b, w_o, norm_w, w_r, b_r)


GATHER_UNROLL = 16
CLEAR_UNROLL = 32


def _dispatch_kernel(pos0_ref, pos1_ref, nt_ref, hp_hbm, xs_ref, hp_ref, src_ref, sem):
    t = pl.program_id(0)
    pos_refs = (pos0_ref, pos1_ref)

    @pl.when(t == 0)
    def _():
        load = pltpu.make_async_copy(hp_hbm, hp_ref, sem)
        load.start()

        def clear(p, _):
            src_ref[p] = 0
            return 0
        lax.fori_loop(0, src_ref.shape[0], clear, 0, unroll=CLEAR_UNROLL)

        def place(c, _):
            tok0 = c * (GATHER_UNROLL // EXPERT_TOPK)
            rows = [p[tok0 + u] for u in range(GATHER_UNROLL // EXPERT_TOPK) for p in pos_refs]
            for n, row in enumerate(rows):
                src_ref[row] = tok0 + n // EXPERT_TOPK
            return 0
        lax.fori_loop(0, pos_refs[0].shape[0] * EXPERT_TOPK // GATHER_UNROLL, place, 0)
        load.wait()

    @pl.when(t < nt_ref[0])
    def _():
        def gather(c, _):
            toks = [src_ref[t * MOE_TM + c * GATHER_UNROLL + u] for u in range(GATHER_UNROLL)]
            for u in range(GATHER_UNROLL):
                xs_ref[c * GATHER_UNROLL + u] = hp_ref[toks[u]]
            return 0
        lax.fori_loop(0, MOE_TM // GATHER_UNROLL, gather, 0)

    @pl.when(t >= nt_ref[0])
    def _():
        xs_ref[...] = jnp.zeros(xs_ref.shape, xs_ref.dtype)


def _dispatch(pos, n_tiles, hp, npad):
    n_tokens = hp.shape[0] // ROW_WORD_CHUNKS
    hp3 = hp.reshape(n_tokens, ROW_WORD_CHUNKS, LANES)
    xs = pl.pallas_call(
        _dispatch_kernel,
        grid_spec=pltpu.PrefetchScalarGridSpec(
            num_scalar_prefetch=3,
            grid=(npad // MOE_TM,),
            in_specs=[pl.BlockSpec(memory_space=pl.ANY)],
            out_specs=pl.BlockSpec((MOE_TM, ROW_WORD_CHUNKS, LANES), lambda t, p0, p1, n: (t, 0, 0)),
            scratch_shapes=[pltpu.VMEM(hp3.shape, hp3.dtype), pltpu.SMEM((npad,), jnp.int32),
                            pltpu.SemaphoreType.DMA(())],
        ),
        out_shape=jax.ShapeDtypeStruct((npad, ROW_WORD_CHUNKS, LANES), hp.dtype),
        compiler_params=pltpu.CompilerParams(
            dimension_semantics=("arbitrary",), vmem_limit_bytes=VMEM_LIMIT),
        name="dispatch",
    )(pos[0], pos[1], n_tiles, hp3)
    return xs.reshape(npad * ROW_WORD_CHUNKS, LANES)


W_SLOTS = 2


def _experts_kernel(tk_ref, eseq_ref, nt_ref, xs_ref, wg_hbm, wu_hbm, wd_hbm, y_ref,
                    wg_buf, wu_buf, wd_buf, wgb_ref, wub_ref, wdb_ref, x_ref, sem):
    t = pl.program_id(0)
    n_used = nt_ref[1]
    hbm_bufs = ((wg_hbm, wg_buf), (wu_hbm, wu_buf), (wd_hbm, wd_buf))

    def weight_copies(k, slot):
        e = eseq_ref[k]
        return [pltpu.make_async_copy(hbm.at[e], buf.at[slot], sem.at[slot, n])
                for n, (hbm, buf) in enumerate(hbm_bufs)]

    @pl.when(t < nt_ref[0])
    def _():
        k = tk_ref[t]

        @pl.when((t == 0) | (k != tk_ref[jnp.maximum(t - 1, 0)]))
        def _():
            @pl.when(k == 0)
            def _():
                for s in range(W_SLOTS):
                    @pl.when(s < n_used)
                    def _():
                        for c in weight_copies(s, s):
                            c.start()

            slot = k % W_SLOTS
            for c in weight_copies(k, slot):
                c.wait()
            wgb_ref[...] = wg_buf[slot].astype(BF16)
            wub_ref[...] = wu_buf[slot].astype(BF16)
            wdb_ref[...] = wd_buf[slot].astype(BF16)

            @pl.when(k + W_SLOTS < n_used)
            def _():
                for c in weight_copies(k + W_SLOTS, slot):
                    c.start()

        for c in range(ROW_WORD_CHUNKS):
            u32 = xs_ref[pl.ds(c, MOE_TM, stride=ROW_WORD_CHUNKS), :]
            x_ref[:, 2 * c * LANES:(2 * c + 1) * LANES] = pltpu.bitcast(u32 << 16, F32).astype(BF16)
            x_ref[:, (2 * c + 1) * LANES:(2 * c + 2) * LANES] = (
                pltpu.bitcast(u32 & jnp.uint32(0xFFFF0000), F32).astype(BF16))
        x = x_ref[...]
        a = _dot(x, wgb_ref[...])
        u = _dot(x, wub_ref[...])
        act = (a * jax.nn.sigmoid(a)) * u
        y_ref[...] = _dot(act.astype(BF16), wdb_ref[...])

    @pl.when(t >= nt_ref[0])
    def _():
        y_ref[...] = jnp.zeros(y_ref.shape, y_ref.dtype)


def _experts(tile_k, expert_seq, n_used, xs, w_gate, w_up, w_down):
    npad = xs.shape[0] // ROW_WORD_CHUNKS
    last = lambda t, nt: jnp.minimum(t, jnp.maximum(nt[0] - 1, 0))
    hbm = pl.BlockSpec(memory_space=pl.ANY)
    return pl.pallas_call(
        _experts_kernel,
        grid_spec=pltpu.PrefetchScalarGridSpec(
            num_scalar_prefetch=3,
            grid=(npad // MOE_TM,),
            in_specs=[
                pl.BlockSpec((MOE_TM * ROW_WORD_CHUNKS, LANES), lambda t, tk, es, nt: (last(t, nt), 0)),
                hbm, hbm, hbm,
            ],
            out_specs=pl.BlockSpec((MOE_TM, D_MODEL), lambda t, tk, es, nt: (t, 0)),
            scratch_shapes=[
                pltpu.VMEM((W_SLOTS, D_MODEL, D_EXPERT), F32),
                pltpu.VMEM((W_SLOTS, D_MODEL, D_EXPERT), F32),
                pltpu.VMEM((W_SLOTS, D_EXPERT, D_MODEL), F32),
                pltpu.VMEM((D_MODEL, D_EXPERT), BF16),
                pltpu.VMEM((D_MODEL, D_EXPERT), BF16),
                pltpu.VMEM((D_EXPERT, D_MODEL), BF16),
                pltpu.VMEM((MOE_TM, D_MODEL), BF16),
                pltpu.SemaphoreType.DMA((W_SLOTS, 3)),
            ],
        ),
        out_shape=jax.ShapeDtypeStruct((npad, D_MODEL), F32),
        compiler_params=pltpu.CompilerParams(
            dimension_semantics=("arbitrary",), vmem_limit_bytes=VMEM_LIMIT),
        name="experts",
    )(tile_k, expert_seq, n_used, xs, w_gate, w_up, w_down)


CMB_UNROLL = 8
CMB_SLOTS = 2


def _combine_kernel(pos0_ref, pos1_ref, x2_ref, meta_ref, nw_ref, y_ref, o_ref, buf_ref, sem):
    i = pl.program_id(0)
    pos_refs = (pos0_ref, pos1_ref)

    def row_copy(p, slot, k, r):
        return pltpu.make_async_copy(y_ref.at[pl.ds(p, 1), :], buf_ref.at[slot, k, pl.ds(r, 1), :],
                                     sem.at[slot])

    def issue(tile, slot):
        for c in range(CMB_TT // CMB_UNROLL):
            tok0 = tile * CMB_TT + c * CMB_UNROLL
            rows = [[p[tok0 + u] for u in range(CMB_UNROLL)] for p in pos_refs]
            for u in range(CMB_UNROLL):
                for k in range(EXPERT_TOPK):
                    row_copy(rows[k][u], slot, k, c * CMB_UNROLL + u).start(priority=k % 2)

    @pl.when(i == 0)
    def _():
        issue(0, 0)

    @pl.when(i + 1 < pl.num_programs(0))
    def _():
        issue(i + 1, (i + 1) % CMB_SLOTS)

    slot = i % CMB_SLOTS
    for k in range(EXPERT_TOPK):
        pltpu.make_async_copy(y_ref.at[pl.ds(0, CMB_TT), :], buf_ref.at[slot, k], sem.at[slot]).wait()

    meta = meta_ref[...]
    x3 = (x2_ref[...] + meta[:, META_W1:META_W1 + 1] * buf_ref[slot, 0]
          + meta[:, META_W2:META_W2 + 1] * buf_ref[slot, 1])
    r = lax.rsqrt(jnp.mean(x3 * x3, axis=-1, keepdims=True) + RMS_EPS)
    o_ref[...] = x3 * r * nw_ref[...]


def _combine(pos, x2, meta, norm_w, y):
    t = x2.shape[0]
    return pl.pallas_call(
        _combine_kernel,
        grid_spec=pltpu.PrefetchScalarGridSpec(
            num_scalar_prefetch=2,
            grid=(t // CMB_TT,),
            in_specs=[
                pl.BlockSpec((CMB_TT, D_MODEL), lambda i, p0, p1: (i, 0)),
                pl.BlockSpec((CMB_TT, LANES), lambda i, p0, p1: (i, 0)),
                pl.BlockSpec((1, D_MODEL), lambda i, p0, p1: (0, 0)),
                pl.BlockSpec(memory_space=pl.ANY),
            ],
            out_specs=pl.BlockSpec((CMB_TT, D_MODEL), lambda i, p0, p1: (i, 0)),
            scratch_shapes=[
                pltpu.VMEM((CMB_SLOTS, EXPERT_TOPK, CMB_TT, D_MODEL), F32),
                pltpu.SemaphoreType.DMA((CMB_SLOTS,)),
            ],
        ),
        out_shape=jax.ShapeDtypeStruct((t, D_MODEL), F32),
        compiler_params=pltpu.CompilerParams(
            dimension_semantics=("arbitrary",), vmem_limit_bytes=VMEM_LIMIT),
        name="combine",
    )(pos[0], pos[1], x2, meta, norm_w, y)


def _layout(meta, counts, n_tokens):
    npad = n_tokens * EXPERT_TOPK + N_EXPERTS * MOE_TM
    counts = counts[0, :N_EXPERTS].astype(jnp.int32)
    padded = ((counts + MOE_TM - 1) // MOE_TM) * MOE_TM
    ends = jnp.cumsum(padded)
    starts = ends - padded
    ids = jnp.arange(N_EXPERTS, dtype=jnp.int32)
    pos = []
    for lane_x, lane_rank in ((META_X1, META_RANK1), (META_X2, META_RANK2)):
        hit = meta[:, lane_x].astype(jnp.int32)[:, None] == ids[None, :]
        pos.append(jnp.sum(jnp.where(hit, starts[None, :], 0), axis=-1) + meta[:, lane_rank].astype(jnp.int32))
    tile_start = jnp.arange(npad // MOE_TM, dtype=jnp.int32) * MOE_TM
    tile_expert = jnp.minimum(jnp.sum(ends[None, :] <= tile_start[:, None], axis=1), N_EXPERTS - 1)
    used = counts > 0
    k_of_expert = jnp.cumsum(used.astype(jnp.int32)) - 1
    expert_seq = jnp.sum(jnp.where(used[None, :] & (k_of_expert[None, :] == ids[:, None]), ids[None, :], 0), axis=1)
    tile_k = jnp.sum(jnp.where(tile_expert[:, None] == ids[None, :], k_of_expert[None, :], 0), axis=1)
    n_tiles = (ends[-1] // MOE_TM).astype(jnp.int32)
    n_used = jnp.stack([n_tiles, jnp.sum(used).astype(jnp.int32)])
    return pos, tile_k.astype(jnp.int32), expert_seq.astype(jnp.int32), n_tiles.reshape(1), n_used, npad


def kernel(x, norm_mix_w, w_in, lambda_q1, lambda_k1, lambda_q2, lambda_k2, diff_subln_w, w_out, norm_ffn_w,
           w_router_group, b_router_group, w_router_expert, b_router_expert, w_gate, w_up, w_down, norm_final_w):
    batch, seq, _ = x.shape
    assert seq % ATT_TQ == 0 and seq // MOBA_BLOCK + 8 <= LANES
    assert w_in.shape[0] == 1, "single-layer block"
    n_tokens = batch * seq
    x2d = x.reshape(n_tokens, D_MODEL)

    col = np.arange(IN_WIDTH)
    is_q = (col < MOBA_HEADS * HEAD_DIM) | ((col >= 3 * MOBA_HEADS * HEAD_DIM) & (col < 4 * MOBA_HEADS * HEAD_DIM))
    col_scale = jnp.asarray(np.where(is_q, QK_SCALE * LOG2E, 1.0).astype(np.float32))[None, :]

    proj = _in_proj(x2d, norm_mix_w[0][None, :], w_in[0], col_scale)
    out_a = _moba(proj, jnp.asarray(_alibi_slope_pieces(MOBA_HEADS)), batch, seq)
    out_b = _diff(proj, jnp.asarray(_alibi_slope_pieces(DIFF_HEADS)), lambda_q1, lambda_k1, lambda_q2, lambda_k2,
                  diff_subln_w, batch, seq)

    w_o = w_out[0].astype(BF16)
    w_r = jnp.concatenate([w_router_group[0], w_router_expert[0]], axis=1)
    w_r = jnp.pad(w_r, ((0, 0), (0, LANES - w_r.shape[1])))
    wr_hi = w_r.astype(BF16)
    wr_lo = (w_r - wr_hi.astype(F32)).astype(BF16)
    b_r = jnp.pad(jnp.concatenate([b_router_group[0], b_router_expert[0]]), (0, LANES - N_GROUPS - N_EXPERTS))[None, :]
    x2, hp, meta, counts = _out_proj(x2d, out_a, out_b, w_o, norm_ffn_w[0][None, :],
                                     jnp.concatenate([wr_hi, wr_lo], axis=1), b_r)

    pos, tile_k, expert_seq, n_tiles, n_used, npad = _layout(meta, counts, n_tokens)
    xs = _dispatch(pos, n_tiles, hp, npad)
    y = _experts(tile_k, expert_seq, n_used, xs, w_gate[0], w_up[0], w_down[0])
    out = _combine(pos, x2, meta, norm_final_w[None, :], y)
    return out.reshape(batch, seq, D_MODEL)
```

```python
import functools

import ml_dtypes
import numpy as np
import jax
import jax.numpy as jnp
from jax import lax
from jax.experimental import pallas as pl
from jax.experimental.pallas import tpu as pltpu

F32 = jnp.float32
BF16 = jnp.bfloat16

D_MODEL = 2048
HEAD_DIM = 128
MOBA_HEADS = 8
MOBA_BLOCK = 256
MOBA_TOPK = 3
DIFF_HEADS = 4
DIFF_V_DIM = 256
IN_WIDTH = 6144
N_GROUPS = 4
EXPERTS_PER_GROUP = 8
N_EXPERTS = 32
EXPERT_TOPK = 2
D_EXPERT = 512
RMS_EPS = 1e-6
ALIBI_MAX_BIAS = 8.0
LAMBDA_INIT = 0.8 - 0.6 * float(np.exp(-0.3 * 0))

LANES = 128
QK_SCALE = HEAD_DIM ** -0.5
LOG2E = float(np.log2(np.e))
NEG_BIG = -1e30
M_INIT = -1e38
POS_SPLIT = 64
SLOPE_PIECES = 3
ROW_WORD_CHUNKS = D_MODEL // (2 * LANES)
META_X1, META_X2, META_W1, META_W2, META_RANK1, META_RANK2 = range(6)

IN_TM, IN_TN = 1024, 1024
ATT_TQ = 1024
ATT_TK = 512
ATT_LOOP_TK = 512
KV_UNROLL = 2
HEADS_PER_STEP = 4
DIFF_HEADS_PER_STEP = 1
OUT_TM = 512
MOE_TM = 256
CMB_TT = 256
VMEM_LIMIT = 56 * 1024 * 1024


def _alibi_slope_pieces(n):
    rem = np.exp2(-ALIBI_MAX_BIAS * (np.arange(n, dtype=np.float64) + 1.0) / n) * LOG2E
    pieces = []
    for _ in range(SLOPE_PIECES):
        p = rem.astype(ml_dtypes.bfloat16).astype(np.float64)
        pieces.append(p)
        rem = rem - p
    return np.stack(pieces, axis=1).reshape(-1).astype(np.float32)


def _dot_nt(a, b):
    return lax.dot_general(a, b, (((1,), (1,)), ((), ())), preferred_element_type=F32)


def _dot(a, b):
    return jnp.dot(a, b, preferred_element_type=F32)


def _inproj_kernel(x_ref, nw_ref, cs_ref, w_ref, o_ref, h_ref):
    @pl.when(pl.program_id(1) == 0)
    def _():
        x = x_ref[...]
        r = lax.rsqrt(jnp.mean(x * x, axis=-1, keepdims=True) + RMS_EPS)
        h_ref[...] = (x * r * nw_ref[...]).astype(BF16)

    acc = _dot(h_ref[...], w_ref[...].astype(BF16))
    o_ref[...] = (acc * cs_ref[...]).astype(BF16)


def _in_proj(x2d, norm_w, w_in, col_scale):
    t = x2d.shape[0]
    return pl.pallas_call(
        _inproj_kernel,
        grid=(t // IN_TM, IN_WIDTH // IN_TN),
        in_specs=[
            pl.BlockSpec((IN_TM, D_MODEL), lambda i, j: (i, 0)),
            pl.BlockSpec((1, D_MODEL), lambda i, j: (0, 0)),
            pl.BlockSpec((1, IN_TN), lambda i, j: (0, j)),
            pl.BlockSpec((D_MODEL, IN_TN), lambda i, j: (0, j)),
        ],
        out_specs=pl.BlockSpec((IN_TM, IN_TN), lambda i, j: (i, j)),
        out_shape=jax.ShapeDtypeStruct((t, IN_WIDTH), BF16),
        scratch_shapes=[pltpu.VMEM((IN_TM, D_MODEL), BF16)],
        compiler_params=pltpu.CompilerParams(
            dimension_semantics=("arbitrary", "arbitrary"), vmem_limit_bytes=VMEM_LIMIT),
        name="in_proj",
    )(x2d, norm_w, col_scale, w_in)


def _key_position_features(seq, first_lane):
    pos = lax.broadcasted_iota(jnp.int32, (seq, LANES), 0)
    lane = lax.broadcasted_iota(jnp.int32, (seq, LANES), 1)
    hi = (lane >= first_lane) & (lane < first_lane + SLOPE_PIECES)
    lo = (lane >= first_lane + SLOPE_PIECES) & (lane < first_lane + 2 * SLOPE_PIECES)
    ext = jnp.where(hi, (pos // POS_SPLIT).astype(F32), 0.0)
    return jnp.where(lo, (pos % POS_SPLIT).astype(F32), ext), pos, lane


def _slope_feature_rows(feat_ref, head, width):
    row = lax.broadcasted_iota(jnp.int32, (8, width), 0)
    out = jnp.zeros((8, width), F32)
    for k in range(SLOPE_PIECES):
        c = feat_ref[head * SLOPE_PIECES + k]
        out = jnp.where(row == k, POS_SPLIT * c, out)
        out = jnp.where(row == SLOPE_PIECES + k, c, out)
    return out


def _causal_tile(s):
    row = lax.broadcasted_iota(jnp.int32, s.shape, 0)
    col = lax.broadcasted_iota(jnp.int32, s.shape, 1)
    return jnp.where(col <= row, s, -jnp.inf)


def _online_step(s, m):
    m_new = jnp.maximum(m, jnp.max(s, axis=-1, keepdims=True))
    return m_new, jnp.exp2(m - m_new), jnp.exp2(s - m_new)


def _past_tiles_loop(i, body, init):
    def trip(t, carry):
        for u in range(KV_UNROLL):
            carry = body(t * KV_UNROLL + u, carry)
        return carry

    return lax.fori_loop(0, i * (ATT_TQ // (ATT_LOOP_TK * KV_UNROLL)), trip, init)


def _rows_from(state, start):
    return state if start == 0 else tuple(a[start:] for a in state)


def _rows_replace(state, start, tail):
    if start == 0:
        return tuple(tail)
    return tuple(jnp.concatenate([a[:start], b], axis=0) for a, b in zip(state, tail))


def _moba_kernel(feat_ref, q_ref, k_ref, v_ref, o_ref, kaug_ref, vaug_ref, kmh_ref, kml_ref, qaug_ref, *, seq):
    nb = seq // MOBA_BLOCK
    hp = pl.program_id(1)
    i = pl.program_id(2)

    @pl.when(i == 0)
    def _():
        ext, pos, lane = _key_position_features(seq, nb)
        ext = jnp.where(lane == pos // MOBA_BLOCK, 1.0, ext).astype(BF16)
        ones_col = jnp.where(lane == 0, 1.0, 0.0).astype(BF16)
        for g in range(HEADS_PER_STEP):
            k = k_ref[:, g * HEAD_DIM:(g + 1) * HEAD_DIM]
            kaug_ref[g, :, 0:HEAD_DIM] = k
            kaug_ref[g, :, HEAD_DIM:2 * HEAD_DIM] = ext
            vaug_ref[g, :, 0:HEAD_DIM] = v_ref[:, g * HEAD_DIM:(g + 1) * HEAD_DIM]
            vaug_ref[g, :, HEAD_DIM:2 * HEAD_DIM] = ones_col
            km = jnp.mean(k.astype(F32).reshape(nb, MOBA_BLOCK, HEAD_DIM), axis=1)
            kmh = km.astype(BF16)
            kmh_ref[g] = jnp.zeros((LANES, HEAD_DIM), BF16)
            kml_ref[g] = jnp.zeros((LANES, HEAD_DIM), BF16)
            kmh_ref[g, 0:nb, :] = kmh
            kml_ref[g, 0:nb, :] = (km - kmh.astype(F32)).astype(BF16)

    blk = lax.broadcasted_iota(jnp.int32, (nb, ATT_TQ), 0)
    col = lax.broadcasted_iota(jnp.int32, (nb, ATT_TQ), 1)
    own = i * (ATT_TQ // MOBA_BLOCK) + col // MOBA_BLOCK
    blk_f = blk.astype(F32)
    for g in range(HEADS_PER_STEP):
        q = q_ref[:, g * HEAD_DIM:(g + 1) * HEAD_DIM]
        gate = (_dot_nt(kmh_ref[g], q) + _dot_nt(kml_ref[g], q))[0:nb, :]
        gate = jnp.where(blk < own, gate, -jnp.inf)
        sel = jnp.zeros((nb, ATT_TQ), F32)
        for _ in range(MOBA_TOPK):
            mx = jnp.max(gate, axis=0, keepdims=True)
            idx = jnp.min(jnp.where(gate == mx, blk_f, float(LANES)), axis=0, keepdims=True)
            pick = (blk_f == idx) & (mx > -jnp.inf)
            sel = jnp.where(pick, 1.0, sel)
            gate = jnp.where(pick, -jnp.inf, gate)
        bias = jnp.where((sel > 0.0) | (blk == own), 0.0, NEG_BIG)
        feat = jnp.concatenate(
            [bias, _slope_feature_rows(feat_ref, hp * HEADS_PER_STEP + g, ATT_TQ),
             jnp.zeros((LANES - nb - 8, ATT_TQ), F32)], axis=0)
        qaug_ref[g, :, 0:HEAD_DIM] = q
        qaug_ref[g, :, HEAD_DIM:2 * HEAD_DIM] = feat.T.astype(BF16)

    def tile_step(g, r, nk, state, row0=0, causal=False):
        m, acc = state
        s = _dot_nt(qaug_ref[g, row0:, :], kaug_ref[g, pl.ds(r, nk), :])
        if causal:
            s = _causal_tile(s)
        m, alpha, p = _online_step(s, m)
        return m, alpha * acc + _dot(p.astype(BF16), vaug_ref[g, pl.ds(r, nk), :])

    def body(j, states):
        r = pl.multiple_of(j * ATT_LOOP_TK, ATT_LOOP_TK)
        return tuple(tile_step(g, r, ATT_LOOP_TK, states[g]) for g in range(HEADS_PER_STEP))

    init = tuple((jnp.full((ATT_TQ, 1), M_INIT, F32), jnp.zeros((ATT_TQ, 2 * HEAD_DIM), F32))
                 for _ in range(HEADS_PER_STEP))
    states = list(_past_tiles_loop(i, body, init))
    for d in range(ATT_TQ // ATT_TK):
        r = pl.multiple_of(i * ATT_TQ + d * ATT_TK, ATT_TK)
        for g in range(HEADS_PER_STEP):
            tail = tile_step(g, r, ATT_TK, _rows_from(states[g], d * ATT_TK), row0=d * ATT_TK, causal=True)
            states[g] = _rows_replace(states[g], d * ATT_TK, tail)
    for g in range(HEADS_PER_STEP):
        acc = states[g][1]
        o_ref[:, g * HEAD_DIM:(g + 1) * HEAD_DIM] = (
            acc[:, 0:HEAD_DIM] / acc[:, HEAD_DIM:HEAD_DIM + 1]).astype(BF16)


def _moba(proj, feats, batch, seq):
    nq = seq // ATT_TQ
    width = HEADS_PER_STEP * HEAD_DIM
    q_col, k_col, v_col = 0, MOBA_HEADS // HEADS_PER_STEP, 2 * MOBA_HEADS // HEADS_PER_STEP
    return pl.pallas_call(
        functools.partial(_moba_kernel, seq=seq),
        grid_spec=pltpu.PrefetchScalarGridSpec(
            num_scalar_prefetch=1,
            grid=(batch, MOBA_HEADS // HEADS_PER_STEP, nq),
            in_specs=[
                pl.BlockSpec((ATT_TQ, width), lambda b, h, i, s: (b * nq + i, q_col + h)),
                pl.BlockSpec((seq, width), lambda b, h, i, s: (b, k_col + h)),
                pl.BlockSpec((seq, width), lambda b, h, i, s: (b, v_col + h)),
            ],
            out_specs=pl.BlockSpec((ATT_TQ, width), lambda b, h, i, s: (b * nq + i, h)),
            scratch_shapes=[
                pltpu.VMEM((HEADS_PER_STEP, seq, 2 * HEAD_DIM), BF16),
                pltpu.VMEM((HEADS_PER_STEP, seq, 2 * HEAD_DIM), BF16),
                pltpu.VMEM((HEADS_PER_STEP, LANES, HEAD_DIM), BF16),
                pltpu.VMEM((HEADS_PER_STEP, LANES, HEAD_DIM), BF16),
                pltpu.VMEM((HEADS_PER_STEP, ATT_TQ, 2 * HEAD_DIM), BF16),
            ],
        ),
        out_shape=jax.ShapeDtypeStruct((batch * seq, MOBA_HEADS * HEAD_DIM), BF16),
        compiler_params=pltpu.CompilerParams(
            dimension_semantics=("arbitrary", "arbitrary", "arbitrary"), vmem_limit_bytes=VMEM_LIMIT),
        name="moba",
    )(feats, proj, proj, proj)


def _diff_kernel(feat_ref, lq1_ref, lk1_ref, lq2_ref, lk2_ref, sw_ref,
                 q_ref, k_ref, v_ref, o_ref, kaug_ref, qaug_ref, *, seq):
    hg = pl.program_id(1)
    i = pl.program_id(2)
    n_chains = 2 * DIFF_HEADS_PER_STEP

    @pl.when(i == 0)
    def _():
        ext = _key_position_features(seq, 0)[0].astype(BF16)
        for n in range(n_chains):
            kaug_ref[n, :, 0:HEAD_DIM] = k_ref[:, n * HEAD_DIM:(n + 1) * HEAD_DIM]
            kaug_ref[n, :, HEAD_DIM:2 * HEAD_DIM] = ext

    for n in range(n_chains):
        if n % 2 == 0:
            head = hg * DIFF_HEADS_PER_STEP + n // 2
            feat = jnp.concatenate([_slope_feature_rows(feat_ref, head, ATT_TQ),
                                    jnp.zeros((LANES - 8, ATT_TQ), F32)], axis=0).T.astype(BF16)
        qaug_ref[n, :, 0:HEAD_DIM] = q_ref[:, n * HEAD_DIM:(n + 1) * HEAD_DIM]
        qaug_ref[n, :, HEAD_DIM:2 * HEAD_DIM] = feat

    def tile_step(n, r, nk, state, row0=0, causal=False):
        m, l, acc = state
        s = _dot_nt(qaug_ref[n, row0:, :], kaug_ref[n, pl.ds(r, nk), :])
        if causal:
            s = _causal_tile(s)
        m, alpha, p = _online_step(s, m)
        v = v_ref[pl.ds(r, nk), (n // 2) * DIFF_V_DIM:(n // 2 + 1) * DIFF_V_DIM]
        p = p.astype(BF16)
        return (m, alpha * l + jnp.sum(p.astype(F32), axis=-1, keepdims=True), alpha * acc + _dot(p, v))

    def body(j, states):
        r = pl.multiple_of(j * ATT_LOOP_TK, ATT_LOOP_TK)
        return tuple(tile_step(n, r, ATT_LOOP_TK, states[n]) for n in range(n_chains))

    init = tuple((jnp.full((ATT_TQ, 1), M_INIT, F32), jnp.zeros((ATT_TQ, 1), F32),
                  jnp.zeros((ATT_TQ, DIFF_V_DIM), F32)) for _ in range(n_chains))
    states = list(_past_tiles_loop(i, body, init))
    for d in range(ATT_TQ // ATT_TK):
        r = pl.multiple_of(i * ATT_TQ + d * ATT_TK, ATT_TK)
        for n in range(n_chains):
            tail = tile_step(n, r, ATT_TK, _rows_from(states[n], d * ATT_TK), row0=d * ATT_TK, causal=True)
            states[n] = _rows_replace(states[n], d * ATT_TK, tail)

    lam = (jnp.exp(jnp.sum(lq1_ref[...] * lk1_ref[...], axis=-1, keepdims=True))
           - jnp.exp(jnp.sum(lq2_ref[...] * lk2_ref[...], axis=-1, keepdims=True))
           + LAMBDA_INIT)
    for hh in range(DIFF_HEADS_PER_STEP):
        (_, l1, acc1), (_, l2, acc2) = states[2 * hh], states[2 * hh + 1]
        o = acc1 / l1 - lam * (acc2 / l2)
        y = o * lax.rsqrt(jnp.mean(o * o, axis=-1, keepdims=True) + RMS_EPS)
        o_ref[:, hh * DIFF_V_DIM:(hh + 1) * DIFF_V_DIM] = ((y * sw_ref[...]) * (1.0 - LAMBDA_INIT)).astype(BF16)


def _diff(proj, feats, lq1, lk1, lq2, lk2, subln_w, batch, seq):
    nq = seq // ATT_TQ
    width = DIFF_HEADS_PER_STEP * 2 * HEAD_DIM
    groups = DIFF_HEADS // DIFF_HEADS_PER_STEP
    q_col = 3 * MOBA_HEADS * HEAD_DIM // width
    k_col = q_col + groups
    v_col = k_col + groups
    vec = lambda n: pl.BlockSpec((1, n), lambda b, h, i, s: (0, 0))
    return pl.pallas_call(
        functools.partial(_diff_kernel, seq=seq),
        grid_spec=pltpu.PrefetchScalarGridSpec(
            num_scalar_prefetch=1,
            grid=(batch, groups, nq),
            in_specs=[
                vec(HEAD_DIM), vec(HEAD_DIM), vec(HEAD_DIM), vec(HEAD_DIM), vec(DIFF_V_DIM),
                pl.BlockSpec((ATT_TQ, width), lambda b, h, i, s: (b * nq + i, q_col + h)),
                pl.BlockSpec((seq, width), lambda b, h, i, s: (b, k_col + h)),
                pl.BlockSpec((seq, width), lambda b, h, i, s: (b, v_col + h)),
            ],
            out_specs=pl.BlockSpec((ATT_TQ, width), lambda b, h, i, s: (b * nq + i, h)),
            scratch_shapes=[
                pltpu.VMEM((2 * DIFF_HEADS_PER_STEP, seq, 2 * HEAD_DIM), BF16),
                pltpu.VMEM((2 * DIFF_HEADS_PER_STEP, ATT_TQ, 2 * HEAD_DIM), BF16),
            ],
        ),
        out_shape=jax.ShapeDtypeStruct((batch * seq, DIFF_HEADS * DIFF_V_DIM), BF16),
        compiler_params=pltpu.CompilerParams(
            dimension_semantics=("arbitrary", "arbitrary", "arbitrary"), vmem_limit_bytes=VMEM_LIMIT),
        name="diff_attn",
    )(feats, lq1, lk1, lq2, lk2, subln_w, proj, proj, proj)


def _route_tile(lg, run, tri):
    lane = lax.broadcasted_iota(jnp.int32, lg.shape, 1).astype(F32)
    rmax = lambda v: jnp.max(v, axis=-1, keepdims=True)
    first = lambda v, m: jnp.min(jnp.where(v == m, lane, float(LANES)), axis=-1, keepdims=True)

    gl = jnp.where(lane < N_GROUPS, lg, -jnp.inf)
    gmax = rmax(gl)
    group = first(gl, gmax)
    p_group = 1.0 / jnp.sum(jnp.exp(gl - gmax), axis=-1, keepdims=True)

    lo = N_GROUPS + EXPERTS_PER_GROUP * group
    el = jnp.where((lane >= lo) & (lane < lo + EXPERTS_PER_GROUP), lg, -jnp.inf)
    e1 = rmax(el)
    lane1 = first(el, e1)
    el = jnp.where(lane == lane1, -jnp.inf, el)
    e2 = rmax(el)
    lane2 = first(el, e2)
    t2 = jnp.exp(e2 - e1)
    w1 = p_group / (1.0 + t2)
    w2 = p_group * t2 / (1.0 + t2)
    x1 = lane1 - N_GROUPS
    x2 = lane2 - N_GROUPS

    hit1 = lane == x1
    hit2 = lane == x2
    onehot = jnp.where(hit1 | hit2, 1.0, 0.0)
    before = _dot(tri, onehot.astype(BF16)) + run
    rank1 = jnp.sum(jnp.where(hit1, before, 0.0), axis=-1, keepdims=True)
    rank2 = jnp.sum(jnp.where(hit2, before, 0.0), axis=-1, keepdims=True)

    meta = jnp.zeros(lg.shape, F32)
    for k, v in ((META_X1, x1), (META_X2, x2), (META_W1, w1), (META_W2, w2),
                 (META_RANK1, rank1), (META_RANK2, rank2)):
        meta = jnp.where(lane == k, v, meta)
    return meta, run + jnp.sum(onehot, axis=0, keepdims=True)


def _outproj_kernel(x_ref, oa_ref, ob_ref, w_ref, nw_ref, wr_ref, br_ref,
                    x2_ref, hp_ref, meta_ref, cnt_ref, run_ref, tri_ref):
    @pl.when(pl.program_id(0) == 0)
    def _():
        row = lax.broadcasted_iota(jnp.int32, tri_ref.shape, 0)
        col = lax.broadcasted_iota(jnp.int32, tri_ref.shape, 1)
        tri_ref[...] = jnp.where(col < row, 1.0, 0.0).astype(BF16)
        run_ref[...] = jnp.zeros(run_ref.shape, F32)

    half = D_MODEL // 2
    x2 = x_ref[...] + _dot(oa_ref[...], w_ref[0:half, :]) + _dot(ob_ref[...], w_ref[half:, :])
    x2_ref[...] = x2
    r = lax.rsqrt(jnp.mean(x2 * x2, axis=-1, keepdims=True) + RMS_EPS)
    h = x2 * r * nw_ref[...]
    hh = h.astype(BF16)
    hl = (h - hh.astype(F32)).astype(BF16)
    prod = _dot(jnp.concatenate([hh, hl], axis=0), wr_ref[...])
    lg = prod[:OUT_TM, :LANES] + prod[OUT_TM:, :LANES] + prod[:OUT_TM, LANES:] + br_ref[...]
    meta, run = _route_tile(lg, run_ref[...], tri_ref[...])
    meta_ref[...] = meta
    run_ref[...] = run
    cnt_ref[...] = run
    bits = pltpu.bitcast(hh.astype(F32), jnp.uint32)
    for c in range(ROW_WORD_CHUNKS):
        lo = bits[:, 2 * c * LANES:(2 * c + 1) * LANES]
        hi = bits[:, (2 * c + 1) * LANES:(2 * c + 2) * LANES]
        hp_ref[pl.ds(c, OUT_TM, stride=ROW_WORD_CHUNKS), :] = (lo >> 16) | (hi & jnp.uint32(0xFFFF0000))


def _out_proj(x2d, out_a, out_b, w_o, norm_w, w_r, b_r):
    t = x2d.shape[0]
    half = D_MODEL // 2
    row = lambda n: pl.BlockSpec((OUT_TM, n), lambda i: (i, 0))
    full = lambda r, c: pl.BlockSpec((r, c), lambda i: (0, 0))
    return pl.pallas_call(
        _outproj_kernel,
        grid=(t // OUT_TM,),
        in_specs=[row(D_MODEL), row(half), row(half), full(D_MODEL, D_MODEL),
                  full(1, D_MODEL), full(D_MODEL, 2 * LANES), full(1, LANES)],
        out_specs=[row(D_MODEL), pl.BlockSpec((OUT_TM * ROW_WORD_CHUNKS, LANES), lambda i: (i, 0)), row(LANES),
                   full(1, LANES)],
        out_shape=[jax.ShapeDtypeStruct((t, D_MODEL), F32),
                   jax.ShapeDtypeStruct((t * ROW_WORD_CHUNKS, LANES), jnp.uint32),
                   jax.ShapeDtypeStruct((t, LANES), F32),
                   jax.ShapeDtypeStruct((1, LANES), F32)],
        scratch_shapes=[pltpu.VMEM((1, LANES), F32), pltpu.VMEM((OUT_TM, OUT_TM), BF16)],
        compiler_params=pltpu.CompilerParams(
            dimension_semantics=("arbitrary",), vmem_limit_bytes=VMEM_LIMIT),
        name="out_proj",
    )(x2d, out_a, out_b, w_o, norm_w, w_r, b_r)


GATHER_UNROLL = 16
CLEAR_UNROLL = 32
DSP_ROWS = 4 * MOE_TM


def _dispatch_kernel(pos0_ref, pos1_ref, nt_ref, hp_hbm, xs_ref, hp_ref, src_ref, sem):
    t = pl.program_id(0)
    pos_refs = (pos0_ref, pos1_ref)

    @pl.when(t == 0)
    def _():
        load = pltpu.make_async_copy(hp_hbm, hp_ref, sem)
        load.start()

        def clear(p, _):
            src_ref[p] = 0
            return 0
        lax.fori_loop(0, src_ref.shape[0], clear, 0, unroll=CLEAR_UNROLL)

        def place(c, _):
            tok0 = c * (GATHER_UNROLL // EXPERT_TOPK)
            rows = [p[tok0 + u] for u in range(GATHER_UNROLL // EXPERT_TOPK) for p in pos_refs]
            for n, row in enumerate(rows):
                src_ref[row] = tok0 + n // EXPERT_TOPK
            return 0
        lax.fori_loop(0, pos_refs[0].shape[0] * EXPERT_TOPK // GATHER_UNROLL, place, 0)
        load.wait()

    n_valid = jnp.clip(nt_ref[0] * MOE_TM - t * DSP_ROWS, 0, DSP_ROWS)

    def gather(c, _):
        toks = [src_ref[t * DSP_ROWS + c * GATHER_UNROLL + u] for u in range(GATHER_UNROLL)]
        for u in range(GATHER_UNROLL):
            xs_ref[c * GATHER_UNROLL + u] = hp_ref[toks[u]]
        return 0
    lax.fori_loop(0, n_valid // GATHER_UNROLL, gather, 0)

    def zero_tile(c, _):
        xs_ref[pl.ds(pl.multiple_of(c * MOE_TM, MOE_TM), MOE_TM)] = jnp.zeros(
            (MOE_TM,) + xs_ref.shape[1:], xs_ref.dtype)
        return 0
    lax.fori_loop(n_valid // MOE_TM, DSP_ROWS // MOE_TM, zero_tile, 0)


def _dispatch(pos, n_tiles, hp, npad):
    n_tokens = hp.shape[0] // ROW_WORD_CHUNKS
    hp3 = hp.reshape(n_tokens, ROW_WORD_CHUNKS, LANES)
    xs = pl.pallas_call(
        _dispatch_kernel,
        grid_spec=pltpu.PrefetchScalarGridSpec(
            num_scalar_prefetch=3,
            grid=(npad // DSP_ROWS,),
            in_specs=[pl.BlockSpec(memory_space=pl.ANY)],
            out_specs=pl.BlockSpec((DSP_ROWS, ROW_WORD_CHUNKS, LANES), lambda t, p0, p1, n: (t, 0, 0)),
            scratch_shapes=[pltpu.VMEM(hp3.shape, hp3.dtype), pltpu.SMEM((npad,), jnp.int32),
                            pltpu.SemaphoreType.DMA(())],
        ),
        out_shape=jax.ShapeDtypeStruct((npad, ROW_WORD_CHUNKS, LANES), hp.dtype),
        compiler_params=pltpu.CompilerParams(
            dimension_semantics=("arbitrary",), vmem_limit_bytes=VMEM_LIMIT),
        name="dispatch",
    )(pos[0], pos[1], n_tiles, hp3)
    return xs.reshape(npad * ROW_WORD_CHUNKS, LANES)


W_SLOTS = 2
EXP_TILES = 2


def _experts_kernel(tk_ref, eseq_ref, nt_ref, xs_ref, wg_hbm, wu_hbm, wd_hbm, y_ref,
                    wg_buf, wu_buf, wd_buf, wgb_ref, wub_ref, wdb_ref, x_ref, sem):
    n_used = nt_ref[1]
    hbm_bufs = ((wg_hbm, wg_buf), (wu_hbm, wu_buf), (wd_hbm, wd_buf))

    def weight_copies(k, slot):
        e = eseq_ref[k]
        return [pltpu.make_async_copy(hbm.at[e], buf.at[slot], sem.at[slot, n])
                for n, (hbm, buf) in enumerate(hbm_bufs)]

    for sub in range(EXP_TILES):
        _expert_tile(pl.program_id(0) * EXP_TILES + sub, sub, tk_ref, nt_ref, n_used, weight_copies,
                     xs_ref, y_ref, wg_buf, wu_buf, wd_buf, wgb_ref, wub_ref, wdb_ref, x_ref)


def _expert_tile(t, sub, tk_ref, nt_ref, n_used, weight_copies,
                 xs_ref, y_ref, wg_buf, wu_buf, wd_buf, wgb_ref, wub_ref, wdb_ref, x_ref):
    xs_row0 = sub * MOE_TM * ROW_WORD_CHUNKS
    y_rows = pl.ds(sub * MOE_TM, MOE_TM)

    @pl.when(t < nt_ref[0])
    def _():
        k = tk_ref[t]

        @pl.when((t == 0) | (k != tk_ref[jnp.maximum(t - 1, 0)]))
        def _():
            @pl.when(k == 0)
            def _():
                for s in range(W_SLOTS):
                    @pl.when(s < n_used)
                    def _():
                        for c in weight_copies(s, s):
                            c.start()

            slot = k % W_SLOTS
            for c in weight_copies(k, slot):
                c.wait()
            wgb_ref[...] = wg_buf[slot].astype(BF16)
            wub_ref[...] = wu_buf[slot].astype(BF16)
            wdb_ref[...] = wd_buf[slot].astype(BF16)

            @pl.when(k + W_SLOTS < n_used)
            def _():
                for c in weight_copies(k + W_SLOTS, slot):
                    c.start()

        for c in range(ROW_WORD_CHUNKS):
            u32 = xs_ref[pl.ds(xs_row0 + c, MOE_TM, stride=ROW_WORD_CHUNKS), :]
            x_ref[:, 2 * c * LANES:(2 * c + 1) * LANES] = pltpu.bitcast(u32 << 16, F32).astype(BF16)
            x_ref[:, (2 * c + 1) * LANES:(2 * c + 2) * LANES] = (
                pltpu.bitcast(u32 & jnp.uint32(0xFFFF0000), F32).astype(BF16))
        x = x_ref[...]
        a = _dot(x, wgb_ref[...])
        u = _dot(x, wub_ref[...])
        act = (a * jax.nn.sigmoid(a)) * u
        y_ref[y_rows, :] = _dot(act.astype(BF16), wdb_ref[...])

    @pl.when(t >= nt_ref[0])
    def _():
        y_ref[y_rows, :] = jnp.zeros((MOE_TM, D_MODEL), y_ref.dtype)


def _experts(tile_k, expert_seq, n_used, xs, w_gate, w_up, w_down):
    npad = xs.shape[0] // ROW_WORD_CHUNKS
    rows = EXP_TILES * MOE_TM
    last = lambda t, nt: jnp.minimum(t, jnp.maximum(nt[0] - 1, 0) // EXP_TILES)
    hbm = pl.BlockSpec(memory_space=pl.ANY)
    return pl.pallas_call(
        _experts_kernel,
        grid_spec=pltpu.PrefetchScalarGridSpec(
            num_scalar_prefetch=3,
            grid=(npad // rows,),
            in_specs=[
                pl.BlockSpec((rows * ROW_WORD_CHUNKS, LANES), lambda t, tk, es, nt: (last(t, nt), 0)),
                hbm, hbm, hbm,
            ],
            out_specs=pl.BlockSpec((rows, D_MODEL), lambda t, tk, es, nt: (t, 0)),
            scratch_shapes=[
                pltpu.VMEM((W_SLOTS, D_MODEL, D_EXPERT), F32),
                pltpu.VMEM((W_SLOTS, D_MODEL, D_EXPERT), F32),
                pltpu.VMEM((W_SLOTS, D_EXPERT, D_MODEL), F32),
                pltpu.VMEM((D_MODEL, D_EXPERT), BF16),
                pltpu.VMEM((D_MODEL, D_EXPERT), BF16),
                pltpu.VMEM((D_EXPERT, D_MODEL), BF16),
                pltpu.VMEM((MOE_TM, D_MODEL), BF16),
                pltpu.SemaphoreType.DMA((W_SLOTS, 3)),
            ],
        ),
        out_shape=jax.ShapeDtypeStruct((npad, D_MODEL), F32),
        compiler_params=pltpu.CompilerParams(
            dimension_semantics=("arbitrary",), vmem_limit_bytes=VMEM_LIMIT),
        name="experts",
    )(tile_k, expert_seq, n_used, xs, w_gate, w_up, w_down)


CMB_UNROLL = 8
CMB_SLOTS = 2


def _combine_kernel(pos0_ref, pos1_ref, x2_ref, meta_ref, nw_ref, y_ref, o_ref, buf_ref, sem):
    i = pl.program_id(0)
    pos_refs = (pos0_ref, pos1_ref)

    def row_copy(p, slot, k, r):
        return pltpu.make_async_copy(y_ref.at[pl.ds(p, 1), :], buf_ref.at[slot, k, pl.ds(r, 1), :],
                                     sem.at[slot])

    def issue(tile, slot):
        for c in range(CMB_TT // CMB_UNROLL):
            tok0 = tile * CMB_TT + c * CMB_UNROLL
            rows = [[p[tok0 + u] for u in range(CMB_UNROLL)] for p in pos_refs]
            for u in range(CMB_UNROLL):
                for k in range(EXPERT_TOPK):
                    row_copy(rows[k][u], slot, k, c * CMB_UNROLL + u).start(priority=k % 2)

    @pl.when(i == 0)
    def _():
        issue(0, 0)

    @pl.when(i + 1 < pl.num_programs(0))
    def _():
        issue(i + 1, (i + 1) % CMB_SLOTS)

    slot = i % CMB_SLOTS
    for k in range(EXPERT_TOPK):
        pltpu.make_async_copy(y_ref.at[pl.ds(0, CMB_TT), :], buf_ref.at[slot, k], sem.at[slot]).wait()

    meta = meta_ref[...]
    x3 = (x2_ref[...] + meta[:, META_W1:META_W1 + 1] * buf_ref[slot, 0]
          + meta[:, META_W2:META_W2 + 1] * buf_ref[slot, 1])
    r = lax.rsqrt(jnp.mean(x3 * x3, axis=-1, keepdims=True) + RMS_EPS)
    o_ref[...] = x3 * r * nw_ref[...]


def _combine(pos, x2, meta, norm_w, y):
    t = x2.shape[0]
    return pl.pallas_call(
        _combine_kernel,
        grid_spec=pltpu.PrefetchScalarGridSpec(
            num_scalar_prefetch=2,
            grid=(t // CMB_TT,),
            in_specs=[
                pl.BlockSpec((CMB_TT, D_MODEL), lambda i, p0, p1: (i, 0)),
                pl.BlockSpec((CMB_TT, LANES), lambda i, p0, p1: (i, 0)),
                pl.BlockSpec((1, D_MODEL), lambda i, p0, p1: (0, 0)),
                pl.BlockSpec(memory_space=pl.ANY),
            ],
            out_specs=pl.BlockSpec((CMB_TT, D_MODEL), lambda i, p0, p1: (i, 0)),
            scratch_shapes=[
                pltpu.VMEM((CMB_SLOTS, EXPERT_TOPK, CMB_TT, D_MODEL), F32),
                pltpu.SemaphoreType.DMA((CMB_SLOTS,)),
            ],
        ),
        out_shape=jax.ShapeDtypeStruct((t, D_MODEL), F32),
        compiler_params=pltpu.CompilerParams(
            dimension_semantics=("arbitrary",), vmem_limit_bytes=VMEM_LIMIT),
        name="combine",
    )(pos[0], pos[1], x2, meta, norm_w, y)


def _layout(meta, counts, n_tokens):
    npad = n_tokens * EXPERT_TOPK + N_EXPERTS * MOE_TM
    counts = counts[0, :N_EXPERTS].astype(jnp.int32)
    padded = ((counts + MOE_TM - 1) // MOE_TM) * MOE_TM
    ends = jnp.cumsum(padded)
    starts = ends - padded
    ids = jnp.arange(N_EXPERTS, dtype=jnp.int32)
    pos = []
    for lane_x, lane_rank in ((META_X1, META_RANK1), (META_X2, META_RANK2)):
        hit = meta[:, lane_x].astype(jnp.int32)[:, None] == ids[None, :]
        pos.append(jnp.sum(jnp.where(hit, starts[None, :], 0), axis=-1) + meta[:, lane_rank].astype(jnp.int32))
    tile_start = jnp.arange(npad // MOE_TM, dtype=jnp.int32) * MOE_TM
    tile_expert = jnp.minimum(jnp.sum(ends[None, :] <= tile_start[:, None], axis=1), N_EXPERTS - 1)
    used = counts > 0
    k_of_expert = jnp.cumsum(used.astype(jnp.int32)) - 1
    expert_seq = jnp.sum(jnp.where(used[None, :] & (k_of_expert[None, :] == ids[:, None]), ids[None, :], 0), axis=1)
    tile_k = jnp.sum(jnp.where(tile_expert[:, None] == ids[None, :], k_of_expert[None, :], 0), axis=1)
    n_tiles = (ends[-1] // MOE_TM).astype(jnp.int32)
    n_used = jnp.stack([n_tiles, jnp.sum(used).astype(jnp.int32)])
    return pos, tile_k.astype(jnp.int32), expert_seq.astype(jnp.int32), n_tiles.reshape(1), n_used, npad


def kernel(x, norm_mix_w, w_in, lambda_q1, lambda_k1, lambda_q2, lambda_k2, diff_subln_w, w_out, norm_ffn_w,
           w_router_group, b_router_group, w_router_expert, b_router_expert, w_gate, w_up, w_down, norm_final_w):
    batch, seq, _ = x.shape
    assert seq % ATT_TQ == 0 and seq // MOBA_BLOCK + 8 <= LANES
    assert w_in.shape[0] == 1, "single-layer block"
    n_tokens = batch * seq
    x2d = x.reshape(n_tokens, D_MODEL)

    col = np.arange(IN_WIDTH)
    is_q = (col < MOBA_HEADS * HEAD_DIM) | ((col >= 3 * MOBA_HEADS * HEAD_DIM) & (col < 4 * MOBA_HEADS * HEAD_DIM))
    col_scale = jnp.asarray(np.where(is_q, QK_SCALE * LOG2E, 1.0).astype(np.float32))[None, :]

    proj = _in_proj(x2d, norm_mix_w[0][None, :], w_in[0], col_scale)
    out_a = _moba(proj, jnp.asarray(_alibi_slope_pieces(MOBA_HEADS)), batch, seq)
    out_b = _diff(proj, jnp.asarray(_alibi_slope_pieces(DIFF_HEADS)), lambda_q1, lambda_k1, lambda_q2, lambda_k2,
                  diff_subln_w, batch, seq)

    w_o = w_out[0].astype(BF16)
    w_r = jnp.concatenate([w_router_group[0], w_router_expert[0]], axis=1)
    w_r = jnp.pad(w_r, ((0, 0), (0, LANES - w_r.shape[1])))
    wr_hi = w_r.astype(BF16)
    wr_lo = (w_r - wr_hi.astype(F32)).astype(BF16)
    b_r = jnp.pad(jnp.concatenate([b_router_group[0], b_router_expert[0]]), (0, LANES - N_GROUPS - N_EXPERTS))[None, :]
    x2, hp, meta, counts = _out_proj(x2d, out_a, out_b, w_o, norm_ffn_w[0][None, :],
                                     jnp.concatenate([wr_hi, wr_lo], axis=1), b_r)

    pos, tile_k, expert_seq, n_tiles, n_used, npad = _layout(meta, counts, n_tokens)
    xs = _dispatch(pos, n_tiles, hp, npad)
    y = _experts(tile_k, expert_seq, n_used, xs, w_gate[0], w_up[0], w_down[0])
    out = _combine(pos, x2, meta, norm_final_w[None, :], y)
    return out.reshape(batch, seq, D_MODEL)
```

```python
import functools

import ml_dtypes
import numpy as np
import jax
import jax.numpy as jnp
from jax import lax
from jax.experimental import pallas as pl
from jax.experimental.pallas import tpu as pltpu

F32 = jnp.float32
BF16 = jnp.bfloat16

D_MODEL = 2048
HEAD_DIM = 128
MOBA_HEADS = 8
MOBA_BLOCK = 256
MOBA_TOPK = 3
DIFF_HEADS = 4
DIFF_V_DIM = 256
IN_WIDTH = 6144
N_GROUPS = 4
EXPERTS_PER_GROUP = 8
N_EXPERTS = 32
EXPERT_TOPK = 2
D_EXPERT = 512
RMS_EPS = 1e-6
ALIBI_MAX_BIAS = 8.0
LAMBDA_INIT = 0.8 - 0.6 * float(np.exp(-0.3 * 0))

LANES = 128
QK_SCALE = HEAD_DIM ** -0.5
LOG2E = float(np.log2(np.e))
NEG_BIG = -1e30
M_INIT = -1e38
POS_SPLIT = 64
SLOPE_PIECES = 3
ROW_WORD_CHUNKS = D_MODEL // (2 * LANES)
META_X1, META_X2, META_W1, META_W2, META_RANK1, META_RANK2 = range(6)

IN_TM, IN_TN = 1024, 1024
ATT_TQ = 1024
ATT_TK = 512
ATT_LOOP_TK = 512
KV_UNROLL = 2
HEADS_PER_STEP = 4
DIFF_HEADS_PER_STEP = 1
OUT_TM = 512
MOE_TM = 256
CMB_TT = 512
VMEM_LIMIT = 56 * 1024 * 1024


def _alibi_slope_pieces(n):
    rem = np.exp2(-ALIBI_MAX_BIAS * (np.arange(n, dtype=np.float64) + 1.0) / n) * LOG2E
    pieces = []
    for _ in range(SLOPE_PIECES):
        p = rem.astype(ml_dtypes.bfloat16).astype(np.float64)
        pieces.append(p)
        rem = rem - p
    return np.stack(pieces, axis=1).reshape(-1).astype(np.float32)


def _dot_nt(a, b):
    return lax.dot_general(a, b, (((1,), (1,)), ((), ())), preferred_element_type=F32)


def _dot(a, b):
    return jnp.dot(a, b, preferred_element_type=F32)


def _inproj_kernel(x_ref, nw_ref, cs_ref, w_ref, o_ref, h_ref):
    @pl.when(pl.program_id(1) == 0)
    def _():
        x = x_ref[...]
        r = lax.rsqrt(jnp.mean(x * x, axis=-1, keepdims=True) + RMS_EPS)
        h_ref[...] = (x * r * nw_ref[...]).astype(BF16)

    acc = _dot(h_ref[...], w_ref[...].astype(BF16))
    o_ref[...] = (acc * cs_ref[...]).astype(BF16)


def _in_proj(x2d, norm_w, w_in, col_scale):
    t = x2d.shape[0]
    return pl.pallas_call(
        _inproj_kernel,
        grid=(t // IN_TM, IN_WIDTH // IN_TN),
        in_specs=[
            pl.BlockSpec((IN_TM, D_MODEL), lambda i, j: (i, 0)),
            pl.BlockSpec((1, D_MODEL), lambda i, j: (0, 0)),
            pl.BlockSpec((1, IN_TN), lambda i, j: (0, j)),
            pl.BlockSpec((D_MODEL, IN_TN), lambda i, j: (0, j)),
        ],
        out_specs=pl.BlockSpec((IN_TM, IN_TN), lambda i, j: (i, j)),
        out_shape=jax.ShapeDtypeStruct((t, IN_WIDTH), BF16),
        scratch_shapes=[pltpu.VMEM((IN_TM, D_MODEL), BF16)],
        compiler_params=pltpu.CompilerParams(
            dimension_semantics=("arbitrary", "arbitrary"), vmem_limit_bytes=VMEM_LIMIT),
        name="in_proj",
    )(x2d, norm_w, col_scale, w_in)


def _key_position_features(seq, first_lane):
    pos = lax.broadcasted_iota(jnp.int32, (seq, LANES), 0)
    lane = lax.broadcasted_iota(jnp.int32, (seq, LANES), 1)
    hi = (lane >= first_lane) & (lane < first_lane + SLOPE_PIECES)
    lo = (lane >= first_lane + SLOPE_PIECES) & (lane < first_lane + 2 * SLOPE_PIECES)
    ext = jnp.where(hi, (pos // POS_SPLIT).astype(F32), 0.0)
    return jnp.where(lo, (pos % POS_SPLIT).astype(F32), ext), pos, lane


def _slope_feature_rows(feat_ref, head, width):
    row = lax.broadcasted_iota(jnp.int32, (8, width), 0)
    out = jnp.zeros((8, width), F32)
    for k in range(SLOPE_PIECES):
        c = feat_ref[head * SLOPE_PIECES + k]
        out = jnp.where(row == k, POS_SPLIT * c, out)
        out = jnp.where(row == SLOPE_PIECES + k, c, out)
    return out


def _causal_tile(s):
    row = lax.broadcasted_iota(jnp.int32, s.shape, 0)
    col = lax.broadcasted_iota(jnp.int32, s.shape, 1)
    return jnp.where(col <= row, s, -jnp.inf)


def _online_step(s, m):
    m_new = jnp.maximum(m, jnp.max(s, axis=-1, keepdims=True))
    return m_new, jnp.exp2(m - m_new), jnp.exp2(s - m_new)


def _past_tiles_loop(i, body, init):
    def trip(t, carry):
        for u in range(KV_UNROLL):
            carry = body(t * KV_UNROLL + u, carry)
        return carry

    return lax.fori_loop(0, i * (ATT_TQ // (ATT_LOOP_TK * KV_UNROLL)), trip, init)


def _rows_from(state, start):
    return state if start == 0 else tuple(a[start:] for a in state)


def _rows_replace(state, start, tail):
    if start == 0:
        return tuple(tail)
    return tuple(jnp.concatenate([a[:start], b], axis=0) for a, b in zip(state, tail))


def _moba_kernel(feat_ref, q_ref, k_ref, v_ref, o_ref, kaug_ref, vaug_ref, kmh_ref, kml_ref, qaug_ref, *, seq):
    nb = seq // MOBA_BLOCK
    hp = pl.program_id(1)
    i = pl.program_id(2)

    @pl.when(i == 0)
    def _():
        ext, pos, lane = _key_position_features(seq, nb)
        ext = jnp.where(lane == pos // MOBA_BLOCK, 1.0, ext).astype(BF16)
        ones_col = jnp.where(lane == 0, 1.0, 0.0).astype(BF16)
        for g in range(HEADS_PER_STEP):
            k = k_ref[:, g * HEAD_DIM:(g + 1) * HEAD_DIM]
            kaug_ref[g, :, 0:HEAD_DIM] = k
            kaug_ref[g, :, HEAD_DIM:2 * HEAD_DIM] = ext
            vaug_ref[g, :, 0:HEAD_DIM] = v_ref[:, g * HEAD_DIM:(g + 1) * HEAD_DIM]
            vaug_ref[g, :, HEAD_DIM:2 * HEAD_DIM] = ones_col
            km = jnp.mean(k.astype(F32).reshape(nb, MOBA_BLOCK, HEAD_DIM), axis=1)
            kmh = km.astype(BF16)
            kmh_ref[g] = jnp.zeros((LANES, HEAD_DIM), BF16)
            kml_ref[g] = jnp.zeros((LANES, HEAD_DIM), BF16)
            kmh_ref[g, 0:nb, :] = kmh
            kml_ref[g, 0:nb, :] = (km - kmh.astype(F32)).astype(BF16)

    blk = lax.broadcasted_iota(jnp.int32, (nb, ATT_TQ), 0)
    col = lax.broadcasted_iota(jnp.int32, (nb, ATT_TQ), 1)
    own = i * (ATT_TQ // MOBA_BLOCK) + col // MOBA_BLOCK
    blk_f = blk.astype(F32)
    for g in range(HEADS_PER_STEP):
        q = q_ref[:, g * HEAD_DIM:(g + 1) * HEAD_DIM]
        gate = (_dot_nt(kmh_ref[g], q) + _dot_nt(kml_ref[g], q))[0:nb, :]
        gate = jnp.where(blk < own, gate, -jnp.inf)
        sel = jnp.zeros((nb, ATT_TQ), F32)
        for _ in range(MOBA_TOPK):
            mx = jnp.max(gate, axis=0, keepdims=True)
            idx = jnp.min(jnp.where(gate == mx, blk_f, float(LANES)), axis=0, keepdims=True)
            pick = (blk_f == idx) & (mx > -jnp.inf)
            sel = jnp.where(pick, 1.0, sel)
            gate = jnp.where(pick, -jnp.inf, gate)
        bias = jnp.where((sel > 0.0) | (blk == own), 0.0, NEG_BIG)
        feat = jnp.concatenate(
            [bias, _slope_feature_rows(feat_ref, hp * HEADS_PER_STEP + g, ATT_TQ),
             jnp.zeros((LANES - nb - 8, ATT_TQ), F32)], axis=0)
        qaug_ref[g, :, 0:HEAD_DIM] = q
        qaug_ref[g, :, HEAD_DIM:2 * HEAD_DIM] = feat.T.astype(BF16)

    def tile_step(g, r, nk, state, row0=0, causal=False):
        m, acc = state
        s = _dot_nt(qaug_ref[g, row0:, :], kaug_ref[g, pl.ds(r, nk), :])
        if causal:
            s = _causal_tile(s)
        m, alpha, p = _online_step(s, m)
        return m, alpha * acc + _dot(p.astype(BF16), vaug_ref[g, pl.ds(r, nk), :])

    def body(j, states):
        r = pl.multiple_of(j * ATT_LOOP_TK, ATT_LOOP_TK)
        return tuple(tile_step(g, r, ATT_LOOP_TK, states[g]) for g in range(HEADS_PER_STEP))

    init = tuple((jnp.full((ATT_TQ, 1), M_INIT, F32), jnp.zeros((ATT_TQ, 2 * HEAD_DIM), F32))
                 for _ in range(HEADS_PER_STEP))
    states = list(_past_tiles_loop(i, body, init))
    for d in range(ATT_TQ // ATT_TK):
        r = pl.multiple_of(i * ATT_TQ + d * ATT_TK, ATT_TK)
        for g in range(HEADS_PER_STEP):
            tail = tile_step(g, r, ATT_TK, _rows_from(states[g], d * ATT_TK), row0=d * ATT_TK, causal=True)
            states[g] = _rows_replace(states[g], d * ATT_TK, tail)
    for g in range(HEADS_PER_STEP):
        acc = states[g][1]
        o_ref[:, g * HEAD_DIM:(g + 1) * HEAD_DIM] = (
            acc[:, 0:HEAD_DIM] / acc[:, HEAD_DIM:HEAD_DIM + 1]).astype(BF16)


def _moba(proj, feats, batch, seq):
    nq = seq // ATT_TQ
    width = HEADS_PER_STEP * HEAD_DIM
    q_col, k_col, v_col = 0, MOBA_HEADS // HEADS_PER_STEP, 2 * MOBA_HEADS // HEADS_PER_STEP
    return pl.pallas_call(
        functools.partial(_moba_kernel, seq=seq),
        grid_spec=pltpu.PrefetchScalarGridSpec(
            num_scalar_prefetch=1,
            grid=(batch, MOBA_HEADS // HEADS_PER_STEP, nq),
            in_specs=[
                pl.BlockSpec((ATT_TQ, width), lambda b, h, i, s: (b * nq + i, q_col + h)),
                pl.BlockSpec((seq, width), lambda b, h, i, s: (b, k_col + h)),
                pl.BlockSpec((seq, width), lambda b, h, i, s: (b, v_col + h)),
            ],
            out_specs=pl.BlockSpec((ATT_TQ, width), lambda b, h, i, s: (b * nq + i, h)),
            scratch_shapes=[
                pltpu.VMEM((HEADS_PER_STEP, seq, 2 * HEAD_DIM), BF16),
                pltpu.VMEM((HEADS_PER_STEP, seq, 2 * HEAD_DIM), BF16),
                pltpu.VMEM((HEADS_PER_STEP, LANES, HEAD_DIM), BF16),
                pltpu.VMEM((HEADS_PER_STEP, LANES, HEAD_DIM), BF16),
                pltpu.VMEM((HEADS_PER_STEP, ATT_TQ, 2 * HEAD_DIM), BF16),
            ],
        ),
        out_shape=jax.ShapeDtypeStruct((batch * seq, MOBA_HEADS * HEAD_DIM), BF16),
        compiler_params=pltpu.CompilerParams(
            dimension_semantics=("arbitrary", "arbitrary", "arbitrary"), vmem_limit_bytes=VMEM_LIMIT),
        name="moba",
    )(feats, proj, proj, proj)


def _diff_kernel(feat_ref, lq1_ref, lk1_ref, lq2_ref, lk2_ref, sw_ref,
                 q_ref, k_ref, v_ref, o_ref, kaug_ref, qaug_ref, *, seq):
    hg = pl.program_id(1)
    i = pl.program_id(2)
    n_chains = 2 * DIFF_HEADS_PER_STEP

    @pl.when(i == 0)
    def _():
        ext = _key_position_features(seq, 0)[0].astype(BF16)
        for n in range(n_chains):
            kaug_ref[n, :, 0:HEAD_DIM] = k_ref[:, n * HEAD_DIM:(n + 1) * HEAD_DIM]
            kaug_ref[n, :, HEAD_DIM:2 * HEAD_DIM] = ext

    for n in range(n_chains):
        if n % 2 == 0:
            head = hg * DIFF_HEADS_PER_STEP + n // 2
            feat = jnp.concatenate([_slope_feature_rows(feat_ref, head, ATT_TQ),
                                    jnp.zeros((LANES - 8, ATT_TQ), F32)], axis=0).T.astype(BF16)
        qaug_ref[n, :, 0:HEAD_DIM] = q_ref[:, n * HEAD_DIM:(n + 1) * HEAD_DIM]
        qaug_ref[n, :, HEAD_DIM:2 * HEAD_DIM] = feat

    def tile_step(n, r, nk, state, row0=0, causal=False):
        m, l, acc = state
        s = _dot_nt(qaug_ref[n, row0:, :], kaug_ref[n, pl.ds(r, nk), :])
        if causal:
            s = _causal_tile(s)
        m, alpha, p = _online_step(s, m)
        v = v_ref[pl.ds(r, nk), (n // 2) * DIFF_V_DIM:(n // 2 + 1) * DIFF_V_DIM]
        p = p.astype(BF16)
        return (m, alpha * l + jnp.sum(p.astype(F32), axis=-1, keepdims=True), alpha * acc + _dot(p, v))

    def body(j, states):
        r = pl.multiple_of(j * ATT_LOOP_TK, ATT_LOOP_TK)
        return tuple(tile_step(n, r, ATT_LOOP_TK, states[n]) for n in range(n_chains))

    init = tuple((jnp.full((ATT_TQ, 1), M_INIT, F32), jnp.zeros((ATT_TQ, 1), F32),
                  jnp.zeros((ATT_TQ, DIFF_V_DIM), F32)) for _ in range(n_chains))
    states = list(_past_tiles_loop(i, body, init))
    for d in range(ATT_TQ // ATT_TK):
        r = pl.multiple_of(i * ATT_TQ + d * ATT_TK, ATT_TK)
        for n in range(n_chains):
            tail = tile_step(n, r, ATT_TK, _rows_from(states[n], d * ATT_TK), row0=d * ATT_TK, causal=True)
            states[n] = _rows_replace(states[n], d * ATT_TK, tail)

    lam = (jnp.exp(jnp.sum(lq1_ref[...] * lk1_ref[...], axis=-1, keepdims=True))
           - jnp.exp(jnp.sum(lq2_ref[...] * lk2_ref[...], axis=-1, keepdims=True))
           + LAMBDA_INIT)
    for hh in range(DIFF_HEADS_PER_STEP):
        (_, l1, acc1), (_, l2, acc2) = states[2 * hh], states[2 * hh + 1]
        o = acc1 / l1 - lam * (acc2 / l2)
        y = o * lax.rsqrt(jnp.mean(o * o, axis=-1, keepdims=True) + RMS_EPS)
        o_ref[:, hh * DIFF_V_DIM:(hh + 1) * DIFF_V_DIM] = ((y * sw_ref[...]) * (1.0 - LAMBDA_INIT)).astype(BF16)


def _diff(proj, feats, lq1, lk1, lq2, lk2, subln_w, batch, seq):
    nq = seq // ATT_TQ
    width = DIFF_HEADS_PER_STEP * 2 * HEAD_DIM
    groups = DIFF_HEADS // DIFF_HEADS_PER_STEP
    q_col = 3 * MOBA_HEADS * HEAD_DIM // width
    k_col = q_col + groups
    v_col = k_col + groups
    vec = lambda n: pl.BlockSpec((1, n), lambda b, h, i, s: (0, 0))
    return pl.pallas_call(
        functools.partial(_diff_kernel, seq=seq),
        grid_spec=pltpu.PrefetchScalarGridSpec(
            num_scalar_prefetch=1,
            grid=(batch, groups, nq),
            in_specs=[
                vec(HEAD_DIM), vec(HEAD_DIM), vec(HEAD_DIM), vec(HEAD_DIM), vec(DIFF_V_DIM),
                pl.BlockSpec((ATT_TQ, width), lambda b, h, i, s: (b * nq + i, q_col + h)),
                pl.BlockSpec((seq, width), lambda b, h, i, s: (b, k_col + h)),
                pl.BlockSpec((seq, width), lambda b, h, i, s: (b, v_col + h)),
            ],
            out_specs=pl.BlockSpec((ATT_TQ, width), lambda b, h, i, s: (b * nq + i, h)),
            scratch_shapes=[
                pltpu.VMEM((2 * DIFF_HEADS_PER_STEP, seq, 2 * HEAD_DIM), BF16),
                pltpu.VMEM((2 * DIFF_HEADS_PER_STEP, ATT_TQ, 2 * HEAD_DIM), BF16),
            ],
        ),
        out_shape=jax.ShapeDtypeStruct((batch * seq, DIFF_HEADS * DIFF_V_DIM), BF16),
        compiler_params=pltpu.CompilerParams(
            dimension_semantics=("arbitrary", "arbitrary", "arbitrary"), vmem_limit_bytes=VMEM_LIMIT),
        name="diff_attn",
    )(feats, lq1, lk1, lq2, lk2, subln_w, proj, proj, proj)


def _route_tile(lg, run, tri):
    lane = lax.broadcasted_iota(jnp.int32, lg.shape, 1).astype(F32)
    rmax = lambda v: jnp.max(v, axis=-1, keepdims=True)
    first = lambda v, m: jnp.min(jnp.where(v == m, lane, float(LANES)), axis=-1, keepdims=True)

    gl = jnp.where(lane < N_GROUPS, lg, -jnp.inf)
    gmax = rmax(gl)
    group = first(gl, gmax)
    p_group = 1.0 / jnp.sum(jnp.exp(gl - gmax), axis=-1, keepdims=True)

    lo = N_GROUPS + EXPERTS_PER_GROUP * group
    el = jnp.where((lane >= lo) & (lane < lo + EXPERTS_PER_GROUP), lg, -jnp.inf)
    e1 = rmax(el)
    lane1 = first(el, e1)
    el = jnp.where(lane == lane1, -jnp.inf, el)
    e2 = rmax(el)
    lane2 = first(el, e2)
    t2 = jnp.exp(e2 - e1)
    w1 = p_group / (1.0 + t2)
    w2 = p_group * t2 / (1.0 + t2)
    x1 = lane1 - N_GROUPS
    x2 = lane2 - N_GROUPS

    hit1 = lane == x1
    hit2 = lane == x2
    onehot = jnp.where(hit1 | hit2, 1.0, 0.0)
    before = _dot(tri, onehot.astype(BF16)) + run
    rank1 = jnp.sum(jnp.where(hit1, before, 0.0), axis=-1, keepdims=True)
    rank2 = jnp.sum(jnp.where(hit2, before, 0.0), axis=-1, keepdims=True)

    meta = jnp.zeros(lg.shape, F32)
    for k, v in ((META_X1, x1), (META_X2, x2), (META_W1, w1), (META_W2, w2),
                 (META_RANK1, rank1), (META_RANK2, rank2)):
        meta = jnp.where(lane == k, v, meta)
    return meta, run + jnp.sum(onehot, axis=0, keepdims=True)


def _outproj_kernel(x_ref, oa_ref, ob_ref, w_ref, nw_ref, wr_ref, br_ref,
                    x2_ref, hp_ref, meta_ref, cnt_ref, run_ref, tri_ref):
    @pl.when(pl.program_id(0) == 0)
    def _():
        row = lax.broadcasted_iota(jnp.int32, tri_ref.shape, 0)
        col = lax.broadcasted_iota(jnp.int32, tri_ref.shape, 1)
        tri_ref[...] = jnp.where(col < row, 1.0, 0.0).astype(BF16)
        run_ref[...] = jnp.zeros(run_ref.shape, F32)

    half = D_MODEL // 2
    x2 = x_ref[...] + _dot(oa_ref[...], w_ref[0:half, :]) + _dot(ob_ref[...], w_ref[half:, :])
    x2_ref[...] = x2
    r = lax.rsqrt(jnp.mean(x2 * x2, axis=-1, keepdims=True) + RMS_EPS)
    h = x2 * r * nw_ref[...]
    hh = h.astype(BF16)
    hl = (h - hh.astype(F32)).astype(BF16)
    prod = _dot(jnp.concatenate([hh, hl], axis=0), wr_ref[...])
    lg = prod[:OUT_TM, :LANES] + prod[OUT_TM:, :LANES] + prod[:OUT_TM, LANES:] + br_ref[...]
    meta, run = _route_tile(lg, run_ref[...], tri_ref[...])
    meta_ref[...] = meta
    run_ref[...] = run
    cnt_ref[...] = run
    bits = pltpu.bitcast(hh.astype(F32), jnp.uint32)
    for c in range(ROW_WORD_CHUNKS):
        lo = bits[:, 2 * c * LANES:(2 * c + 1) * LANES]
        hi = bits[:, (2 * c + 1) * LANES:(2 * c + 2) * LANES]
        hp_ref[pl.ds(c, OUT_TM, stride=ROW_WORD_CHUNKS), :] = (lo >> 16) | (hi & jnp.uint32(0xFFFF0000))


def _out_proj(x2d, out_a, out_b, w_o, norm_w, w_r, b_r):
    t = x2d.shape[0]
    half = D_MODEL // 2
    row = lambda n: pl.BlockSpec((OUT_TM, n), lambda i: (i, 0))
    full = lambda r, c: pl.BlockSpec((r, c), lambda i: (0, 0))
    return pl.pallas_call(
        _outproj_kernel,
        grid=(t // OUT_TM,),
        in_specs=[row(D_MODEL), row(half), row(half), full(D_MODEL, D_MODEL),
                  full(1, D_MODEL), full(D_MODEL, 2 * LANES), full(1, LANES)],
        out_specs=[row(D_MODEL), pl.BlockSpec((OUT_TM * ROW_WORD_CHUNKS, LANES), lambda i: (i, 0)), row(LANES),
                   full(1, LANES)],
        out_shape=[jax.ShapeDtypeStruct((t, D_MODEL), F32),
                   jax.ShapeDtypeStruct((t * ROW_WORD_CHUNKS, LANES), jnp.uint32),
                   jax.ShapeDtypeStruct((t, LANES), F32),
                   jax.ShapeDtypeStruct((1, LANES), F32)],
        scratch_shapes=[pltpu.VMEM((1, LANES), F32), pltpu.VMEM((OUT_TM, OUT_TM), BF16)],
        compiler_params=pltpu.CompilerParams(
            dimension_semantics=("arbitrary",), vmem_limit_bytes=VMEM_LIMIT),
        name="out_proj",
    )(x2d, out_a, out_b, w_o, norm_w, w_r, b_r)


GATHER_UNROLL = 16
CLEAR_UNROLL = 32
DSP_ROWS = 8 * MOE_TM


def _dispatch_kernel(pos0_ref, pos1_ref, nt_ref, hp_hbm, xs_ref, hp_ref, src_ref, sem):
    t = pl.program_id(0)
    pos_refs = (pos0_ref, pos1_ref)

    @pl.when(t == 0)
    def _():
        load = pltpu.make_async_copy(hp_hbm, hp_ref, sem)
        load.start()

        def clear(p, _):
            src_ref[p] = 0
            return 0
        lax.fori_loop(0, src_ref.shape[0], clear, 0, unroll=CLEAR_UNROLL)

        def place(c, _):
            tok0 = c * (GATHER_UNROLL // EXPERT_TOPK)
            rows = [p[tok0 + u] for u in range(GATHER_UNROLL // EXPERT_TOPK) for p in pos_refs]
            for n, row in enumerate(rows):
                src_ref[row] = tok0 + n // EXPERT_TOPK
            return 0
        lax.fori_loop(0, pos_refs[0].shape[0] * EXPERT_TOPK // GATHER_UNROLL, place, 0)
        load.wait()

    n_valid = jnp.clip(nt_ref[0] * MOE_TM - t * DSP_ROWS, 0, DSP_ROWS)

    def gather(c, _):
        toks = [src_ref[t * DSP_ROWS + c * GATHER_UNROLL + u] for u in range(GATHER_UNROLL)]
        for u in range(GATHER_UNROLL):
            xs_ref[c * GATHER_UNROLL + u] = hp_ref[toks[u]]
        return 0
    lax.fori_loop(0, n_valid // GATHER_UNROLL, gather, 0)

    def zero_tile(c, _):
        xs_ref[pl.ds(pl.multiple_of(c * MOE_TM, MOE_TM), MOE_TM)] = jnp.zeros(
            (MOE_TM,) + xs_ref.shape[1:], xs_ref.dtype)
        return 0
    lax.fori_loop(n_valid // MOE_TM, DSP_ROWS // MOE_TM, zero_tile, 0)


def _dispatch(pos, n_tiles, hp, npad):
    n_tokens = hp.shape[0] // ROW_WORD_CHUNKS
    hp3 = hp.reshape(n_tokens, ROW_WORD_CHUNKS, LANES)
    xs = pl.pallas_call(
        _dispatch_kernel,
        grid_spec=pltpu.PrefetchScalarGridSpec(
            num_scalar_prefetch=3,
            grid=(npad // DSP_ROWS,),
            in_specs=[pl.BlockSpec(memory_space=pl.ANY)],
            out_specs=pl.BlockSpec((DSP_ROWS, ROW_WORD_CHUNKS, LANES), lambda t, p0, p1, n: (t, 0, 0)),
            scratch_shapes=[pltpu.VMEM(hp3.shape, hp3.dtype), pltpu.SMEM((npad,), jnp.int32),
                            pltpu.SemaphoreType.DMA(())],
        ),
        out_shape=jax.ShapeDtypeStruct((npad, ROW_WORD_CHUNKS, LANES), hp.dtype),
        compiler_params=pltpu.CompilerParams(
            dimension_semantics=("arbitrary",), vmem_limit_bytes=VMEM_LIMIT),
        name="dispatch",
    )(pos[0], pos[1], n_tiles, hp3)
    return xs.reshape(npad * ROW_WORD_CHUNKS, LANES)


W_SLOTS = 2
EXP_TILES = 3


def _experts_kernel(tk_ref, eseq_ref, nt_ref, xs_ref, wg_hbm, wu_hbm, wd_hbm, y_ref,
                    wg_buf, wu_buf, wd_buf, wgb_ref, wub_ref, wdb_ref, x_ref, sem):
    n_used = nt_ref[1]
    hbm_bufs = ((wg_hbm, wg_buf), (wu_hbm, wu_buf), (wd_hbm, wd_buf))

    def weight_copies(k, slot):
        e = eseq_ref[k]
        return [pltpu.make_async_copy(hbm.at[e], buf.at[slot], sem.at[slot, n])
                for n, (hbm, buf) in enumerate(hbm_bufs)]

    for sub in range(EXP_TILES):
        _expert_tile(pl.program_id(0) * EXP_TILES + sub, sub, tk_ref, nt_ref, n_used, weight_copies,
                     xs_ref, y_ref, wg_buf, wu_buf, wd_buf, wgb_ref, wub_ref, wdb_ref, x_ref)


def _expert_tile(t, sub, tk_ref, nt_ref, n_used, weight_copies,
                 xs_ref, y_ref, wg_buf, wu_buf, wd_buf, wgb_ref, wub_ref, wdb_ref, x_ref):
    xs_row0 = sub * MOE_TM * ROW_WORD_CHUNKS
    y_rows = pl.ds(sub * MOE_TM, MOE_TM)

    @pl.when(t < nt_ref[0])
    def _():
        k = tk_ref[t]

        @pl.when((t == 0) | (k != tk_ref[jnp.maximum(t - 1, 0)]))
        def _():
            @pl.when(k == 0)
            def _():
                for s in range(W_SLOTS):
                    @pl.when(s < n_used)
                    def _():
                        for c in weight_copies(s, s):
                            c.start()

            slot = k % W_SLOTS
            for c in weight_copies(k, slot):
                c.wait()
            wgb_ref[...] = wg_buf[slot].astype(BF16)
            wub_ref[...] = wu_buf[slot].astype(BF16)
            wdb_ref[...] = wd_buf[slot].astype(BF16)

            @pl.when(k + W_SLOTS < n_used)
            def _():
                for c in weight_copies(k + W_SLOTS, slot):
                    c.start()

        for c in range(ROW_WORD_CHUNKS):
            u32 = xs_ref[pl.ds(xs_row0 + c, MOE_TM, stride=ROW_WORD_CHUNKS), :]
            x_ref[:, 2 * c * LANES:(2 * c + 1) * LANES] = pltpu.bitcast(u32 << 16, F32).astype(BF16)
            x_ref[:, (2 * c + 1) * LANES:(2 * c + 2) * LANES] = (
                pltpu.bitcast(u32 & jnp.uint32(0xFFFF0000), F32).astype(BF16))
        x = x_ref[...]
        a = _dot(x, wgb_ref[...])
        u = _dot(x, wub_ref[...])
        act = (a * jax.nn.sigmoid(a)) * u
        y_ref[y_rows, :] = _dot(act.astype(BF16), wdb_ref[...])

    @pl.when(t >= nt_ref[0])
    def _():
        y_ref[y_rows, :] = jnp.zeros((MOE_TM, D_MODEL), y_ref.dtype)


def _experts(tile_k, expert_seq, n_used, xs, w_gate, w_up, w_down):
    npad = xs.shape[0] // ROW_WORD_CHUNKS
    rows = EXP_TILES * MOE_TM
    assert npad % rows == 0 and npad % DSP_ROWS == 0
    last = lambda t, nt: jnp.minimum(t, jnp.maximum(nt[0] - 1, 0) // EXP_TILES)
    hbm = pl.BlockSpec(memory_space=pl.ANY)
    return pl.pallas_call(
        _experts_kernel,
        grid_spec=pltpu.PrefetchScalarGridSpec(
            num_scalar_prefetch=3,
            grid=(npad // rows,),
            in_specs=[
                pl.BlockSpec((rows * ROW_WORD_CHUNKS, LANES), lambda t, tk, es, nt: (last(t, nt), 0)),
                hbm, hbm, hbm,
            ],
            out_specs=pl.BlockSpec((rows, D_MODEL), lambda t, tk, es, nt: (t, 0)),
            scratch_shapes=[
                pltpu.VMEM((W_SLOTS, D_MODEL, D_EXPERT), F32),
                pltpu.VMEM((W_SLOTS, D_MODEL, D_EXPERT), F32),
                pltpu.VMEM((W_SLOTS, D_EXPERT, D_MODEL), F32),
                pltpu.VMEM((D_MODEL, D_EXPERT), BF16),
                pltpu.VMEM((D_MODEL, D_EXPERT), BF16),
                pltpu.VMEM((D_EXPERT, D_MODEL), BF16),
                pltpu.VMEM((MOE_TM, D_MODEL), BF16),
                pltpu.SemaphoreType.DMA((W_SLOTS, 3)),
            ],
        ),
        out_shape=jax.ShapeDtypeStruct((npad, D_MODEL), F32),
        compiler_params=pltpu.CompilerParams(
            dimension_semantics=("arbitrary",), vmem_limit_bytes=VMEM_LIMIT),
        name="experts",
    )(tile_k, expert_seq, n_used, xs, w_gate, w_up, w_down)


CMB_UNROLL = 8
CMB_SLOTS = 2


def _combine_kernel(pos0_ref, pos1_ref, x2_ref, meta_ref, nw_ref, y_ref, o_ref, buf_ref, sem):
    i = pl.program_id(0)
    pos_refs = (pos0_ref, pos1_ref)

    def row_copy(p, slot, k, r):
        return pltpu.make_async_copy(y_ref.at[pl.ds(p, 1), :], buf_ref.at[slot, k, pl.ds(r, 1), :],
                                     sem.at[slot])

    def issue(tile, slot):
        for c in range(CMB_TT // CMB_UNROLL):
            tok0 = tile * CMB_TT + c * CMB_UNROLL
            rows = [[p[tok0 + u] for u in range(CMB_UNROLL)] for p in pos_refs]
            for u in range(CMB_UNROLL):
                for k in range(EXPERT_TOPK):
                    row_copy(rows[k][u], slot, k, c * CMB_UNROLL + u).start(priority=k % 2)

    @pl.when(i == 0)
    def _():
        issue(0, 0)

    @pl.when(i + 1 < pl.num_programs(0))
    def _():
        issue(i + 1, (i + 1) % CMB_SLOTS)

    slot = i % CMB_SLOTS
    for k in range(EXPERT_TOPK):
        pltpu.make_async_copy(y_ref.at[pl.ds(0, CMB_TT), :], buf_ref.at[slot, k], sem.at[slot]).wait()

    meta = meta_ref[...]
    x3 = (x2_ref[...] + meta[:, META_W1:META_W1 + 1] * buf_ref[slot, 0]
          + meta[:, META_W2:META_W2 + 1] * buf_ref[slot, 1])
    r = lax.rsqrt(jnp.mean(x3 * x3, axis=-1, keepdims=True) + RMS_EPS)
    o_ref[...] = x3 * r * nw_ref[...]


def _combine(pos, x2, meta, norm_w, y):
    t = x2.shape[0]
    return pl.pallas_call(
        _combine_kernel,
        grid_spec=pltpu.PrefetchScalarGridSpec(
            num_scalar_prefetch=2,
            grid=(t // CMB_TT,),
            in_specs=[
                pl.BlockSpec((CMB_TT, D_MODEL), lambda i, p0, p1: (i, 0)),
                pl.BlockSpec((CMB_TT, LANES), lambda i, p0, p1: (i, 0)),
                pl.BlockSpec((1, D_MODEL), lambda i, p0, p1: (0, 0)),
                pl.BlockSpec(memory_space=pl.ANY),
            ],
            out_specs=pl.BlockSpec((CMB_TT, D_MODEL), lambda i, p0, p1: (i, 0)),
            scratch_shapes=[
                pltpu.VMEM((CMB_SLOTS, EXPERT_TOPK, CMB_TT, D_MODEL), F32),
                pltpu.SemaphoreType.DMA((CMB_SLOTS,)),
            ],
        ),
        out_shape=jax.ShapeDtypeStruct((t, D_MODEL), F32),
        compiler_params=pltpu.CompilerParams(
            dimension_semantics=("arbitrary",), vmem_limit_bytes=VMEM_LIMIT),
        name="combine",
    )(pos[0], pos[1], x2, meta, norm_w, y)


def _layout(meta, counts, n_tokens):
    npad = n_tokens * EXPERT_TOPK + N_EXPERTS * MOE_TM
    counts = counts[0, :N_EXPERTS].astype(jnp.int32)
    padded = ((counts + MOE_TM - 1) // MOE_TM) * MOE_TM
    ends = jnp.cumsum(padded)
    starts = ends - padded
    ids = jnp.arange(N_EXPERTS, dtype=jnp.int32)
    pos = []
    for lane_x, lane_rank in ((META_X1, META_RANK1), (META_X2, META_RANK2)):
        hit = meta[:, lane_x].astype(jnp.int32)[:, None] == ids[None, :]
        pos.append(jnp.sum(jnp.where(hit, starts[None, :], 0), axis=-1) + meta[:, lane_rank].astype(jnp.int32))
    tile_start = jnp.arange(npad // MOE_TM, dtype=jnp.int32) * MOE_TM
    tile_expert = jnp.minimum(jnp.sum(ends[None, :] <= tile_start[:, None], axis=1), N_EXPERTS - 1)
    used = counts > 0
    k_of_expert = jnp.cumsum(used.astype(jnp.int32)) - 1
    expert_seq = jnp.sum(jnp.where(used[None, :] & (k_of_expert[None, :] == ids[:, None]), ids[None, :], 0), axis=1)
    tile_k = jnp.sum(jnp.where(tile_expert[:, None] == ids[None, :], k_of_expert[None, :], 0), axis=1)
    n_tiles = (ends[-1] // MOE_TM).astype(jnp.int32)
    n_used = jnp.stack([n_tiles, jnp.sum(used).astype(jnp.int32)])
    return pos, tile_k.astype(jnp.int32), expert_seq.astype(jnp.int32), n_tiles.reshape(1), n_used, npad


def kernel(x, norm_mix_w, w_in, lambda_q1, lambda_k1, lambda_q2, lambda_k2, diff_subln_w, w_out, norm_ffn_w,
           w_router_group, b_router_group, w_router_expert, b_router_expert, w_gate, w_up, w_down, norm_final_w):
    batch, seq, _ = x.shape
    assert seq % ATT_TQ == 0 and seq // MOBA_BLOCK + 8 <= LANES
    assert w_in.shape[0] == 1, "single-layer block"
    n_tokens = batch * seq
    x2d = x.reshape(n_tokens, D_MODEL)

    col = np.arange(IN_WIDTH)
    is_q = (col < MOBA_HEADS * HEAD_DIM) | ((col >= 3 * MOBA_HEADS * HEAD_DIM) & (col < 4 * MOBA_HEADS * HEAD_DIM))
    col_scale = jnp.asarray(np.where(is_q, QK_SCALE * LOG2E, 1.0).astype(np.float32))[None, :]

    proj = _in_proj(x2d, norm_mix_w[0][None, :], w_in[0], col_scale)
    out_a = _moba(proj, jnp.asarray(_alibi_slope_pieces(MOBA_HEADS)), batch, seq)
    out_b = _diff(proj, jnp.asarray(_alibi_slope_pieces(DIFF_HEADS)), lambda_q1, lambda_k1, lambda_q2, lambda_k2,
                  diff_subln_w, batch, seq)

    w_o = w_out[0].astype(BF16)
    w_r = jnp.concatenate([w_router_group[0], w_router_expert[0]], axis=1)
    w_r = jnp.pad(w_r, ((0, 0), (0, LANES - w_r.shape[1])))
    wr_hi = w_r.astype(BF16)
    wr_lo = (w_r - wr_hi.astype(F32)).astype(BF16)
    b_r = jnp.pad(jnp.concatenate([b_router_group[0], b_router_expert[0]]), (0, LANES - N_GROUPS - N_EXPERTS))[None, :]
    x2, hp, meta, counts = _out_proj(x2d, out_a, out_b, w_o, norm_ffn_w[0][None, :],
                                     jnp.concatenate([wr_hi, wr_lo], axis=1), b_r)

    pos, tile_k, expert_seq, n_tiles, n_used, npad = _layout(meta, counts, n_tokens)
    xs = _dispatch(pos, n_tiles, hp, npad)
    y = _experts(tile_k, expert_seq, n_used, xs, w_gate[0], w_up[0], w_down[0])
    out = _combine(pos, x2, meta, norm_final_w[None, :], y)
    return out.reshape(batch, seq, D_MODEL)
```

```python
import functools

import ml_dtypes
import numpy as np
import jax
import jax.numpy as jnp
from jax import lax
from jax.experimental import pallas as pl
from jax.experimental.pallas import tpu as pltpu

F32 = jnp.float32
BF16 = jnp.bfloat16

D_MODEL = 2048
HEAD_DIM = 128
MOBA_HEADS = 8
MOBA_BLOCK = 256
MOBA_TOPK = 3
DIFF_HEADS = 4
DIFF_V_DIM = 256
IN_WIDTH = 6144
N_GROUPS = 4
EXPERTS_PER_GROUP = 8
N_EXPERTS = 32
EXPERT_TOPK = 2
D_EXPERT = 512
RMS_EPS = 1e-6
ALIBI_MAX_BIAS = 8.0
LAMBDA_INIT = 0.8 - 0.6 * float(np.exp(-0.3 * 0))

LANES = 128
QK_SCALE = HEAD_DIM ** -0.5
LOG2E = float(np.log2(np.e))
NEG_BIG = -1e30
M_INIT = -1e38
POS_SPLIT = 64
SLOPE_PIECES = 3
ROW_WORD_CHUNKS = D_MODEL // (2 * LANES)
META_X1, META_X2, META_W1, META_W2, META_RANK1, META_RANK2 = range(6)

IN_TM, IN_TN = 1024, 1024
ATT_TQ = 1024
ATT_TK = 512
KV_UNROLL = ATT_TQ // ATT_TK
HEADS_PER_STEP = 4
DIFF_HEADS_PER_STEP = 2
OUT_TM = 512
MOE_TM = 256
CMB_TT = 256
VMEM_LIMIT = 56 * 1024 * 1024


def _alibi_slope_pieces(n):
    rem = np.exp2(-ALIBI_MAX_BIAS * (np.arange(n, dtype=np.float64) + 1.0) / n) * LOG2E
    pieces = []
    for _ in range(SLOPE_PIECES):
        p = rem.astype(ml_dtypes.bfloat16).astype(np.float64)
        pieces.append(p)
        rem = rem - p
    return np.stack(pieces, axis=1).reshape(-1).astype(np.float32)


def _dot_nt(a, b):
    return lax.dot_general(a, b, (((1,), (1,)), ((), ())), preferred_element_type=F32)


def _dot(a, b):
    return jnp.dot(a, b, preferred_element_type=F32)


def _inproj_kernel(x_ref, nw_ref, cs_ref, w_ref, o_ref, h_ref):
    @pl.when(pl.program_id(1) == 0)
    def _():
        x = x_ref[...]
        r = lax.rsqrt(jnp.mean(x * x, axis=-1, keepdims=True) + RMS_EPS)
        h_ref[...] = (x * r * nw_ref[...]).astype(BF16)

    acc = _dot(h_ref[...], w_ref[...].astype(BF16))
    o_ref[...] = (acc * cs_ref[...]).astype(BF16)


def _in_proj(x2d, norm_w, w_in, col_scale):
    t = x2d.shape[0]
    return pl.pallas_call(
        _inproj_kernel,
        grid=(t // IN_TM, IN_WIDTH // IN_TN),
        in_specs=[
            pl.BlockSpec((IN_TM, D_MODEL), lambda i, j: (i, 0)),
            pl.BlockSpec((1, D_MODEL), lambda i, j: (0, 0)),
            pl.BlockSpec((1, IN_TN), lambda i, j: (0, j)),
            pl.BlockSpec((D_MODEL, IN_TN), lambda i, j: (0, j)),
        ],
        out_specs=pl.BlockSpec((IN_TM, IN_TN), lambda i, j: (i, j)),
        out_shape=jax.ShapeDtypeStruct((t, IN_WIDTH), BF16),
        scratch_shapes=[pltpu.VMEM((IN_TM, D_MODEL), BF16)],
        compiler_params=pltpu.CompilerParams(
            dimension_semantics=("arbitrary", "arbitrary"), vmem_limit_bytes=VMEM_LIMIT),
        name="in_proj",
    )(x2d, norm_w, col_scale, w_in)


def _key_position_features(seq, first_lane):
    pos = lax.broadcasted_iota(jnp.int32, (seq, LANES), 0)
    lane = lax.broadcasted_iota(jnp.int32, (seq, LANES), 1)
    hi = (lane >= first_lane) & (lane < first_lane + SLOPE_PIECES)
    lo = (lane >= first_lane + SLOPE_PIECES) & (lane < first_lane + 2 * SLOPE_PIECES)
    ext = jnp.where(hi, (pos // POS_SPLIT).astype(F32), 0.0)
    return jnp.where(lo, (pos % POS_SPLIT).astype(F32), ext), pos, lane


def _slope_feature_rows(feat_ref, head, width):
    row = lax.broadcasted_iota(jnp.int32, (8, width), 0)
    out = jnp.zeros((8, width), F32)
    for k in range(SLOPE_PIECES):
        c = feat_ref[head * SLOPE_PIECES + k]
        out = jnp.where(row == k, POS_SPLIT * c, out)
        out = jnp.where(row == SLOPE_PIECES + k, c, out)
    return out


def _transposed_step(kaug_ref, qaug_ref, n, j, m, col0, causal):
    r = pl.multiple_of(j * ATT_TK, ATT_TK)
    s = _dot_nt(kaug_ref[n, pl.ds(r, ATT_TK), :], qaug_ref[n, col0:, :])
    if causal:
        key = lax.broadcasted_iota(jnp.int32, s.shape, 0)
        qry = lax.broadcasted_iota(jnp.int32, s.shape, 1)
        s = jnp.where(key <= qry, s, -jnp.inf)
    m_new = jnp.maximum(m, jnp.max(s, axis=0, keepdims=True))
    return m_new, jnp.exp2(m - m_new), jnp.exp2(s - m_new).astype(BF16)


def _past_tiles_loop(i, body, init):
    def trip(t, carry):
        for u in range(KV_UNROLL):
            carry = body(t * KV_UNROLL + u, carry)
        return carry

    return lax.fori_loop(0, i * (ATT_TQ // (ATT_TK * KV_UNROLL)), trip, init)


def _cols_from(state, start):
    return state if start == 0 else tuple(a[:, start:] for a in state)


def _cols_replace(state, start, tail):
    if start == 0:
        return tuple(tail)
    return tuple(jnp.concatenate([a[:, :start], b], axis=1) for a, b in zip(state, tail))


def _moba_kernel(feat_ref, q_ref, k_ref, v_ref, o_ref, kaug_ref, vt_ref, kmh_ref, kml_ref, qaug_ref, *, seq):
    nb = seq // MOBA_BLOCK
    hp = pl.program_id(1)
    i = pl.program_id(2)

    @pl.when(i == 0)
    def _():
        ext, pos, lane = _key_position_features(seq, nb)
        ext = jnp.where(lane == pos // MOBA_BLOCK, 1.0, ext).astype(BF16)
        ones_row = jnp.where(lax.broadcasted_iota(jnp.int32, (HEAD_DIM, ATT_TK), 0) == 0, 1.0, 0.0).astype(BF16)
        for g in range(HEADS_PER_STEP):
            k = k_ref[:, g * HEAD_DIM:(g + 1) * HEAD_DIM]
            kaug_ref[g, :, 0:HEAD_DIM] = k
            kaug_ref[g, :, HEAD_DIM:2 * HEAD_DIM] = ext
            for t in range(seq // ATT_TK):
                v = v_ref[t * ATT_TK:(t + 1) * ATT_TK, g * HEAD_DIM:(g + 1) * HEAD_DIM]
                vt_ref[g, t, 0:HEAD_DIM, :] = v.astype(F32).T.astype(BF16)
                vt_ref[g, t, HEAD_DIM:2 * HEAD_DIM, :] = ones_row
            km = jnp.mean(k.astype(F32).reshape(nb, MOBA_BLOCK, HEAD_DIM), axis=1)
            kmh = km.astype(BF16)
            kmh_ref[g] = jnp.zeros((LANES, HEAD_DIM), BF16)
            kml_ref[g] = jnp.zeros((LANES, HEAD_DIM), BF16)
            kmh_ref[g, 0:nb, :] = kmh
            kml_ref[g, 0:nb, :] = (km - kmh.astype(F32)).astype(BF16)

    blk = lax.broadcasted_iota(jnp.int32, (nb, ATT_TQ), 0)
    col = lax.broadcasted_iota(jnp.int32, (nb, ATT_TQ), 1)
    own = i * (ATT_TQ // MOBA_BLOCK) + col // MOBA_BLOCK
    blk_f = blk.astype(F32)
    for g in range(HEADS_PER_STEP):
        q = q_ref[:, g * HEAD_DIM:(g + 1) * HEAD_DIM]
        gate = (_dot_nt(kmh_ref[g], q) + _dot_nt(kml_ref[g], q))[0:nb, :]
        gate = jnp.where(blk < own, gate, -jnp.inf)
        sel = jnp.zeros((nb, ATT_TQ), F32)
        for _ in range(MOBA_TOPK):
            mx = jnp.max(gate, axis=0, keepdims=True)
            idx = jnp.min(jnp.where(gate == mx, blk_f, float(LANES)), axis=0, keepdims=True)
            pick = (blk_f == idx) & (mx > -jnp.inf)
            sel = jnp.where(pick, 1.0, sel)
            gate = jnp.where(pick, -jnp.inf, gate)
        bias = jnp.where((sel > 0.0) | (blk == own), 0.0, NEG_BIG)
        feat = jnp.concatenate(
            [bias, _slope_feature_rows(feat_ref, hp * HEADS_PER_STEP + g, ATT_TQ),
             jnp.zeros((LANES - nb - 8, ATT_TQ), F32)], axis=0)
        qaug_ref[g, :, 0:HEAD_DIM] = q
        qaug_ref[g, :, HEAD_DIM:2 * HEAD_DIM] = feat.T.astype(BF16)

    def tile_step(g, j, state, col0=0, causal=False):
        m, acc = state
        m, alpha, p = _transposed_step(kaug_ref, qaug_ref, g, j, m, col0, causal)
        return m, alpha * acc + _dot(vt_ref[g, j], p)

    def body(j, states):
        return tuple(tile_step(g, j, states[g]) for g in range(HEADS_PER_STEP))

    init = tuple((jnp.full((1, ATT_TQ), M_INIT, F32), jnp.zeros((2 * HEAD_DIM, ATT_TQ), F32))
                 for _ in range(HEADS_PER_STEP))
    states = list(_past_tiles_loop(i, body, init))
    for d in range(ATT_TQ // ATT_TK):
        for g in range(HEADS_PER_STEP):
            tail = tile_step(g, i * (ATT_TQ // ATT_TK) + d, _cols_from(states[g], d * ATT_TK),
                             col0=d * ATT_TK, causal=True)
            states[g] = _cols_replace(states[g], d * ATT_TK, tail)
    for g in range(HEADS_PER_STEP):
        acc = states[g][1]
        o_ref[:, g * HEAD_DIM:(g + 1) * HEAD_DIM] = (
            acc[0:HEAD_DIM, :] / acc[HEAD_DIM:HEAD_DIM + 1, :]).T.astype(BF16)


def _moba(proj, feats, batch, seq):
    nq = seq // ATT_TQ
    width = HEADS_PER_STEP * HEAD_DIM
    q_col, k_col, v_col = 0, MOBA_HEADS // HEADS_PER_STEP, 2 * MOBA_HEADS // HEADS_PER_STEP
    return pl.pallas_call(
        functools.partial(_moba_kernel, seq=seq),
        grid_spec=pltpu.PrefetchScalarGridSpec(
            num_scalar_prefetch=1,
            grid=(batch, MOBA_HEADS // HEADS_PER_STEP, nq),
            in_specs=[
                pl.BlockSpec((ATT_TQ, width), lambda b, h, i, s: (b * nq + i, q_col + h)),
                pl.BlockSpec((seq, width), lambda b, h, i, s: (b, k_col + h)),
                pl.BlockSpec((seq, width), lambda b, h, i, s: (b, v_col + h)),
            ],
            out_specs=pl.BlockSpec((ATT_TQ, width), lambda b, h, i, s: (b * nq + i, h)),
            scratch_shapes=[
                pltpu.VMEM((HEADS_PER_STEP, seq, 2 * HEAD_DIM), BF16),
                pltpu.VMEM((HEADS_PER_STEP, seq // ATT_TK, 2 * HEAD_DIM, ATT_TK), BF16),
                pltpu.VMEM((HEADS_PER_STEP, LANES, HEAD_DIM), BF16),
                pltpu.VMEM((HEADS_PER_STEP, LANES, HEAD_DIM), BF16),
                pltpu.VMEM((HEADS_PER_STEP, ATT_TQ, 2 * HEAD_DIM), BF16),
            ],
        ),
        out_shape=jax.ShapeDtypeStruct((batch * seq, MOBA_HEADS * HEAD_DIM), BF16),
        compiler_params=pltpu.CompilerParams(
            dimension_semantics=("arbitrary", "arbitrary", "arbitrary"), vmem_limit_bytes=VMEM_LIMIT),
        name="moba",
    )(feats, proj, proj, proj)


def _diff_kernel(feat_ref, lq1_ref, lk1_ref, lq2_ref, lk2_ref, sw_ref,
                 q_ref, k_ref, v_ref, o_ref, kaug_ref, qaug_ref, vt_ref, *, seq):
    hg = pl.program_id(1)
    i = pl.program_id(2)
    n_chains = 2 * DIFF_HEADS_PER_STEP

    @pl.when(i == 0)
    def _():
        ext = _key_position_features(seq, 0)[0].astype(BF16)
        for n in range(n_chains):
            kaug_ref[n, :, 0:HEAD_DIM] = k_ref[:, n * HEAD_DIM:(n + 1) * HEAD_DIM]
            kaug_ref[n, :, HEAD_DIM:2 * HEAD_DIM] = ext
        for hh in range(DIFF_HEADS_PER_STEP):
            for t in range(seq // ATT_TK):
                v = v_ref[t * ATT_TK:(t + 1) * ATT_TK, hh * DIFF_V_DIM:(hh + 1) * DIFF_V_DIM]
                vt_ref[hh, t] = v.astype(F32).T.astype(BF16)

    for n in range(n_chains):
        if n % 2 == 0:
            head = hg * DIFF_HEADS_PER_STEP + n // 2
            feat = jnp.concatenate([_slope_feature_rows(feat_ref, head, ATT_TQ),
                                    jnp.zeros((LANES - 8, ATT_TQ), F32)], axis=0).T.astype(BF16)
        qaug_ref[n, :, 0:HEAD_DIM] = q_ref[:, n * HEAD_DIM:(n + 1) * HEAD_DIM]
        qaug_ref[n, :, HEAD_DIM:2 * HEAD_DIM] = feat

    def tile_step(n, j, state, col0=0, causal=False):
        m, l, acc = state
        m, alpha, p = _transposed_step(kaug_ref, qaug_ref, n, j, m, col0, causal)
        l = alpha * l + jnp.sum(p.astype(F32), axis=0, keepdims=True)
        return m, l, alpha * acc + _dot(vt_ref[n // 2, j], p)

    def body(j, states):
        return tuple(tile_step(n, j, states[n]) for n in range(n_chains))

    init = tuple((jnp.full((1, ATT_TQ), M_INIT, F32), jnp.zeros((1, ATT_TQ), F32),
                  jnp.zeros((DIFF_V_DIM, ATT_TQ), F32)) for _ in range(n_chains))
    states = list(_past_tiles_loop(i, body, init))
    for d in range(ATT_TQ // ATT_TK):
        for n in range(n_chains):
            tail = tile_step(n, i * (ATT_TQ // ATT_TK) + d, _cols_from(states[n], d * ATT_TK),
                             col0=d * ATT_TK, causal=True)
            states[n] = _cols_replace(states[n], d * ATT_TK, tail)

    lam = (jnp.exp(jnp.sum(lq1_ref[...] * lk1_ref[...], axis=-1, keepdims=True))
           - jnp.exp(jnp.sum(lq2_ref[...] * lk2_ref[...], axis=-1, keepdims=True))
           + LAMBDA_INIT)
    for hh in range(DIFF_HEADS_PER_STEP):
        (_, l1, acc1), (_, l2, acc2) = states[2 * hh], states[2 * hh + 1]
        o = (acc1 / l1 - lam * (acc2 / l2)).T
        y = o * lax.rsqrt(jnp.mean(o * o, axis=-1, keepdims=True) + RMS_EPS)
        o_ref[:, hh * DIFF_V_DIM:(hh + 1) * DIFF_V_DIM] = ((y * sw_ref[...]) * (1.0 - LAMBDA_INIT)).astype(BF16)


def _diff(proj, feats, lq1, lk1, lq2, lk2, subln_w, batch, seq):
    nq = seq // ATT_TQ
    width = DIFF_HEADS_PER_STEP * 2 * HEAD_DIM
    groups = DIFF_HEADS // DIFF_HEADS_PER_STEP
    q_col = 3 * MOBA_HEADS * HEAD_DIM // width
    k_col = q_col + groups
    v_col = k_col + groups
    vec = lambda n: pl.BlockSpec((1, n), lambda b, h, i, s: (0, 0))
    return pl.pallas_call(
        functools.partial(_diff_kernel, seq=seq),
        grid_spec=pltpu.PrefetchScalarGridSpec(
            num_scalar_prefetch=1,
            grid=(batch, groups, nq),
            in_specs=[
                vec(HEAD_DIM), vec(HEAD_DIM), vec(HEAD_DIM), vec(HEAD_DIM), vec(DIFF_V_DIM),
                pl.BlockSpec((ATT_TQ, width), lambda b, h, i, s: (b * nq + i, q_col + h)),
                pl.BlockSpec((seq, width), lambda b, h, i, s: (b, k_col + h)),
                pl.BlockSpec((seq, width), lambda b, h, i, s: (b, v_col + h)),
            ],
            out_specs=pl.BlockSpec((ATT_TQ, width), lambda b, h, i, s: (b * nq + i, h)),
            scratch_shapes=[
                pltpu.VMEM((2 * DIFF_HEADS_PER_STEP, seq, 2 * HEAD_DIM), BF16),
                pltpu.VMEM((2 * DIFF_HEADS_PER_STEP, ATT_TQ, 2 * HEAD_DIM), BF16),
                pltpu.VMEM((DIFF_HEADS_PER_STEP, seq // ATT_TK, DIFF_V_DIM, ATT_TK), BF16),
            ],
        ),
        out_shape=jax.ShapeDtypeStruct((batch * seq, DIFF_HEADS * DIFF_V_DIM), BF16),
        compiler_params=pltpu.CompilerParams(
            dimension_semantics=("arbitrary", "arbitrary", "arbitrary"), vmem_limit_bytes=VMEM_LIMIT),
        name="diff_attn",
    )(feats, lq1, lk1, lq2, lk2, subln_w, proj, proj, proj)


def _route_tile(lg, run, tri):
    lane = lax.broadcasted_iota(jnp.int32, lg.shape, 1).astype(F32)
    rmax = lambda v: jnp.max(v, axis=-1, keepdims=True)
    first = lambda v, m: jnp.min(jnp.where(v == m, lane, float(LANES)), axis=-1, keepdims=True)

    gl = jnp.where(lane < N_GROUPS, lg, -jnp.inf)
    gmax = rmax(gl)
    group = first(gl, gmax)
    p_group = 1.0 / jnp.sum(jnp.exp(gl - gmax), axis=-1, keepdims=True)

    lo = N_GROUPS + EXPERTS_PER_GROUP * group
    el = jnp.where((lane >= lo) & (lane < lo + EXPERTS_PER_GROUP), lg, -jnp.inf)
    e1 = rmax(el)
    lane1 = first(el, e1)
    el = jnp.where(lane == lane1, -jnp.inf, el)
    e2 = rmax(el)
    lane2 = first(el, e2)
    t2 = jnp.exp(e2 - e1)
    w1 = p_group / (1.0 + t2)
    w2 = p_group * t2 / (1.0 + t2)
    x1 = lane1 - N_GROUPS
    x2 = lane2 - N_GROUPS

    hit1 = lane == x1
    hit2 = lane == x2
    onehot = jnp.where(hit1 | hit2, 1.0, 0.0)
    before = _dot(tri, onehot.astype(BF16)) + run
    rank1 = jnp.sum(jnp.where(hit1, before, 0.0), axis=-1, keepdims=True)
    rank2 = jnp.sum(jnp.where(hit2, before, 0.0), axis=-1, keepdims=True)

    meta = jnp.zeros(lg.shape, F32)
    for k, v in ((META_X1, x1), (META_X2, x2), (META_W1, w1), (META_W2, w2),
                 (META_RANK1, rank1), (META_RANK2, rank2)):
        meta = jnp.where(lane == k, v, meta)
    return meta, run + jnp.sum(onehot, axis=0, keepdims=True)


def _outproj_kernel(x_ref, oa_ref, ob_ref, w_ref, nw_ref, wr_ref, br_ref,
                    x2_ref, hp_ref, meta_ref, cnt_ref, run_ref, tri_ref):
    @pl.when(pl.program_id(0) == 0)
    def _():
        row = lax.broadcasted_iota(jnp.int32, tri_ref.shape, 0)
        col = lax.broadcasted_iota(jnp.int32, tri_ref.shape, 1)
        tri_ref[...] = jnp.where(col < row, 1.0, 0.0).astype(BF16)
        run_ref[...] = jnp.zeros(run_ref.shape, F32)

    half = D_MODEL // 2
    x2 = x_ref[...] + _dot(oa_ref[...], w_ref[0:half, :]) + _dot(ob_ref[...], w_ref[half:, :])
    x2_ref[...] = x2
    r = lax.rsqrt(jnp.mean(x2 * x2, axis=-1, keepdims=True) + RMS_EPS)
    h = x2 * r * nw_ref[...]
    hh = h.astype(BF16)
    hl = (h - hh.astype(F32)).astype(BF16)
    prod = _dot(jnp.concatenate([hh, hl], axis=0), wr_ref[...])
    lg = prod[:OUT_TM, :LANES] + prod[OUT_TM:, :LANES] + prod[:OUT_TM, LANES:] + br_ref[...]
    meta, run = _route_tile(lg, run_ref[...], tri_ref[...])
    meta_ref[...] = meta
    run_ref[...] = run
    cnt_ref[...] = run
    bits = pltpu.bitcast(hh.astype(F32), jnp.uint32)
    for c in range(ROW_WORD_CHUNKS):
        lo = bits[:, 2 * c * LANES:(2 * c + 1) * LANES]
        hi = bits[:, (2 * c + 1) * LANES:(2 * c + 2) * LANES]
        hp_ref[pl.ds(c, OUT_TM, stride=ROW_WORD_CHUNKS), :] = (lo >> 16) | (hi & jnp.uint32(0xFFFF0000))


def _out_proj(x2d, out_a, out_b, w_o, norm_w, w_r, b_r):
    t = x2d.shape[0]
    half = D_MODEL // 2
    row = lambda n: pl.BlockSpec((OUT_TM, n), lambda i: (i, 0))
    full = lambda r, c: pl.BlockSpec((r, c), lambda i: (0, 0))
    return pl.pallas_call(
        _outproj_kernel,
        grid=(t // OUT_TM,),
        in_specs=[row(D_MODEL), row(half), row(half), full(D_MODEL, D_MODEL),
                  full(1, D_MODEL), full(D_MODEL, 2 * LANES), full(1, LANES)],
        out_specs=[row(D_MODEL), pl.BlockSpec((OUT_TM * ROW_WORD_CHUNKS, LANES), lambda i: (i, 0)), row(LANES),
                   full(1, LANES)],
        out_shape=[jax.ShapeDtypeStruct((t, D_MODEL), F32),
                   jax.ShapeDtypeStruct((t * ROW_WORD_CHUNKS, LANES), jnp.uint32),
                   jax.ShapeDtypeStruct((t, LANES), F32),
                   jax.ShapeDtypeStruct((1, LANES), F32)],
        scratch_shapes=[pltpu.VMEM((1, LANES), F32), pltpu.VMEM((OUT_TM, OUT_TM), BF16)],
        compiler_params=pltpu.CompilerParams(
            dimension_semantics=("arbitrary",), vmem_limit_bytes=VMEM_LIMIT),
        name="out_proj",
    )(x2d, out_a, out_b, w_o, norm_w, w_r, b_r)


GATHER_UNROLL = 16
CLEAR_UNROLL = 32
DSP_ROWS = 8 * MOE_TM


def _dispatch_kernel(pos0_ref, pos1_ref, nt_ref, hp_hbm, xs_ref, hp_ref, src_ref, sem):
    t = pl.program_id(0)
    pos_refs = (pos0_ref, pos1_ref)

    @pl.when(t == 0)
    def _():
        load = pltpu.make_async_copy(hp_hbm, hp_ref, sem)
        load.start()

        def clear(p, _):
            src_ref[p] = 0
            return 0
        lax.fori_loop(0, src_ref.shape[0], clear, 0, unroll=CLEAR_UNROLL)

        def place(c, _):
            tok0 = c * (GATHER_UNROLL // EXPERT_TOPK)
            rows = [p[tok0 + u] for u in range(GATHER_UNROLL // EXPERT_TOPK) for p in pos_refs]
            for n, row in enumerate(rows):
                src_ref[row] = tok0 + n // EXPERT_TOPK
            return 0
        lax.fori_loop(0, pos_refs[0].shape[0] * EXPERT_TOPK // GATHER_UNROLL, place, 0)
        load.wait()

    n_valid = jnp.clip(nt_ref[0] * MOE_TM - t * DSP_ROWS, 0, DSP_ROWS)

    def gather(c, _):
        toks = [src_ref[t * DSP_ROWS + c * GATHER_UNROLL + u] for u in range(GATHER_UNROLL)]
        for u in range(GATHER_UNROLL):
            xs_ref[c * GATHER_UNROLL + u] = hp_ref[toks[u]]
        return 0
    lax.fori_loop(0, n_valid // GATHER_UNROLL, gather, 0)

    def zero_tile(c, _):
        xs_ref[pl.ds(pl.multiple_of(c * MOE_TM, MOE_TM), MOE_TM)] = jnp.zeros(
            (MOE_TM,) + xs_ref.shape[1:], xs_ref.dtype)
        return 0
    lax.fori_loop(n_valid // MOE_TM, DSP_ROWS // MOE_TM, zero_tile, 0)


def _dispatch(pos, n_tiles, hp, npad):
    n_tokens = hp.shape[0] // ROW_WORD_CHUNKS
    hp3 = hp.reshape(n_tokens, ROW_WORD_CHUNKS, LANES)
    xs = pl.pallas_call(
        _dispatch_kernel,
        grid_spec=pltpu.PrefetchScalarGridSpec(
            num_scalar_prefetch=3,
            grid=(npad // DSP_ROWS,),
            in_specs=[pl.BlockSpec(memory_space=pl.ANY)],
            out_specs=pl.BlockSpec((DSP_ROWS, ROW_WORD_CHUNKS, LANES), lambda t, p0, p1, n: (t, 0, 0)),
            scratch_shapes=[pltpu.VMEM(hp3.shape, hp3.dtype), pltpu.SMEM((npad,), jnp.int32),
                            pltpu.SemaphoreType.DMA(())],
        ),
        out_shape=jax.ShapeDtypeStruct((npad, ROW_WORD_CHUNKS, LANES), hp.dtype),
        compiler_params=pltpu.CompilerParams(
            dimension_semantics=("arbitrary",), vmem_limit_bytes=VMEM_LIMIT),
        name="dispatch",
    )(pos[0], pos[1], n_tiles, hp3)
    return xs.reshape(npad * ROW_WORD_CHUNKS, LANES)


W_SLOTS = 2
EXP_TILES = 3


def _experts_kernel(tk_ref, eseq_ref, nt_ref, xs_ref, wg_hbm, wu_hbm, wd_hbm, y_ref,
                    wg_buf, wu_buf, wd_buf, wgb_ref, wub_ref, wdb_ref, x_ref, sem):
    n_used = nt_ref[1]
    hbm_bufs = ((wg_hbm, wg_buf), (wu_hbm, wu_buf), (wd_hbm, wd_buf))

    def weight_copies(k, slot):
        e = eseq_ref[k]
        return [pltpu.make_async_copy(hbm.at[e], buf.at[slot], sem.at[slot, n])
                for n, (hbm, buf) in enumerate(hbm_bufs)]

    for sub in range(EXP_TILES):
        _expert_tile(pl.program_id(0) * EXP_TILES + sub, sub, tk_ref, nt_ref, n_used, weight_copies,
                     xs_ref, y_ref, wg_buf, wu_buf, wd_buf, wgb_ref, wub_ref, wdb_ref, x_ref)


def _expert_tile(t, sub, tk_ref, nt_ref, n_used, weight_copies,
                 xs_ref, y_ref, wg_buf, wu_buf, wd_buf, wgb_ref, wub_ref, wdb_ref, x_ref):
    xs_row0 = sub * MOE_TM * ROW_WORD_CHUNKS
    y_rows = pl.ds(sub * MOE_TM, MOE_TM)

    @pl.when(t < nt_ref[0])
    def _():
        k = tk_ref[t]

        @pl.when((t == 0) | (k != tk_ref[jnp.maximum(t - 1, 0)]))
        def _():
            @pl.when(k == 0)
            def _():
                for s in range(W_SLOTS):
                    @pl.when(s < n_used)
                    def _():
                        for c in weight_copies(s, s):
                            c.start()

            slot = k % W_SLOTS
            for c in weight_copies(k, slot):
                c.wait()
            wgb_ref[...] = wg_buf[slot].astype(BF16)
            wub_ref[...] = wu_buf[slot].astype(BF16)
            wdb_ref[...] = wd_buf[slot].astype(BF16)

            @pl.when(k + W_SLOTS < n_used)
            def _():
                for c in weight_copies(k + W_SLOTS, slot):
                    c.start()

        for c in range(ROW_WORD_CHUNKS):
            u32 = xs_ref[pl.ds(xs_row0 + c, MOE_TM, stride=ROW_WORD_CHUNKS), :]
            x_ref[:, 2 * c * LANES:(2 * c + 1) * LANES] = pltpu.bitcast(u32 << 16, F32).astype(BF16)
            x_ref[:, (2 * c + 1) * LANES:(2 * c + 2) * LANES] = (
                pltpu.bitcast(u32 & jnp.uint32(0xFFFF0000), F32).astype(BF16))
        x = x_ref[...]
        a = _dot(x, wgb_ref[...])
        u = _dot(x, wub_ref[...])
        act = (a * jax.nn.sigmoid(a)) * u
        y_ref[y_rows, :] = _dot(act.astype(BF16), wdb_ref[...])

    @pl.when(t >= nt_ref[0])
    def _():
        y_ref[y_rows, :] = jnp.zeros((MOE_TM, D_MODEL), y_ref.dtype)


def _experts(tile_k, expert_seq, n_used, xs, w_gate, w_up, w_down):
    npad = xs.shape[0] // ROW_WORD_CHUNKS
    rows = EXP_TILES * MOE_TM
    assert npad % rows == 0 and npad % DSP_ROWS == 0
    last = lambda t, nt: jnp.minimum(t, jnp.maximum(nt[0] - 1, 0) // EXP_TILES)
    hbm = pl.BlockSpec(memory_space=pl.ANY)
    return pl.pallas_call(
        _experts_kernel,
        grid_spec=pltpu.PrefetchScalarGridSpec(
            num_scalar_prefetch=3,
            grid=(npad // rows,),
            in_specs=[
                pl.BlockSpec((rows * ROW_WORD_CHUNKS, LANES), lambda t, tk, es, nt: (last(t, nt), 0)),
                hbm, hbm, hbm,
            ],
            out_specs=pl.BlockSpec((rows, D_MODEL), lambda t, tk, es, nt: (t, 0)),
            scratch_shapes=[
                pltpu.VMEM((W_SLOTS, D_MODEL, D_EXPERT), F32),
                pltpu.VMEM((W_SLOTS, D_MODEL, D_EXPERT), F32),
                pltpu.VMEM((W_SLOTS, D_EXPERT, D_MODEL), F32),
                pltpu.VMEM((D_MODEL, D_EXPERT), BF16),
                pltpu.VMEM((D_MODEL, D_EXPERT), BF16),
                pltpu.VMEM((D_EXPERT, D_MODEL), BF16),
                pltpu.VMEM((MOE_TM, D_MODEL), BF16),
                pltpu.SemaphoreType.DMA((W_SLOTS, 3)),
            ],
        ),
        out_shape=jax.ShapeDtypeStruct((npad, D_MODEL), F32),
        compiler_params=pltpu.CompilerParams(
            dimension_semantics=("arbitrary",), vmem_limit_bytes=VMEM_LIMIT),
        name="experts",
    )(tile_k, expert_seq, n_used, xs, w_gate, w_up, w_down)


CMB_UNROLL = 8
CMB_SLOTS = 2


def _combine_kernel(pos0_ref, pos1_ref, x2_ref, meta_ref, nw_ref, y_ref, o_ref, buf_ref, sem):
    i = pl.program_id(0)
    pos_refs = (pos0_ref, pos1_ref)

    def row_copy(p, slot, k, r):
        return pltpu.make_async_copy(y_ref.at[pl.ds(p, 1), :], buf_ref.at[slot, k, pl.ds(r, 1), :],
                                     sem.at[slot])

    def issue(tile, slot):
        for c in range(CMB_TT // CMB_UNROLL):
            tok0 = tile * CMB_TT + c * CMB_UNROLL
            rows = [[p[tok0 + u] for u in range(CMB_UNROLL)] for p in pos_refs]
            for u in range(CMB_UNROLL):
                for k in range(EXPERT_TOPK):
                    row_copy(rows[k][u], slot, k, c * CMB_UNROLL + u).start(priority=k % 2)

    @pl.when(i == 0)
    def _():
        issue(0, 0)

    @pl.when(i + 1 < pl.num_programs(0))
    def _():
        issue(i + 1, (i + 1) % CMB_SLOTS)

    slot = i % CMB_SLOTS
    for k in range(EXPERT_TOPK):
        pltpu.make_async_copy(y_ref.at[pl.ds(0, CMB_TT), :], buf_ref.at[slot, k], sem.at[slot]).wait()

    meta = meta_ref[...]
    x3 = (x2_ref[...] + meta[:, META_W1:META_W1 + 1] * buf_ref[slot, 0]
          + meta[:, META_W2:META_W2 + 1] * buf_ref[slot, 1])
    r = lax.rsqrt(jnp.mean(x3 * x3, axis=-1, keepdims=True) + RMS_EPS)
    o_ref[...] = x3 * r * nw_ref[...]


def _combine(pos, x2, meta, norm_w, y):
    t = x2.shape[0]
    return pl.pallas_call(
        _combine_kernel,
        grid_spec=pltpu.PrefetchScalarGridSpec(
            num_scalar_prefetch=2,
            grid=(t // CMB_TT,),
            in_specs=[
                pl.BlockSpec((CMB_TT, D_MODEL), lambda i, p0, p1: (i, 0)),
                pl.BlockSpec((CMB_TT, LANES), lambda i, p0, p1: (i, 0)),
                pl.BlockSpec((1, D_MODEL), lambda i, p0, p1: (0, 0)),
                pl.BlockSpec(memory_space=pl.ANY),
            ],
            out_specs=pl.BlockSpec((CMB_TT, D_MODEL), lambda i, p0, p1: (i, 0)),
            scratch_shapes=[
                pltpu.VMEM((CMB_SLOTS, EXPERT_TOPK, CMB_TT, D_MODEL), F32),
                pltpu.SemaphoreType.DMA((CMB_SLOTS,)),
            ],
        ),
        out_shape=jax.ShapeDtypeStruct((t, D_MODEL), F32),
        compiler_params=pltpu.CompilerParams(
            dimension_semantics=("arbitrary",), vmem_limit_bytes=VMEM_LIMIT),
        name="combine",
    )(pos[0], pos[1], x2, meta, norm_w, y)


def _layout(meta, counts, n_tokens):
    npad = n_tokens * EXPERT_TOPK + N_EXPERTS * MOE_TM
    counts = counts[0, :N_EXPERTS].astype(jnp.int32)
    padded = ((counts + MOE_TM - 1) // MOE_TM) * MOE_TM
    ends = jnp.cumsum(padded)
    starts = ends - padded
    ids = jnp.arange(N_EXPERTS, dtype=jnp.int32)
    pos = []
    for lane_x, lane_rank in ((META_X1, META_RANK1), (META_X2, META_RANK2)):
        hit = meta[:, lane_x].astype(jnp.int32)[:, None] == ids[None, :]
        pos.append(jnp.sum(jnp.where(hit, starts[None, :], 0), axis=-1) + meta[:, lane_rank].astype(jnp.int32))
    tile_start = jnp.arange(npad // MOE_TM, dtype=jnp.int32) * MOE_TM
    tile_expert = jnp.minimum(jnp.sum(ends[None, :] <= tile_start[:, None], axis=1), N_EXPERTS - 1)
    used = counts > 0
    k_of_expert = jnp.cumsum(used.astype(jnp.int32)) - 1
    expert_seq = jnp.sum(jnp.where(used[None, :] & (k_of_expert[None, :] == ids[:, None]), ids[None, :], 0), axis=1)
    tile_k = jnp.sum(jnp.where(tile_expert[:, None] == ids[None, :], k_of_expert[None, :], 0), axis=1)
    n_tiles = (ends[-1] // MOE_TM).astype(jnp.int32)
    n_used = jnp.stack([n_tiles, jnp.sum(used).astype(jnp.int32)])
    return pos, tile_k.astype(jnp.int32), expert_seq.astype(jnp.int32), n_tiles.reshape(1), n_used, npad


def kernel(x, norm_mix_w, w_in, lambda_q1, lambda_k1, lambda_q2, lambda_k2, diff_subln_w, w_out, norm_ffn_w,
           w_router_group, b_router_group, w_router_expert, b_router_expert, w_gate, w_up, w_down, norm_final_w):
    batch, seq, _ = x.shape
    assert seq % ATT_TQ == 0 and seq // MOBA_BLOCK + 8 <= LANES
    assert w_in.shape[0] == 1, "single-layer block"
    n_tokens = batch * seq
    x2d = x.reshape(n_tokens, D_MODEL)

    col = np.arange(IN_WIDTH)
    is_q = (col < MOBA_HEADS * HEAD_DIM) | ((col >= 3 * MOBA_HEADS * HEAD_DIM) & (col < 4 * MOBA_HEADS * HEAD_DIM))
    col_scale = jnp.asarray(np.where(is_q, QK_SCALE * LOG2E, 1.0).astype(np.float32))[None, :]

    proj = _in_proj(x2d, norm_mix_w[0][None, :], w_in[0], col_scale)
    out_a = _moba(proj, jnp.asarray(_alibi_slope_pieces(MOBA_HEADS)), batch, seq)
    out_b = _diff(proj, jnp.asarray(_alibi_slope_pieces(DIFF_HEADS)), lambda_q1, lambda_k1, lambda_q2, lambda_k2,
                  diff_subln_w, batch, seq)

    w_o = w_out[0].astype(BF16)
    w_r = jnp.concatenate([w_router_group[0], w_router_expert[0]], axis=1)
    w_r = jnp.pad(w_r, ((0, 0), (0, LANES - w_r.shape[1])))
    wr_hi = w_r.astype(BF16)
    wr_lo = (w_r - wr_hi.astype(F32)).astype(BF16)
    b_r = jnp.pad(jnp.concatenate([b_router_group[0], b_router_expert[0]]), (0, LANES - N_GROUPS - N_EXPERTS))[None, :]
    x2, hp, meta, counts = _out_proj(x2d, out_a, out_b, w_o, norm_ffn_w[0][None, :],
                                     jnp.concatenate([wr_hi, wr_lo], axis=1), b_r)

    pos, tile_k, expert_seq, n_tiles, n_used, npad = _layout(meta, counts, n_tokens)
    xs = _dispatch(pos, n_tiles, hp, npad)
    y = _experts(tile_k, expert_seq, n_used, xs, w_gate[0], w_up[0], w_down[0])
    out = _combine(pos, x2, meta, norm_final_w[None, :], y)
    return out.reshape(batch, seq, D_MODEL)
```

```python
import functools

import ml_dtypes
import numpy as np
import jax
import jax.numpy as jnp
from jax import lax
from jax.experimental import pallas as pl
from jax.experimental.pallas import tpu as pltpu

F32 = jnp.float32
BF16 = jnp.bfloat16

D_MODEL = 2048
HEAD_DIM = 128
MOBA_HEADS = 8
MOBA_BLOCK = 256
MOBA_TOPK = 3
DIFF_HEADS = 4
DIFF_V_DIM = 256
IN_WIDTH = 6144
N_GROUPS = 4
EXPERTS_PER_GROUP = 8
N_EXPERTS = 32
EXPERT_TOPK = 2
D_EXPERT = 512
RMS_EPS = 1e-6
ALIBI_MAX_BIAS = 8.0
LAMBDA_INIT = 0.8 - 0.6 * float(np.exp(-0.3 * 0))

LANES = 128
QK_SCALE = HEAD_DIM ** -0.5
LOG2E = float(np.log2(np.e))
NEG_BIG = -1e30
M_INIT = -1e38
POS_SPLIT = 64
SLOPE_PIECES = 3
ROW_WORD_CHUNKS = D_MODEL // (2 * LANES)
META_X1, META_X2, META_W1, META_W2, META_RANK1, META_RANK2 = range(6)

IN_TM, IN_TN = 1024, 1024
ATT_TQ = 1024
ATT_TK = 512
KV_UNROLL = 2
HEADS_PER_STEP = 4
DIFF_HEADS_PER_STEP = 1
OUT_TM = 512
MOE_TM = 256
CMB_TT = 256
VMEM_LIMIT = 56 * 1024 * 1024


def _alibi_slope_pieces(n):
    rem = np.exp2(-ALIBI_MAX_BIAS * (np.arange(n, dtype=np.float64) + 1.0) / n) * LOG2E
    pieces = []
    for _ in range(SLOPE_PIECES):
        p = rem.astype(ml_dtypes.bfloat16).astype(np.float64)
        pieces.append(p)
        rem = rem - p
    return np.stack(pieces, axis=1).reshape(-1).astype(np.float32)


def _dot_nt(a, b):
    return lax.dot_general(a, b, (((1,), (1,)), ((), ())), preferred_element_type=F32)


def _dot(a, b):
    return jnp.dot(a, b, preferred_element_type=F32)


def _inproj_kernel(x_ref, nw_ref, cs_ref, w_ref, o_ref, h_ref):
    @pl.when(pl.program_id(1) == 0)
    def _():
        x = x_ref[...]
        r = lax.rsqrt(jnp.mean(x * x, axis=-1, keepdims=True) + RMS_EPS)
        h_ref[...] = (x * r * nw_ref[...]).astype(BF16)

    acc = _dot(h_ref[...], w_ref[...].astype(BF16))
    o_ref[...] = (acc * cs_ref[...]).astype(BF16)


def _in_proj(x2d, norm_w, w_in, col_scale):
    t = x2d.shape[0]
    return pl.pallas_call(
        _inproj_kernel,
        grid=(t // IN_TM, IN_WIDTH // IN_TN),
        in_specs=[
            pl.BlockSpec((IN_TM, D_MODEL), lambda i, j: (i, 0)),
            pl.BlockSpec((1, D_MODEL), lambda i, j: (0, 0)),
            pl.BlockSpec((1, IN_TN), lambda i, j: (0, j)),
            pl.BlockSpec((D_MODEL, IN_TN), lambda i, j: (0, j)),
        ],
        out_specs=pl.BlockSpec((IN_TM, IN_TN), lambda i, j: (i, j)),
        out_shape=jax.ShapeDtypeStruct((t, IN_WIDTH), BF16),
        scratch_shapes=[pltpu.VMEM((IN_TM, D_MODEL), BF16)],
        compiler_params=pltpu.CompilerParams(
            dimension_semantics=("arbitrary", "arbitrary"), vmem_limit_bytes=VMEM_LIMIT),
        name="in_proj",
    )(x2d, norm_w, col_scale, w_in)


def _key_position_features(seq, first_lane):
    pos = lax.broadcasted_iota(jnp.int32, (seq, LANES), 0)
    lane = lax.broadcasted_iota(jnp.int32, (seq, LANES), 1)
    hi = (lane >= first_lane) & (lane < first_lane + SLOPE_PIECES)
    lo = (lane >= first_lane + SLOPE_PIECES) & (lane < first_lane + 2 * SLOPE_PIECES)
    ext = jnp.where(hi, (pos // POS_SPLIT).astype(F32), 0.0)
    return jnp.where(lo, (pos % POS_SPLIT).astype(F32), ext), pos, lane


def _slope_feature_rows(feat_ref, head, width):
    row = lax.broadcasted_iota(jnp.int32, (8, width), 0)
    out = jnp.zeros((8, width), F32)
    for k in range(SLOPE_PIECES):
        c = feat_ref[head * SLOPE_PIECES + k]
        out = jnp.where(row == k, POS_SPLIT * c, out)
        out = jnp.where(row == SLOPE_PIECES + k, c, out)
    return out


def _causal_tile(s):
    row = lax.broadcasted_iota(jnp.int32, s.shape, 0)
    col = lax.broadcasted_iota(jnp.int32, s.shape, 1)
    return jnp.where(col <= row, s, -jnp.inf)


def _online_step(s, m):
    m_new = jnp.maximum(m, jnp.max(s, axis=-1, keepdims=True))
    return m_new, jnp.exp2(m - m_new), jnp.exp2(s - m_new)


def _past_tiles_loop(i, body, init):
    def trip(t, carry):
        for u in range(KV_UNROLL):
            carry = body(t * KV_UNROLL + u, carry)
        return carry

    return lax.fori_loop(0, i * (ATT_TQ // (ATT_TK * KV_UNROLL)), trip, init)


def _rows_from(state, start):
    return state if start == 0 else tuple(a[start:] for a in state)


def _rows_replace(state, start, tail):
    if start == 0:
        return tuple(tail)
    return tuple(jnp.concatenate([a[:start], b], axis=0) for a, b in zip(state, tail))


def _moba_kernel(feat_ref, q_ref, k_ref, v_ref, o_ref, kaug_ref, vaug_ref, kmh_ref, kml_ref, qaug_ref, *, seq):
    nb = seq // MOBA_BLOCK
    hp = pl.program_id(1)
    i = pl.program_id(2)

    @pl.when(i == 0)
    def _():
        ext, pos, lane = _key_position_features(seq, nb)
        ext = jnp.where(lane == pos // MOBA_BLOCK, 1.0, ext).astype(BF16)
        ones_col = jnp.where(lane == 0, 1.0, 0.0).astype(BF16)
        for g in range(HEADS_PER_STEP):
            k = k_ref[:, g * HEAD_DIM:(g + 1) * HEAD_DIM]
            kaug_ref[g, :, 0:HEAD_DIM] = k
            kaug_ref[g, :, HEAD_DIM:2 * HEAD_DIM] = ext
            vaug_ref[g, :, 0:HEAD_DIM] = v_ref[:, g * HEAD_DIM:(g + 1) * HEAD_DIM]
            vaug_ref[g, :, HEAD_DIM:2 * HEAD_DIM] = ones_col
            km = jnp.mean(k.astype(F32).reshape(nb, MOBA_BLOCK, HEAD_DIM), axis=1)
            kmh = km.astype(BF16)
            kmh_ref[g] = jnp.zeros((LANES, HEAD_DIM), BF16)
            kml_ref[g] = jnp.zeros((LANES, HEAD_DIM), BF16)
            kmh_ref[g, 0:nb, :] = kmh
            kml_ref[g, 0:nb, :] = (km - kmh.astype(F32)).astype(BF16)

    blk = lax.broadcasted_iota(jnp.int32, (nb, ATT_TQ), 0)
    col = lax.broadcasted_iota(jnp.int32, (nb, ATT_TQ), 1)
    own = i * (ATT_TQ // MOBA_BLOCK) + col // MOBA_BLOCK
    blk_f = blk.astype(F32)
    for g in range(HEADS_PER_STEP):
        q = q_ref[:, g * HEAD_DIM:(g + 1) * HEAD_DIM]
        gate = (_dot_nt(kmh_ref[g], q) + _dot_nt(kml_ref[g], q))[0:nb, :]
        gate = jnp.where(blk < own, gate, -jnp.inf)
        sel = jnp.zeros((nb, ATT_TQ), F32)
        for _ in range(MOBA_TOPK):
            mx = jnp.max(gate, axis=0, keepdims=True)
            idx = jnp.min(jnp.where(gate == mx, blk_f, float(LANES)), axis=0, keepdims=True)
            pick = (blk_f == idx) & (mx > -jnp.inf)
            sel = jnp.where(pick, 1.0, sel)
            gate = jnp.where(pick, -jnp.inf, gate)
        bias = jnp.where((sel > 0.0) | (blk == own), 0.0, NEG_BIG)
        feat = jnp.concatenate(
            [bias, _slope_feature_rows(feat_ref, hp * HEADS_PER_STEP + g, ATT_TQ),
             jnp.zeros((LANES - nb - 8, ATT_TQ), F32)], axis=0)
        qaug_ref[g, :, 0:HEAD_DIM] = q
        qaug_ref[g, :, HEAD_DIM:2 * HEAD_DIM] = feat.T.astype(BF16)

    def tile_step(g, r, state, row0=0, causal=False):
        m, acc = state
        s = _dot_nt(qaug_ref[g, row0:, :], kaug_ref[g, pl.ds(r, ATT_TK), :])
        if causal:
            s = _causal_tile(s)
        m, alpha, p = _online_step(s, m)
        return m, alpha * acc + _dot(p.astype(BF16), vaug_ref[g, pl.ds(r, ATT_TK), :])

    def body(j, states):
        r = pl.multiple_of(j * ATT_TK, ATT_TK)
        return tuple(tile_step(g, r, states[g]) for g in range(HEADS_PER_STEP))

    init = tuple((jnp.full((ATT_TQ, 1), M_INIT, F32), jnp.zeros((ATT_TQ, 2 * HEAD_DIM), F32))
                 for _ in range(HEADS_PER_STEP))
    states = list(_past_tiles_loop(i, body, init))
    for d in range(ATT_TQ // ATT_TK):
        r = pl.multiple_of(i * ATT_TQ + d * ATT_TK, ATT_TK)
        for g in range(HEADS_PER_STEP):
            tail = tile_step(g, r, _rows_from(states[g], d * ATT_TK), row0=d * ATT_TK, causal=True)
            states[g] = _rows_replace(states[g], d * ATT_TK, tail)
    for g in range(HEADS_PER_STEP):
        acc = states[g][1]
        o_ref[:, g * HEAD_DIM:(g + 1) * HEAD_DIM] = (
            acc[:, 0:HEAD_DIM] / acc[:, HEAD_DIM:HEAD_DIM + 1]).astype(BF16)


def _moba(proj, feats, batch, seq):
    nq = seq // ATT_TQ
    width = HEADS_PER_STEP * HEAD_DIM
    q_col, k_col, v_col = 0, MOBA_HEADS // HEADS_PER_STEP, 2 * MOBA_HEADS // HEADS_PER_STEP
    return pl.pallas_call(
        functools.partial(_moba_kernel, seq=seq),
        grid_spec=pltpu.PrefetchScalarGridSpec(
            num_scalar_prefetch=1,
            grid=(batch, MOBA_HEADS // HEADS_PER_STEP, nq),
            in_specs=[
                pl.BlockSpec((ATT_TQ, width), lambda b, h, i, s: (b * nq + i, q_col + h)),
                pl.BlockSpec((seq, width), lambda b, h, i, s: (b, k_col + h)),
                pl.BlockSpec((seq, width), lambda b, h, i, s: (b, v_col + h)),
            ],
            out_specs=pl.BlockSpec((ATT_TQ, width), lambda b, h, i, s: (b * nq + i, h)),
            scratch_shapes=[
                pltpu.VMEM((HEADS_PER_STEP, seq, 2 * HEAD_DIM), BF16),
                pltpu.VMEM((HEADS_PER_STEP, seq, 2 * HEAD_DIM), BF16),
                pltpu.VMEM((HEADS_PER_STEP, LANES, HEAD_DIM), BF16),
                pltpu.VMEM((HEADS_PER_STEP, LANES, HEAD_DIM), BF16),
                pltpu.VMEM((HEADS_PER_STEP, ATT_TQ, 2 * HEAD_DIM), BF16),
            ],
        ),
        out_shape=jax.ShapeDtypeStruct((batch * seq, MOBA_HEADS * HEAD_DIM), BF16),
        compiler_params=pltpu.CompilerParams(
            dimension_semantics=("arbitrary", "arbitrary", "arbitrary"), vmem_limit_bytes=VMEM_LIMIT),
        name="moba",
    )(feats, proj, proj, proj)


def _diff_kernel(feat_ref, lq1_ref, lk1_ref, lq2_ref, lk2_ref, sw_ref,
                 q_ref, k_ref, v_ref, o_ref, kaug_ref, qaug_ref, *, seq):
    hg = pl.program_id(1)
    i = pl.program_id(2)
    n_chains = 2 * DIFF_HEADS_PER_STEP

    @pl.when(i == 0)
    def _():
        ext = _key_position_features(seq, 0)[0].astype(BF16)
        for n in range(n_chains):
            kaug_ref[n, :, 0:HEAD_DIM] = k_ref[:, n * HEAD_DIM:(n + 1) * HEAD_DIM]
            kaug_ref[n, :, HEAD_DIM:2 * HEAD_DIM] = ext

    for n in range(n_chains):
        if n % 2 == 0:
            head = hg * DIFF_HEADS_PER_STEP + n // 2
            feat = jnp.concatenate([_slope_feature_rows(feat_ref, head, ATT_TQ),
                                    jnp.zeros((LANES - 8, ATT_TQ), F32)], axis=0).T.astype(BF16)
        qaug_ref[n, :, 0:HEAD_DIM] = q_ref[:, n * HEAD_DIM:(n + 1) * HEAD_DIM]
        qaug_ref[n, :, HEAD_DIM:2 * HEAD_DIM] = feat

    def tile_step(n, r, state, row0=0, causal=False):
        m, l, acc = state
        s = _dot_nt(qaug_ref[n, row0:, :], kaug_ref[n, pl.ds(r, ATT_TK), :])
        if causal:
            s = _causal_tile(s)
        m, alpha, p = _online_step(s, m)
        v = v_ref[pl.ds(r, ATT_TK), (n // 2) * DIFF_V_DIM:(n // 2 + 1) * DIFF_V_DIM]
        p = p.astype(BF16)
        return (m, alpha * l + jnp.sum(p.astype(F32), axis=-1, keepdims=True), alpha * acc + _dot(p, v))

    def body(j, states):
        r = pl.multiple_of(j * ATT_TK, ATT_TK)
        return tuple(tile_step(n, r, states[n]) for n in range(n_chains))

    init = tuple((jnp.full((ATT_TQ, 1), M_INIT, F32), jnp.zeros((ATT_TQ, 1), F32),
                  jnp.zeros((ATT_TQ, DIFF_V_DIM), F32)) for _ in range(n_chains))
    states = list(_past_tiles_loop(i, body, init))
    for d in range(ATT_TQ // ATT_TK):
        r = pl.multiple_of(i * ATT_TQ + d * ATT_TK, ATT_TK)
        for n in range(n_chains):
            tail = tile_step(n, r, _rows_from(states[n], d * ATT_TK), row0=d * ATT_TK, causal=True)
            states[n] = _rows_replace(states[n], d * ATT_TK, tail)

    lam = (jnp.exp(jnp.sum(lq1_ref[...] * lk1_ref[...], axis=-1, keepdims=True))
           - jnp.exp(jnp.sum(lq2_ref[...] * lk2_ref[...], axis=-1, keepdims=True))
           + LAMBDA_INIT)
    for hh in range(DIFF_HEADS_PER_STEP):
        (_, l1, acc1), (_, l2, acc2) = states[2 * hh], states[2 * hh + 1]
        o = acc1 / l1 - lam * (acc2 / l2)
        y = o * lax.rsqrt(jnp.mean(o * o, axis=-1, keepdims=True) + RMS_EPS)
        o_ref[:, hh * DIFF_V_DIM:(hh + 1) * DIFF_V_DIM] = ((y * sw_ref[...]) * (1.0 - LAMBDA_INIT)).astype(BF16)


def _diff(proj, feats, lq1, lk1, lq2, lk2, subln_w, batch, seq):
    nq = seq // ATT_TQ
    width = DIFF_HEADS_PER_STEP * 2 * HEAD_DIM
    groups = DIFF_HEADS // DIFF_HEADS_PER_STEP
    q_col = 3 * MOBA_HEADS * HEAD_DIM // width
    k_col = q_col + groups
    v_col = k_col + groups
    vec = lambda n: pl.BlockSpec((1, n), lambda b, h, i, s: (0, 0))
    return pl.pallas_call(
        functools.partial(_diff_kernel, seq=seq),
        grid_spec=pltpu.PrefetchScalarGridSpec(
            num_scalar_prefetch=1,
            grid=(batch, groups, nq),
            in_specs=[
                vec(HEAD_DIM), vec(HEAD_DIM), vec(HEAD_DIM), vec(HEAD_DIM), vec(DIFF_V_DIM),
                pl.BlockSpec((ATT_TQ, width), lambda b, h, i, s: (b * nq + i, q_col + h)),
                pl.BlockSpec((seq, width), lambda b, h, i, s: (b, k_col + h)),
                pl.BlockSpec((seq, width), lambda b, h, i, s: (b, v_col + h)),
            ],
            out_specs=pl.BlockSpec((ATT_TQ, width), lambda b, h, i, s: (b * nq + i, h)),
            scratch_shapes=[
                pltpu.VMEM((2 * DIFF_HEADS_PER_STEP, seq, 2 * HEAD_DIM), BF16),
                pltpu.VMEM((2 * DIFF_HEADS_PER_STEP, ATT_TQ, 2 * HEAD_DIM), BF16),
            ],
        ),
        out_shape=jax.ShapeDtypeStruct((batch * seq, DIFF_HEADS * DIFF_V_DIM), BF16),
        compiler_params=pltpu.CompilerParams(
            dimension_semantics=("arbitrary", "arbitrary", "arbitrary"), vmem_limit_bytes=VMEM_LIMIT),
        name="diff_attn",
    )(feats, lq1, lk1, lq2, lk2, subln_w, proj, proj, proj)


def _route_tile(lg, run, tri):
    lane = lax.broadcasted_iota(jnp.int32, lg.shape, 1).astype(F32)
    rmax = lambda v: jnp.max(v, axis=-1, keepdims=True)
    first = lambda v, m: jnp.min(jnp.where(v == m, lane, float(LANES)), axis=-1, keepdims=True)

    gl = jnp.where(lane < N_GROUPS, lg, -jnp.inf)
    gmax = rmax(gl)
    group = first(gl, gmax)
    p_group = 1.0 / jnp.sum(jnp.exp(gl - gmax), axis=-1, keepdims=True)

    lo = N_GROUPS + EXPERTS_PER_GROUP * group
    el = jnp.where((lane >= lo) & (lane < lo + EXPERTS_PER_GROUP), lg, -jnp.inf)
    e1 = rmax(el)
    lane1 = first(el, e1)
    el = jnp.where(lane == lane1, -jnp.inf, el)
    e2 = rmax(el)
    lane2 = first(el, e2)
    t2 = jnp.exp(e2 - e1)
    w1 = p_group / (1.0 + t2)
    w2 = p_group * t2 / (1.0 + t2)
    x1 = lane1 - N_GROUPS
    x2 = lane2 - N_GROUPS

    hit1 = lane == x1
    hit2 = lane == x2
    onehot = jnp.where(hit1 | hit2, 1.0, 0.0)
    before = _dot(tri, onehot.astype(BF16)) + run
    rank1 = jnp.sum(jnp.where(hit1, before, 0.0), axis=-1, keepdims=True)
    rank2 = jnp.sum(jnp.where(hit2, before, 0.0), axis=-1, keepdims=True)

    meta = jnp.zeros(lg.shape, F32)
    for k, v in ((META_X1, x1), (META_X2, x2), (META_W1, w1), (META_W2, w2),
                 (META_RANK1, rank1), (META_RANK2, rank2)):
        meta = jnp.where(lane == k, v, meta)
    return meta, run + jnp.sum(onehot, axis=0, keepdims=True)


def _outproj_kernel(x_ref, oa_ref, ob_ref, w_ref, nw_ref, wr_ref, br_ref,
                    x2_ref, hp_ref, meta_ref, cnt_ref, run_ref, tri_ref):
    @pl.when(pl.program_id(0) == 0)
    def _():
        row = lax.broadcasted_iota(jnp.int32, tri_ref.shape, 0)
        col = lax.broadcasted_iota(jnp.int32, tri_ref.shape, 1)
        tri_ref[...] = jnp.where(col < row, 1.0, 0.0).astype(BF16)
        run_ref[...] = jnp.zeros(run_ref.shape, F32)

    half = D_MODEL // 2
    x2 = x_ref[...] + _dot(oa_ref[...], w_ref[0:half, :]) + _dot(ob_ref[...], w_ref[half:, :])
    x2_ref[...] = x2
    r = lax.rsqrt(jnp.mean(x2 * x2, axis=-1, keepdims=True) + RMS_EPS)
    h = x2 * r * nw_ref[...]
    hh = h.astype(BF16)
    hl = (h - hh.astype(F32)).astype(BF16)
    prod = _dot(jnp.concatenate([hh, hl], axis=0), wr_ref[...])
    lg = prod[:OUT_TM, :LANES] + prod[OUT_TM:, :LANES] + prod[:OUT_TM, LANES:] + br_ref[...]
    meta, run = _route_tile(lg, run_ref[...], tri_ref[...])
    meta_ref[...] = meta
    run_ref[...] = run
    cnt_ref[...] = run
    bits = pltpu.bitcast(hh.astype(F32), jnp.uint32)
    for c in range(ROW_WORD_CHUNKS):
        lo = bits[:, 2 * c * LANES:(2 * c + 1) * LANES]
        hi = bits[:, (2 * c + 1) * LANES:(2 * c + 2) * LANES]
        hp_ref[pl.ds(c, OUT_TM, stride=ROW_WORD_CHUNKS), :] = (lo >> 16) | (hi & jnp.uint32(0xFFFF0000))


def _out_proj(x2d, out_a, out_b, w_o, norm_w, w_r, b_r):
    t = x2d.shape[0]
    half = D_MODEL // 2
    row = lambda n: pl.BlockSpec((OUT_TM, n), lambda i: (i, 0))
    full = lambda r, c: pl.BlockSpec((r, c), lambda i: (0, 0))
    return pl.pallas_call(
        _outproj_kernel,
        grid=(t // OUT_TM,),
        in_specs=[row(D_MODEL), row(half), row(half), full(D_MODEL, D_MODEL),
                  full(1, D_MODEL), full(D_MODEL, 2 * LANES), full(1, LANES)],
        out_specs=[row(D_MODEL), pl.BlockSpec((OUT_TM * ROW_WORD_CHUNKS, LANES), lambda i: (i, 0)), row(LANES),
                   full(1, LANES)],
        out_shape=[jax.ShapeDtypeStruct((t, D_MODEL), F32),
                   jax.ShapeDtypeStruct((t * ROW_WORD_CHUNKS, LANES), jnp.uint32),
                   jax.ShapeDtypeStruct((t, LANES), F32),
                   jax.ShapeDtypeStruct((1, LANES), F32)],
        scratch_shapes=[pltpu.VMEM((1, LANES), F32), pltpu.VMEM((OUT_TM, OUT_TM), BF16)],
        compiler_params=pltpu.CompilerParams(
            dimension_semantics=("arbitrary",), vmem_limit_bytes=VMEM_LIMIT),
        name="out_proj",
    )(x2d, out_a, out_b, w_o, norm_w, w_r, b_r)


GATHER_UNROLL = 16
CLEAR_UNROLL = 32
DSP_ROWS = 8 * MOE_TM


def _dispatch_kernel(pos0_ref, pos1_ref, nt_ref, hp_hbm, xs_ref, hp_ref, src_ref, sem):
    t = pl.program_id(0)
    pos_refs = (pos0_ref, pos1_ref)

    @pl.when(t == 0)
    def _():
        load = pltpu.make_async_copy(hp_hbm, hp_ref, sem)
        load.start()

        def clear(p, _):
            src_ref[p] = 0
            return 0
        lax.fori_loop(0, src_ref.shape[0], clear, 0, unroll=CLEAR_UNROLL)

        def place(c, _):
            tok0 = c * (GATHER_UNROLL // EXPERT_TOPK)
            rows = [p[tok0 + u] for u in range(GATHER_UNROLL // EXPERT_TOPK) for p in pos_refs]
            for n, row in enumerate(rows):
                src_ref[row] = tok0 + n // EXPERT_TOPK
            return 0
        lax.fori_loop(0, pos_refs[0].shape[0] * EXPERT_TOPK // GATHER_UNROLL, place, 0)
        load.wait()

    n_valid = jnp.clip(nt_ref[0] * MOE_TM - t * DSP_ROWS, 0, DSP_ROWS)

    def gather(c, _):
        toks = [src_ref[t * DSP_ROWS + c * GATHER_UNROLL + u] for u in range(GATHER_UNROLL)]
        for u in range(GATHER_UNROLL):
            xs_ref[c * GATHER_UNROLL + u] = hp_ref[toks[u]]
        return 0
    lax.fori_loop(0, n_valid // GATHER_UNROLL, gather, 0)

    def zero_tile(c, _):
        xs_ref[pl.ds(pl.multiple_of(c * MOE_TM, MOE_TM), MOE_TM)] = jnp.zeros(
            (MOE_TM,) + xs_ref.shape[1:], xs_ref.dtype)
        return 0
    lax.fori_loop(n_valid // MOE_TM, DSP_ROWS // MOE_TM, zero_tile, 0)


def _dispatch(pos, n_tiles, hp, npad):
    n_tokens = hp.shape[0] // ROW_WORD_CHUNKS
    hp3 = hp.reshape(n_tokens, ROW_WORD_CHUNKS, LANES)
    xs = pl.pallas_call(
        _dispatch_kernel,
        grid_spec=pltpu.PrefetchScalarGridSpec(
            num_scalar_prefetch=3,
            grid=(npad // DSP_ROWS,),
            in_specs=[pl.BlockSpec(memory_space=pl.ANY)],
            out_specs=pl.BlockSpec((DSP_ROWS, ROW_WORD_CHUNKS, LANES), lambda t, p0, p1, n: (t, 0, 0)),
            scratch_shapes=[pltpu.VMEM(hp3.shape, hp3.dtype), pltpu.SMEM((npad,), jnp.int32),
                            pltpu.SemaphoreType.DMA(())],
        ),
        out_shape=jax.ShapeDtypeStruct((npad, ROW_WORD_CHUNKS, LANES), hp.dtype),
        compiler_params=pltpu.CompilerParams(
            dimension_semantics=("arbitrary",), vmem_limit_bytes=VMEM_LIMIT),
        name="dispatch",
    )(pos[0], pos[1], n_tiles, hp3)
    return xs.reshape(npad * ROW_WORD_CHUNKS, LANES)


W_SLOTS = 2
EXP_TILES = 3


def _experts_kernel(tk_ref, eseq_ref, nt_ref, xs_ref, wg_hbm, wu_hbm, wd_hbm, y_ref,
                    wg_buf, wu_buf, wd_buf, wgb_ref, wub_ref, wdb_ref, x_ref, sem):
    n_used = nt_ref[1]
    hbm_bufs = ((wg_hbm, wg_buf), (wu_hbm, wu_buf), (wd_hbm, wd_buf))

    def weight_copies(k, slot):
        e = eseq_ref[k]
        return [pltpu.make_async_copy(hbm.at[e], buf.at[slot], sem.at[slot, n])
                for n, (hbm, buf) in enumerate(hbm_bufs)]

    for sub in range(EXP_TILES):
        _expert_tile(pl.program_id(0) * EXP_TILES + sub, sub, tk_ref, nt_ref, n_used, weight_copies,
                     xs_ref, y_ref, wg_buf, wu_buf, wd_buf, wgb_ref, wub_ref, wdb_ref, x_ref)


def _expert_tile(t, sub, tk_ref, nt_ref, n_used, weight_copies,
                 xs_ref, y_ref, wg_buf, wu_buf, wd_buf, wgb_ref, wub_ref, wdb_ref, x_ref):
    xs_row0 = sub * MOE_TM * ROW_WORD_CHUNKS
    y_rows = pl.ds(sub * MOE_TM, MOE_TM)

    @pl.when(t < nt_ref[0])
    def _():
        k = tk_ref[t]

        @pl.when((t == 0) | (k != tk_ref[jnp.maximum(t - 1, 0)]))
        def _():
            @pl.when(k == 0)
            def _():
                for s in range(W_SLOTS):
                    @pl.when(s < n_used)
                    def _():
                        for c in weight_copies(s, s):
                            c.start()

            slot = k % W_SLOTS
            for c in weight_copies(k, slot):
                c.wait()
            wgb_ref[...] = wg_buf[slot].astype(BF16)
            wub_ref[...] = wu_buf[slot].astype(BF16)
            wdb_ref[...] = wd_buf[slot].astype(BF16)

            @pl.when(k + W_SLOTS < n_used)
            def _():
                for c in weight_copies(k + W_SLOTS, slot):
                    c.start()

        for c in range(ROW_WORD_CHUNKS):
            u32 = xs_ref[pl.ds(xs_row0 + c, MOE_TM, stride=ROW_WORD_CHUNKS), :]
            x_ref[:, 2 * c * LANES:(2 * c + 1) * LANES] = pltpu.bitcast(u32 << 16, F32).astype(BF16)
            x_ref[:, (2 * c + 1) * LANES:(2 * c + 2) * LANES] = (
                pltpu.bitcast(u32 & jnp.uint32(0xFFFF0000), F32).astype(BF16))
        x = x_ref[...]
        a = _dot(x, wgb_ref[...])
        u = _dot(x, wub_ref[...])
        act = (a * jax.nn.sigmoid(a)) * u
        y_ref[y_rows, :] = _dot(act.astype(BF16), wdb_ref[...])

    @pl.when(t >= nt_ref[0])
    def _():
        y_ref[y_rows, :] = jnp.zeros((MOE_TM, D_MODEL), y_ref.dtype)


def _experts(tile_k, expert_seq, n_used, xs, w_gate, w_up, w_down):
    npad = xs.shape[0] // ROW_WORD_CHUNKS
    rows = EXP_TILES * MOE_TM
    assert npad % rows == 0 and npad % DSP_ROWS == 0
    last = lambda t, nt: jnp.minimum(t, jnp.maximum(nt[0] - 1, 0) // EXP_TILES)
    hbm = pl.BlockSpec(memory_space=pl.ANY)
    return pl.pallas_call(
        _experts_kernel,
        grid_spec=pltpu.PrefetchScalarGridSpec(
            num_scalar_prefetch=3,
            grid=(npad // rows,),
            in_specs=[
                pl.BlockSpec((rows * ROW_WORD_CHUNKS, LANES), lambda t, tk, es, nt: (last(t, nt), 0)),
                hbm, hbm, hbm,
            ],
            out_specs=pl.BlockSpec((rows, D_MODEL), lambda t, tk, es, nt: (t, 0)),
            scratch_shapes=[
                pltpu.VMEM((W_SLOTS, D_MODEL, D_EXPERT), F32),
                pltpu.VMEM((W_SLOTS, D_MODEL, D_EXPERT), F32),
                pltpu.VMEM((W_SLOTS, D_EXPERT, D_MODEL), F32),
                pltpu.VMEM((D_MODEL, D_EXPERT), BF16),
                pltpu.VMEM((D_MODEL, D_EXPERT), BF16),
                pltpu.VMEM((D_EXPERT, D_MODEL), BF16),
                pltpu.VMEM((MOE_TM, D_MODEL), BF16),
                pltpu.SemaphoreType.DMA((W_SLOTS, 3)),
            ],
        ),
        out_shape=jax.ShapeDtypeStruct((npad, D_MODEL), F32),
        compiler_params=pltpu.CompilerParams(
            dimension_semantics=("arbitrary",), vmem_limit_bytes=VMEM_LIMIT),
        name="experts",
    )(tile_k, expert_seq, n_used, xs, w_gate, w_up, w_down)


CMB_UNROLL = 8
CMB_SLOTS = 2


def _combine_kernel(pos0_ref, pos1_ref, x2_ref, meta_ref, nw_ref, y_ref, o_ref, buf_ref, sem):
    i = pl.program_id(0)
    pos_refs = (pos0_ref, pos1_ref)

    def row_copy(p, slot, k, r):
        return pltpu.make_async_copy(y_ref.at[pl.ds(p, 1), :], buf_ref.at[slot, k, pl.ds(r, 1), :],
                                     sem.at[slot])

    def issue(tile, slot):
        for c in range(CMB_TT // CMB_UNROLL):
            tok0 = tile * CMB_TT + c * CMB_UNROLL
            rows = [[p[tok0 + u] for u in range(CMB_UNROLL)] for p in pos_refs]
            for u in range(CMB_UNROLL):
                for k in range(EXPERT_TOPK):
                    row_copy(rows[k][u], slot, k, c * CMB_UNROLL + u).start(priority=k % 2)

    @pl.when(i == 0)
    def _():
        issue(0, 0)

    @pl.when(i + 1 < pl.num_programs(0))
    def _():
        issue(i + 1, (i + 1) % CMB_SLOTS)

    slot = i % CMB_SLOTS
    for k in range(EXPERT_TOPK):
        pltpu.make_async_copy(y_ref.at[pl.ds(0, CMB_TT), :], buf_ref.at[slot, k], sem.at[slot]).wait()

    meta = meta_ref[...]
    x3 = (x2_ref[...] + meta[:, META_W1:META_W1 + 1] * buf_ref[slot, 0]
          + meta[:, META_W2:META_W2 + 1] * buf_ref[slot, 1])
    r = lax.rsqrt(jnp.mean(x3 * x3, axis=-1, keepdims=True) + RMS_EPS)
    o_ref[...] = x3 * r * nw_ref[...]


def _combine(pos, x2, meta, norm_w, y):
    t = x2.shape[0]
    return pl.pallas_call(
        _combine_kernel,
        grid_spec=pltpu.PrefetchScalarGridSpec(
            num_scalar_prefetch=2,
            grid=(t // CMB_TT,),
            in_specs=[
                pl.BlockSpec((CMB_TT, D_MODEL), lambda i, p0, p1: (i, 0)),
                pl.BlockSpec((CMB_TT, LANES), lambda i, p0, p1: (i, 0)),
                pl.BlockSpec((1, D_MODEL), lambda i, p0, p1: (0, 0)),
                pl.BlockSpec(memory_space=pl.ANY),
            ],
            out_specs=pl.BlockSpec((CMB_TT, D_MODEL), lambda i, p0, p1: (i, 0)),
            scratch_shapes=[
                pltpu.VMEM((CMB_SLOTS, EXPERT_TOPK, CMB_TT, D_MODEL), F32),
                pltpu.SemaphoreType.DMA((CMB_SLOTS,)),
            ],
        ),
        out_shape=jax.ShapeDtypeStruct((t, D_MODEL), F32),
        compiler_params=pltpu.CompilerParams(
            dimension_semantics=("arbitrary",), vmem_limit_bytes=VMEM_LIMIT),
        name="combine",
    )(pos[0], pos[1], x2, meta, norm_w, y)


def _layout(meta, counts, n_tokens):
    npad = n_tokens * EXPERT_TOPK + N_EXPERTS * MOE_TM
    counts = counts[0, :N_EXPERTS].astype(jnp.int32)
    padded = ((counts + MOE_TM - 1) // MOE_TM) * MOE_TM
    ends = jnp.cumsum(padded)
    starts = ends - padded
    ids = jnp.arange(N_EXPERTS, dtype=jnp.int32)
    pos = []
    for lane_x, lane_rank in ((META_X1, META_RANK1), (META_X2, META_RANK2)):
        hit = meta[:, lane_x].astype(jnp.int32)[:, None] == ids[None, :]
        pos.append(jnp.sum(jnp.where(hit, starts[None, :], 0), axis=-1) + meta[:, lane_rank].astype(jnp.int32))
    tile_start = jnp.arange(npad // MOE_TM, dtype=jnp.int32) * MOE_TM
    tile_expert = jnp.minimum(jnp.sum(ends[None, :] <= tile_start[:, None], axis=1), N_EXPERTS - 1)
    used = counts > 0
    k_of_expert = jnp.cumsum(used.astype(jnp.int32)) - 1
    expert_seq = jnp.sum(jnp.where(used[None, :] & (k_of_expert[None, :] == ids[:, None]), ids[None, :], 0), axis=1)
    tile_k = jnp.sum(jnp.where(tile_expert[:, None] == ids[None, :], k_of_expert[None, :], 0), axis=1)
    n_tiles = (ends[-1] // MOE_TM).astype(jnp.int32)
    n_used = jnp.stack([n_tiles, jnp.sum(used).astype(jnp.int32)])
    return pos, tile_k.astype(jnp.int32), expert_seq.astype(jnp.int32), n_tiles.reshape(1), n_used, npad


def kernel(x, norm_mix_w, w_in, lambda_q1, lambda_k1, lambda_q2, lambda_k2, diff_subln_w, w_out, norm_ffn_w,
           w_router_group, b_router_group, w_router_expert, b_router_expert, w_gate, w_up, w_down, norm_final_w):
    batch, seq, _ = x.shape
    assert seq % ATT_TQ == 0 and seq // MOBA_BLOCK + 8 <= LANES
    assert w_in.shape[0] == 1, "single-layer block"
    n_tokens = batch * seq
    x2d = x.reshape(n_tokens, D_MODEL)

    col = np.arange(IN_WIDTH)
    is_q = (col < MOBA_HEADS * HEAD_DIM) | ((col >= 3 * MOBA_HEADS * HEAD_DIM) & (col < 4 * MOBA_HEADS * HEAD_DIM))
    col_scale = jnp.asarray(np.where(is_q, QK_SCALE * LOG2E, 1.0).astype(np.float32))[None, :]

    proj = _in_proj(x2d, norm_mix_w[0][None, :], w_in[0], col_scale)
    out_a = _moba(proj, jnp.asarray(_alibi_slope_pieces(MOBA_HEADS)), batch, seq)
    out_b = _diff(proj, jnp.asarray(_alibi_slope_pieces(DIFF_HEADS)), lambda_q1, lambda_k1, lambda_q2, lambda_k2,
                  diff_subln_w, batch, seq)

    w_o = w_out[0].astype(BF16)
    w_r = jnp.concatenate([w_router_group[0], w_router_expert[0]], axis=1)
    w_r = jnp.pad(w_r, ((0, 0), (0, LANES - w_r.shape[1])))
    wr_hi = w_r.astype(BF16)
    wr_lo = (w_r - wr_hi.astype(F32)).astype(BF16)
    b_r = jnp.pad(jnp.concatenate([b_router_group[0], b_router_expert[0]]), (0, LANES - N_GROUPS - N_EXPERTS))[None, :]
    x2, hp, meta, counts = _out_proj(x2d, out_a, out_b, w_o, norm_ffn_w[0][None, :],
                                     jnp.concatenate([wr_hi, wr_lo], axis=1), b_r)

    pos, tile_k, expert_seq, n_tiles, n_used, npad = _layout(meta, counts, n_tokens)
    xs = _dispatch(pos, n_tiles, hp, npad)
    y = _experts(tile_k, expert_seq, n_used, xs, w_gate[0], w_up[0], w_down[0])
    out = _combine(pos, x2, meta, norm_final_w[None, :], y)
    return out.reshape(batch, seq, D_MODEL)
```

```python
import functools

import ml_dtypes
import numpy as np
import jax
import jax.numpy as jnp
from jax import lax
from jax.experimental import pallas as pl
from jax.experimental.pallas import tpu as pltpu

F32 = jnp.float32
BF16 = jnp.bfloat16

D_MODEL = 2048
HEAD_DIM = 128
MOBA_HEADS = 8
MOBA_BLOCK = 256
MOBA_TOPK = 3
DIFF_HEADS = 4
DIFF_V_DIM = 256
IN_WIDTH = 6144
N_GROUPS = 4
EXPERTS_PER_GROUP = 8
N_EXPERTS = 32
EXPERT_TOPK = 2
D_EXPERT = 512
RMS_EPS = 1e-6
ALIBI_MAX_BIAS = 8.0
LAMBDA_INIT = 0.8 - 0.6 * float(np.exp(-0.3 * 0))

LANES = 128
QK_SCALE = HEAD_DIM ** -0.5
LOG2E = float(np.log2(np.e))
NEG_BIG = -1e30
M_INIT = -1e38
POS_SPLIT = 64
SLOPE_PIECES = 3
ROW_WORD_CHUNKS = D_MODEL // (2 * LANES)
META_X1, META_X2, META_W1, META_W2, META_RANK1, META_RANK2 = range(6)

IN_TM, IN_TN = 1024, 1024
ATT_TQ = 1024
ATT_TK = 512
KV_UNROLL = 2
HEADS_PER_STEP = 4
DIFF_HEADS_PER_STEP = 1
OUT_TM = 512
MOE_TM = 256
CMB_TT = 256
VMEM_LIMIT = 56 * 1024 * 1024


def _alibi_slope_pieces(n):
    rem = np.exp2(-ALIBI_MAX_BIAS * (np.arange(n, dtype=np.float64) + 1.0) / n) * LOG2E
    pieces = []
    for _ in range(SLOPE_PIECES):
        p = rem.astype(ml_dtypes.bfloat16).astype(np.float64)
        pieces.append(p)
        rem = rem - p
    return np.stack(pieces, axis=1).reshape(-1).astype(np.float32)


def _dot_nt(a, b):
    return lax.dot_general(a, b, (((1,), (1,)), ((), ())), preferred_element_type=F32)


def _dot(a, b):
    return jnp.dot(a, b, preferred_element_type=F32)


def _inproj_kernel(x_ref, nw_ref, cs_ref, w_ref, o_ref, h_ref):
    @pl.when(pl.program_id(1) == 0)
    def _():
        x = x_ref[...]
        r = lax.rsqrt(jnp.mean(x * x, axis=-1, keepdims=True) + RMS_EPS)
        h_ref[...] = (x * r * nw_ref[...]).astype(BF16)

    acc = _dot(h_ref[...], w_ref[...].astype(BF16))
    o_ref[...] = (acc * cs_ref[...]).astype(BF16)


def _in_proj(x2d, norm_w, w_in, col_scale):
    t = x2d.shape[0]
    return pl.pallas_call(
        _inproj_kernel,
        grid=(t // IN_TM, IN_WIDTH // IN_TN),
        in_specs=[
            pl.BlockSpec((IN_TM, D_MODEL), lambda i, j: (i, 0)),
            pl.BlockSpec((1, D_MODEL), lambda i, j: (0, 0)),
            pl.BlockSpec((1, IN_TN), lambda i, j: (0, j)),
            pl.BlockSpec((D_MODEL, IN_TN), lambda i, j: (0, j)),
        ],
        out_specs=pl.BlockSpec((IN_TM, IN_TN), lambda i, j: (i, j)),
        out_shape=jax.ShapeDtypeStruct((t, IN_WIDTH), BF16),
        scratch_shapes=[pltpu.VMEM((IN_TM, D_MODEL), BF16)],
        compiler_params=pltpu.CompilerParams(
            dimension_semantics=("arbitrary", "arbitrary"), vmem_limit_bytes=VMEM_LIMIT),
        name="in_proj",
    )(x2d, norm_w, col_scale, w_in)


def _key_position_features(seq, first_lane):
    pos = lax.broadcasted_iota(jnp.int32, (seq, LANES), 0)
    lane = lax.broadcasted_iota(jnp.int32, (seq, LANES), 1)
    hi = (lane >= first_lane) & (lane < first_lane + SLOPE_PIECES)
    lo = (lane >= first_lane + SLOPE_PIECES) & (lane < first_lane + 2 * SLOPE_PIECES)
    ext = jnp.where(hi, (pos // POS_SPLIT).astype(F32), 0.0)
    return jnp.where(lo, (pos % POS_SPLIT).astype(F32), ext), pos, lane


def _slope_feature_rows(feat_ref, head, width):
    row = lax.broadcasted_iota(jnp.int32, (8, width), 0)
    out = jnp.zeros((8, width), F32)
    for k in range(SLOPE_PIECES):
        c = feat_ref[head * SLOPE_PIECES + k]
        out = jnp.where(row == k, POS_SPLIT * c, out)
        out = jnp.where(row == SLOPE_PIECES + k, c, out)
    return out


def _causal_tile(s):
    row = lax.broadcasted_iota(jnp.int32, s.shape, 0)
    col = lax.broadcasted_iota(jnp.int32, s.shape, 1)
    return jnp.where(col <= row, s, -jnp.inf)


def _online_step(s, m):
    m_new = jnp.maximum(m, jnp.max(s, axis=-1, keepdims=True))
    return m_new, jnp.exp2(m - m_new), jnp.exp2(s - m_new)


def _past_tiles_loop(i, body, init):
    def trip(t, carry):
        for u in range(KV_UNROLL):
            carry = body(t * KV_UNROLL + u, carry)
        return carry

    return lax.fori_loop(0, i * (ATT_TQ // (ATT_TK * KV_UNROLL)), trip, init)


def _rows_from(state, start):
    return state if start == 0 else tuple(a[start:] for a in state)


def _rows_replace(state, start, tail):
    if start == 0:
        return tuple(tail)
    return tuple(jnp.concatenate([a[:start], b], axis=0) for a, b in zip(state, tail))


def _moba_kernel(feat_ref, q_ref, k_ref, v_ref, o_ref, kaug_ref, vaug_ref, kmh_ref, kml_ref, qaug_ref, *, seq):
    nb = seq // MOBA_BLOCK
    hp = pl.program_id(1)
    i = pl.program_id(2)

    @pl.when(i == 0)
    def _():
        ext, pos, lane = _key_position_features(seq, nb)
        ext = jnp.where(lane == pos // MOBA_BLOCK, 1.0, ext).astype(BF16)
        ones_col = jnp.where(lane == 0, 1.0, 0.0).astype(BF16)
        for g in range(HEADS_PER_STEP):
            k = k_ref[:, g * HEAD_DIM:(g + 1) * HEAD_DIM]
            kaug_ref[g, :, 0:HEAD_DIM] = k
            kaug_ref[g, :, HEAD_DIM:2 * HEAD_DIM] = ext
            vaug_ref[g, :, 0:HEAD_DIM] = v_ref[:, g * HEAD_DIM:(g + 1) * HEAD_DIM]
            vaug_ref[g, :, HEAD_DIM:2 * HEAD_DIM] = ones_col
            km = jnp.mean(k.astype(F32).reshape(nb, MOBA_BLOCK, HEAD_DIM), axis=1)
            kmh = km.astype(BF16)
            kmh_ref[g] = jnp.zeros((LANES, HEAD_DIM), BF16)
            kml_ref[g] = jnp.zeros((LANES, HEAD_DIM), BF16)
            kmh_ref[g, 0:nb, :] = kmh
            kml_ref[g, 0:nb, :] = (km - kmh.astype(F32)).astype(BF16)

    blk = lax.broadcasted_iota(jnp.int32, (nb, ATT_TQ), 0)
    col = lax.broadcasted_iota(jnp.int32, (nb, ATT_TQ), 1)
    own = i * (ATT_TQ // MOBA_BLOCK) + col // MOBA_BLOCK
    blk_f = blk.astype(F32)
    for g in range(HEADS_PER_STEP):
        q = q_ref[:, g * HEAD_DIM:(g + 1) * HEAD_DIM]
        gate = (_dot_nt(kmh_ref[g], q) + _dot_nt(kml_ref[g], q))[0:nb, :]
        gate = jnp.where(blk < own, gate, -jnp.inf)
        sel = jnp.zeros((nb, ATT_TQ), F32)
        for _ in range(MOBA_TOPK):
            mx = jnp.max(gate, axis=0, keepdims=True)
            idx = jnp.min(jnp.where(gate == mx, blk_f, float(LANES)), axis=0, keepdims=True)
            pick = (blk_f == idx) & (mx > -jnp.inf)
            sel = jnp.where(pick, 1.0, sel)
            gate = jnp.where(pick, -jnp.inf, gate)
        bias = jnp.where((sel > 0.0) | (blk == own), 0.0, NEG_BIG)
        feat = jnp.concatenate(
            [bias, _slope_feature_rows(feat_ref, hp * HEADS_PER_STEP + g, ATT_TQ),
             jnp.zeros((LANES - nb - 8, ATT_TQ), F32)], axis=0)
        qaug_ref[g, :, 0:HEAD_DIM] = q
        qaug_ref[g, :, HEAD_DIM:2 * HEAD_DIM] = feat.T.astype(BF16)

    def tile_step(g, r, state, row0=0, causal=False):
        m, acc = state
        s = _dot_nt(qaug_ref[g, row0:, :], kaug_ref[g, pl.ds(r, ATT_TK), :])
        if causal:
            s = _causal_tile(s)
        m, alpha, p = _online_step(s, m)
        return m, alpha * acc + _dot(p.astype(BF16), vaug_ref[g, pl.ds(r, ATT_TK), :])

    def body(j, states):
        r = pl.multiple_of(j * ATT_TK, ATT_TK)
        return tuple(tile_step(g, r, states[g]) for g in range(HEADS_PER_STEP))

    init = tuple((jnp.full((ATT_TQ, 1), M_INIT, F32), jnp.zeros((ATT_TQ, 2 * HEAD_DIM), F32))
                 for _ in range(HEADS_PER_STEP))
    states = list(_past_tiles_loop(i, body, init))
    for d in range(ATT_TQ // ATT_TK):
        r = pl.multiple_of(i * ATT_TQ + d * ATT_TK, ATT_TK)
        for g in range(HEADS_PER_STEP):
            tail = tile_step(g, r, _rows_from(states[g], d * ATT_TK), row0=d * ATT_TK, causal=True)
            states[g] = _rows_replace(states[g], d * ATT_TK, tail)
    for g in range(HEADS_PER_STEP):
        acc = states[g][1]
        o_ref[:, g * HEAD_DIM:(g + 1) * HEAD_DIM] = (
            acc[:, 0:HEAD_DIM] / acc[:, HEAD_DIM:HEAD_DIM + 1]).astype(BF16)


def _moba(proj, feats, batch, seq):
    nq = seq // ATT_TQ
    width = HEADS_PER_STEP * HEAD_DIM
    q_col, k_col, v_col = 0, MOBA_HEADS // HEADS_PER_STEP, 2 * MOBA_HEADS // HEADS_PER_STEP
    return pl.pallas_call(
        functools.partial(_moba_kernel, seq=seq),
        grid_spec=pltpu.PrefetchScalarGridSpec(
            num_scalar_prefetch=1,
            grid=(batch, MOBA_HEADS // HEADS_PER_STEP, nq),
            in_specs=[
                pl.BlockSpec((ATT_TQ, width), lambda b, h, i, s: (b * nq + i, q_col + h)),
                pl.BlockSpec((seq, width), lambda b, h, i, s: (b, k_col + h)),
                pl.BlockSpec((seq, width), lambda b, h, i, s: (b, v_col + h)),
            ],
            out_specs=pl.BlockSpec((ATT_TQ, width), lambda b, h, i, s: (b * nq + i, h)),
            scratch_shapes=[
                pltpu.VMEM((HEADS_PER_STEP, seq, 2 * HEAD_DIM), BF16),
                pltpu.VMEM((HEADS_PER_STEP, seq, 2 * HEAD_DIM), BF16),
                pltpu.VMEM((HEADS_PER_STEP, LANES, HEAD_DIM), BF16),
                pltpu.VMEM((HEADS_PER_STEP, LANES, HEAD_DIM), BF16),
                pltpu.VMEM((HEADS_PER_STEP, ATT_TQ, 2 * HEAD_DIM), BF16),
            ],
        ),
        out_shape=jax.ShapeDtypeStruct((batch * seq, MOBA_HEADS * HEAD_DIM), BF16),
        compiler_params=pltpu.CompilerParams(
            dimension_semantics=("arbitrary", "arbitrary", "arbitrary"), vmem_limit_bytes=VMEM_LIMIT),
        name="moba",
    )(feats, proj, proj, proj)


def _diff_kernel(feat_ref, lq1_ref, lk1_ref, lq2_ref, lk2_ref, sw_ref,
                 q_ref, k_ref, v_ref, o_ref, kaug_ref, qaug_ref, *, seq):
    hg = pl.program_id(1)
    i = pl.program_id(2)
    n_chains = 2 * DIFF_HEADS_PER_STEP

    @pl.when(i == 0)
    def _():
        ext = _key_position_features(seq, 0)[0].astype(BF16)
        for n in range(n_chains):
            kaug_ref[n, :, 0:HEAD_DIM] = k_ref[:, n * HEAD_DIM:(n + 1) * HEAD_DIM]
            kaug_ref[n, :, HEAD_DIM:2 * HEAD_DIM] = ext

    for n in range(n_chains):
        if n % 2 == 0:
            head = hg * DIFF_HEADS_PER_STEP + n // 2
            feat = jnp.concatenate([_slope_feature_rows(feat_ref, head, ATT_TQ),
                                    jnp.zeros((LANES - 8, ATT_TQ), F32)], axis=0).T.astype(BF16)
        qaug_ref[n, :, 0:HEAD_DIM] = q_ref[:, n * HEAD_DIM:(n + 1) * HEAD_DIM]
        qaug_ref[n, :, HEAD_DIM:2 * HEAD_DIM] = feat

    def tile_step(n, r, state, row0=0, causal=False):
        m, l, acc = state
        s = _dot_nt(qaug_ref[n, row0:, :], kaug_ref[n, pl.ds(r, ATT_TK), :])
        if causal:
            s = _causal_tile(s)
        m, alpha, p = _online_step(s, m)
        v = v_ref[pl.ds(r, ATT_TK), (n // 2) * DIFF_V_DIM:(n // 2 + 1) * DIFF_V_DIM]
        p = p.astype(BF16)
        return (m, alpha * l + jnp.sum(p.astype(F32), axis=-1, keepdims=True), alpha * acc + _dot(p, v))

    def body(j, states):
        r = pl.multiple_of(j * ATT_TK, ATT_TK)
        return tuple(tile_step(n, r, states[n]) for n in range(n_chains))

    init = tuple((jnp.full((ATT_TQ, 1), M_INIT, F32), jnp.zeros((ATT_TQ, 1), F32),
                  jnp.zeros((ATT_TQ, DIFF_V_DIM), F32)) for _ in range(n_chains))
    states = list(_past_tiles_loop(i, body, init))
    for d in range(ATT_TQ // ATT_TK):
        r = pl.multiple_of(i * ATT_TQ + d * ATT_TK, ATT_TK)
        for n in range(n_chains):
            tail = tile_step(n, r, _rows_from(states[n], d * ATT_TK), row0=d * ATT_TK, causal=True)
            states[n] = _rows_replace(states[n], d * ATT_TK, tail)

    lam = (jnp.exp(jnp.sum(lq1_ref[...] * lk1_ref[...], axis=-1, keepdims=True))
           - jnp.exp(jnp.sum(lq2_ref[...] * lk2_ref[...], axis=-1, keepdims=True))
           + LAMBDA_INIT)
    for hh in range(DIFF_HEADS_PER_STEP):
        (_, l1, acc1), (_, l2, acc2) = states[2 * hh], states[2 * hh + 1]
        o = acc1 / l1 - lam * (acc2 / l2)
        y = o * lax.rsqrt(jnp.mean(o * o, axis=-1, keepdims=True) + RMS_EPS)
        o_ref[:, hh * DIFF_V_DIM:(hh + 1) * DIFF_V_DIM] = ((y * sw_ref[...]) * (1.0 - LAMBDA_INIT)).astype(BF16)


def _diff(proj, feats, lq1, lk1, lq2, lk2, subln_w, batch, seq):
    nq = seq // ATT_TQ
    width = DIFF_HEADS_PER_STEP * 2 * HEAD_DIM
    groups = DIFF_HEADS // DIFF_HEADS_PER_STEP
    q_col = 3 * MOBA_HEADS * HEAD_DIM // width
    k_col = q_col + groups
    v_col = k_col + groups
    vec = lambda n: pl.BlockSpec((1, n), lambda b, h, i, s: (0, 0))
    return pl.pallas_call(
        functools.partial(_diff_kernel, seq=seq),
        grid_spec=pltpu.PrefetchScalarGridSpec(
            num_scalar_prefetch=1,
            grid=(batch, groups, nq),
            in_specs=[
                vec(HEAD_DIM), vec(HEAD_DIM), vec(HEAD_DIM), vec(HEAD_DIM), vec(DIFF_V_DIM),
                pl.BlockSpec((ATT_TQ, width), lambda b, h, i, s: (b * nq + i, q_col + h)),
                pl.BlockSpec((seq, width), lambda b, h, i, s: (b, k_col + h)),
                pl.BlockSpec((seq, width), lambda b, h, i, s: (b, v_col + h)),
            ],
            out_specs=pl.BlockSpec((ATT_TQ, width), lambda b, h, i, s: (b * nq + i, h)),
            scratch_shapes=[
                pltpu.VMEM((2 * DIFF_HEADS_PER_STEP, seq, 2 * HEAD_DIM), BF16),
                pltpu.VMEM((2 * DIFF_HEADS_PER_STEP, ATT_TQ, 2 * HEAD_DIM), BF16),
            ],
        ),
        out_shape=jax.ShapeDtypeStruct((batch * seq, DIFF_HEADS * DIFF_V_DIM), BF16),
        compiler_params=pltpu.CompilerParams(
            dimension_semantics=("arbitrary", "arbitrary", "arbitrary"), vmem_limit_bytes=VMEM_LIMIT),
        name="diff_attn",
    )(feats, lq1, lk1, lq2, lk2, subln_w, proj, proj, proj)


def _route_tile(lg, run, tri):
    lane = lax.broadcasted_iota(jnp.int32, lg.shape, 1).astype(F32)
    rmax = lambda v: jnp.max(v, axis=-1, keepdims=True)
    first = lambda v, m: jnp.min(jnp.where(v == m, lane, float(LANES)), axis=-1, keepdims=True)

    gl = jnp.where(lane < N_GROUPS, lg, -jnp.inf)
    gmax = rmax(gl)
    group = first(gl, gmax)
    p_group = 1.0 / jnp.sum(jnp.exp(gl - gmax), axis=-1, keepdims=True)

    lo = N_GROUPS + EXPERTS_PER_GROUP * group
    el = jnp.where((lane >= lo) & (lane < lo + EXPERTS_PER_GROUP), lg, -jnp.inf)
    e1 = rmax(el)
    lane1 = first(el, e1)
    el = jnp.where(lane == lane1, -jnp.inf, el)
    e2 = rmax(el)
    lane2 = first(el, e2)
    t2 = jnp.exp(e2 - e1)
    w1 = p_group / (1.0 + t2)
    w2 = p_group * t2 / (1.0 + t2)
    x1 = lane1 - N_GROUPS
    x2 = lane2 - N_GROUPS

    hit1 = lane == x1
    hit2 = lane == x2
    onehot = jnp.where(hit1 | hit2, 1.0, 0.0)
    before = _dot(tri, onehot.astype(BF16)) + run
    rank1 = jnp.sum(jnp.where(hit1, before, 0.0), axis=-1, keepdims=True)
    rank2 = jnp.sum(jnp.where(hit2, before, 0.0), axis=-1, keepdims=True)

    meta = jnp.zeros(lg.shape, F32)
    for k, v in ((META_X1, x1), (META_X2, x2), (META_W1, w1), (META_W2, w2),
                 (META_RANK1, rank1), (META_RANK2, rank2)):
        meta = jnp.where(lane == k, v, meta)
    return meta, run + jnp.sum(onehot, axis=0, keepdims=True)


def _outproj_kernel(x_ref, oa_ref, ob_ref, wf_ref, nw_ref, wr_ref, br_ref,
                    x2_ref, hp_ref, meta_ref, cnt_ref, run_ref, tri_ref, w_ref):
    @pl.when(pl.program_id(0) == 0)
    def _():
        row = lax.broadcasted_iota(jnp.int32, tri_ref.shape, 0)
        col = lax.broadcasted_iota(jnp.int32, tri_ref.shape, 1)
        tri_ref[...] = jnp.where(col < row, 1.0, 0.0).astype(BF16)
        run_ref[...] = jnp.zeros(run_ref.shape, F32)
        w_ref[...] = wf_ref[...].astype(BF16)

    half = D_MODEL // 2
    x2 = x_ref[...] + _dot(oa_ref[...], w_ref[0:half, :]) + _dot(ob_ref[...], w_ref[half:, :])
    x2_ref[...] = x2
    r = lax.rsqrt(jnp.mean(x2 * x2, axis=-1, keepdims=True) + RMS_EPS)
    h = x2 * r * nw_ref[...]
    hh = h.astype(BF16)
    hl = (h - hh.astype(F32)).astype(BF16)
    prod = _dot(jnp.concatenate([hh, hl], axis=0), wr_ref[...])
    lg = prod[:OUT_TM, :LANES] + prod[OUT_TM:, :LANES] + prod[:OUT_TM, LANES:] + br_ref[...]
    meta, run = _route_tile(lg, run_ref[...], tri_ref[...])
    meta_ref[...] = meta
    run_ref[...] = run
    cnt_ref[...] = run
    bits = pltpu.bitcast(hh.astype(F32), jnp.uint32)
    for c in range(ROW_WORD_CHUNKS):
        lo = bits[:, 2 * c * LANES:(2 * c + 1) * LANES]
        hi = bits[:, (2 * c + 1) * LANES:(2 * c + 2) * LANES]
        hp_ref[pl.ds(c, OUT_TM, stride=ROW_WORD_CHUNKS), :] = (lo >> 16) | (hi & jnp.uint32(0xFFFF0000))


def _out_proj(x2d, out_a, out_b, w_o, norm_w, w_r, b_r):
    t = x2d.shape[0]
    half = D_MODEL // 2
    row = lambda n: pl.BlockSpec((OUT_TM, n), lambda i: (i, 0))
    full = lambda r, c: pl.BlockSpec((r, c), lambda i: (0, 0))
    return pl.pallas_call(
        _outproj_kernel,
        grid=(t // OUT_TM,),
        in_specs=[row(D_MODEL), row(half), row(half), pl.BlockSpec(memory_space=pltpu.VMEM),
                  full(1, D_MODEL), full(D_MODEL, 2 * LANES), full(1, LANES)],
        out_specs=[row(D_MODEL), pl.BlockSpec((OUT_TM * ROW_WORD_CHUNKS, LANES), lambda i: (i, 0)), row(LANES),
                   full(1, LANES)],
        out_shape=[jax.ShapeDtypeStruct((t, D_MODEL), F32),
                   jax.ShapeDtypeStruct((t * ROW_WORD_CHUNKS, LANES), jnp.uint32),
                   jax.ShapeDtypeStruct((t, LANES), F32),
                   jax.ShapeDtypeStruct((1, LANES), F32)],
        scratch_shapes=[pltpu.VMEM((1, LANES), F32), pltpu.VMEM((OUT_TM, OUT_TM), BF16),
                        pltpu.VMEM((D_MODEL, D_MODEL), BF16)],
        compiler_params=pltpu.CompilerParams(
            dimension_semantics=("arbitrary",), vmem_limit_bytes=VMEM_LIMIT),
        name="out_proj",
    )(x2d, out_a, out_b, w_o, norm_w, w_r, b_r)


GATHER_UNROLL = 16
DSP_ROWS = 8 * MOE_TM


def _dispatch_kernel(pos0_ref, pos1_ref, padlo_ref, nt_ref, hp_hbm, xs_ref, hp_ref, src_ref, sem):
    t = pl.program_id(0)
    pos_refs = (pos0_ref, pos1_ref)

    @pl.when(t == 0)
    def _():
        load = pltpu.make_async_copy(hp_hbm, hp_ref, sem)
        load.start()

        def clear(e, _):
            lo = padlo_ref[e]
            for u in range(MOE_TM):
                src_ref[lo + u] = 0
            return 0
        lax.fori_loop(0, N_EXPERTS, clear, 0)

        def place(c, _):
            tok0 = c * (GATHER_UNROLL // EXPERT_TOPK)
            rows = [p[tok0 + u] for u in range(GATHER_UNROLL // EXPERT_TOPK) for p in pos_refs]
            for n, row in enumerate(rows):
                src_ref[row] = tok0 + n // EXPERT_TOPK
            return 0
        lax.fori_loop(0, pos_refs[0].shape[0] * EXPERT_TOPK // GATHER_UNROLL, place, 0)
        load.wait()

    n_valid = jnp.clip(nt_ref[0] * MOE_TM - t * DSP_ROWS, 0, DSP_ROWS)

    def gather(c, _):
        toks = [src_ref[t * DSP_ROWS + c * GATHER_UNROLL + u] for u in range(GATHER_UNROLL)]
        for u in range(GATHER_UNROLL):
            xs_ref[c * GATHER_UNROLL + u] = hp_ref[toks[u]]
        return 0
    lax.fori_loop(0, n_valid // GATHER_UNROLL, gather, 0)

    def zero_tile(c, _):
        xs_ref[pl.ds(pl.multiple_of(c * MOE_TM, MOE_TM), MOE_TM)] = jnp.zeros(
            (MOE_TM,) + xs_ref.shape[1:], xs_ref.dtype)
        return 0
    lax.fori_loop(n_valid // MOE_TM, DSP_ROWS // MOE_TM, zero_tile, 0)


def _dispatch(pos, pad_lo, n_tiles, hp, npad):
    n_tokens = hp.shape[0] // ROW_WORD_CHUNKS
    hp3 = hp.reshape(n_tokens, ROW_WORD_CHUNKS, LANES)
    xs = pl.pallas_call(
        _dispatch_kernel,
        grid_spec=pltpu.PrefetchScalarGridSpec(
            num_scalar_prefetch=4,
            grid=(npad // DSP_ROWS,),
            in_specs=[pl.BlockSpec(memory_space=pl.ANY)],
            out_specs=pl.BlockSpec((DSP_ROWS, ROW_WORD_CHUNKS, LANES), lambda t, p0, p1, lo, n: (t, 0, 0)),
            scratch_shapes=[pltpu.VMEM(hp3.shape, hp3.dtype), pltpu.SMEM((npad + MOE_TM,), jnp.int32),
                            pltpu.SemaphoreType.DMA(())],
        ),
        out_shape=jax.ShapeDtypeStruct((npad, ROW_WORD_CHUNKS, LANES), hp.dtype),
        compiler_params=pltpu.CompilerParams(
            dimension_semantics=("arbitrary",), vmem_limit_bytes=VMEM_LIMIT),
        name="dispatch",
    )(pos[0], pos[1], pad_lo, n_tiles, hp3)
    return xs.reshape(npad * ROW_WORD_CHUNKS, LANES)


W_SLOTS = 2
EXP_TILES = 3


def _experts_kernel(tk_ref, eseq_ref, nt_ref, xs_ref, wg_hbm, wu_hbm, wd_hbm, y_ref,
                    wg_buf, wu_buf, wd_buf, wgb_ref, wub_ref, wdb_ref, x_ref, sem):
    n_used = nt_ref[1]
    hbm_bufs = ((wg_hbm, wg_buf), (wu_hbm, wu_buf), (wd_hbm, wd_buf))

    def weight_copies(k, slot):
        e = eseq_ref[k]
        return [pltpu.make_async_copy(hbm.at[e], buf.at[slot], sem.at[slot, n])
                for n, (hbm, buf) in enumerate(hbm_bufs)]

    for sub in range(EXP_TILES):
        _expert_tile(pl.program_id(0) * EXP_TILES + sub, sub, tk_ref, nt_ref, n_used, weight_copies,
                     xs_ref, y_ref, wg_buf, wu_buf, wd_buf, wgb_ref, wub_ref, wdb_ref, x_ref)


def _expert_tile(t, sub, tk_ref, nt_ref, n_used, weight_copies,
                 xs_ref, y_ref, wg_buf, wu_buf, wd_buf, wgb_ref, wub_ref, wdb_ref, x_ref):
    xs_row0 = sub * MOE_TM * ROW_WORD_CHUNKS
    y_rows = pl.ds(sub * MOE_TM, MOE_TM)

    @pl.when(t < nt_ref[0])
    def _():
        k = tk_ref[t]

        @pl.when((t == 0) | (k != tk_ref[jnp.maximum(t - 1, 0)]))
        def _():
            @pl.when(k == 0)
            def _():
                for s in range(W_SLOTS):
                    @pl.when(s < n_used)
                    def _():
                        for c in weight_copies(s, s):
                            c.start()

            slot = k % W_SLOTS
            for c in weight_copies(k, slot):
                c.wait()
            wgb_ref[...] = wg_buf[slot].astype(BF16)
            wub_ref[...] = wu_buf[slot].astype(BF16)
            wdb_ref[...] = wd_buf[slot].astype(BF16)

            @pl.when(k + W_SLOTS < n_used)
            def _():
                for c in weight_copies(k + W_SLOTS, slot):
                    c.start()

        for c in range(ROW_WORD_CHUNKS):
            u32 = xs_ref[pl.ds(xs_row0 + c, MOE_TM, stride=ROW_WORD_CHUNKS), :]
            x_ref[:, 2 * c * LANES:(2 * c + 1) * LANES] = pltpu.bitcast(u32 << 16, F32).astype(BF16)
            x_ref[:, (2 * c + 1) * LANES:(2 * c + 2) * LANES] = (
                pltpu.bitcast(u32 & jnp.uint32(0xFFFF0000), F32).astype(BF16))
        x = x_ref[...]
        a = _dot(x, wgb_ref[...])
        u = _dot(x, wub_ref[...])
        act = (a * jax.nn.sigmoid(a)) * u
        y_ref[y_rows, :] = _dot(act.astype(BF16), wdb_ref[...])

    @pl.when(t >= nt_ref[0])
    def _():
        y_ref[y_rows, :] = jnp.zeros((MOE_TM, D_MODEL), y_ref.dtype)


def _experts(tile_k, expert_seq, n_used, xs, w_gate, w_up, w_down):
    npad = xs.shape[0] // ROW_WORD_CHUNKS
    rows = EXP_TILES * MOE_TM
    assert npad % rows == 0 and npad % DSP_ROWS == 0
    last = lambda t, nt: jnp.minimum(t, jnp.maximum(nt[0] - 1, 0) // EXP_TILES)
    hbm = pl.BlockSpec(memory_space=pl.ANY)
    return pl.pallas_call(
        _experts_kernel,
        grid_spec=pltpu.PrefetchScalarGridSpec(
            num_scalar_prefetch=3,
            grid=(npad // rows,),
            in_specs=[
                pl.BlockSpec((rows * ROW_WORD_CHUNKS, LANES), lambda t, tk, es, nt: (last(t, nt), 0)),
                hbm, hbm, hbm,
            ],
            out_specs=pl.BlockSpec((rows, D_MODEL), lambda t, tk, es, nt: (t, 0)),
            scratch_shapes=[
                pltpu.VMEM((W_SLOTS, D_MODEL, D_EXPERT), F32),
                pltpu.VMEM((W_SLOTS, D_MODEL, D_EXPERT), F32),
                pltpu.VMEM((W_SLOTS, D_EXPERT, D_MODEL), F32),
                pltpu.VMEM((D_MODEL, D_EXPERT), BF16),
                pltpu.VMEM((D_MODEL, D_EXPERT), BF16),
                pltpu.VMEM((D_EXPERT, D_MODEL), BF16),
                pltpu.VMEM((MOE_TM, D_MODEL), BF16),
                pltpu.SemaphoreType.DMA((W_SLOTS, 3)),
            ],
        ),
        out_shape=jax.ShapeDtypeStruct((npad, D_MODEL), F32),
        compiler_params=pltpu.CompilerParams(
            dimension_semantics=("arbitrary",), vmem_limit_bytes=VMEM_LIMIT),
        name="experts",
    )(tile_k, expert_seq, n_used, xs, w_gate, w_up, w_down)


CMB_UNROLL = 8
CMB_SLOTS = 2


def _combine_kernel(pos0_ref, pos1_ref, x2_ref, meta_ref, nw_ref, y_ref, o_ref, buf_ref, sem):
    i = pl.program_id(0)
    pos_refs = (pos0_ref, pos1_ref)

    def row_copy(p, slot, k, r):
        return pltpu.make_async_copy(y_ref.at[pl.ds(p, 1), :], buf_ref.at[slot, k, pl.ds(r, 1), :],
                                     sem.at[slot])

    def issue(tile, slot):
        for c in range(CMB_TT // CMB_UNROLL):
            tok0 = tile * CMB_TT + c * CMB_UNROLL
            rows = [[p[tok0 + u] for u in range(CMB_UNROLL)] for p in pos_refs]
            for u in range(CMB_UNROLL):
                for k in range(EXPERT_TOPK):
                    row_copy(rows[k][u], slot, k, c * CMB_UNROLL + u).start(priority=k % 2)

    @pl.when(i == 0)
    def _():
        issue(0, 0)

    @pl.when(i + 1 < pl.num_programs(0))
    def _():
        issue(i + 1, (i + 1) % CMB_SLOTS)

    slot = i % CMB_SLOTS
    for k in range(EXPERT_TOPK):
        pltpu.make_async_copy(y_ref.at[pl.ds(0, CMB_TT), :], buf_ref.at[slot, k], sem.at[slot]).wait()

    meta = meta_ref[...]
    x3 = (x2_ref[...] + meta[:, META_W1:META_W1 + 1] * buf_ref[slot, 0]
          + meta[:, META_W2:META_W2 + 1] * buf_ref[slot, 1])
    r = lax.rsqrt(jnp.mean(x3 * x3, axis=-1, keepdims=True) + RMS_EPS)
    o_ref[...] = x3 * r * nw_ref[...]


def _combine(pos, x2, meta, norm_w, y):
    t = x2.shape[0]
    return pl.pallas_call(
        _combine_kernel,
        grid_spec=pltpu.PrefetchScalarGridSpec(
            num_scalar_prefetch=2,
            grid=(t // CMB_TT,),
            in_specs=[
                pl.BlockSpec((CMB_TT, D_MODEL), lambda i, p0, p1: (i, 0)),
                pl.BlockSpec((CMB_TT, LANES), lambda i, p0, p1: (i, 0)),
                pl.BlockSpec((1, D_MODEL), lambda i, p0, p1: (0, 0)),
                pl.BlockSpec(memory_space=pl.ANY),
            ],
            out_specs=pl.BlockSpec((CMB_TT, D_MODEL), lambda i, p0, p1: (i, 0)),
            scratch_shapes=[
                pltpu.VMEM((CMB_SLOTS, EXPERT_TOPK, CMB_TT, D_MODEL), F32),
                pltpu.SemaphoreType.DMA((CMB_SLOTS,)),
            ],
        ),
        out_shape=jax.ShapeDtypeStruct((t, D_MODEL), F32),
        compiler_params=pltpu.CompilerParams(
            dimension_semantics=("arbitrary",), vmem_limit_bytes=VMEM_LIMIT),
        name="combine",
    )(pos[0], pos[1], x2, meta, norm_w, y)


def _layout(meta, counts, n_tokens):
    npad = n_tokens * EXPERT_TOPK + N_EXPERTS * MOE_TM
    counts = counts[0, :N_EXPERTS].astype(jnp.int32)
    padded = ((counts + MOE_TM - 1) // MOE_TM) * MOE_TM
    ends = jnp.cumsum(padded)
    starts = ends - padded
    ids = jnp.arange(N_EXPERTS, dtype=jnp.int32)
    pos = []
    for lane_x, lane_rank in ((META_X1, META_RANK1), (META_X2, META_RANK2)):
        hit = meta[:, lane_x].astype(jnp.int32)[:, None] == ids[None, :]
        pos.append(jnp.sum(jnp.where(hit, starts[None, :], 0), axis=-1) + meta[:, lane_rank].astype(jnp.int32))
    tile_start = jnp.arange(npad // MOE_TM, dtype=jnp.int32) * MOE_TM
    tile_expert = jnp.minimum(jnp.sum(ends[None, :] <= tile_start[:, None], axis=1), N_EXPERTS - 1)
    used = counts > 0
    k_of_expert = jnp.cumsum(used.astype(jnp.int32)) - 1
    expert_seq = jnp.sum(jnp.where(used[None, :] & (k_of_expert[None, :] == ids[:, None]), ids[None, :], 0), axis=1)
    tile_k = jnp.sum(jnp.where(tile_expert[:, None] == ids[None, :], k_of_expert[None, :], 0), axis=1)
    n_tiles = (ends[-1] // MOE_TM).astype(jnp.int32)
    n_used = jnp.stack([n_tiles, jnp.sum(used).astype(jnp.int32)])
    return (pos, starts + counts, tile_k.astype(jnp.int32), expert_seq.astype(jnp.int32), n_tiles.reshape(1),
            n_used, npad)


def kernel(x, norm_mix_w, w_in, lambda_q1, lambda_k1, lambda_q2, lambda_k2, diff_subln_w, w_out, norm_ffn_w,
           w_router_group, b_router_group, w_router_expert, b_router_expert, w_gate, w_up, w_down, norm_final_w):
    batch, seq, _ = x.shape
    assert seq % ATT_TQ == 0 and seq // MOBA_BLOCK + 8 <= LANES
    assert w_in.shape[0] == 1, "single-layer block"
    n_tokens = batch * seq
    x2d = x.reshape(n_tokens, D_MODEL)

    col = np.arange(IN_WIDTH)
    is_q = (col < MOBA_HEADS * HEAD_DIM) | ((col >= 3 * MOBA_HEADS * HEAD_DIM) & (col < 4 * MOBA_HEADS * HEAD_DIM))
    col_scale = jnp.asarray(np.where(is_q, QK_SCALE * LOG2E, 1.0).astype(np.float32))[None, :]

    proj = _in_proj(x2d, norm_mix_w[0][None, :], w_in[0], col_scale)
    out_a = _moba(proj, jnp.asarray(_alibi_slope_pieces(MOBA_HEADS)), batch, seq)
    out_b = _diff(proj, jnp.asarray(_alibi_slope_pieces(DIFF_HEADS)), lambda_q1, lambda_k1, lambda_q2, lambda_k2,
                  diff_subln_w, batch, seq)

    w_r = jnp.concatenate([w_router_group[0], w_router_expert[0]], axis=1)
    w_r = jnp.pad(w_r, ((0, 0), (0, LANES - w_r.shape[1])))
    wr_hi = w_r.astype(BF16)
    wr_lo = (w_r - wr_hi.astype(F32)).astype(BF16)
    b_r = jnp.pad(jnp.concatenate([b_router_group[0], b_router_expert[0]]), (0, LANES - N_GROUPS - N_EXPERTS))[None, :]
    x2, hp, meta, counts = _out_proj(x2d, out_a, out_b, w_out[0], norm_ffn_w[0][None, :],
                                     jnp.concatenate([wr_hi, wr_lo], axis=1), b_r)

    pos, pad_lo, tile_k, expert_seq, n_tiles, n_used, npad = _layout(meta, counts, n_tokens)
    xs = _dispatch(pos, pad_lo, n_tiles, hp, npad)
    y = _experts(tile_k, expert_seq, n_used, xs, w_gate[0], w_up[0], w_down[0])
    out = _combine(pos, x2, meta, norm_final_w[None, :], y)
    return out.reshape(batch, seq, D_MODEL)
```

```python
import functools

import ml_dtypes
import numpy as np
import jax
import jax.numpy as jnp
from jax import lax
from jax.experimental import pallas as pl
from jax.experimental.pallas import tpu as pltpu

F32 = jnp.float32
BF16 = jnp.bfloat16

D_MODEL = 2048
HEAD_DIM = 128
MOBA_HEADS = 8
MOBA_BLOCK = 256
MOBA_TOPK = 3
DIFF_HEADS = 4
DIFF_V_DIM = 256
IN_WIDTH = 6144
N_GROUPS = 4
EXPERTS_PER_GROUP = 8
N_EXPERTS = 32
EXPERT_TOPK = 2
D_EXPERT = 512
RMS_EPS = 1e-6
ALIBI_MAX_BIAS = 8.0
LAMBDA_INIT = 0.8 - 0.6 * float(np.exp(-0.3 * 0))

LANES = 128
QK_SCALE = HEAD_DIM ** -0.5
LOG2E = float(np.log2(np.e))
NEG_BIG = -1e30
M_INIT = -1e38
POS_SPLIT = 64
SLOPE_PIECES = 3
ROW_WORD_CHUNKS = D_MODEL // (2 * LANES)
META_X1, META_X2, META_W1, META_W2, META_RANK1, META_RANK2 = range(6)

IN_TM, IN_TN = 1024, 1024
ATT_TQ = 1024
ATT_TK = 512
KV_UNROLL = 2
HEADS_PER_STEP = 4
DIFF_HEADS_PER_STEP = 1
OUT_TM = 512
MOE_TM = 256
CMB_TT = 256
VMEM_LIMIT = 56 * 1024 * 1024


def _alibi_slope_pieces(n):
    rem = np.exp2(-ALIBI_MAX_BIAS * (np.arange(n, dtype=np.float64) + 1.0) / n) * LOG2E
    pieces = []
    for _ in range(SLOPE_PIECES):
        p = rem.astype(ml_dtypes.bfloat16).astype(np.float64)
        pieces.append(p)
        rem = rem - p
    return np.stack(pieces, axis=1).reshape(-1).astype(np.float32)


def _dot_nt(a, b):
    return lax.dot_general(a, b, (((1,), (1,)), ((), ())), preferred_element_type=F32)


def _dot(a, b):
    return jnp.dot(a, b, preferred_element_type=F32)


def _inproj_kernel(x_ref, nw_ref, cs_ref, w_ref, o_ref, h_ref):
    @pl.when(pl.program_id(1) == 0)
    def _():
        x = x_ref[...]
        r = lax.rsqrt(jnp.mean(x * x, axis=-1, keepdims=True) + RMS_EPS)
        h_ref[...] = (x * r * nw_ref[...]).astype(BF16)

    acc = _dot(h_ref[...], w_ref[...].astype(BF16))
    o_ref[...] = (acc * cs_ref[...]).astype(BF16)


def _in_proj(x2d, norm_w, w_in, col_scale):
    t = x2d.shape[0]
    return pl.pallas_call(
        _inproj_kernel,
        grid=(t // IN_TM, IN_WIDTH // IN_TN),
        in_specs=[
            pl.BlockSpec((IN_TM, D_MODEL), lambda i, j: (i, 0)),
            pl.BlockSpec((1, D_MODEL), lambda i, j: (0, 0)),
            pl.BlockSpec((1, IN_TN), lambda i, j: (0, j)),
            pl.BlockSpec((D_MODEL, IN_TN), lambda i, j: (0, j)),
        ],
        out_specs=pl.BlockSpec((IN_TM, IN_TN), lambda i, j: (i, j)),
        out_shape=jax.ShapeDtypeStruct((t, IN_WIDTH), BF16),
        scratch_shapes=[pltpu.VMEM((IN_TM, D_MODEL), BF16)],
        compiler_params=pltpu.CompilerParams(
            dimension_semantics=("arbitrary", "arbitrary"), vmem_limit_bytes=VMEM_LIMIT),
        name="in_proj",
    )(x2d, norm_w, col_scale, w_in)


def _key_position_features(seq, first_lane):
    pos = lax.broadcasted_iota(jnp.int32, (seq, LANES), 0)
    lane = lax.broadcasted_iota(jnp.int32, (seq, LANES), 1)
    hi = (lane >= first_lane) & (lane < first_lane + SLOPE_PIECES)
    lo = (lane >= first_lane + SLOPE_PIECES) & (lane < first_lane + 2 * SLOPE_PIECES)
    ext = jnp.where(hi, (pos // POS_SPLIT).astype(F32), 0.0)
    return jnp.where(lo, (pos % POS_SPLIT).astype(F32), ext), pos, lane


def _slope_feature_rows(feat_ref, head, width):
    row = lax.broadcasted_iota(jnp.int32, (8, width), 0)
    out = jnp.zeros((8, width), F32)
    for k in range(SLOPE_PIECES):
        c = feat_ref[head * SLOPE_PIECES + k]
        out = jnp.where(row == k, POS_SPLIT * c, out)
        out = jnp.where(row == SLOPE_PIECES + k, c, out)
    return out


def _causal_tile(s):
    row = lax.broadcasted_iota(jnp.int32, s.shape, 0)
    col = lax.broadcasted_iota(jnp.int32, s.shape, 1)
    return jnp.where(col <= row, s, -jnp.inf)


def _online_step(s, m):
    m_new = jnp.maximum(m, jnp.max(s, axis=-1, keepdims=True))
    return m_new, jnp.exp2(m - m_new), jnp.exp2(s - m_new)


def _past_tiles_loop(i, body, init):
    def trip(t, carry):
        for u in range(KV_UNROLL):
            carry = body(t * KV_UNROLL + u, carry)
        return carry

    return lax.fori_loop(0, i * (ATT_TQ // (ATT_TK * KV_UNROLL)), trip, init)


def _rows_from(state, start):
    return state if start == 0 else tuple(a[start:] for a in state)


def _rows_replace(state, start, tail):
    if start == 0:
        return tuple(tail)
    return tuple(jnp.concatenate([a[:start], b], axis=0) for a, b in zip(state, tail))


def _moba_kernel(feat_ref, q_ref, k_ref, v_ref, o_ref, kaug_ref, vaug_ref, kmh_ref, kml_ref, qaug_ref, *, seq):
    nb = seq // MOBA_BLOCK
    hp = pl.program_id(1)
    i = pl.program_id(2)

    @pl.when(i == 0)
    def _():
        ext, pos, lane = _key_position_features(seq, nb)
        ext = jnp.where(lane == pos // MOBA_BLOCK, 1.0, ext).astype(BF16)
        ones_col = jnp.where(lane == 0, 1.0, 0.0).astype(BF16)
        for g in range(HEADS_PER_STEP):
            k = k_ref[:, g * HEAD_DIM:(g + 1) * HEAD_DIM]
            kaug_ref[g, :, 0:HEAD_DIM] = k
            kaug_ref[g, :, HEAD_DIM:2 * HEAD_DIM] = ext
            vaug_ref[g, :, 0:HEAD_DIM] = v_ref[:, g * HEAD_DIM:(g + 1) * HEAD_DIM]
            vaug_ref[g, :, HEAD_DIM:2 * HEAD_DIM] = ones_col
            km = jnp.mean(k.astype(F32).reshape(nb, MOBA_BLOCK, HEAD_DIM), axis=1)
            kmh = km.astype(BF16)
            kmh_ref[g] = jnp.zeros((LANES, HEAD_DIM), BF16)
            kml_ref[g] = jnp.zeros((LANES, HEAD_DIM), BF16)
            kmh_ref[g, 0:nb, :] = kmh
            kml_ref[g, 0:nb, :] = (km - kmh.astype(F32)).astype(BF16)

    blk = lax.broadcasted_iota(jnp.int32, (nb, ATT_TQ), 0)
    col = lax.broadcasted_iota(jnp.int32, (nb, ATT_TQ), 1)
    own = i * (ATT_TQ // MOBA_BLOCK) + col // MOBA_BLOCK
    blk_f = blk.astype(F32)
    for g in range(HEADS_PER_STEP):
        q = q_ref[:, g * HEAD_DIM:(g + 1) * HEAD_DIM]
        gate = (_dot_nt(kmh_ref[g], q) + _dot_nt(kml_ref[g], q))[0:nb, :]
        gate = jnp.where(blk < own, gate, -jnp.inf)
        sel = jnp.zeros((nb, ATT_TQ), F32)
        for _ in range(MOBA_TOPK):
            mx = jnp.max(gate, axis=0, keepdims=True)
            idx = jnp.min(jnp.where(gate == mx, blk_f, float(LANES)), axis=0, keepdims=True)
            pick = (blk_f == idx) & (mx > -jnp.inf)
            sel = jnp.where(pick, 1.0, sel)
            gate = jnp.where(pick, -jnp.inf, gate)
        bias = jnp.where((sel > 0.0) | (blk == own), 0.0, NEG_BIG)
        feat = jnp.concatenate(
            [bias, _slope_feature_rows(feat_ref, hp * HEADS_PER_STEP + g, ATT_TQ),
             jnp.zeros((LANES - nb - 8, ATT_TQ), F32)], axis=0)
        qaug_ref[g, :, 0:HEAD_DIM] = q
        qaug_ref[g, :, HEAD_DIM:2 * HEAD_DIM] = feat.T.astype(BF16)

    def tile_step(g, r, state, row0=0, causal=False):
        m, acc = state
        s = _dot_nt(qaug_ref[g, row0:, :], kaug_ref[g, pl.ds(r, ATT_TK), :])
        if causal:
            s = _causal_tile(s)
        m, alpha, p = _online_step(s, m)
        return m, alpha * acc + _dot(p.astype(BF16), vaug_ref[g, pl.ds(r, ATT_TK), :])

    def body(j, states):
        r = pl.multiple_of(j * ATT_TK, ATT_TK)
        return tuple(tile_step(g, r, states[g]) for g in range(HEADS_PER_STEP))

    init = tuple((jnp.full((ATT_TQ, 1), M_INIT, F32), jnp.zeros((ATT_TQ, 2 * HEAD_DIM), F32))
                 for _ in range(HEADS_PER_STEP))
    states = list(_past_tiles_loop(i, body, init))
    for d in range(ATT_TQ // ATT_TK):
        r = pl.multiple_of(i * ATT_TQ + d * ATT_TK, ATT_TK)
        for g in range(HEADS_PER_STEP):
            tail = tile_step(g, r, _rows_from(states[g], d * ATT_TK), row0=d * ATT_TK, causal=True)
            states[g] = _rows_replace(states[g], d * ATT_TK, tail)
    for g in range(HEADS_PER_STEP):
        acc = states[g][1]
        o_ref[:, g * HEAD_DIM:(g + 1) * HEAD_DIM] = (
            acc[:, 0:HEAD_DIM] / acc[:, HEAD_DIM:HEAD_DIM + 1]).astype(BF16)


def _moba(proj, feats, batch, seq):
    nq = seq // ATT_TQ
    width = HEADS_PER_STEP * HEAD_DIM
    q_col, k_col, v_col = 0, MOBA_HEADS // HEADS_PER_STEP, 2 * MOBA_HEADS // HEADS_PER_STEP
    return pl.pallas_call(
        functools.partial(_moba_kernel, seq=seq),
        grid_spec=pltpu.PrefetchScalarGridSpec(
            num_scalar_prefetch=1,
            grid=(batch, MOBA_HEADS // HEADS_PER_STEP, nq),
            in_specs=[
                pl.BlockSpec((ATT_TQ, width), lambda b, h, i, s: (b * nq + i, q_col + h)),
                pl.BlockSpec((seq, width), lambda b, h, i, s: (b, k_col + h)),
                pl.BlockSpec((seq, width), lambda b, h, i, s: (b, v_col + h)),
            ],
            out_specs=pl.BlockSpec((ATT_TQ, width), lambda b, h, i, s: (b * nq + i, h)),
            scratch_shapes=[
                pltpu.VMEM((HEADS_PER_STEP, seq, 2 * HEAD_DIM), BF16),
                pltpu.VMEM((HEADS_PER_STEP, seq, 2 * HEAD_DIM), BF16),
                pltpu.VMEM((HEADS_PER_STEP, LANES, HEAD_DIM), BF16),
                pltpu.VMEM((HEADS_PER_STEP, LANES, HEAD_DIM), BF16),
                pltpu.VMEM((HEADS_PER_STEP, ATT_TQ, 2 * HEAD_DIM), BF16),
            ],
        ),
        out_shape=jax.ShapeDtypeStruct((batch * seq, MOBA_HEADS * HEAD_DIM), BF16),
        compiler_params=pltpu.CompilerParams(
            dimension_semantics=("arbitrary", "arbitrary", "arbitrary"), vmem_limit_bytes=VMEM_LIMIT),
        name="moba",
    )(feats, proj, proj, proj)


def _diff_kernel(feat_ref, lq1_ref, lk1_ref, lq2_ref, lk2_ref, sw_ref,
                 q_ref, k_ref, v_ref, o_ref, kaug_ref, qaug_ref, *, seq):
    hg = pl.program_id(1)
    i = pl.program_id(2)
    n_chains = 2 * DIFF_HEADS_PER_STEP

    @pl.when(i == 0)
    def _():
        ext = _key_position_features(seq, 0)[0].astype(BF16)
        for n in range(n_chains):
            kaug_ref[n, :, 0:HEAD_DIM] = k_ref[:, n * HEAD_DIM:(n + 1) * HEAD_DIM]
            kaug_ref[n, :, HEAD_DIM:2 * HEAD_DIM] = ext

    for n in range(n_chains):
        if n % 2 == 0:
            head = hg * DIFF_HEADS_PER_STEP + n // 2
            feat = jnp.concatenate([_slope_feature_rows(feat_ref, head, ATT_TQ),
                                    jnp.zeros((LANES - 8, ATT_TQ), F32)], axis=0).T.astype(BF16)
        qaug_ref[n, :, 0:HEAD_DIM] = q_ref[:, n * HEAD_DIM:(n + 1) * HEAD_DIM]
        qaug_ref[n, :, HEAD_DIM:2 * HEAD_DIM] = feat

    def tile_step(n, r, state, row0=0, causal=False):
        m, l, acc = state
        s = _dot_nt(qaug_ref[n, row0:, :], kaug_ref[n, pl.ds(r, ATT_TK), :])
        if causal:
            s = _causal_tile(s)
        m, alpha, p = _online_step(s, m)
        v = v_ref[pl.ds(r, ATT_TK), (n // 2) * DIFF_V_DIM:(n // 2 + 1) * DIFF_V_DIM]
        p = p.astype(BF16)
        return (m, alpha * l + jnp.sum(p.astype(F32), axis=-1, keepdims=True), alpha * acc + _dot(p, v))

    def body(j, states):
        r = pl.multiple_of(j * ATT_TK, ATT_TK)
        return tuple(tile_step(n, r, states[n]) for n in range(n_chains))

    init = tuple((jnp.full((ATT_TQ, 1), M_INIT, F32), jnp.zeros((ATT_TQ, 1), F32),
                  jnp.zeros((ATT_TQ, DIFF_V_DIM), F32)) for _ in range(n_chains))
    states = list(_past_tiles_loop(i, body, init))
    for d in range(ATT_TQ // ATT_TK):
        r = pl.multiple_of(i * ATT_TQ + d * ATT_TK, ATT_TK)
        for n in range(n_chains):
            tail = tile_step(n, r, _rows_from(states[n], d * ATT_TK), row0=d * ATT_TK, causal=True)
            states[n] = _rows_replace(states[n], d * ATT_TK, tail)

    lam = (jnp.exp(jnp.sum(lq1_ref[...] * lk1_ref[...], axis=-1, keepdims=True))
           - jnp.exp(jnp.sum(lq2_ref[...] * lk2_ref[...], axis=-1, keepdims=True))
           + LAMBDA_INIT)
    for hh in range(DIFF_HEADS_PER_STEP):
        (_, l1, acc1), (_, l2, acc2) = states[2 * hh], states[2 * hh + 1]
        o = acc1 / l1 - lam * (acc2 / l2)
        y = o * lax.rsqrt(jnp.mean(o * o, axis=-1, keepdims=True) + RMS_EPS)
        o_ref[:, hh * DIFF_V_DIM:(hh + 1) * DIFF_V_DIM] = ((y * sw_ref[...]) * (1.0 - LAMBDA_INIT)).astype(BF16)


def _diff(proj, feats, lq1, lk1, lq2, lk2, subln_w, batch, seq):
    nq = seq // ATT_TQ
    width = DIFF_HEADS_PER_STEP * 2 * HEAD_DIM
    groups = DIFF_HEADS // DIFF_HEADS_PER_STEP
    q_col = 3 * MOBA_HEADS * HEAD_DIM // width
    k_col = q_col + groups
    v_col = k_col + groups
    vec = lambda n: pl.BlockSpec((1, n), lambda b, h, i, s: (0, 0))
    return pl.pallas_call(
        functools.partial(_diff_kernel, seq=seq),
        grid_spec=pltpu.PrefetchScalarGridSpec(
            num_scalar_prefetch=1,
            grid=(batch, groups, nq),
            in_specs=[
                vec(HEAD_DIM), vec(HEAD_DIM), vec(HEAD_DIM), vec(HEAD_DIM), vec(DIFF_V_DIM),
                pl.BlockSpec((ATT_TQ, width), lambda b, h, i, s: (b * nq + i, q_col + h)),
                pl.BlockSpec((seq, width), lambda b, h, i, s: (b, k_col + h)),
                pl.BlockSpec((seq, width), lambda b, h, i, s: (b, v_col + h)),
            ],
            out_specs=pl.BlockSpec((ATT_TQ, width), lambda b, h, i, s: (b * nq + i, h)),
            scratch_shapes=[
                pltpu.VMEM((2 * DIFF_HEADS_PER_STEP, seq, 2 * HEAD_DIM), BF16),
                pltpu.VMEM((2 * DIFF_HEADS_PER_STEP, ATT_TQ, 2 * HEAD_DIM), BF16),
            ],
        ),
        out_shape=jax.ShapeDtypeStruct((batch * seq, DIFF_HEADS * DIFF_V_DIM), BF16),
        compiler_params=pltpu.CompilerParams(
            dimension_semantics=("arbitrary", "arbitrary", "arbitrary"), vmem_limit_bytes=VMEM_LIMIT),
        name="diff_attn",
    )(feats, lq1, lk1, lq2, lk2, subln_w, proj, proj, proj)


def _route_tile(lg, run, tri):
    lane = lax.broadcasted_iota(jnp.int32, lg.shape, 1).astype(F32)
    rmax = lambda v: jnp.max(v, axis=-1, keepdims=True)
    first = lambda v, m: jnp.min(jnp.where(v == m, lane, float(LANES)), axis=-1, keepdims=True)

    gl = jnp.where(lane < N_GROUPS, lg, -jnp.inf)
    gmax = rmax(gl)
    group = first(gl, gmax)
    p_group = 1.0 / jnp.sum(jnp.exp(gl - gmax), axis=-1, keepdims=True)

    lo = N_GROUPS + EXPERTS_PER_GROUP * group
    el = jnp.where((lane >= lo) & (lane < lo + EXPERTS_PER_GROUP), lg, -jnp.inf)
    e1 = rmax(el)
    lane1 = first(el, e1)
    el = jnp.where(lane == lane1, -jnp.inf, el)
    e2 = rmax(el)
    lane2 = first(el, e2)
    t2 = jnp.exp(e2 - e1)
    w1 = p_group / (1.0 + t2)
    w2 = p_group * t2 / (1.0 + t2)
    x1 = lane1 - N_GROUPS
    x2 = lane2 - N_GROUPS

    hit1 = lane == x1
    hit2 = lane == x2
    onehot = jnp.where(hit1 | hit2, 1.0, 0.0)
    before = _dot(tri, onehot.astype(BF16)) + run
    rank1 = jnp.sum(jnp.where(hit1, before, 0.0), axis=-1, keepdims=True)
    rank2 = jnp.sum(jnp.where(hit2, before, 0.0), axis=-1, keepdims=True)

    meta = jnp.zeros(lg.shape, F32)
    for k, v in ((META_X1, x1), (META_X2, x2), (META_W1, w1), (META_W2, w2),
                 (META_RANK1, rank1), (META_RANK2, rank2)):
        meta = jnp.where(lane == k, v, meta)
    return meta, run + jnp.sum(onehot, axis=0, keepdims=True)


def _outproj_kernel(x_ref, oa_ref, ob_ref, wf_ref, nw_ref, wr_ref, br_ref,
                    x2_ref, hp_ref, meta_ref, cnt_ref, run_ref, tri_ref, w_ref):
    @pl.when(pl.program_id(0) == 0)
    def _():
        row = lax.broadcasted_iota(jnp.int32, tri_ref.shape, 0)
        col = lax.broadcasted_iota(jnp.int32, tri_ref.shape, 1)
        tri_ref[...] = jnp.where(col < row, 1.0, 0.0).astype(BF16)
        run_ref[...] = jnp.zeros(run_ref.shape, F32)
        w_ref[...] = wf_ref[...].astype(BF16)

    half = D_MODEL // 2
    x2 = x_ref[...] + _dot(oa_ref[...], w_ref[0:half, :]) + _dot(ob_ref[...], w_ref[half:, :])
    x2_ref[...] = x2
    r = lax.rsqrt(jnp.mean(x2 * x2, axis=-1, keepdims=True) + RMS_EPS)
    h = x2 * r * nw_ref[...]
    hh = h.astype(BF16)
    hl = (h - hh.astype(F32)).astype(BF16)
    prod = _dot(jnp.concatenate([hh, hl], axis=0), wr_ref[...])
    lg = prod[:OUT_TM, :LANES] + prod[OUT_TM:, :LANES] + prod[:OUT_TM, LANES:] + br_ref[...]
    meta, run = _route_tile(lg, run_ref[...], tri_ref[...])
    meta_ref[...] = meta
    run_ref[...] = run
    cnt_ref[...] = run
    bits = pltpu.bitcast(hh.astype(F32), jnp.uint32)
    for c in range(ROW_WORD_CHUNKS):
        lo = bits[:, 2 * c * LANES:(2 * c + 1) * LANES]
        hi = bits[:, (2 * c + 1) * LANES:(2 * c + 2) * LANES]
        hp_ref[pl.ds(c, OUT_TM, stride=ROW_WORD_CHUNKS), :] = (lo >> 16) | (hi & jnp.uint32(0xFFFF0000))


def _out_proj(x2d, out_a, out_b, w_o, norm_w, w_r, b_r):
    t = x2d.shape[0]
    half = D_MODEL // 2
    row = lambda n: pl.BlockSpec((OUT_TM, n), lambda i: (i, 0))
    full = lambda r, c: pl.BlockSpec((r, c), lambda i: (0, 0))
    return pl.pallas_call(
        _outproj_kernel,
        grid=(t // OUT_TM,),
        in_specs=[row(D_MODEL), row(half), row(half), pl.BlockSpec(memory_space=pltpu.VMEM),
                  full(1, D_MODEL), full(D_MODEL, 2 * LANES), full(1, LANES)],
        out_specs=[row(D_MODEL), pl.BlockSpec((OUT_TM * ROW_WORD_CHUNKS, LANES), lambda i: (i, 0)), row(LANES),
                   full(1, LANES)],
        out_shape=[jax.ShapeDtypeStruct((t, D_MODEL), F32),
                   jax.ShapeDtypeStruct((t * ROW_WORD_CHUNKS, LANES), jnp.uint32),
                   jax.ShapeDtypeStruct((t, LANES), F32),
                   jax.ShapeDtypeStruct((1, LANES), F32)],
        scratch_shapes=[pltpu.VMEM((1, LANES), F32), pltpu.VMEM((OUT_TM, OUT_TM), BF16),
                        pltpu.VMEM((D_MODEL, D_MODEL), BF16)],
        compiler_params=pltpu.CompilerParams(
            dimension_semantics=("arbitrary",), vmem_limit_bytes=VMEM_LIMIT),
        name="out_proj",
    )(x2d, out_a, out_b, w_o, norm_w, w_r, b_r)


GATHER_UNROLL = 16
DSP_ROWS = 8 * MOE_TM


def _dispatch_kernel(pos0_ref, pos1_ref, padlo_ref, nt_ref, hp_hbm, xs_ref, hp_ref, src_ref, sem):
    t = pl.program_id(0)
    pos_refs = (pos0_ref, pos1_ref)

    @pl.when(t == 0)
    def _():
        load = pltpu.make_async_copy(hp_hbm, hp_ref, sem)
        load.start()

        def clear(e, _):
            lo = padlo_ref[e]
            for u in range(MOE_TM):
                src_ref[lo + u] = 0
            return 0
        lax.fori_loop(0, N_EXPERTS, clear, 0)

        def place(c, _):
            tok0 = c * (GATHER_UNROLL // EXPERT_TOPK)
            rows = [p[tok0 + u] for u in range(GATHER_UNROLL // EXPERT_TOPK) for p in pos_refs]
            for n, row in enumerate(rows):
                src_ref[row] = tok0 + n // EXPERT_TOPK
            return 0
        lax.fori_loop(0, pos_refs[0].shape[0] * EXPERT_TOPK // GATHER_UNROLL, place, 0)
        load.wait()

    n_valid = jnp.clip(nt_ref[0] * MOE_TM - t * DSP_ROWS, 0, DSP_ROWS)

    def gather(c, _):
        toks = [src_ref[t * DSP_ROWS + c * GATHER_UNROLL + u] for u in range(GATHER_UNROLL)]
        for u in range(GATHER_UNROLL):
            xs_ref[c * GATHER_UNROLL + u] = hp_ref[toks[u]]
        return 0
    lax.fori_loop(0, n_valid // GATHER_UNROLL, gather, 0)

    def zero_tile(c, _):
        xs_ref[pl.ds(pl.multiple_of(c * MOE_TM, MOE_TM), MOE_TM)] = jnp.zeros(
            (MOE_TM,) + xs_ref.shape[1:], xs_ref.dtype)
        return 0
    lax.fori_loop(n_valid // MOE_TM, DSP_ROWS // MOE_TM, zero_tile, 0)


def _dispatch(pos, pad_lo, n_tiles, hp, npad):
    n_tokens = hp.shape[0] // ROW_WORD_CHUNKS
    hp3 = hp.reshape(n_tokens, ROW_WORD_CHUNKS, LANES)
    xs = pl.pallas_call(
        _dispatch_kernel,
        grid_spec=pltpu.PrefetchScalarGridSpec(
            num_scalar_prefetch=4,
            grid=(npad // DSP_ROWS,),
            in_specs=[pl.BlockSpec(memory_space=pl.ANY)],
            out_specs=pl.BlockSpec((DSP_ROWS, ROW_WORD_CHUNKS, LANES), lambda t, p0, p1, lo, n: (t, 0, 0)),
            scratch_shapes=[pltpu.VMEM(hp3.shape, hp3.dtype), pltpu.SMEM((npad + MOE_TM,), jnp.int32),
                            pltpu.SemaphoreType.DMA(())],
        ),
        out_shape=jax.ShapeDtypeStruct((npad, ROW_WORD_CHUNKS, LANES), hp.dtype),
        compiler_params=pltpu.CompilerParams(
            dimension_semantics=("arbitrary",), vmem_limit_bytes=VMEM_LIMIT),
        name="dispatch",
    )(pos[0], pos[1], pad_lo, n_tiles, hp3)
    return xs.reshape(npad * ROW_WORD_CHUNKS, LANES)


W_SLOTS = 2
EXP_TILES = 3


def _experts_kernel(tk_ref, eseq_ref, nt_ref, xs_ref, wg_hbm, wu_hbm, wd_hbm, y_ref,
                    wg_buf, wu_buf, wd_buf, x_ref, sem):
    n_used = nt_ref[1]
    hbm_bufs = ((wg_hbm, wg_buf), (wu_hbm, wu_buf), (wd_hbm, wd_buf))

    def weight_copies(k, slot):
        e = eseq_ref[k]
        return [pltpu.make_async_copy(hbm.at[e], buf.at[slot], sem.at[slot, n])
                for n, (hbm, buf) in enumerate(hbm_bufs)]

    for sub in range(EXP_TILES):
        _expert_tile(pl.program_id(0) * EXP_TILES + sub, sub, tk_ref, nt_ref, n_used, weight_copies,
                     xs_ref, y_ref, wg_buf, wu_buf, wd_buf, x_ref)


def _expert_tile(t, sub, tk_ref, nt_ref, n_used, weight_copies,
                 xs_ref, y_ref, wg_buf, wu_buf, wd_buf, x_ref):
    xs_row0 = sub * MOE_TM * ROW_WORD_CHUNKS
    y_rows = pl.ds(sub * MOE_TM, MOE_TM)

    @pl.when(t < nt_ref[0])
    def _():
        k = tk_ref[t]
        slot = k % W_SLOTS

        @pl.when((t == 0) | (k != tk_ref[jnp.maximum(t - 1, 0)]))
        def _():
            @pl.when(k == 0)
            def _():
                for s in range(W_SLOTS):
                    @pl.when(s < n_used)
                    def _():
                        for c in weight_copies(s, s):
                            c.start()

            for c in weight_copies(k, slot):
                c.wait()

        for c in range(ROW_WORD_CHUNKS):
            u32 = xs_ref[pl.ds(xs_row0 + c, MOE_TM, stride=ROW_WORD_CHUNKS), :]
            x_ref[:, 2 * c * LANES:(2 * c + 1) * LANES] = pltpu.bitcast(u32 << 16, F32).astype(BF16)
            x_ref[:, (2 * c + 1) * LANES:(2 * c + 2) * LANES] = (
                pltpu.bitcast(u32 & jnp.uint32(0xFFFF0000), F32).astype(BF16))
        x = x_ref[...]
        a = _dot(x, wg_buf[slot].astype(BF16))
        u = _dot(x, wu_buf[slot].astype(BF16))
        act = (a * jax.nn.sigmoid(a)) * u
        y_ref[y_rows, :] = _dot(act.astype(BF16), wd_buf[slot].astype(BF16))

        last_tile = (t + 1 == nt_ref[0]) | (k != tk_ref[jnp.minimum(t + 1, tk_ref.shape[0] - 1)])

        @pl.when(last_tile & (k + W_SLOTS < n_used))
        def _():
            for c in weight_copies(k + W_SLOTS, slot):
                c.start()

    @pl.when(t >= nt_ref[0])
    def _():
        y_ref[y_rows, :] = jnp.zeros((MOE_TM, D_MODEL), y_ref.dtype)


def _experts(tile_k, expert_seq, n_used, xs, w_gate, w_up, w_down):
    npad = xs.shape[0] // ROW_WORD_CHUNKS
    rows = EXP_TILES * MOE_TM
    assert npad % rows == 0 and npad % DSP_ROWS == 0
    last = lambda t, nt: jnp.minimum(t, jnp.maximum(nt[0] - 1, 0) // EXP_TILES)
    hbm = pl.BlockSpec(memory_space=pl.ANY)
    return pl.pallas_call(
        _experts_kernel,
        grid_spec=pltpu.PrefetchScalarGridSpec(
            num_scalar_prefetch=3,
            grid=(npad // rows,),
            in_specs=[
                pl.BlockSpec((rows * ROW_WORD_CHUNKS, LANES), lambda t, tk, es, nt: (last(t, nt), 0)),
                hbm, hbm, hbm,
            ],
            out_specs=pl.BlockSpec((rows, D_MODEL), lambda t, tk, es, nt: (t, 0)),
            scratch_shapes=[
                pltpu.VMEM((W_SLOTS, D_MODEL, D_EXPERT), F32),
                pltpu.VMEM((W_SLOTS, D_MODEL, D_EXPERT), F32),
                pltpu.VMEM((W_SLOTS, D_EXPERT, D_MODEL), F32),
                pltpu.VMEM((MOE_TM, D_MODEL), BF16),
                pltpu.SemaphoreType.DMA((W_SLOTS, 3)),
            ],
        ),
        out_shape=jax.ShapeDtypeStruct((npad, D_MODEL), F32),
        compiler_params=pltpu.CompilerParams(
            dimension_semantics=("arbitrary",), vmem_limit_bytes=VMEM_LIMIT),
        name="experts",
    )(tile_k, expert_seq, n_used, xs, w_gate, w_up, w_down)


CMB_UNROLL = 8
CMB_SLOTS = 2


def _combine_kernel(pos0_ref, pos1_ref, x2_ref, meta_ref, nw_ref, y_ref, o_ref, buf_ref, sem):
    i = pl.program_id(0)
    pos_refs = (pos0_ref, pos1_ref)

    def row_copy(p, slot, k, r):
        return pltpu.make_async_copy(y_ref.at[pl.ds(p, 1), :], buf_ref.at[slot, k, pl.ds(r, 1), :],
                                     sem.at[slot])

    def issue(tile, slot):
        for c in range(CMB_TT // CMB_UNROLL):
            tok0 = tile * CMB_TT + c * CMB_UNROLL
            rows = [[p[tok0 + u] for u in range(CMB_UNROLL)] for p in pos_refs]
            for u in range(CMB_UNROLL):
                for k in range(EXPERT_TOPK):
                    row_copy(rows[k][u], slot, k, c * CMB_UNROLL + u).start(priority=k % 2)

    @pl.when(i == 0)
    def _():
        issue(0, 0)

    @pl.when(i + 1 < pl.num_programs(0))
    def _():
        issue(i + 1, (i + 1) % CMB_SLOTS)

    slot = i % CMB_SLOTS
    for k in range(EXPERT_TOPK):
        pltpu.make_async_copy(y_ref.at[pl.ds(0, CMB_TT), :], buf_ref.at[slot, k], sem.at[slot]).wait()

    meta = meta_ref[...]
    x3 = (x2_ref[...] + meta[:, META_W1:META_W1 + 1] * buf_ref[slot, 0]
          + meta[:, META_W2:META_W2 + 1] * buf_ref[slot, 1])
    r = lax.rsqrt(jnp.mean(x3 * x3, axis=-1, keepdims=True) + RMS_EPS)
    o_ref[...] = x3 * r * nw_ref[...]


def _combine(pos, x2, meta, norm_w, y):
    t = x2.shape[0]
    return pl.pallas_call(
        _combine_kernel,
        grid_spec=pltpu.PrefetchScalarGridSpec(
            num_scalar_prefetch=2,
            grid=(t // CMB_TT,),
            in_specs=[
                pl.BlockSpec((CMB_TT, D_MODEL), lambda i, p0, p1: (i, 0)),
                pl.BlockSpec((CMB_TT, LANES), lambda i, p0, p1: (i, 0)),
                pl.BlockSpec((1, D_MODEL), lambda i, p0, p1: (0, 0)),
                pl.BlockSpec(memory_space=pl.ANY),
            ],
            out_specs=pl.BlockSpec((CMB_TT, D_MODEL), lambda i, p0, p1: (i, 0)),
            scratch_shapes=[
                pltpu.VMEM((CMB_SLOTS, EXPERT_TOPK, CMB_TT, D_MODEL), F32),
                pltpu.SemaphoreType.DMA((CMB_SLOTS,)),
            ],
        ),
        out_shape=jax.ShapeDtypeStruct((t, D_MODEL), F32),
        compiler_params=pltpu.CompilerParams(
            dimension_semantics=("arbitrary",), vmem_limit_bytes=VMEM_LIMIT),
        name="combine",
    )(pos[0], pos[1], x2, meta, norm_w, y)


def _layout(meta, counts, n_tokens):
    npad = n_tokens * EXPERT_TOPK + N_EXPERTS * MOE_TM
    counts = counts[0, :N_EXPERTS].astype(jnp.int32)
    padded = ((counts + MOE_TM - 1) // MOE_TM) * MOE_TM
    ends = jnp.cumsum(padded)
    starts = ends - padded
    ids = jnp.arange(N_EXPERTS, dtype=jnp.int32)
    pos = []
    for lane_x, lane_rank in ((META_X1, META_RANK1), (META_X2, META_RANK2)):
        hit = meta[:, lane_x].astype(jnp.int32)[:, None] == ids[None, :]
        pos.append(jnp.sum(jnp.where(hit, starts[None, :], 0), axis=-1) + meta[:, lane_rank].astype(jnp.int32))
    tile_start = jnp.arange(npad // MOE_TM, dtype=jnp.int32) * MOE_TM
    tile_expert = jnp.minimum(jnp.sum(ends[None, :] <= tile_start[:, None], axis=1), N_EXPERTS - 1)
    used = counts > 0
    k_of_expert = jnp.cumsum(used.astype(jnp.int32)) - 1
    expert_seq = jnp.sum(jnp.where(used[None, :] & (k_of_expert[None, :] == ids[:, None]), ids[None, :], 0), axis=1)
    tile_k = jnp.sum(jnp.where(tile_expert[:, None] == ids[None, :], k_of_expert[None, :], 0), axis=1)
    n_tiles = (ends[-1] // MOE_TM).astype(jnp.int32)
    n_used = jnp.stack([n_tiles, jnp.sum(used).astype(jnp.int32)])
    return (pos, starts + counts, tile_k.astype(jnp.int32), expert_seq.astype(jnp.int32), n_tiles.reshape(1),
            n_used, npad)


def kernel(x, norm_mix_w, w_in, lambda_q1, lambda_k1, lambda_q2, lambda_k2, diff_subln_w, w_out, norm_ffn_w,
           w_router_group, b_router_group, w_router_expert, b_router_expert, w_gate, w_up, w_down, norm_final_w):
    batch, seq, _ = x.shape
    assert seq % ATT_TQ == 0 and seq // MOBA_BLOCK + 8 <= LANES
    assert w_in.shape[0] == 1, "single-layer block"
    n_tokens = batch * seq
    x2d = x.reshape(n_tokens, D_MODEL)

    col = np.arange(IN_WIDTH)
    is_q = (col < MOBA_HEADS * HEAD_DIM) | ((col >= 3 * MOBA_HEADS * HEAD_DIM) & (col < 4 * MOBA_HEADS * HEAD_DIM))
    col_scale = jnp.asarray(np.where(is_q, QK_SCALE * LOG2E, 1.0).astype(np.float32))[None, :]

    proj = _in_proj(x2d, norm_mix_w[0][None, :], w_in[0], col_scale)
    out_a = _moba(proj, jnp.asarray(_alibi_slope_pieces(MOBA_HEADS)), batch, seq)
    out_b = _diff(proj, jnp.asarray(_alibi_slope_pieces(DIFF_HEADS)), lambda_q1, lambda_k1, lambda_q2, lambda_k2,
                  diff_subln_w, batch, seq)

    w_r = jnp.concatenate([w_router_group[0], w_router_expert[0]], axis=1)
    w_r = jnp.pad(w_r, ((0, 0), (0, LANES - w_r.shape[1])))
    wr_hi = w_r.astype(BF16)
    wr_lo = (w_r - wr_hi.astype(F32)).astype(BF16)
    b_r = jnp.pad(jnp.concatenate([b_router_group[0], b_router_expert[0]]), (0, LANES - N_GROUPS - N_EXPERTS))[None, :]
    x2, hp, meta, counts = _out_proj(x2d, out_a, out_b, w_out[0], norm_ffn_w[0][None, :],
                                     jnp.concatenate([wr_hi, wr_lo], axis=1), b_r)

    pos, pad_lo, tile_k, expert_seq, n_tiles, n_used, npad = _layout(meta, counts, n_tokens)
    xs = _dispatch(pos, pad_lo, n_tiles, hp, npad)
    y = _experts(tile_k, expert_seq, n_used, xs, w_gate[0], w_up[0], w_down[0])
    out = _combine(pos, x2, meta, norm_final_w[None, :], y)
    return out.reshape(batch, seq, D_MODEL)
```

```python
import functools

import ml_dtypes
import numpy as np
import jax
import jax.numpy as jnp
from jax import lax
from jax.experimental import pallas as pl
from jax.experimental.pallas import tpu as pltpu

F32 = jnp.float32
BF16 = jnp.bfloat16

D_MODEL = 2048
HEAD_DIM = 128
MOBA_HEADS = 8
MOBA_BLOCK = 256
MOBA_TOPK = 3
DIFF_HEADS = 4
DIFF_V_DIM = 256
IN_WIDTH = 6144
N_GROUPS = 4
EXPERTS_PER_GROUP = 8
N_EXPERTS = 32
EXPERT_TOPK = 2
D_EXPERT = 512
RMS_EPS = 1e-6
ALIBI_MAX_BIAS = 8.0
LAMBDA_INIT = 0.8 - 0.6 * float(np.exp(-0.3 * 0))

LANES = 128
QK_SCALE = HEAD_DIM ** -0.5
LOG2E = float(np.log2(np.e))
NEG_BIG = -1e30
M_INIT = -1e38
POS_SPLIT = 64
SLOPE_PIECES = 3
ROW_WORD_CHUNKS = D_MODEL // (2 * LANES)
META_X1, META_X2, META_W1, META_W2, META_RANK1, META_RANK2 = range(6)

IN_TM, IN_TN = 1024, 1024
ATT_TQ = 1024
ATT_TK = 512
KV_UNROLL = 2
HEADS_PER_STEP = 4
DIFF_HEADS_PER_STEP = 1
OUT_TM = 512
MOE_TM = 256
CMB_TT = 256
VMEM_LIMIT = 56 * 1024 * 1024


def _alibi_slope_pieces(n):
    rem = np.exp2(-ALIBI_MAX_BIAS * (np.arange(n, dtype=np.float64) + 1.0) / n) * LOG2E
    pieces = []
    for _ in range(SLOPE_PIECES):
        p = rem.astype(ml_dtypes.bfloat16).astype(np.float64)
        pieces.append(p)
        rem = rem - p
    return np.stack(pieces, axis=1).reshape(-1).astype(np.float32)


def _dot_nt(a, b):
    return lax.dot_general(a, b, (((1,), (1,)), ((), ())), preferred_element_type=F32)


def _dot(a, b):
    return jnp.dot(a, b, preferred_element_type=F32)


def _inproj_kernel(x_ref, nw_ref, cs_ref, w_ref, o_ref, h_ref):
    @pl.when(pl.program_id(1) == 0)
    def _():
        x = x_ref[...]
        r = lax.rsqrt(jnp.mean(x * x, axis=-1, keepdims=True) + RMS_EPS)
        h_ref[...] = (x * r * nw_ref[...]).astype(BF16)

    acc = _dot(h_ref[...], w_ref[...].astype(BF16))
    o_ref[...] = (acc * cs_ref[...]).astype(BF16)


def _in_proj(x2d, norm_w, w_in, col_scale):
    t = x2d.shape[0]
    return pl.pallas_call(
        _inproj_kernel,
        grid=(t // IN_TM, IN_WIDTH // IN_TN),
        in_specs=[
            pl.BlockSpec((IN_TM, D_MODEL), lambda i, j: (i, 0)),
            pl.BlockSpec((1, D_MODEL), lambda i, j: (0, 0)),
            pl.BlockSpec((1, IN_TN), lambda i, j: (0, j)),
            pl.BlockSpec((D_MODEL, IN_TN), lambda i, j: (0, j)),
        ],
        out_specs=pl.BlockSpec((IN_TM, IN_TN), lambda i, j: (i, j)),
        out_shape=jax.ShapeDtypeStruct((t, IN_WIDTH), BF16),
        scratch_shapes=[pltpu.VMEM((IN_TM, D_MODEL), BF16)],
        compiler_params=pltpu.CompilerParams(
            dimension_semantics=("arbitrary", "arbitrary"), vmem_limit_bytes=VMEM_LIMIT),
        name="in_proj",
    )(x2d, norm_w, col_scale, w_in)


def _key_position_features(seq, first_lane):
    pos = lax.broadcasted_iota(jnp.int32, (seq, LANES), 0)
    lane = lax.broadcasted_iota(jnp.int32, (seq, LANES), 1)
    hi = (lane >= first_lane) & (lane < first_lane + SLOPE_PIECES)
    lo = (lane >= first_lane + SLOPE_PIECES) & (lane < first_lane + 2 * SLOPE_PIECES)
    ext = jnp.where(hi, (pos // POS_SPLIT).astype(F32), 0.0)
    return jnp.where(lo, (pos % POS_SPLIT).astype(F32), ext), pos, lane


def _slope_feature_rows(feat_ref, head, width):
    row = lax.broadcasted_iota(jnp.int32, (8, width), 0)
    out = jnp.zeros((8, width), F32)
    for k in range(SLOPE_PIECES):
        c = feat_ref[head * SLOPE_PIECES + k]
        out = jnp.where(row == k, POS_SPLIT * c, out)
        out = jnp.where(row == SLOPE_PIECES + k, c, out)
    return out


def _causal_tile(s):
    row = lax.broadcasted_iota(jnp.int32, s.shape, 0)
    col = lax.broadcasted_iota(jnp.int32, s.shape, 1)
    return jnp.where(col <= row, s, -jnp.inf)


def _online_step(s, m):
    m_new = jnp.maximum(m, jnp.max(s, axis=-1, keepdims=True))
    return m_new, jnp.exp2(m - m_new), jnp.exp2(s - m_new)


def _past_tiles_loop(i, body, init):
    def trip(t, carry):
        for u in range(KV_UNROLL):
            carry = body(t * KV_UNROLL + u, carry)
        return carry

    return lax.fori_loop(0, i * (ATT_TQ // (ATT_TK * KV_UNROLL)), trip, init)


def _rows_from(state, start):
    return state if start == 0 else tuple(a[start:] for a in state)


def _rows_replace(state, start, tail):
    if start == 0:
        return tuple(tail)
    return tuple(jnp.concatenate([a[:start], b], axis=0) for a, b in zip(state, tail))


def _moba_kernel(feat_ref, q_ref, k_ref, v_ref, o_ref, kaug_ref, vaug_ref, kmh_ref, kml_ref, qaug_ref, *, seq):
    nb = seq // MOBA_BLOCK
    hp = pl.program_id(1)
    i = pl.program_id(2)

    @pl.when(i == 0)
    def _():
        ext, pos, lane = _key_position_features(seq, nb)
        ext = jnp.where(lane == pos // MOBA_BLOCK, 1.0, ext).astype(BF16)
        ones_col = jnp.where(lane == 0, 1.0, 0.0).astype(BF16)
        for g in range(HEADS_PER_STEP):
            k = k_ref[:, g * HEAD_DIM:(g + 1) * HEAD_DIM]
            kaug_ref[g, :, 0:HEAD_DIM] = k
            kaug_ref[g, :, HEAD_DIM:2 * HEAD_DIM] = ext
            vaug_ref[g, :, 0:HEAD_DIM] = v_ref[:, g * HEAD_DIM:(g + 1) * HEAD_DIM]
            vaug_ref[g, :, HEAD_DIM:2 * HEAD_DIM] = ones_col
            km = jnp.mean(k.astype(F32).reshape(nb, MOBA_BLOCK, HEAD_DIM), axis=1)
            kmh = km.astype(BF16)
            kmh_ref[g] = jnp.zeros((LANES, HEAD_DIM), BF16)
            kml_ref[g] = jnp.zeros((LANES, HEAD_DIM), BF16)
            kmh_ref[g, 0:nb, :] = kmh
            kml_ref[g, 0:nb, :] = (km - kmh.astype(F32)).astype(BF16)

    blk = lax.broadcasted_iota(jnp.int32, (nb, ATT_TQ), 0)
    col = lax.broadcasted_iota(jnp.int32, (nb, ATT_TQ), 1)
    own = i * (ATT_TQ // MOBA_BLOCK) + col // MOBA_BLOCK
    blk_f = blk.astype(F32)
    for g in range(HEADS_PER_STEP):
        q = q_ref[:, g * HEAD_DIM:(g + 1) * HEAD_DIM]
        gate = (_dot_nt(kmh_ref[g], q) + _dot_nt(kml_ref[g], q))[0:nb, :]
        gate = jnp.where(blk < own, gate, -jnp.inf)
        sel = jnp.zeros((nb, ATT_TQ), F32)
        for _ in range(MOBA_TOPK):
            mx = jnp.max(gate, axis=0, keepdims=True)
            idx = jnp.min(jnp.where(gate == mx, blk_f, float(LANES)), axis=0, keepdims=True)
            pick = (blk_f == idx) & (mx > -jnp.inf)
            sel = jnp.where(pick, 1.0, sel)
            gate = jnp.where(pick, -jnp.inf, gate)
        bias = jnp.where((sel > 0.0) | (blk == own), 0.0, NEG_BIG)
        feat = jnp.concatenate(
            [bias, _slope_feature_rows(feat_ref, hp * HEADS_PER_STEP + g, ATT_TQ),
             jnp.zeros((LANES - nb - 8, ATT_TQ), F32)], axis=0)
        qaug_ref[g, :, 0:HEAD_DIM] = q
        qaug_ref[g, :, HEAD_DIM:2 * HEAD_DIM] = feat.T.astype(BF16)

    def tile_step(g, r, state, row0=0, causal=False):
        m, acc = state
        s = _dot_nt(qaug_ref[g, row0:, :], kaug_ref[g, pl.ds(r, ATT_TK), :])
        if causal:
            s = _causal_tile(s)
        m, alpha, p = _online_step(s, m)
        return m, alpha * acc + _dot(p.astype(BF16), vaug_ref[g, pl.ds(r, ATT_TK), :])

    def body(j, states):
        r = pl.multiple_of(j * ATT_TK, ATT_TK)
        return tuple(tile_step(g, r, states[g]) for g in range(HEADS_PER_STEP))

    init = tuple((jnp.full((ATT_TQ, 1), M_INIT, F32), jnp.zeros((ATT_TQ, 2 * HEAD_DIM), F32))
                 for _ in range(HEADS_PER_STEP))
    states = list(_past_tiles_loop(i, body, init))
    for d in range(ATT_TQ // ATT_TK):
        r = pl.multiple_of(i * ATT_TQ + d * ATT_TK, ATT_TK)
        for g in range(HEADS_PER_STEP):
            tail = tile_step(g, r, _rows_from(states[g], d * ATT_TK), row0=d * ATT_TK, causal=True)
            states[g] = _rows_replace(states[g], d * ATT_TK, tail)
    for g in range(HEADS_PER_STEP):
        acc = states[g][1]
        o_ref[:, g * HEAD_DIM:(g + 1) * HEAD_DIM] = (
            acc[:, 0:HEAD_DIM] / acc[:, HEAD_DIM:HEAD_DIM + 1]).astype(BF16)


def _moba(proj, feats, batch, seq):
    nq = seq // ATT_TQ
    width = HEADS_PER_STEP * HEAD_DIM
    q_col, k_col, v_col = 0, MOBA_HEADS // HEADS_PER_STEP, 2 * MOBA_HEADS // HEADS_PER_STEP
    return pl.pallas_call(
        functools.partial(_moba_kernel, seq=seq),
        grid_spec=pltpu.PrefetchScalarGridSpec(
            num_scalar_prefetch=1,
            grid=(batch, MOBA_HEADS // HEADS_PER_STEP, nq),
            in_specs=[
                pl.BlockSpec((ATT_TQ, width), lambda b, h, i, s: (b * nq + i, q_col + h)),
                pl.BlockSpec((seq, width), lambda b, h, i, s: (b, k_col + h)),
                pl.BlockSpec((seq, width), lambda b, h, i, s: (b, v_col + h)),
            ],
            out_specs=pl.BlockSpec((ATT_TQ, width), lambda b, h, i, s: (b * nq + i, h)),
            scratch_shapes=[
                pltpu.VMEM((HEADS_PER_STEP, seq, 2 * HEAD_DIM), BF16),
                pltpu.VMEM((HEADS_PER_STEP, seq, 2 * HEAD_DIM), BF16),
                pltpu.VMEM((HEADS_PER_STEP, LANES, HEAD_DIM), BF16),
                pltpu.VMEM((HEADS_PER_STEP, LANES, HEAD_DIM), BF16),
                pltpu.VMEM((HEADS_PER_STEP, ATT_TQ, 2 * HEAD_DIM), BF16),
            ],
        ),
        out_shape=jax.ShapeDtypeStruct((batch * seq, MOBA_HEADS * HEAD_DIM), BF16),
        compiler_params=pltpu.CompilerParams(
            dimension_semantics=("arbitrary", "arbitrary", "arbitrary"), vmem_limit_bytes=VMEM_LIMIT),
        name="moba",
    )(feats, proj, proj, proj)


def _diff_kernel(feat_ref, lq1_ref, lk1_ref, lq2_ref, lk2_ref, sw_ref,
                 q_ref, k_ref, v_ref, o_ref, kaug_ref, qaug_ref, *, seq):
    hg = pl.program_id(1)
    i = pl.program_id(2)
    n_chains = 2 * DIFF_HEADS_PER_STEP

    @pl.when(i == 0)
    def _():
        ext = _key_position_features(seq, 0)[0].astype(BF16)
        for n in range(n_chains):
            kaug_ref[n, :, 0:HEAD_DIM] = k_ref[:, n * HEAD_DIM:(n + 1) * HEAD_DIM]
            kaug_ref[n, :, HEAD_DIM:2 * HEAD_DIM] = ext

    for n in range(n_chains):
        if n % 2 == 0:
            head = hg * DIFF_HEADS_PER_STEP + n // 2
            feat = jnp.concatenate([_slope_feature_rows(feat_ref, head, ATT_TQ),
                                    jnp.zeros((LANES - 8, ATT_TQ), F32)], axis=0).T.astype(BF16)
        qaug_ref[n, :, 0:HEAD_DIM] = q_ref[:, n * HEAD_DIM:(n + 1) * HEAD_DIM]
        qaug_ref[n, :, HEAD_DIM:2 * HEAD_DIM] = feat

    def tile_step(n, r, state, row0=0, causal=False):
        m, l, acc = state
        s = _dot_nt(qaug_ref[n, row0:, :], kaug_ref[n, pl.ds(r, ATT_TK), :])
        if causal:
            s = _causal_tile(s)
        m, alpha, p = _online_step(s, m)
        v = v_ref[pl.ds(r, ATT_TK), (n // 2) * DIFF_V_DIM:(n // 2 + 1) * DIFF_V_DIM]
        p = p.astype(BF16)
        return (m, alpha * l + jnp.sum(p.astype(F32), axis=-1, keepdims=True), alpha * acc + _dot(p, v))

    def body(j, states):
        r = pl.multiple_of(j * ATT_TK, ATT_TK)
        return tuple(tile_step(n, r, states[n]) for n in range(n_chains))

    init = tuple((jnp.full((ATT_TQ, 1), M_INIT, F32), jnp.zeros((ATT_TQ, 1), F32),
                  jnp.zeros((ATT_TQ, DIFF_V_DIM), F32)) for _ in range(n_chains))
    states = list(_past_tiles_loop(i, body, init))
    for d in range(ATT_TQ // ATT_TK):
        r = pl.multiple_of(i * ATT_TQ + d * ATT_TK, ATT_TK)
        for n in range(n_chains):
            tail = tile_step(n, r, _rows_from(states[n], d * ATT_TK), row0=d * ATT_TK, causal=True)
            states[n] = _rows_replace(states[n], d * ATT_TK, tail)

    lam = (jnp.exp(jnp.sum(lq1_ref[...] * lk1_ref[...], axis=-1, keepdims=True))
           - jnp.exp(jnp.sum(lq2_ref[...] * lk2_ref[...], axis=-1, keepdims=True))
           + LAMBDA_INIT)
    for hh in range(DIFF_HEADS_PER_STEP):
        (_, l1, acc1), (_, l2, acc2) = states[2 * hh], states[2 * hh + 1]
        o = acc1 / l1 - lam * (acc2 / l2)
        y = o * lax.rsqrt(jnp.mean(o * o, axis=-1, keepdims=True) + RMS_EPS)
        o_ref[:, hh * DIFF_V_DIM:(hh + 1) * DIFF_V_DIM] = ((y * sw_ref[...]) * (1.0 - LAMBDA_INIT)).astype(BF16)


def _diff(proj, feats, lq1, lk1, lq2, lk2, subln_w, batch, seq):
    nq = seq // ATT_TQ
    width = DIFF_HEADS_PER_STEP * 2 * HEAD_DIM
    groups = DIFF_HEADS // DIFF_HEADS_PER_STEP
    q_col = 3 * MOBA_HEADS * HEAD_DIM // width
    k_col = q_col + groups
    v_col = k_col + groups
    vec = lambda n: pl.BlockSpec((1, n), lambda b, h, i, s: (0, 0))
    return pl.pallas_call(
        functools.partial(_diff_kernel, seq=seq),
        grid_spec=pltpu.PrefetchScalarGridSpec(
            num_scalar_prefetch=1,
            grid=(batch, groups, nq),
            in_specs=[
                vec(HEAD_DIM), vec(HEAD_DIM), vec(HEAD_DIM), vec(HEAD_DIM), vec(DIFF_V_DIM),
                pl.BlockSpec((ATT_TQ, width), lambda b, h, i, s: (b * nq + i, q_col + h)),
                pl.BlockSpec((seq, width), lambda b, h, i, s: (b, k_col + h)),
                pl.BlockSpec((seq, width), lambda b, h, i, s: (b, v_col + h)),
            ],
            out_specs=pl.BlockSpec((ATT_TQ, width), lambda b, h, i, s: (b * nq + i, h)),
            scratch_shapes=[
                pltpu.VMEM((2 * DIFF_HEADS_PER_STEP, seq, 2 * HEAD_DIM), BF16),
                pltpu.VMEM((2 * DIFF_HEADS_PER_STEP, ATT_TQ, 2 * HEAD_DIM), BF16),
            ],
        ),
        out_shape=jax.ShapeDtypeStruct((batch * seq, DIFF_HEADS * DIFF_V_DIM), BF16),
        compiler_params=pltpu.CompilerParams(
            dimension_semantics=("arbitrary", "arbitrary", "arbitrary"), vmem_limit_bytes=VMEM_LIMIT),
        name="diff_attn",
    )(feats, lq1, lk1, lq2, lk2, subln_w, proj, proj, proj)


def _route_tile(lg, run, tri):
    lane = lax.broadcasted_iota(jnp.int32, lg.shape, 1).astype(F32)
    rmax = lambda v: jnp.max(v, axis=-1, keepdims=True)
    first = lambda v, m: jnp.min(jnp.where(v == m, lane, float(LANES)), axis=-1, keepdims=True)

    gl = jnp.where(lane < N_GROUPS, lg, -jnp.inf)
    gmax = rmax(gl)
    group = first(gl, gmax)
    p_group = 1.0 / jnp.sum(jnp.exp(gl - gmax), axis=-1, keepdims=True)

    lo = N_GROUPS + EXPERTS_PER_GROUP * group
    el = jnp.where((lane >= lo) & (lane < lo + EXPERTS_PER_GROUP), lg, -jnp.inf)
    e1 = rmax(el)
    lane1 = first(el, e1)
    el = jnp.where(lane == lane1, -jnp.inf, el)
    e2 = rmax(el)
    lane2 = first(el, e2)
    t2 = jnp.exp(e2 - e1)
    w1 = p_group / (1.0 + t2)
    w2 = p_group * t2 / (1.0 + t2)
    x1 = lane1 - N_GROUPS
    x2 = lane2 - N_GROUPS

    hit1 = lane == x1
    hit2 = lane == x2
    onehot = jnp.where(hit1 | hit2, 1.0, 0.0)
    before = _dot(tri, onehot.astype(BF16)) + run
    rank1 = jnp.sum(jnp.where(hit1, before, 0.0), axis=-1, keepdims=True)
    rank2 = jnp.sum(jnp.where(hit2, before, 0.0), axis=-1, keepdims=True)

    meta = jnp.zeros(lg.shape, F32)
    for k, v in ((META_X1, x1), (META_X2, x2), (META_W1, w1), (META_W2, w2),
                 (META_RANK1, rank1), (META_RANK2, rank2)):
        meta = jnp.where(lane == k, v, meta)
    return meta, run + jnp.sum(onehot, axis=0, keepdims=True)


def _outproj_kernel(x_ref, oa_ref, ob_ref, wf_ref, nw_ref, wr_ref, br_ref,
                    x2_ref, hp_ref, meta_ref, cnt_ref, run_ref, tri_ref, w_ref):
    @pl.when(pl.program_id(0) == 0)
    def _():
        row = lax.broadcasted_iota(jnp.int32, tri_ref.shape, 0)
        col = lax.broadcasted_iota(jnp.int32, tri_ref.shape, 1)
        tri_ref[...] = jnp.where(col < row, 1.0, 0.0).astype(BF16)
        run_ref[...] = jnp.zeros(run_ref.shape, F32)
        w_ref[...] = wf_ref[...].astype(BF16)

    half = D_MODEL // 2
    x2 = x_ref[...] + _dot(oa_ref[...], w_ref[0:half, :]) + _dot(ob_ref[...], w_ref[half:, :])
    x2_ref[...] = x2
    r = lax.rsqrt(jnp.mean(x2 * x2, axis=-1, keepdims=True) + RMS_EPS)
    h = x2 * r * nw_ref[...]
    hh = h.astype(BF16)
    hl = (h - hh.astype(F32)).astype(BF16)
    prod = _dot(jnp.concatenate([hh, hl], axis=0), wr_ref[...])
    lg = prod[:OUT_TM, :LANES] + prod[OUT_TM:, :LANES] + prod[:OUT_TM, LANES:] + br_ref[...]
    meta, run = _route_tile(lg, run_ref[...], tri_ref[...])
    meta_ref[...] = meta
    run_ref[...] = run
    cnt_ref[...] = run
    bits = pltpu.bitcast(hh.astype(F32), jnp.uint32)
    for c in range(ROW_WORD_CHUNKS):
        lo = bits[:, 2 * c * LANES:(2 * c + 1) * LANES]
        hi = bits[:, (2 * c + 1) * LANES:(2 * c + 2) * LANES]
        hp_ref[pl.ds(c, OUT_TM, stride=ROW_WORD_CHUNKS), :] = (lo >> 16) | (hi & jnp.uint32(0xFFFF0000))


def _out_proj(x2d, out_a, out_b, w_o, norm_w, w_r, b_r):
    t = x2d.shape[0]
    half = D_MODEL // 2
    row = lambda n: pl.BlockSpec((OUT_TM, n), lambda i: (i, 0))
    full = lambda r, c: pl.BlockSpec((r, c), lambda i: (0, 0))
    return pl.pallas_call(
        _outproj_kernel,
        grid=(t // OUT_TM,),
        in_specs=[row(D_MODEL), row(half), row(half), pl.BlockSpec(memory_space=pltpu.VMEM),
                  full(1, D_MODEL), full(D_MODEL, 2 * LANES), full(1, LANES)],
        out_specs=[row(D_MODEL), pl.BlockSpec((OUT_TM * ROW_WORD_CHUNKS, LANES), lambda i: (i, 0)), row(LANES),
                   full(1, LANES)],
        out_shape=[jax.ShapeDtypeStruct((t, D_MODEL), F32),
                   jax.ShapeDtypeStruct((t * ROW_WORD_CHUNKS, LANES), jnp.uint32),
                   jax.ShapeDtypeStruct((t, LANES), F32),
                   jax.ShapeDtypeStruct((1, LANES), F32)],
        scratch_shapes=[pltpu.VMEM((1, LANES), F32), pltpu.VMEM((OUT_TM, OUT_TM), BF16),
                        pltpu.VMEM((D_MODEL, D_MODEL), BF16)],
        compiler_params=pltpu.CompilerParams(
            dimension_semantics=("arbitrary",), vmem_limit_bytes=VMEM_LIMIT),
        name="out_proj",
    )(x2d, out_a, out_b, w_o, norm_w, w_r, b_r)


GATHER_UNROLL = 16
DSP_ROWS = 8 * MOE_TM


def _dispatch_kernel(pos0_ref, pos1_ref, padlo_ref, nt_ref, hp_hbm, xs_ref, hp_ref, src_ref, sem):
    t = pl.program_id(0)
    pos_refs = (pos0_ref, pos1_ref)

    @pl.when(t == 0)
    def _():
        load = pltpu.make_async_copy(hp_hbm, hp_ref, sem)
        load.start()

        def clear(e, _):
            lo = padlo_ref[e]
            for u in range(MOE_TM):
                src_ref[lo + u] = 0
            return 0
        lax.fori_loop(0, N_EXPERTS, clear, 0)

        def place(c, _):
            tok0 = c * (GATHER_UNROLL // EXPERT_TOPK)
            rows = [p[tok0 + u] for u in range(GATHER_UNROLL // EXPERT_TOPK) for p in pos_refs]
            for n, row in enumerate(rows):
                src_ref[row] = tok0 + n // EXPERT_TOPK
            return 0
        lax.fori_loop(0, pos_refs[0].shape[0] * EXPERT_TOPK // GATHER_UNROLL, place, 0)
        load.wait()

    n_valid = jnp.clip(nt_ref[0] * MOE_TM - t * DSP_ROWS, 0, DSP_ROWS)

    def gather(c, _):
        toks = [src_ref[t * DSP_ROWS + c * GATHER_UNROLL + u] for u in range(GATHER_UNROLL)]
        for u in range(GATHER_UNROLL):
            xs_ref[c * GATHER_UNROLL + u] = hp_ref[toks[u]]
        return 0
    lax.fori_loop(0, n_valid // GATHER_UNROLL, gather, 0)

    def zero_tile(c, _):
        xs_ref[pl.ds(pl.multiple_of(c * MOE_TM, MOE_TM), MOE_TM)] = jnp.zeros(
            (MOE_TM,) + xs_ref.shape[1:], xs_ref.dtype)
        return 0
    lax.fori_loop(n_valid // MOE_TM, DSP_ROWS // MOE_TM, zero_tile, 0)


def _dispatch(pos, pad_lo, n_tiles, hp, npad):
    n_tokens = hp.shape[0] // ROW_WORD_CHUNKS
    hp3 = hp.reshape(n_tokens, ROW_WORD_CHUNKS, LANES)
    xs = pl.pallas_call(
        _dispatch_kernel,
        grid_spec=pltpu.PrefetchScalarGridSpec(
            num_scalar_prefetch=4,
            grid=(npad // DSP_ROWS,),
            in_specs=[pl.BlockSpec(memory_space=pl.ANY)],
            out_specs=pl.BlockSpec((DSP_ROWS, ROW_WORD_CHUNKS, LANES), lambda t, p0, p1, lo, n: (t, 0, 0)),
            scratch_shapes=[pltpu.VMEM(hp3.shape, hp3.dtype), pltpu.SMEM((npad + MOE_TM,), jnp.int32),
                            pltpu.SemaphoreType.DMA(())],
        ),
        out_shape=jax.ShapeDtypeStruct((npad, ROW_WORD_CHUNKS, LANES), hp.dtype),
        compiler_params=pltpu.CompilerParams(
            dimension_semantics=("arbitrary",), vmem_limit_bytes=VMEM_LIMIT),
        name="dispatch",
    )(pos[0], pos[1], pad_lo, n_tiles, hp3)
    return xs.reshape(npad * ROW_WORD_CHUNKS, LANES)


W_SLOTS = 2
EXP_TILES = 4


def _experts_kernel(tk_ref, eseq_ref, nt_ref, xs_ref, wg_hbm, wu_hbm, wd_hbm, y_ref,
                    wg_buf, wu_buf, wd_buf, x_ref, sem):
    n_used = nt_ref[1]
    hbm_bufs = ((wg_hbm, wg_buf), (wu_hbm, wu_buf), (wd_hbm, wd_buf))

    def weight_copies(k, slot):
        e = eseq_ref[k]
        return [pltpu.make_async_copy(hbm.at[e], buf.at[slot], sem.at[slot, n])
                for n, (hbm, buf) in enumerate(hbm_bufs)]

    for sub in range(EXP_TILES):
        _expert_tile(pl.program_id(0) * EXP_TILES + sub, sub, tk_ref, nt_ref, n_used, weight_copies,
                     xs_ref, y_ref, wg_buf, wu_buf, wd_buf, x_ref)


def _expert_tile(t, sub, tk_ref, nt_ref, n_used, weight_copies,
                 xs_ref, y_ref, wg_buf, wu_buf, wd_buf, x_ref):
    xs_row0 = sub * MOE_TM * ROW_WORD_CHUNKS
    y_rows = pl.ds(sub * MOE_TM, MOE_TM)

    @pl.when(t < nt_ref[0])
    def _():
        k = tk_ref[t]
        slot = k % W_SLOTS

        @pl.when((t == 0) | (k != tk_ref[jnp.maximum(t - 1, 0)]))
        def _():
            @pl.when(k == 0)
            def _():
                for s in range(W_SLOTS):
                    @pl.when(s < n_used)
                    def _():
                        for c in weight_copies(s, s):
                            c.start()

            for c in weight_copies(k, slot):
                c.wait()

        for c in range(ROW_WORD_CHUNKS):
            u32 = xs_ref[pl.ds(xs_row0 + c, MOE_TM, stride=ROW_WORD_CHUNKS), :]
            x_ref[:, 2 * c * LANES:(2 * c + 1) * LANES] = pltpu.bitcast(u32 << 16, F32).astype(BF16)
            x_ref[:, (2 * c + 1) * LANES:(2 * c + 2) * LANES] = (
                pltpu.bitcast(u32 & jnp.uint32(0xFFFF0000), F32).astype(BF16))
        x = x_ref[...]
        a = _dot(x, wg_buf[slot].astype(BF16))
        u = _dot(x, wu_buf[slot].astype(BF16))
        act = (a * jax.nn.sigmoid(a)) * u
        y_ref[y_rows, :] = _dot(act.astype(BF16), wd_buf[slot].astype(BF16))

        last_tile = (t + 1 == nt_ref[0]) | (k != tk_ref[jnp.minimum(t + 1, tk_ref.shape[0] - 1)])

        @pl.when(last_tile & (k + W_SLOTS < n_used))
        def _():
            for c in weight_copies(k + W_SLOTS, slot):
                c.start()

    @pl.when(t >= nt_ref[0])
    def _():
        y_ref[y_rows, :] = jnp.zeros((MOE_TM, D_MODEL), y_ref.dtype)


def _experts(tile_k, expert_seq, n_used, xs, w_gate, w_up, w_down):
    npad = xs.shape[0] // ROW_WORD_CHUNKS
    rows = EXP_TILES * MOE_TM
    assert npad % rows == 0 and npad % DSP_ROWS == 0
    last = lambda t, nt: jnp.minimum(t, jnp.maximum(nt[0] - 1, 0) // EXP_TILES)
    hbm = pl.BlockSpec(memory_space=pl.ANY)
    return pl.pallas_call(
        _experts_kernel,
        grid_spec=pltpu.PrefetchScalarGridSpec(
            num_scalar_prefetch=3,
            grid=(npad // rows,),
            in_specs=[
                pl.BlockSpec((rows * ROW_WORD_CHUNKS, LANES), lambda t, tk, es, nt: (last(t, nt), 0)),
                hbm, hbm, hbm,
            ],
            out_specs=pl.BlockSpec((rows, D_MODEL), lambda t, tk, es, nt: (t, 0)),
            scratch_shapes=[
                pltpu.VMEM((W_SLOTS, D_MODEL, D_EXPERT), F32),
                pltpu.VMEM((W_SLOTS, D_MODEL, D_EXPERT), F32),
                pltpu.VMEM((W_SLOTS, D_EXPERT, D_MODEL), F32),
                pltpu.VMEM((MOE_TM, D_MODEL), BF16),
                pltpu.SemaphoreType.DMA((W_SLOTS, 3)),
            ],
        ),
        out_shape=jax.ShapeDtypeStruct((npad, D_MODEL), F32),
        compiler_params=pltpu.CompilerParams(
            dimension_semantics=("arbitrary",), vmem_limit_bytes=VMEM_LIMIT),
        name="experts",
    )(tile_k, expert_seq, n_used, xs, w_gate, w_up, w_down)


CMB_UNROLL = 8
CMB_SLOTS = 2


def _combine_kernel(pos0_ref, pos1_ref, x2_ref, meta_ref, nw_ref, y_ref, o_ref, buf_ref, sem):
    i = pl.program_id(0)
    pos_refs = (pos0_ref, pos1_ref)

    def row_copy(p, slot, k, r):
        return pltpu.make_async_copy(y_ref.at[pl.ds(p, 1), :], buf_ref.at[slot, k, pl.ds(r, 1), :],
                                     sem.at[slot])

    def issue(tile, slot):
        for c in range(CMB_TT // CMB_UNROLL):
            tok0 = tile * CMB_TT + c * CMB_UNROLL
            rows = [[p[tok0 + u] for u in range(CMB_UNROLL)] for p in pos_refs]
            for u in range(CMB_UNROLL):
                for k in range(EXPERT_TOPK):
                    row_copy(rows[k][u], slot, k, c * CMB_UNROLL + u).start(priority=k % 2)

    @pl.when(i == 0)
    def _():
        issue(0, 0)

    @pl.when(i + 1 < pl.num_programs(0))
    def _():
        issue(i + 1, (i + 1) % CMB_SLOTS)

    slot = i % CMB_SLOTS
    for k in range(EXPERT_TOPK):
        pltpu.make_async_copy(y_ref.at[pl.ds(0, CMB_TT), :], buf_ref.at[slot, k], sem.at[slot]).wait()

    meta = meta_ref[...]
    x3 = (x2_ref[...] + meta[:, META_W1:META_W1 + 1] * buf_ref[slot, 0]
          + meta[:, META_W2:META_W2 + 1] * buf_ref[slot, 1])
    r = lax.rsqrt(jnp.mean(x3 * x3, axis=-1, keepdims=True) + RMS_EPS)
    o_ref[...] = x3 * r * nw_ref[...]


def _combine(pos, x2, meta, norm_w, y):
    t = x2.shape[0]
    return pl.pallas_call(
        _combine_kernel,
        grid_spec=pltpu.PrefetchScalarGridSpec(
            num_scalar_prefetch=2,
            grid=(t // CMB_TT,),
            in_specs=[
                pl.BlockSpec((CMB_TT, D_MODEL), lambda i, p0, p1: (i, 0)),
                pl.BlockSpec((CMB_TT, LANES), lambda i, p0, p1: (i, 0)),
                pl.BlockSpec((1, D_MODEL), lambda i, p0, p1: (0, 0)),
                pl.BlockSpec(memory_space=pl.ANY),
            ],
            out_specs=pl.BlockSpec((CMB_TT, D_MODEL), lambda i, p0, p1: (i, 0)),
            scratch_shapes=[
                pltpu.VMEM((CMB_SLOTS, EXPERT_TOPK, CMB_TT, D_MODEL), F32),
                pltpu.SemaphoreType.DMA((CMB_SLOTS,)),
            ],
        ),
        out_shape=jax.ShapeDtypeStruct((t, D_MODEL), F32),
        compiler_params=pltpu.CompilerParams(
            dimension_semantics=("arbitrary",), vmem_limit_bytes=VMEM_LIMIT),
        name="combine",
    )(pos[0], pos[1], x2, meta, norm_w, y)


def _layout(meta, counts, n_tokens):
    npad = n_tokens * EXPERT_TOPK + N_EXPERTS * MOE_TM
    counts = counts[0, :N_EXPERTS].astype(jnp.int32)
    padded = ((counts + MOE_TM - 1) // MOE_TM) * MOE_TM
    ends = jnp.cumsum(padded)
    starts = ends - padded
    ids = jnp.arange(N_EXPERTS, dtype=jnp.int32)
    pos = []
    for lane_x, lane_rank in ((META_X1, META_RANK1), (META_X2, META_RANK2)):
        hit = meta[:, lane_x].astype(jnp.int32)[:, None] == ids[None, :]
        pos.append(jnp.sum(jnp.where(hit, starts[None, :], 0), axis=-1) + meta[:, lane_rank].astype(jnp.int32))
    tile_start = jnp.arange(npad // MOE_TM, dtype=jnp.int32) * MOE_TM
    tile_expert = jnp.minimum(jnp.sum(ends[None, :] <= tile_start[:, None], axis=1), N_EXPERTS - 1)
    used = counts > 0
    k_of_expert = jnp.cumsum(used.astype(jnp.int32)) - 1
    expert_seq = jnp.sum(jnp.where(used[None, :] & (k_of_expert[None, :] == ids[:, None]), ids[None, :], 0), axis=1)
    tile_k = jnp.sum(jnp.where(tile_expert[:, None] == ids[None, :], k_of_expert[None, :], 0), axis=1)
    n_tiles = (ends[-1] // MOE_TM).astype(jnp.int32)
    n_used = jnp.stack([n_tiles, jnp.sum(used).astype(jnp.int32)])
    return (pos, starts + counts, tile_k.astype(jnp.int32), expert_seq.astype(jnp.int32), n_tiles.reshape(1),
            n_used, npad)


def kernel(x, norm_mix_w, w_in, lambda_q1, lambda_k1, lambda_q2, lambda_k2, diff_subln_w, w_out, norm_ffn_w,
           w_router_group, b_router_group, w_router_expert, b_router_expert, w_gate, w_up, w_down, norm_final_w):
    batch, seq, _ = x.shape
    assert seq % ATT_TQ == 0 and seq // MOBA_BLOCK + 8 <= LANES
    assert w_in.shape[0] == 1, "single-layer block"
    n_tokens = batch * seq
    x2d = x.reshape(n_tokens, D_MODEL)

    col = np.arange(IN_WIDTH)
    is_q = (col < MOBA_HEADS * HEAD_DIM) | ((col >= 3 * MOBA_HEADS * HEAD_DIM) & (col < 4 * MOBA_HEADS * HEAD_DIM))
    col_scale = jnp.asarray(np.where(is_q, QK_SCALE * LOG2E, 1.0).astype(np.float32))[None, :]

    proj = _in_proj(x2d, norm_mix_w[0][None, :], w_in[0], col_scale)
    out_a = _moba(proj, jnp.asarray(_alibi_slope_pieces(MOBA_HEADS)), batch, seq)
    out_b = _diff(proj, jnp.asarray(_alibi_slope_pieces(DIFF_HEADS)), lambda_q1, lambda_k1, lambda_q2, lambda_k2,
                  diff_subln_w, batch, seq)

    w_r = jnp.concatenate([w_router_group[0], w_router_expert[0]], axis=1)
    w_r = jnp.pad(w_r, ((0, 0), (0, LANES - w_r.shape[1])))
    wr_hi = w_r.astype(BF16)
    wr_lo = (w_r - wr_hi.astype(F32)).astype(BF16)
    b_r = jnp.pad(jnp.concatenate([b_router_group[0], b_router_expert[0]]), (0, LANES - N_GROUPS - N_EXPERTS))[None, :]
    x2, hp, meta, counts = _out_proj(x2d, out_a, out_b, w_out[0], norm_ffn_w[0][None, :],
                                     jnp.concatenate([wr_hi, wr_lo], axis=1), b_r)

    pos, pad_lo, tile_k, expert_seq, n_tiles, n_used, npad = _layout(meta, counts, n_tokens)
    xs = _dispatch(pos, pad_lo, n_tiles, hp, npad)
    y = _experts(tile_k, expert_seq, n_used, xs, w_gate[0], w_up[0], w_down[0])
    out = _combine(pos, x2, meta, norm_final_w[None, :], y)
    return out.reshape(batch, seq, D_MODEL)
```

```python
import functools

import ml_dtypes
import numpy as np
import jax
import jax.numpy as jnp
from jax import lax
from jax.experimental import pallas as pl
from jax.experimental.pallas import tpu as pltpu

F32 = jnp.float32
BF16 = jnp.bfloat16

D_MODEL = 2048
HEAD_DIM = 128
MOBA_HEADS = 8
MOBA_BLOCK = 256
MOBA_TOPK = 3
DIFF_HEADS = 4
DIFF_V_DIM = 256
IN_WIDTH = 6144
N_GROUPS = 4
EXPERTS_PER_GROUP = 8
N_EXPERTS = 32
EXPERT_TOPK = 2
D_EXPERT = 512
RMS_EPS = 1e-6
ALIBI_MAX_BIAS = 8.0
LAMBDA_INIT = 0.8 - 0.6 * float(np.exp(-0.3 * 0))

LANES = 128
QK_SCALE = HEAD_DIM ** -0.5
LOG2E = float(np.log2(np.e))
NEG_BIG = -1e30
M_INIT = -1e38
POS_SPLIT = 64
SLOPE_PIECES = 3
ROW_WORD_CHUNKS = D_MODEL // (2 * LANES)
META_X1, META_X2, META_W1, META_W2, META_RANK1, META_RANK2 = range(6)

IN_TM, IN_TN = 1024, 1024
ATT_TQ = 1024
ATT_TK = 512
KV_UNROLL = 2
HEADS_PER_STEP = 4
DIFF_HEADS_PER_STEP = 1
OUT_TM = 512
MOE_TM = 256
CMB_TT = 256
VMEM_LIMIT = 56 * 1024 * 1024


def _alibi_slope_pieces(n):
    rem = np.exp2(-ALIBI_MAX_BIAS * (np.arange(n, dtype=np.float64) + 1.0) / n) * LOG2E
    pieces = []
    for _ in range(SLOPE_PIECES):
        p = rem.astype(ml_dtypes.bfloat16).astype(np.float64)
        pieces.append(p)
        rem = rem - p
    return np.stack(pieces, axis=1).reshape(-1).astype(np.float32)


def _dot_nt(a, b):
    return lax.dot_general(a, b, (((1,), (1,)), ((), ())), preferred_element_type=F32)


def _dot(a, b):
    return jnp.dot(a, b, preferred_element_type=F32)


def _inproj_kernel(x_ref, nw_ref, cs_ref, w_ref, o_ref, h_ref):
    @pl.when(pl.program_id(1) == 0)
    def _():
        x = x_ref[...]
        r = lax.rsqrt(jnp.mean(x * x, axis=-1, keepdims=True) + RMS_EPS)
        h_ref[...] = (x * r * nw_ref[...]).astype(BF16)

    acc = _dot(h_ref[...], w_ref[...].astype(BF16))
    o_ref[...] = (acc * cs_ref[...]).astype(BF16)


def _in_proj(x2d, norm_w, w_in, col_scale):
    t = x2d.shape[0]
    return pl.pallas_call(
        _inproj_kernel,
        grid=(t // IN_TM, IN_WIDTH // IN_TN),
        in_specs=[
            pl.BlockSpec((IN_TM, D_MODEL), lambda i, j: (i, 0)),
            pl.BlockSpec((1, D_MODEL), lambda i, j: (0, 0)),
            pl.BlockSpec((1, IN_TN), lambda i, j: (0, j)),
            pl.BlockSpec((D_MODEL, IN_TN), lambda i, j: (0, j)),
        ],
        out_specs=pl.BlockSpec((IN_TM, IN_TN), lambda i, j: (i, j)),
        out_shape=jax.ShapeDtypeStruct((t, IN_WIDTH), BF16),
        scratch_shapes=[pltpu.VMEM((IN_TM, D_MODEL), BF16)],
        compiler_params=pltpu.CompilerParams(
            dimension_semantics=("arbitrary", "arbitrary"), vmem_limit_bytes=VMEM_LIMIT),
        name="in_proj",
    )(x2d, norm_w, col_scale, w_in)


def _key_position_features(seq, first_lane):
    pos = lax.broadcasted_iota(jnp.int32, (seq, LANES), 0)
    lane = lax.broadcasted_iota(jnp.int32, (seq, LANES), 1)
    hi = (lane >= first_lane) & (lane < first_lane + SLOPE_PIECES)
    lo = (lane >= first_lane + SLOPE_PIECES) & (lane < first_lane + 2 * SLOPE_PIECES)
    ext = jnp.where(hi, (pos // POS_SPLIT).astype(F32), 0.0)
    return jnp.where(lo, (pos % POS_SPLIT).astype(F32), ext), pos, lane


def _slope_feature_rows(feat_ref, head, width):
    row = lax.broadcasted_iota(jnp.int32, (8, width), 0)
    out = jnp.zeros((8, width), F32)
    for k in range(SLOPE_PIECES):
        c = feat_ref[head * SLOPE_PIECES + k]
        out = jnp.where(row == k, POS_SPLIT * c, out)
        out = jnp.where(row == SLOPE_PIECES + k, c, out)
    return out


def _causal_tile(s):
    row = lax.broadcasted_iota(jnp.int32, s.shape, 0)
    col = lax.broadcasted_iota(jnp.int32, s.shape, 1)
    return jnp.where(col <= row, s, -jnp.inf)


def _online_step(s, m):
    m_new = jnp.maximum(m, jnp.max(s, axis=-1, keepdims=True))
    return m_new, jnp.exp2(m - m_new), jnp.exp2(s - m_new)


def _past_tiles_loop(i, body, init):
    def trip(t, carry):
        for u in range(KV_UNROLL):
            carry = body(t * KV_UNROLL + u, carry)
        return carry

    return lax.fori_loop(0, i * (ATT_TQ // (ATT_TK * KV_UNROLL)), trip, init)


def _rows_from(state, start):
    return state if start == 0 else tuple(a[start:] for a in state)


def _rows_replace(state, start, tail):
    if start == 0:
        return tuple(tail)
    return tuple(jnp.concatenate([a[:start], b], axis=0) for a, b in zip(state, tail))


def _moba_kernel(feat_ref, q_ref, k_ref, v_ref, o_ref, kaug_ref, vaug_ref, kmh_ref, kml_ref, qaug_ref, *, seq):
    nb = seq // MOBA_BLOCK
    hp = pl.program_id(1)
    i = pl.program_id(2)

    @pl.when(i == 0)
    def _():
        ext, pos, lane = _key_position_features(seq, nb)
        ext = jnp.where(lane == pos // MOBA_BLOCK, 1.0, ext).astype(BF16)
        ones_col = jnp.where(lane == 0, 1.0, 0.0).astype(BF16)
        for g in range(HEADS_PER_STEP):
            k = k_ref[:, g * HEAD_DIM:(g + 1) * HEAD_DIM]
            kaug_ref[g, :, 0:HEAD_DIM] = k
            kaug_ref[g, :, HEAD_DIM:2 * HEAD_DIM] = ext
            vaug_ref[g, :, 0:HEAD_DIM] = v_ref[:, g * HEAD_DIM:(g + 1) * HEAD_DIM]
            vaug_ref[g, :, HEAD_DIM:2 * HEAD_DIM] = ones_col
            km = jnp.mean(k.astype(F32).reshape(nb, MOBA_BLOCK, HEAD_DIM), axis=1)
            kmh = km.astype(BF16)
            kmh_ref[g] = jnp.zeros((LANES, HEAD_DIM), BF16)
            kml_ref[g] = jnp.zeros((LANES, HEAD_DIM), BF16)
            kmh_ref[g, 0:nb, :] = kmh
            kml_ref[g, 0:nb, :] = (km - kmh.astype(F32)).astype(BF16)

    blk = lax.broadcasted_iota(jnp.int32, (nb, ATT_TQ), 0)
    col = lax.broadcasted_iota(jnp.int32, (nb, ATT_TQ), 1)
    own = i * (ATT_TQ // MOBA_BLOCK) + col // MOBA_BLOCK
    blk_f = blk.astype(F32)
    for g in range(HEADS_PER_STEP):
        q = q_ref[:, g * HEAD_DIM:(g + 1) * HEAD_DIM]
        gate = (_dot_nt(kmh_ref[g], q) + _dot_nt(kml_ref[g], q))[0:nb, :]
        gate = jnp.where(blk < own, gate, -jnp.inf)
        sel = jnp.zeros((nb, ATT_TQ), F32)
        for _ in range(MOBA_TOPK):
            mx = jnp.max(gate, axis=0, keepdims=True)
            idx = jnp.min(jnp.where(gate == mx, blk_f, float(LANES)), axis=0, keepdims=True)
            pick = (blk_f == idx) & (mx > -jnp.inf)
            sel = jnp.where(pick, 1.0, sel)
            gate = jnp.where(pick, -jnp.inf, gate)
        bias = jnp.where((sel > 0.0) | (blk == own), 0.0, NEG_BIG)
        feat = jnp.concatenate(
            [bias, _slope_feature_rows(feat_ref, hp * HEADS_PER_STEP + g, ATT_TQ),
             jnp.zeros((LANES - nb - 8, ATT_TQ), F32)], axis=0)
        qaug_ref[g, :, 0:HEAD_DIM] = q
        qaug_ref[g, :, HEAD_DIM:2 * HEAD_DIM] = feat.T.astype(BF16)

    def tile_step(g, r, state, rows=(0, ATT_TQ), nk=ATT_TK, causal=False):
        m, acc = state
        s = _dot_nt(qaug_ref[g, rows[0]:rows[1], :], kaug_ref[g, pl.ds(r, nk), :])
        if causal:
            row = lax.broadcasted_iota(jnp.int32, s.shape, 0) + rows[0]
            s = jnp.where(lax.broadcasted_iota(jnp.int32, s.shape, 1) <= row, s, -jnp.inf)
        m, alpha, p = _online_step(s, m)
        return m, alpha * acc + _dot(p.astype(BF16), vaug_ref[g, pl.ds(r, nk), :])

    def body(j, states):
        r = pl.multiple_of(j * ATT_TK, ATT_TK)
        return tuple(tile_step(g, r, states[g]) for g in range(HEADS_PER_STEP))

    init = tuple((jnp.full((ATT_TQ, 1), M_INIT, F32), jnp.zeros((ATT_TQ, 2 * HEAD_DIM), F32))
                 for _ in range(HEADS_PER_STEP))
    states = list(_past_tiles_loop(i, body, init))
    r0 = pl.multiple_of(i * ATT_TQ, ATT_TQ)
    for g in range(HEADS_PER_STEP):
        parts = [tile_step(g, r0, tuple(a[d * ATT_TK:(d + 1) * ATT_TK] for a in states[g]),
                           rows=(d * ATT_TK, (d + 1) * ATT_TK), nk=(d + 1) * ATT_TK, causal=True)
                 for d in range(ATT_TQ // ATT_TK)]
        states[g] = tuple(jnp.concatenate(cols, axis=0) for cols in zip(*parts))
    for g in range(HEADS_PER_STEP):
        acc = states[g][1]
        o_ref[:, g * HEAD_DIM:(g + 1) * HEAD_DIM] = (
            acc[:, 0:HEAD_DIM] / acc[:, HEAD_DIM:HEAD_DIM + 1]).astype(BF16)


def _moba(proj, feats, batch, seq):
    nq = seq // ATT_TQ
    width = HEADS_PER_STEP * HEAD_DIM
    q_col, k_col, v_col = 0, MOBA_HEADS // HEADS_PER_STEP, 2 * MOBA_HEADS // HEADS_PER_STEP
    return pl.pallas_call(
        functools.partial(_moba_kernel, seq=seq),
        grid_spec=pltpu.PrefetchScalarGridSpec(
            num_scalar_prefetch=1,
            grid=(batch, MOBA_HEADS // HEADS_PER_STEP, nq),
            in_specs=[
                pl.BlockSpec((ATT_TQ, width), lambda b, h, i, s: (b * nq + i, q_col + h)),
                pl.BlockSpec((seq, width), lambda b, h, i, s: (b, k_col + h)),
                pl.BlockSpec((seq, width), lambda b, h, i, s: (b, v_col + h)),
            ],
            out_specs=pl.BlockSpec((ATT_TQ, width), lambda b, h, i, s: (b * nq + i, h)),
            scratch_shapes=[
                pltpu.VMEM((HEADS_PER_STEP, seq, 2 * HEAD_DIM), BF16),
                pltpu.VMEM((HEADS_PER_STEP, seq, 2 * HEAD_DIM), BF16),
                pltpu.VMEM((HEADS_PER_STEP, LANES, HEAD_DIM), BF16),
                pltpu.VMEM((HEADS_PER_STEP, LANES, HEAD_DIM), BF16),
                pltpu.VMEM((HEADS_PER_STEP, ATT_TQ, 2 * HEAD_DIM), BF16),
            ],
        ),
        out_shape=jax.ShapeDtypeStruct((batch * seq, MOBA_HEADS * HEAD_DIM), BF16),
        compiler_params=pltpu.CompilerParams(
            dimension_semantics=("arbitrary", "arbitrary", "arbitrary"), vmem_limit_bytes=VMEM_LIMIT),
        name="moba",
    )(feats, proj, proj, proj)


def _diff_kernel(feat_ref, lq1_ref, lk1_ref, lq2_ref, lk2_ref, sw_ref,
                 q_ref, k_ref, v_ref, o_ref, kaug_ref, qaug_ref, *, seq):
    hg = pl.program_id(1)
    i = pl.program_id(2)
    n_chains = 2 * DIFF_HEADS_PER_STEP

    @pl.when(i == 0)
    def _():
        ext = _key_position_features(seq, 0)[0].astype(BF16)
        for n in range(n_chains):
            kaug_ref[n, :, 0:HEAD_DIM] = k_ref[:, n * HEAD_DIM:(n + 1) * HEAD_DIM]
            kaug_ref[n, :, HEAD_DIM:2 * HEAD_DIM] = ext

    for n in range(n_chains):
        if n % 2 == 0:
            head = hg * DIFF_HEADS_PER_STEP + n // 2
            feat = jnp.concatenate([_slope_feature_rows(feat_ref, head, ATT_TQ),
                                    jnp.zeros((LANES - 8, ATT_TQ), F32)], axis=0).T.astype(BF16)
        qaug_ref[n, :, 0:HEAD_DIM] = q_ref[:, n * HEAD_DIM:(n + 1) * HEAD_DIM]
        qaug_ref[n, :, HEAD_DIM:2 * HEAD_DIM] = feat

    def tile_step(n, r, state, row0=0, causal=False):
        m, l, acc = state
        s = _dot_nt(qaug_ref[n, row0:, :], kaug_ref[n, pl.ds(r, ATT_TK), :])
        if causal:
            s = _causal_tile(s)
        m, alpha, p = _online_step(s, m)
        v = v_ref[pl.ds(r, ATT_TK), (n // 2) * DIFF_V_DIM:(n // 2 + 1) * DIFF_V_DIM]
        p = p.astype(BF16)
        return (m, alpha * l + jnp.sum(p.astype(F32), axis=-1, keepdims=True), alpha * acc + _dot(p, v))

    def body(j, states):
        r = pl.multiple_of(j * ATT_TK, ATT_TK)
        return tuple(tile_step(n, r, states[n]) for n in range(n_chains))

    init = tuple((jnp.full((ATT_TQ, 1), M_INIT, F32), jnp.zeros((ATT_TQ, 1), F32),
                  jnp.zeros((ATT_TQ, DIFF_V_DIM), F32)) for _ in range(n_chains))
    states = list(_past_tiles_loop(i, body, init))
    for d in range(ATT_TQ // ATT_TK):
        r = pl.multiple_of(i * ATT_TQ + d * ATT_TK, ATT_TK)
        for n in range(n_chains):
            tail = tile_step(n, r, _rows_from(states[n], d * ATT_TK), row0=d * ATT_TK, causal=True)
            states[n] = _rows_replace(states[n], d * ATT_TK, tail)

    lam = (jnp.exp(jnp.sum(lq1_ref[...] * lk1_ref[...], axis=-1, keepdims=True))
           - jnp.exp(jnp.sum(lq2_ref[...] * lk2_ref[...], axis=-1, keepdims=True))
           + LAMBDA_INIT)
    for hh in range(DIFF_HEADS_PER_STEP):
        (_, l1, acc1), (_, l2, acc2) = states[2 * hh], states[2 * hh + 1]
        o = acc1 / l1 - lam * (acc2 / l2)
        y = o * lax.rsqrt(jnp.mean(o * o, axis=-1, keepdims=True) + RMS_EPS)
        o_ref[:, hh * DIFF_V_DIM:(hh + 1) * DIFF_V_DIM] = ((y * sw_ref[...]) * (1.0 - LAMBDA_INIT)).astype(BF16)


def _diff(proj, feats, lq1, lk1, lq2, lk2, subln_w, batch, seq):
    nq = seq // ATT_TQ
    width = DIFF_HEADS_PER_STEP * 2 * HEAD_DIM
    groups = DIFF_HEADS // DIFF_HEADS_PER_STEP
    q_col = 3 * MOBA_HEADS * HEAD_DIM // width
    k_col = q_col + groups
    v_col = k_col + groups
    vec = lambda n: pl.BlockSpec((1, n), lambda b, h, i, s: (0, 0))
    return pl.pallas_call(
        functools.partial(_diff_kernel, seq=seq),
        grid_spec=pltpu.PrefetchScalarGridSpec(
            num_scalar_prefetch=1,
            grid=(batch, groups, nq),
            in_specs=[
                vec(HEAD_DIM), vec(HEAD_DIM), vec(HEAD_DIM), vec(HEAD_DIM), vec(DIFF_V_DIM),
                pl.BlockSpec((ATT_TQ, width), lambda b, h, i, s: (b * nq + i, q_col + h)),
                pl.BlockSpec((seq, width), lambda b, h, i, s: (b, k_col + h)),
                pl.BlockSpec((seq, width), lambda b, h, i, s: (b, v_col + h)),
            ],
            out_specs=pl.BlockSpec((ATT_TQ, width), lambda b, h, i, s: (b * nq + i, h)),
            scratch_shapes=[
                pltpu.VMEM((2 * DIFF_HEADS_PER_STEP, seq, 2 * HEAD_DIM), BF16),
                pltpu.VMEM((2 * DIFF_HEADS_PER_STEP, ATT_TQ, 2 * HEAD_DIM), BF16),
            ],
        ),
        out_shape=jax.ShapeDtypeStruct((batch * seq, DIFF_HEADS * DIFF_V_DIM), BF16),
        compiler_params=pltpu.CompilerParams(
            dimension_semantics=("arbitrary", "arbitrary", "arbitrary"), vmem_limit_bytes=VMEM_LIMIT),
        name="diff_attn",
    )(feats, lq1, lk1, lq2, lk2, subln_w, proj, proj, proj)


def _route_tile(lg, run, tri):
    lane = lax.broadcasted_iota(jnp.int32, lg.shape, 1).astype(F32)
    rmax = lambda v: jnp.max(v, axis=-1, keepdims=True)
    first = lambda v, m: jnp.min(jnp.where(v == m, lane, float(LANES)), axis=-1, keepdims=True)

    gl = jnp.where(lane < N_GROUPS, lg, -jnp.inf)
    gmax = rmax(gl)
    group = first(gl, gmax)
    p_group = 1.0 / jnp.sum(jnp.exp(gl - gmax), axis=-1, keepdims=True)

    lo = N_GROUPS + EXPERTS_PER_GROUP * group
    el = jnp.where((lane >= lo) & (lane < lo + EXPERTS_PER_GROUP), lg, -jnp.inf)
    e1 = rmax(el)
    lane1 = first(el, e1)
    el = jnp.where(lane == lane1, -jnp.inf, el)
    e2 = rmax(el)
    lane2 = first(el, e2)
    t2 = jnp.exp(e2 - e1)
    w1 = p_group / (1.0 + t2)
    w2 = p_group * t2 / (1.0 + t2)
    x1 = lane1 - N_GROUPS
    x2 = lane2 - N_GROUPS

    hit1 = lane == x1
    hit2 = lane == x2
    onehot = jnp.where(hit1 | hit2, 1.0, 0.0)
    before = _dot(tri, onehot.astype(BF16)) + run
    rank1 = jnp.sum(jnp.where(hit1, before, 0.0), axis=-1, keepdims=True)
    rank2 = jnp.sum(jnp.where(hit2, before, 0.0), axis=-1, keepdims=True)

    meta = jnp.zeros(lg.shape, F32)
    for k, v in ((META_X1, x1), (META_X2, x2), (META_W1, w1), (META_W2, w2),
                 (META_RANK1, rank1), (META_RANK2, rank2)):
        meta = jnp.where(lane == k, v, meta)
    return meta, run + jnp.sum(onehot, axis=0, keepdims=True)


def _outproj_kernel(x_ref, oa_ref, ob_ref, wf_ref, nw_ref, wr_ref, br_ref,
                    x2_ref, hp_ref, meta_ref, cnt_ref, run_ref, tri_ref, w_ref):
    @pl.when(pl.program_id(0) == 0)
    def _():
        row = lax.broadcasted_iota(jnp.int32, tri_ref.shape, 0)
        col = lax.broadcasted_iota(jnp.int32, tri_ref.shape, 1)
        tri_ref[...] = jnp.where(col < row, 1.0, 0.0).astype(BF16)
        run_ref[...] = jnp.zeros(run_ref.shape, F32)
        w_ref[...] = wf_ref[...].astype(BF16)

    half = D_MODEL // 2
    x2 = x_ref[...] + _dot(oa_ref[...], w_ref[0:half, :]) + _dot(ob_ref[...], w_ref[half:, :])
    x2_ref[...] = x2
    r = lax.rsqrt(jnp.mean(x2 * x2, axis=-1, keepdims=True) + RMS_EPS)
    h = x2 * r * nw_ref[...]
    hh = h.astype(BF16)
    hl = (h - hh.astype(F32)).astype(BF16)
    prod = _dot(jnp.concatenate([hh, hl], axis=0), wr_ref[...])
    lg = prod[:OUT_TM, :LANES] + prod[OUT_TM:, :LANES] + prod[:OUT_TM, LANES:] + br_ref[...]
    meta, run = _route_tile(lg, run_ref[...], tri_ref[...])
    meta_ref[...] = meta
    run_ref[...] = run
    cnt_ref[...] = run
    bits = pltpu.bitcast(hh.astype(F32), jnp.uint32)
    for c in range(ROW_WORD_CHUNKS):
        lo = bits[:, 2 * c * LANES:(2 * c + 1) * LANES]
        hi = bits[:, (2 * c + 1) * LANES:(2 * c + 2) * LANES]
        hp_ref[pl.ds(c, OUT_TM, stride=ROW_WORD_CHUNKS), :] = (lo >> 16) | (hi & jnp.uint32(0xFFFF0000))


def _out_proj(x2d, out_a, out_b, w_o, norm_w, w_r, b_r):
    t = x2d.shape[0]
    half = D_MODEL // 2
    row = lambda n: pl.BlockSpec((OUT_TM, n), lambda i: (i, 0))
    full = lambda r, c: pl.BlockSpec((r, c), lambda i: (0, 0))
    return pl.pallas_call(
        _outproj_kernel,
        grid=(t // OUT_TM,),
        in_specs=[row(D_MODEL), row(half), row(half), pl.BlockSpec(memory_space=pltpu.VMEM),
                  full(1, D_MODEL), full(D_MODEL, 2 * LANES), full(1, LANES)],
        out_specs=[row(D_MODEL), pl.BlockSpec((OUT_TM * ROW_WORD_CHUNKS, LANES), lambda i: (i, 0)), row(LANES),
                   full(1, LANES)],
        out_shape=[jax.ShapeDtypeStruct((t, D_MODEL), F32),
                   jax.ShapeDtypeStruct((t * ROW_WORD_CHUNKS, LANES), jnp.uint32),
                   jax.ShapeDtypeStruct((t, LANES), F32),
                   jax.ShapeDtypeStruct((1, LANES), F32)],
        scratch_shapes=[pltpu.VMEM((1, LANES), F32), pltpu.VMEM((OUT_TM, OUT_TM), BF16),
                        pltpu.VMEM((D_MODEL, D_MODEL), BF16)],
        compiler_params=pltpu.CompilerParams(
            dimension_semantics=("arbitrary",), vmem_limit_bytes=VMEM_LIMIT),
        name="out_proj",
    )(x2d, out_a, out_b, w_o, norm_w, w_r, b_r)


GATHER_UNROLL = 16
DSP_ROWS = 8 * MOE_TM


def _dispatch_kernel(pos0_ref, pos1_ref, padlo_ref, nt_ref, hp_hbm, xs_ref, hp_ref, src_ref, sem):
    t = pl.program_id(0)
    pos_refs = (pos0_ref, pos1_ref)

    @pl.when(t == 0)
    def _():
        load = pltpu.make_async_copy(hp_hbm, hp_ref, sem)
        load.start()

        def clear(e, _):
            lo = padlo_ref[e]
            for u in range(MOE_TM):
                src_ref[lo + u] = 0
            return 0
        lax.fori_loop(0, N_EXPERTS, clear, 0)

        def place(c, _):
            tok0 = c * (GATHER_UNROLL // EXPERT_TOPK)
            rows = [p[tok0 + u] for u in range(GATHER_UNROLL // EXPERT_TOPK) for p in pos_refs]
            for n, row in enumerate(rows):
                src_ref[row] = tok0 + n // EXPERT_TOPK
            return 0
        lax.fori_loop(0, pos_refs[0].shape[0] * EXPERT_TOPK // GATHER_UNROLL, place, 0)
        load.wait()

    n_valid = jnp.clip(nt_ref[0] * MOE_TM - t * DSP_ROWS, 0, DSP_ROWS)

    def gather(c, _):
        toks = [src_ref[t * DSP_ROWS + c * GATHER_UNROLL + u] for u in range(GATHER_UNROLL)]
        for u in range(GATHER_UNROLL):
            xs_ref[c * GATHER_UNROLL + u] = hp_ref[toks[u]]
        return 0
    lax.fori_loop(0, n_valid // GATHER_UNROLL, gather, 0)

    def zero_tile(c, _):
        xs_ref[pl.ds(pl.multiple_of(c * MOE_TM, MOE_TM), MOE_TM)] = jnp.zeros(
            (MOE_TM,) + xs_ref.shape[1:], xs_ref.dtype)
        return 0
    lax.fori_loop(n_valid // MOE_TM, DSP_ROWS // MOE_TM, zero_tile, 0)


def _dispatch(pos, pad_lo, n_tiles, hp, npad):
    n_tokens = hp.shape[0] // ROW_WORD_CHUNKS
    hp3 = hp.reshape(n_tokens, ROW_WORD_CHUNKS, LANES)
    xs = pl.pallas_call(
        _dispatch_kernel,
        grid_spec=pltpu.PrefetchScalarGridSpec(
            num_scalar_prefetch=4,
            grid=(npad // DSP_ROWS,),
            in_specs=[pl.BlockSpec(memory_space=pl.ANY)],
            out_specs=pl.BlockSpec((DSP_ROWS, ROW_WORD_CHUNKS, LANES), lambda t, p0, p1, lo, n: (t, 0, 0)),
            scratch_shapes=[pltpu.VMEM(hp3.shape, hp3.dtype), pltpu.SMEM((npad + MOE_TM,), jnp.int32),
                            pltpu.SemaphoreType.DMA(())],
        ),
        out_shape=jax.ShapeDtypeStruct((npad, ROW_WORD_CHUNKS, LANES), hp.dtype),
        compiler_params=pltpu.CompilerParams(
            dimension_semantics=("arbitrary",), vmem_limit_bytes=VMEM_LIMIT),
        name="dispatch",
    )(pos[0], pos[1], pad_lo, n_tiles, hp3)
    return xs.reshape(npad * ROW_WORD_CHUNKS, LANES)


W_SLOTS = 2
EXP_TILES = 3


def _experts_kernel(tk_ref, eseq_ref, nt_ref, xs_ref, wg_hbm, wu_hbm, wd_hbm, y_ref,
                    wg_buf, wu_buf, wd_buf, x_ref, sem):
    n_used = nt_ref[1]
    hbm_bufs = ((wg_hbm, wg_buf), (wu_hbm, wu_buf), (wd_hbm, wd_buf))

    def weight_copies(k, slot):
        e = eseq_ref[k]
        return [pltpu.make_async_copy(hbm.at[e], buf.at[slot], sem.at[slot, n])
                for n, (hbm, buf) in enumerate(hbm_bufs)]

    for sub in range(EXP_TILES):
        _expert_tile(pl.program_id(0) * EXP_TILES + sub, sub, tk_ref, nt_ref, n_used, weight_copies,
                     xs_ref, y_ref, wg_buf, wu_buf, wd_buf, x_ref)


def _expert_tile(t, sub, tk_ref, nt_ref, n_used, weight_copies,
                 xs_ref, y_ref, wg_buf, wu_buf, wd_buf, x_ref):
    xs_row0 = sub * MOE_TM * ROW_WORD_CHUNKS
    y_rows = pl.ds(sub * MOE_TM, MOE_TM)

    @pl.when(t < nt_ref[0])
    def _():
        k = tk_ref[t]
        slot = k % W_SLOTS

        @pl.when((t == 0) | (k != tk_ref[jnp.maximum(t - 1, 0)]))
        def _():
            @pl.when(k == 0)
            def _():
                for s in range(W_SLOTS):
                    @pl.when(s < n_used)
                    def _():
                        for c in weight_copies(s, s):
                            c.start()

            for c in weight_copies(k, slot):
                c.wait()

        for c in range(ROW_WORD_CHUNKS):
            u32 = xs_ref[pl.ds(xs_row0 + c, MOE_TM, stride=ROW_WORD_CHUNKS), :]
            x_ref[:, 2 * c * LANES:(2 * c + 1) * LANES] = pltpu.bitcast(u32 << 16, F32).astype(BF16)
            x_ref[:, (2 * c + 1) * LANES:(2 * c + 2) * LANES] = (
                pltpu.bitcast(u32 & jnp.uint32(0xFFFF0000), F32).astype(BF16))
        x = x_ref[...]
        a = _dot(x, wg_buf[slot].astype(BF16))
        u = _dot(x, wu_buf[slot].astype(BF16))
        act = (a * jax.nn.sigmoid(a)) * u
        y_ref[y_rows, :] = _dot(act.astype(BF16), wd_buf[slot].astype(BF16))

        last_tile = (t + 1 == nt_ref[0]) | (k != tk_ref[jnp.minimum(t + 1, tk_ref.shape[0] - 1)])

        @pl.when(last_tile & (k + W_SLOTS < n_used))
        def _():
            for c in weight_copies(k + W_SLOTS, slot):
                c.start()

    @pl.when(t >= nt_ref[0])
    def _():
        y_ref[y_rows, :] = jnp.zeros((MOE_TM, D_MODEL), y_ref.dtype)


def _experts(tile_k, expert_seq, n_used, xs, w_gate, w_up, w_down):
    npad = xs.shape[0] // ROW_WORD_CHUNKS
    rows = EXP_TILES * MOE_TM
    assert npad % rows == 0 and npad % DSP_ROWS == 0
    last = lambda t, nt: jnp.minimum(t, jnp.maximum(nt[0] - 1, 0) // EXP_TILES)
    hbm = pl.BlockSpec(memory_space=pl.ANY)
    return pl.pallas_call(
        _experts_kernel,
        grid_spec=pltpu.PrefetchScalarGridSpec(
            num_scalar_prefetch=3,
            grid=(npad // rows,),
            in_specs=[
                pl.BlockSpec((rows * ROW_WORD_CHUNKS, LANES), lambda t, tk, es, nt: (last(t, nt), 0)),
                hbm, hbm, hbm,
            ],
            out_specs=pl.BlockSpec((rows, D_MODEL), lambda t, tk, es, nt: (t, 0)),
            scratch_shapes=[
                pltpu.VMEM((W_SLOTS, D_MODEL, D_EXPERT), F32),
                pltpu.VMEM((W_SLOTS, D_MODEL, D_EXPERT), F32),
                pltpu.VMEM((W_SLOTS, D_EXPERT, D_MODEL), F32),
                pltpu.VMEM((MOE_TM, D_MODEL), BF16),
                pltpu.SemaphoreType.DMA((W_SLOTS, 3)),
            ],
        ),
        out_shape=jax.ShapeDtypeStruct((npad, D_MODEL), F32),
        compiler_params=pltpu.CompilerParams(
            dimension_semantics=("arbitrary",), vmem_limit_bytes=VMEM_LIMIT),
        name="experts",
    )(tile_k, expert_seq, n_used, xs, w_gate, w_up, w_down)


CMB_UNROLL = 8
CMB_SLOTS = 2


def _combine_kernel(pos0_ref, pos1_ref, x2_ref, meta_ref, nw_ref, y_ref, o_ref, buf_ref, sem):
    i = pl.program_id(0)
    pos_refs = (pos0_ref, pos1_ref)

    def row_copy(p, slot, k, r):
        return pltpu.make_async_copy(y_ref.at[pl.ds(p, 1), :], buf_ref.at[slot, k, pl.ds(r, 1), :],
                                     sem.at[slot])

    def issue(tile, slot):
        for c in range(CMB_TT // CMB_UNROLL):
            tok0 = tile * CMB_TT + c * CMB_UNROLL
            rows = [[p[tok0 + u] for u in range(CMB_UNROLL)] for p in pos_refs]
            for u in range(CMB_UNROLL):
                for k in range(EXPERT_TOPK):
                    row_copy(rows[k][u], slot, k, c * CMB_UNROLL + u).start(priority=k % 2)

    @pl.when(i == 0)
    def _():
        issue(0, 0)

    @pl.when(i + 1 < pl.num_programs(0))
    def _():
        issue(i + 1, (i + 1) % CMB_SLOTS)

    slot = i % CMB_SLOTS
    for k in range(EXPERT_TOPK):
        pltpu.make_async_copy(y_ref.at[pl.ds(0, CMB_TT), :], buf_ref.at[slot, k], sem.at[slot]).wait()

    meta = meta_ref[...]
    x3 = (x2_ref[...] + meta[:, META_W1:META_W1 + 1] * buf_ref[slot, 0]
          + meta[:, META_W2:META_W2 + 1] * buf_ref[slot, 1])
    r = lax.rsqrt(jnp.mean(x3 * x3, axis=-1, keepdims=True) + RMS_EPS)
    o_ref[...] = x3 * r * nw_ref[...]


def _combine(pos, x2, meta, norm_w, y):
    t = x2.shape[0]
    return pl.pallas_call(
        _combine_kernel,
        grid_spec=pltpu.PrefetchScalarGridSpec(
            num_scalar_prefetch=2,
            grid=(t // CMB_TT,),
            in_specs=[
                pl.BlockSpec((CMB_TT, D_MODEL), lambda i, p0, p1: (i, 0)),
                pl.BlockSpec((CMB_TT, LANES), lambda i, p0, p1: (i, 0)),
                pl.BlockSpec((1, D_MODEL), lambda i, p0, p1: (0, 0)),
                pl.BlockSpec(memory_space=pl.ANY),
            ],
            out_specs=pl.BlockSpec((CMB_TT, D_MODEL), lambda i, p0, p1: (i, 0)),
            scratch_shapes=[
                pltpu.VMEM((CMB_SLOTS, EXPERT_TOPK, CMB_TT, D_MODEL), F32),
                pltpu.SemaphoreType.DMA((CMB_SLOTS,)),
            ],
        ),
        out_shape=jax.ShapeDtypeStruct((t, D_MODEL), F32),
        compiler_params=pltpu.CompilerParams(
            dimension_semantics=("arbitrary",), vmem_limit_bytes=VMEM_LIMIT),
        name="combine",
    )(pos[0], pos[1], x2, meta, norm_w, y)


def _layout(meta, counts, n_tokens):
    npad = n_tokens * EXPERT_TOPK + N_EXPERTS * MOE_TM
    counts = counts[0, :N_EXPERTS].astype(jnp.int32)
    padded = ((counts + MOE_TM - 1) // MOE_TM) * MOE_TM
    ends = jnp.cumsum(padded)
    starts = ends - padded
    ids = jnp.arange(N_EXPERTS, dtype=jnp.int32)
    pos = []
    for lane_x, lane_rank in ((META_X1, META_RANK1), (META_X2, META_RANK2)):
        hit = meta[:, lane_x].astype(jnp.int32)[:, None] == ids[None, :]
        pos.append(jnp.sum(jnp.where(hit, starts[None, :], 0), axis=-1) + meta[:, lane_rank].astype(jnp.int32))
    tile_start = jnp.arange(npad // MOE_TM, dtype=jnp.int32) * MOE_TM
    tile_expert = jnp.minimum(jnp.sum(ends[None, :] <= tile_start[:, None], axis=1), N_EXPERTS - 1)
    used = counts > 0
    k_of_expert = jnp.cumsum(used.astype(jnp.int32)) - 1
    expert_seq = jnp.sum(jnp.where(used[None, :] & (k_of_expert[None, :] == ids[:, None]), ids[None, :], 0), axis=1)
    tile_k = jnp.sum(jnp.where(tile_expert[:, None] == ids[None, :], k_of_expert[None, :], 0), axis=1)
    n_tiles = (ends[-1] // MOE_TM).astype(jnp.int32)
    n_used = jnp.stack([n_tiles, jnp.sum(used).astype(jnp.int32)])
    return (pos, starts + counts, tile_k.astype(jnp.int32), expert_seq.astype(jnp.int32), n_tiles.reshape(1),
            n_used, npad)


def kernel(x, norm_mix_w, w_in, lambda_q1, lambda_k1, lambda_q2, lambda_k2, diff_subln_w, w_out, norm_ffn_w,
           w_router_group, b_router_group, w_router_expert, b_router_expert, w_gate, w_up, w_down, norm_final_w):
    batch, seq, _ = x.shape
    assert seq % ATT_TQ == 0 and seq // MOBA_BLOCK + 8 <= LANES
    assert w_in.shape[0] == 1, "single-layer block"
    n_tokens = batch * seq
    x2d = x.reshape(n_tokens, D_MODEL)

    col = np.arange(IN_WIDTH)
    is_q = (col < MOBA_HEADS * HEAD_DIM) | ((col >= 3 * MOBA_HEADS * HEAD_DIM) & (col < 4 * MOBA_HEADS * HEAD_DIM))
    col_scale = jnp.asarray(np.where(is_q, QK_SCALE * LOG2E, 1.0).astype(np.float32))[None, :]

    proj = _in_proj(x2d, norm_mix_w[0][None, :], w_in[0], col_scale)
    out_a = _moba(proj, jnp.asarray(_alibi_slope_pieces(MOBA_HEADS)), batch, seq)
    out_b = _diff(proj, jnp.asarray(_alibi_slope_pieces(DIFF_HEADS)), lambda_q1, lambda_k1, lambda_q2, lambda_k2,
                  diff_subln_w, batch, seq)

    w_r = jnp.concatenate([w_router_group[0], w_router_expert[0]], axis=1)
    w_r = jnp.pad(w_r, ((0, 0), (0, LANES - w_r.shape[1])))
    wr_hi = w_r.astype(BF16)
    wr_lo = (w_r - wr_hi.astype(F32)).astype(BF16)
    b_r = jnp.pad(jnp.concatenate([b_router_group[0], b_router_expert[0]]), (0, LANES - N_GROUPS - N_EXPERTS))[None, :]
    x2, hp, meta, counts = _out_proj(x2d, out_a, out_b, w_out[0], norm_ffn_w[0][None, :],
                                     jnp.concatenate([wr_hi, wr_lo], axis=1), b_r)

    pos, pad_lo, tile_k, expert_seq, n_tiles, n_used, npad = _layout(meta, counts, n_tokens)
    xs = _dispatch(pos, pad_lo, n_tiles, hp, npad)
    y = _experts(tile_k, expert_seq, n_used, xs, w_gate[0], w_up[0], w_down[0])
    out = _combine(pos, x2, meta, norm_final_w[None, :], y)
    return out.reshape(batch, seq, D_MODEL)
```

```python
import functools

import ml_dtypes
import numpy as np
import jax
import jax.numpy as jnp
from jax import lax
from jax.experimental import pallas as pl
from jax.experimental.pallas import tpu as pltpu

F32 = jnp.float32
BF16 = jnp.bfloat16

D_MODEL = 2048
HEAD_DIM = 128
MOBA_HEADS = 8
MOBA_BLOCK = 256
MOBA_TOPK = 3
DIFF_HEADS = 4
DIFF_V_DIM = 256
IN_WIDTH = 6144
N_GROUPS = 4
EXPERTS_PER_GROUP = 8
N_EXPERTS = 32
EXPERT_TOPK = 2
D_EXPERT = 512
RMS_EPS = 1e-6
ALIBI_MAX_BIAS = 8.0
LAMBDA_INIT = 0.8 - 0.6 * float(np.exp(-0.3 * 0))

LANES = 128
QK_SCALE = HEAD_DIM ** -0.5
LOG2E = float(np.log2(np.e))
NEG_BIG = -1e30
M_INIT = -1e38
POS_SPLIT = 64
SLOPE_PIECES = 3
ROW_WORD_CHUNKS = D_MODEL // (2 * LANES)
META_X1, META_X2, META_W1, META_W2, META_RANK1, META_RANK2 = range(6)

IN_TM, IN_TN = 1024, 1024
ATT_TQ = 1024
ATT_TK = 512
KV_UNROLL = 2
HEADS_PER_STEP = 4
DIFF_HEADS_PER_STEP = 1
OUT_TM = 512
MOE_TM = 256
CMB_TT = 256
VMEM_LIMIT = 56 * 1024 * 1024


def _alibi_slope_pieces(n):
    rem = np.exp2(-ALIBI_MAX_BIAS * (np.arange(n, dtype=np.float64) + 1.0) / n) * LOG2E
    pieces = []
    for _ in range(SLOPE_PIECES):
        p = rem.astype(ml_dtypes.bfloat16).astype(np.float64)
        pieces.append(p)
        rem = rem - p
    return np.stack(pieces, axis=1).reshape(-1).astype(np.float32)


def _dot_nt(a, b):
    return lax.dot_general(a, b, (((1,), (1,)), ((), ())), preferred_element_type=F32)


def _dot(a, b):
    return jnp.dot(a, b, preferred_element_type=F32)


def _inproj_kernel(x_ref, nw_ref, cs_ref, w_ref, o_ref, h_ref):
    @pl.when(pl.program_id(1) == 0)
    def _():
        x = x_ref[...]
        r = lax.rsqrt(jnp.mean(x * x, axis=-1, keepdims=True) + RMS_EPS)
        h_ref[...] = (x * r * nw_ref[...]).astype(BF16)

    acc = _dot(h_ref[...], w_ref[...].astype(BF16))
    o_ref[...] = (acc * cs_ref[...]).astype(BF16)


def _in_proj(x2d, norm_w, w_in, col_scale):
    t = x2d.shape[0]
    return pl.pallas_call(
        _inproj_kernel,
        grid=(t // IN_TM, IN_WIDTH // IN_TN),
        in_specs=[
            pl.BlockSpec((IN_TM, D_MODEL), lambda i, j: (i, 0)),
            pl.BlockSpec((1, D_MODEL), lambda i, j: (0, 0)),
            pl.BlockSpec((1, IN_TN), lambda i, j: (0, j)),
            pl.BlockSpec((D_MODEL, IN_TN), lambda i, j: (0, j)),
        ],
        out_specs=pl.BlockSpec((IN_TM, IN_TN), lambda i, j: (i, j)),
        out_shape=jax.ShapeDtypeStruct((t, IN_WIDTH), BF16),
        scratch_shapes=[pltpu.VMEM((IN_TM, D_MODEL), BF16)],
        compiler_params=pltpu.CompilerParams(
            dimension_semantics=("arbitrary", "arbitrary"), vmem_limit_bytes=VMEM_LIMIT),
        name="in_proj",
    )(x2d, norm_w, col_scale, w_in)


def _key_position_features(seq, first_lane):
    pos = lax.broadcasted_iota(jnp.int32, (seq, LANES), 0)
    lane = lax.broadcasted_iota(jnp.int32, (seq, LANES), 1)
    hi = (lane >= first_lane) & (lane < first_lane + SLOPE_PIECES)
    lo = (lane >= first_lane + SLOPE_PIECES) & (lane < first_lane + 2 * SLOPE_PIECES)
    ext = jnp.where(hi, (pos // POS_SPLIT).astype(F32), 0.0)
    return jnp.where(lo, (pos % POS_SPLIT).astype(F32), ext), pos, lane


def _slope_feature_rows(feat_ref, head, width):
    row = lax.broadcasted_iota(jnp.int32, (8, width), 0)
    out = jnp.zeros((8, width), F32)
    for k in range(SLOPE_PIECES):
        c = feat_ref[head * SLOPE_PIECES + k]
        out = jnp.where(row == k, POS_SPLIT * c, out)
        out = jnp.where(row == SLOPE_PIECES + k, c, out)
    return out


def _causal_tile(s):
    row = lax.broadcasted_iota(jnp.int32, s.shape, 0)
    col = lax.broadcasted_iota(jnp.int32, s.shape, 1)
    return jnp.where(col <= row, s, -jnp.inf)


def _online_step(s, m):
    m_new = jnp.maximum(m, jnp.max(s, axis=-1, keepdims=True))
    return m_new, jnp.exp2(m - m_new), jnp.exp2(s - m_new)


def _past_tiles_loop(i, body, init):
    def trip(t, carry):
        for u in range(KV_UNROLL):
            carry = body(t * KV_UNROLL + u, carry)
        return carry

    return lax.fori_loop(0, i * (ATT_TQ // (ATT_TK * KV_UNROLL)), trip, init)


def _rows_from(state, start):
    return state if start == 0 else tuple(a[start:] for a in state)


def _rows_replace(state, start, tail):
    if start == 0:
        return tuple(tail)
    return tuple(jnp.concatenate([a[:start], b], axis=0) for a, b in zip(state, tail))


def _moba_kernel(feat_ref, q_ref, k_ref, v_ref, o_ref, kaug_ref, vaug_ref, kmh_ref, kml_ref, qaug_ref, *, seq):
    nb = seq // MOBA_BLOCK
    hp = pl.program_id(1)
    i = pl.program_id(2)

    @pl.when(i == 0)
    def _():
        ext, pos, lane = _key_position_features(seq, nb)
        ext = jnp.where(lane == pos // MOBA_BLOCK, 1.0, ext).astype(BF16)
        ones_col = jnp.where(lane == 0, 1.0, 0.0).astype(BF16)
        for g in range(HEADS_PER_STEP):
            k = k_ref[:, g * HEAD_DIM:(g + 1) * HEAD_DIM]
            kaug_ref[g, :, 0:HEAD_DIM] = k
            kaug_ref[g, :, HEAD_DIM:2 * HEAD_DIM] = ext
            vaug_ref[g, :, 0:HEAD_DIM] = v_ref[:, g * HEAD_DIM:(g + 1) * HEAD_DIM]
            vaug_ref[g, :, HEAD_DIM:2 * HEAD_DIM] = ones_col
            km = jnp.mean(k.astype(F32).reshape(nb, MOBA_BLOCK, HEAD_DIM), axis=1)
            kmh = km.astype(BF16)
            kmh_ref[g] = jnp.zeros((LANES, HEAD_DIM), BF16)
            kml_ref[g] = jnp.zeros((LANES, HEAD_DIM), BF16)
            kmh_ref[g, 0:nb, :] = kmh
            kml_ref[g, 0:nb, :] = (km - kmh.astype(F32)).astype(BF16)

    blk = lax.broadcasted_iota(jnp.int32, (nb, ATT_TQ), 0)
    col = lax.broadcasted_iota(jnp.int32, (nb, ATT_TQ), 1)
    own = i * (ATT_TQ // MOBA_BLOCK) + col // MOBA_BLOCK
    blk_f = blk.astype(F32)
    for g in range(HEADS_PER_STEP):
        q = q_ref[:, g * HEAD_DIM:(g + 1) * HEAD_DIM]
        gate = (_dot_nt(kmh_ref[g], q) + _dot_nt(kml_ref[g], q))[0:nb, :]
        gate = jnp.where(blk < own, gate, -jnp.inf)
        sel = jnp.zeros((nb, ATT_TQ), F32)
        for _ in range(MOBA_TOPK):
            mx = jnp.max(gate, axis=0, keepdims=True)
            idx = jnp.min(jnp.where(gate == mx, blk_f, float(LANES)), axis=0, keepdims=True)
            pick = (blk_f == idx) & (mx > -jnp.inf)
            sel = jnp.where(pick, 1.0, sel)
            gate = jnp.where(pick, -jnp.inf, gate)
        bias = jnp.where((sel > 0.0) | (blk == own), 0.0, NEG_BIG)
        feat = jnp.concatenate(
            [bias, _slope_feature_rows(feat_ref, hp * HEADS_PER_STEP + g, ATT_TQ),
             jnp.zeros((LANES - nb - 8, ATT_TQ), F32)], axis=0)
        qaug_ref[g, :, 0:HEAD_DIM] = q
        qaug_ref[g, :, HEAD_DIM:2 * HEAD_DIM] = feat.T.astype(BF16)

    def tile_step(g, r, state, rows=(0, ATT_TQ), nk=ATT_TK, causal=False):
        m, acc = state
        s = _dot_nt(qaug_ref[g, rows[0]:rows[1], :], kaug_ref[g, pl.ds(r, nk), :])
        if causal:
            row = lax.broadcasted_iota(jnp.int32, s.shape, 0) + rows[0]
            s = jnp.where(lax.broadcasted_iota(jnp.int32, s.shape, 1) <= row, s, -jnp.inf)
        m, alpha, p = _online_step(s, m)
        return m, alpha * acc + _dot(p.astype(BF16), vaug_ref[g, pl.ds(r, nk), :])

    def body(j, states):
        r = pl.multiple_of(j * ATT_TK, ATT_TK)
        return tuple(tile_step(g, r, states[g]) for g in range(HEADS_PER_STEP))

    init = tuple((jnp.full((ATT_TQ, 1), M_INIT, F32), jnp.zeros((ATT_TQ, 2 * HEAD_DIM), F32))
                 for _ in range(HEADS_PER_STEP))
    states = list(_past_tiles_loop(i, body, init))
    r0 = pl.multiple_of(i * ATT_TQ, ATT_TQ)
    for g in range(HEADS_PER_STEP):
        parts = [tile_step(g, r0, tuple(a[d * ATT_TK:(d + 1) * ATT_TK] for a in states[g]),
                           rows=(d * ATT_TK, (d + 1) * ATT_TK), nk=(d + 1) * ATT_TK, causal=True)
                 for d in range(ATT_TQ // ATT_TK)]
        states[g] = tuple(jnp.concatenate(cols, axis=0) for cols in zip(*parts))
    for g in range(HEADS_PER_STEP):
        acc = states[g][1]
        o_ref[:, g * HEAD_DIM:(g + 1) * HEAD_DIM] = (
            acc[:, 0:HEAD_DIM] / acc[:, HEAD_DIM:HEAD_DIM + 1]).astype(BF16)


def _moba(proj, feats, batch, seq):
    nq = seq // ATT_TQ
    width = HEADS_PER_STEP * HEAD_DIM
    q_col, k_col, v_col = 0, MOBA_HEADS // HEADS_PER_STEP, 2 * MOBA_HEADS // HEADS_PER_STEP
    return pl.pallas_call(
        functools.partial(_moba_kernel, seq=seq),
        grid_spec=pltpu.PrefetchScalarGridSpec(
            num_scalar_prefetch=1,
            grid=(batch, MOBA_HEADS // HEADS_PER_STEP, nq),
            in_specs=[
                pl.BlockSpec((ATT_TQ, width), lambda b, h, i, s: (b * nq + i, q_col + h)),
                pl.BlockSpec((seq, width), lambda b, h, i, s: (b, k_col + h)),
                pl.BlockSpec((seq, width), lambda b, h, i, s: (b, v_col + h)),
            ],
            out_specs=pl.BlockSpec((ATT_TQ, width), lambda b, h, i, s: (b * nq + i, h)),
            scratch_shapes=[
                pltpu.VMEM((HEADS_PER_STEP, seq, 2 * HEAD_DIM), BF16),
                pltpu.VMEM((HEADS_PER_STEP, seq, 2 * HEAD_DIM), BF16),
                pltpu.VMEM((HEADS_PER_STEP, LANES, HEAD_DIM), BF16),
                pltpu.VMEM((HEADS_PER_STEP, LANES, HEAD_DIM), BF16),
                pltpu.VMEM((HEADS_PER_STEP, ATT_TQ, 2 * HEAD_DIM), BF16),
            ],
        ),
        out_shape=jax.ShapeDtypeStruct((batch * seq, MOBA_HEADS * HEAD_DIM), BF16),
        compiler_params=pltpu.CompilerParams(
            dimension_semantics=("arbitrary", "arbitrary", "arbitrary"), vmem_limit_bytes=VMEM_LIMIT),
        name="moba",
    )(feats, proj, proj, proj)


def _diff_kernel(feat_ref, lq1_ref, lk1_ref, lq2_ref, lk2_ref, sw_ref,
                 q_ref, k_ref, v_ref, o_ref, kaug_ref, qaug_ref, *, seq):
    hg = pl.program_id(1)
    i = pl.program_id(2)
    n_chains = 2 * DIFF_HEADS_PER_STEP

    @pl.when(i == 0)
    def _():
        ext = _key_position_features(seq, 0)[0].astype(BF16)
        for n in range(n_chains):
            kaug_ref[n, :, 0:HEAD_DIM] = k_ref[:, n * HEAD_DIM:(n + 1) * HEAD_DIM]
            kaug_ref[n, :, HEAD_DIM:2 * HEAD_DIM] = ext

    for n in range(n_chains):
        if n % 2 == 0:
            head = hg * DIFF_HEADS_PER_STEP + n // 2
            feat = jnp.concatenate([_slope_feature_rows(feat_ref, head, ATT_TQ),
                                    jnp.zeros((LANES - 8, ATT_TQ), F32)], axis=0).T.astype(BF16)
        qaug_ref[n, :, 0:HEAD_DIM] = q_ref[:, n * HEAD_DIM:(n + 1) * HEAD_DIM]
        qaug_ref[n, :, HEAD_DIM:2 * HEAD_DIM] = feat

    def tile_step(n, r, state, row0=0, causal=False):
        m, l, acc = state
        s = _dot_nt(qaug_ref[n, row0:, :], kaug_ref[n, pl.ds(r, ATT_TK), :])
        if causal:
            s = _causal_tile(s)
        m, alpha, p = _online_step(s, m)
        v = v_ref[pl.ds(r, ATT_TK), (n // 2) * DIFF_V_DIM:(n // 2 + 1) * DIFF_V_DIM]
        p = p.astype(BF16)
        return (m, alpha * l + jnp.sum(p.astype(F32), axis=-1, keepdims=True), alpha * acc + _dot(p, v))

    def body(j, states):
        r = pl.multiple_of(j * ATT_TK, ATT_TK)
        return tuple(tile_step(n, r, states[n]) for n in range(n_chains))

    init = tuple((jnp.full((ATT_TQ, 1), M_INIT, F32), jnp.zeros((ATT_TQ, 1), F32),
                  jnp.zeros((ATT_TQ, DIFF_V_DIM), F32)) for _ in range(n_chains))
    states = list(_past_tiles_loop(i, body, init))
    for d in range(ATT_TQ // ATT_TK):
        r = pl.multiple_of(i * ATT_TQ + d * ATT_TK, ATT_TK)
        for n in range(n_chains):
            tail = tile_step(n, r, _rows_from(states[n], d * ATT_TK), row0=d * ATT_TK, causal=True)
            states[n] = _rows_replace(states[n], d * ATT_TK, tail)

    lam = (jnp.exp(jnp.sum(lq1_ref[...] * lk1_ref[...], axis=-1, keepdims=True))
           - jnp.exp(jnp.sum(lq2_ref[...] * lk2_ref[...], axis=-1, keepdims=True))
           + LAMBDA_INIT)
    for hh in range(DIFF_HEADS_PER_STEP):
        (_, l1, acc1), (_, l2, acc2) = states[2 * hh], states[2 * hh + 1]
        o = acc1 / l1 - lam * (acc2 / l2)
        y = o * lax.rsqrt(jnp.mean(o * o, axis=-1, keepdims=True) + RMS_EPS)
        o_ref[:, hh * DIFF_V_DIM:(hh + 1) * DIFF_V_DIM] = ((y * sw_ref[...]) * (1.0 - LAMBDA_INIT)).astype(BF16)


def _diff(proj, feats, lq1, lk1, lq2, lk2, subln_w, batch, seq):
    nq = seq // ATT_TQ
    width = DIFF_HEADS_PER_STEP * 2 * HEAD_DIM
    groups = DIFF_HEADS // DIFF_HEADS_PER_STEP
    q_col = 3 * MOBA_HEADS * HEAD_DIM // width
    k_col = q_col + groups
    v_col = k_col + groups
    vec = lambda n: pl.BlockSpec((1, n), lambda b, h, i, s: (0, 0))
    return pl.pallas_call(
        functools.partial(_diff_kernel, seq=seq),
        grid_spec=pltpu.PrefetchScalarGridSpec(
            num_scalar_prefetch=1,
            grid=(batch, groups, nq),
            in_specs=[
                vec(HEAD_DIM), vec(HEAD_DIM), vec(HEAD_DIM), vec(HEAD_DIM), vec(DIFF_V_DIM),
                pl.BlockSpec((ATT_TQ, width), lambda b, h, i, s: (b * nq + i, q_col + h)),
                pl.BlockSpec((seq, width), lambda b, h, i, s: (b, k_col + h)),
                pl.BlockSpec((seq, width), lambda b, h, i, s: (b, v_col + h)),
            ],
            out_specs=pl.BlockSpec((ATT_TQ, width), lambda b, h, i, s: (b * nq + i, h)),
            scratch_shapes=[
                pltpu.VMEM((2 * DIFF_HEADS_PER_STEP, seq, 2 * HEAD_DIM), BF16),
                pltpu.VMEM((2 * DIFF_HEADS_PER_STEP, ATT_TQ, 2 * HEAD_DIM), BF16),
            ],
        ),
        out_shape=jax.ShapeDtypeStruct((batch * seq, DIFF_HEADS * DIFF_V_DIM), BF16),
        compiler_params=pltpu.CompilerParams(
            dimension_semantics=("arbitrary", "arbitrary", "arbitrary"), vmem_limit_bytes=VMEM_LIMIT),
        name="diff_attn",
    )(feats, lq1, lk1, lq2, lk2, subln_w, proj, proj, proj)


def _route_tile(lg, run, tri):
    lane = lax.broadcasted_iota(jnp.int32, lg.shape, 1).astype(F32)
    rmax = lambda v: jnp.max(v, axis=-1, keepdims=True)
    first = lambda v, m: jnp.min(jnp.where(v == m, lane, float(LANES)), axis=-1, keepdims=True)

    gl = jnp.where(lane < N_GROUPS, lg, -jnp.inf)
    gmax = rmax(gl)
    group = first(gl, gmax)
    p_group = 1.0 / jnp.sum(jnp.exp(gl - gmax), axis=-1, keepdims=True)

    lo = N_GROUPS + EXPERTS_PER_GROUP * group
    el = jnp.where((lane >= lo) & (lane < lo + EXPERTS_PER_GROUP), lg, -jnp.inf)
    e1 = rmax(el)
    lane1 = first(el, e1)
    el = jnp.where(lane == lane1, -jnp.inf, el)
    e2 = rmax(el)
    lane2 = first(el, e2)
    t2 = jnp.exp(e2 - e1)
    w1 = p_group / (1.0 + t2)
    w2 = p_group * t2 / (1.0 + t2)
    x1 = lane1 - N_GROUPS
    x2 = lane2 - N_GROUPS

    hit1 = lane == x1
    hit2 = lane == x2
    onehot = jnp.where(hit1 | hit2, 1.0, 0.0)
    before = _dot(tri, onehot.astype(BF16)) + run
    rank1 = jnp.sum(jnp.where(hit1, before, 0.0), axis=-1, keepdims=True)
    rank2 = jnp.sum(jnp.where(hit2, before, 0.0), axis=-1, keepdims=True)

    meta = jnp.zeros(lg.shape, F32)
    for k, v in ((META_X1, x1), (META_X2, x2), (META_W1, w1), (META_W2, w2),
                 (META_RANK1, rank1), (META_RANK2, rank2)):
        meta = jnp.where(lane == k, v, meta)
    return meta, run + jnp.sum(onehot, axis=0, keepdims=True)


def _outproj_kernel(x_ref, oa_ref, ob_ref, wf_ref, nw_ref, wr_ref, br_ref,
                    x2_ref, hp_ref, meta_ref, cnt_ref, run_ref, tri_ref, w_ref):
    @pl.when(pl.program_id(0) == 0)
    def _():
        row = lax.broadcasted_iota(jnp.int32, tri_ref.shape, 0)
        col = lax.broadcasted_iota(jnp.int32, tri_ref.shape, 1)
        tri_ref[...] = jnp.where(col < row, 1.0, 0.0).astype(BF16)
        run_ref[...] = jnp.zeros(run_ref.shape, F32)
        w_ref[...] = wf_ref[...].astype(BF16)

    half = D_MODEL // 2
    x2 = x_ref[...] + _dot(oa_ref[...], w_ref[0:half, :]) + _dot(ob_ref[...], w_ref[half:, :])
    x2_ref[...] = x2
    r = lax.rsqrt(jnp.mean(x2 * x2, axis=-1, keepdims=True) + RMS_EPS)
    h = x2 * r * nw_ref[...]
    hh = h.astype(BF16)
    hl = (h - hh.astype(F32)).astype(BF16)
    prod = _dot(jnp.concatenate([hh, hl], axis=0), wr_ref[...])
    lg = prod[:OUT_TM, :LANES] + prod[OUT_TM:, :LANES] + prod[:OUT_TM, LANES:] + br_ref[...]
    meta, run = _route_tile(lg, run_ref[...], tri_ref[...])
    meta_ref[...] = meta
    run_ref[...] = run
    cnt_ref[...] = run
    bits = pltpu.bitcast(hh.astype(F32), jnp.uint32)
    for c in range(ROW_WORD_CHUNKS):
        lo = bits[:, 2 * c * LANES:(2 * c + 1) * LANES]
        hi = bits[:, (2 * c + 1) * LANES:(2 * c + 2) * LANES]
        hp_ref[pl.ds(c, OUT_TM, stride=ROW_WORD_CHUNKS), :] = (lo >> 16) | (hi & jnp.uint32(0xFFFF0000))


def _out_proj(x2d, out_a, out_b, w_o, norm_w, w_r, b_r):
    t = x2d.shape[0]
    half = D_MODEL // 2
    row = lambda n: pl.BlockSpec((OUT_TM, n), lambda i: (i, 0))
    full = lambda r, c: pl.BlockSpec((r, c), lambda i: (0, 0))
    return pl.pallas_call(
        _outproj_kernel,
        grid=(t // OUT_TM,),
        in_specs=[row(D_MODEL), row(half), row(half), pl.BlockSpec(memory_space=pltpu.VMEM),
                  full(1, D_MODEL), full(D_MODEL, 2 * LANES), full(1, LANES)],
        out_specs=[row(D_MODEL), pl.BlockSpec((OUT_TM * ROW_WORD_CHUNKS, LANES), lambda i: (i, 0)), row(LANES),
                   full(1, LANES)],
        out_shape=[jax.ShapeDtypeStruct((t, D_MODEL), F32),
                   jax.ShapeDtypeStruct((t * ROW_WORD_CHUNKS, LANES), jnp.uint32),
                   jax.ShapeDtypeStruct((t, LANES), F32),
                   jax.ShapeDtypeStruct((1, LANES), F32)],
        scratch_shapes=[pltpu.VMEM((1, LANES), F32), pltpu.VMEM((OUT_TM, OUT_TM), BF16),
                        pltpu.VMEM((D_MODEL, D_MODEL), BF16)],
        compiler_params=pltpu.CompilerParams(
            dimension_semantics=("arbitrary",), vmem_limit_bytes=VMEM_LIMIT),
        name="out_proj",
    )(x2d, out_a, out_b, w_o, norm_w, w_r, b_r)


GATHER_UNROLL = 16
DSP_ROWS = 8 * MOE_TM


def _dispatch_kernel(pos0_ref, pos1_ref, padlo_ref, nt_ref, hp_hbm, xs_ref, hp_ref, src_ref, sem):
    t = pl.program_id(0)
    pos_refs = (pos0_ref, pos1_ref)

    @pl.when(t == 0)
    def _():
        load = pltpu.make_async_copy(hp_hbm, hp_ref, sem)
        load.start()

        def clear(e, _):
            lo = padlo_ref[e]
            for u in range(MOE_TM):
                src_ref[lo + u] = 0
            return 0
        lax.fori_loop(0, N_EXPERTS, clear, 0)

        def place(c, _):
            tok0 = c * (GATHER_UNROLL // EXPERT_TOPK)
            rows = [p[tok0 + u] for u in range(GATHER_UNROLL // EXPERT_TOPK) for p in pos_refs]
            for n, row in enumerate(rows):
                src_ref[row] = tok0 + n // EXPERT_TOPK
            return 0
        lax.fori_loop(0, pos_refs[0].shape[0] * EXPERT_TOPK // GATHER_UNROLL, place, 0)
        load.wait()

    n_valid = jnp.clip(nt_ref[0] * MOE_TM - t * DSP_ROWS, 0, DSP_ROWS)

    def gather(c, _):
        toks = [src_ref[t * DSP_ROWS + c * GATHER_UNROLL + u] for u in range(GATHER_UNROLL)]
        for u in range(GATHER_UNROLL):
            xs_ref[c * GATHER_UNROLL + u] = hp_ref[toks[u]]
        return 0
    lax.fori_loop(0, n_valid // GATHER_UNROLL, gather, 0)

    def zero_tile(c, _):
        xs_ref[pl.ds(pl.multiple_of(c * MOE_TM, MOE_TM), MOE_TM)] = jnp.zeros(
            (MOE_TM,) + xs_ref.shape[1:], xs_ref.dtype)
        return 0
    lax.fori_loop(n_valid // MOE_TM, DSP_ROWS // MOE_TM, zero_tile, 0)


def _dispatch(pos, pad_lo, n_tiles, hp, npad):
    n_tokens = hp.shape[0] // ROW_WORD_CHUNKS
    hp3 = hp.reshape(n_tokens, ROW_WORD_CHUNKS, LANES)
    xs = pl.pallas_call(
        _dispatch_kernel,
        grid_spec=pltpu.PrefetchScalarGridSpec(
            num_scalar_prefetch=4,
            grid=(npad // DSP_ROWS,),
            in_specs=[pl.BlockSpec(memory_space=pl.ANY)],
            out_specs=pl.BlockSpec((DSP_ROWS, ROW_WORD_CHUNKS, LANES), lambda t, p0, p1, lo, n: (t, 0, 0)),
            scratch_shapes=[pltpu.VMEM(hp3.shape, hp3.dtype), pltpu.SMEM((npad + MOE_TM,), jnp.int32),
                            pltpu.SemaphoreType.DMA(())],
        ),
        out_shape=jax.ShapeDtypeStruct((npad, ROW_WORD_CHUNKS, LANES), hp.dtype),
        compiler_params=pltpu.CompilerParams(
            dimension_semantics=("arbitrary",), vmem_limit_bytes=VMEM_LIMIT),
        name="dispatch",
    )(pos[0], pos[1], pad_lo, n_tiles, hp3)
    return xs.reshape(npad * ROW_WORD_CHUNKS, LANES)


W_SLOTS = 2
EXP_TILES = 3


def _experts_kernel(tk_ref, eseq_ref, nt_ref, xs_ref, wg_hbm, wu_hbm, wd_hbm, y_ref,
                    wg_buf, wu_buf, wd_buf, x_ref, sem):
    n_used = nt_ref[1]
    hbm_bufs = ((wg_hbm, wg_buf), (wu_hbm, wu_buf), (wd_hbm, wd_buf))

    def weight_copies(k, slot):
        e = eseq_ref[k]
        return [pltpu.make_async_copy(hbm.at[e], buf.at[slot], sem.at[slot, n])
                for n, (hbm, buf) in enumerate(hbm_bufs)]

    for sub in range(EXP_TILES):
        _expert_tile(pl.program_id(0) * EXP_TILES + sub, sub, tk_ref, nt_ref, n_used, weight_copies,
                     xs_ref, y_ref, wg_buf, wu_buf, wd_buf, x_ref)


def _expert_tile(t, sub, tk_ref, nt_ref, n_used, weight_copies,
                 xs_ref, y_ref, wg_buf, wu_buf, wd_buf, x_ref):
    xs_row0 = sub * MOE_TM * ROW_WORD_CHUNKS
    y_rows = pl.ds(sub * MOE_TM, MOE_TM)

    @pl.when(t < nt_ref[0])
    def _():
        k = tk_ref[t]
        slot = k % W_SLOTS

        @pl.when((t == 0) | (k != tk_ref[jnp.maximum(t - 1, 0)]))
        def _():
            @pl.when(k == 0)
            def _():
                for s in range(W_SLOTS):
                    @pl.when(s < n_used)
                    def _():
                        for c in weight_copies(s, s):
                            c.start()

            for c in weight_copies(k, slot):
                c.wait()

        for c in range(ROW_WORD_CHUNKS):
            u32 = xs_ref[pl.ds(xs_row0 + c, MOE_TM, stride=ROW_WORD_CHUNKS), :]
            x_ref[:, 2 * c * LANES:(2 * c + 1) * LANES] = pltpu.bitcast(u32 << 16, F32).astype(BF16)
            x_ref[:, (2 * c + 1) * LANES:(2 * c + 2) * LANES] = (
                pltpu.bitcast(u32 & jnp.uint32(0xFFFF0000), F32).astype(BF16))
        x = x_ref[...]
        a = _dot(x, wg_buf[slot].astype(BF16))
        u = _dot(x, wu_buf[slot].astype(BF16))
        act = (a * jax.nn.sigmoid(a)) * u
        y = _dot(act.astype(BF16), wd_buf[slot].astype(BF16))
        bits = pltpu.bitcast(y.astype(BF16).astype(F32), jnp.uint32)
        y_ref[y_rows, :] = (bits[:, :D_MODEL // 2] >> 16) | (bits[:, D_MODEL // 2:] & jnp.uint32(0xFFFF0000))

        last_tile = (t + 1 == nt_ref[0]) | (k != tk_ref[jnp.minimum(t + 1, tk_ref.shape[0] - 1)])

        @pl.when(last_tile & (k + W_SLOTS < n_used))
        def _():
            for c in weight_copies(k + W_SLOTS, slot):
                c.start()

    @pl.when(t >= nt_ref[0])
    def _():
        y_ref[y_rows, :] = jnp.zeros((MOE_TM, D_MODEL // 2), y_ref.dtype)


def _experts(tile_k, expert_seq, n_used, xs, w_gate, w_up, w_down):
    npad = xs.shape[0] // ROW_WORD_CHUNKS
    rows = EXP_TILES * MOE_TM
    assert npad % rows == 0 and npad % DSP_ROWS == 0
    last = lambda t, nt: jnp.minimum(t, jnp.maximum(nt[0] - 1, 0) // EXP_TILES)
    hbm = pl.BlockSpec(memory_space=pl.ANY)
    return pl.pallas_call(
        _experts_kernel,
        grid_spec=pltpu.PrefetchScalarGridSpec(
            num_scalar_prefetch=3,
            grid=(npad // rows,),
            in_specs=[
                pl.BlockSpec((rows * ROW_WORD_CHUNKS, LANES), lambda t, tk, es, nt: (last(t, nt), 0)),
                hbm, hbm, hbm,
            ],
            out_specs=pl.BlockSpec((rows, D_MODEL // 2), lambda t, tk, es, nt: (t, 0)),
            scratch_shapes=[
                pltpu.VMEM((W_SLOTS, D_MODEL, D_EXPERT), F32),
                pltpu.VMEM((W_SLOTS, D_MODEL, D_EXPERT), F32),
                pltpu.VMEM((W_SLOTS, D_EXPERT, D_MODEL), F32),
                pltpu.VMEM((MOE_TM, D_MODEL), BF16),
                pltpu.SemaphoreType.DMA((W_SLOTS, 3)),
            ],
        ),
        out_shape=jax.ShapeDtypeStruct((npad, D_MODEL // 2), jnp.uint32),
        compiler_params=pltpu.CompilerParams(
            dimension_semantics=("arbitrary",), vmem_limit_bytes=VMEM_LIMIT),
        name="experts",
    )(tile_k, expert_seq, n_used, xs, w_gate, w_up, w_down)


CMB_UNROLL = 8
CMB_SLOTS = 2


def _combine_kernel(pos0_ref, pos1_ref, x2_ref, meta_ref, nw_ref, y_ref, o_ref, buf_ref, sem):
    i = pl.program_id(0)
    pos_refs = (pos0_ref, pos1_ref)

    def row_copy(p, slot, k, r):
        return pltpu.make_async_copy(y_ref.at[pl.ds(p, 1), :], buf_ref.at[slot, k, pl.ds(r, 1), :],
                                     sem.at[slot])

    def issue(tile, slot):
        for c in range(CMB_TT // CMB_UNROLL):
            tok0 = tile * CMB_TT + c * CMB_UNROLL
            rows = [[p[tok0 + u] for u in range(CMB_UNROLL)] for p in pos_refs]
            for u in range(CMB_UNROLL):
                for k in range(EXPERT_TOPK):
                    row_copy(rows[k][u], slot, k, c * CMB_UNROLL + u).start(priority=k % 2)

    @pl.when(i == 0)
    def _():
        issue(0, 0)

    @pl.when(i + 1 < pl.num_programs(0))
    def _():
        issue(i + 1, (i + 1) % CMB_SLOTS)

    slot = i % CMB_SLOTS
    for k in range(EXPERT_TOPK):
        pltpu.make_async_copy(y_ref.at[pl.ds(0, CMB_TT), :], buf_ref.at[slot, k], sem.at[slot]).wait()

    meta = meta_ref[...]
    def unpack(words):
        return jnp.concatenate([pltpu.bitcast(words << 16, F32),
                                pltpu.bitcast(words & jnp.uint32(0xFFFF0000), F32)], axis=1)

    x3 = (x2_ref[...] + meta[:, META_W1:META_W1 + 1] * unpack(buf_ref[slot, 0])
          + meta[:, META_W2:META_W2 + 1] * unpack(buf_ref[slot, 1]))
    r = lax.rsqrt(jnp.mean(x3 * x3, axis=-1, keepdims=True) + RMS_EPS)
    o_ref[...] = x3 * r * nw_ref[...]


def _combine(pos, x2, meta, norm_w, y):
    t = x2.shape[0]
    return pl.pallas_call(
        _combine_kernel,
        grid_spec=pltpu.PrefetchScalarGridSpec(
            num_scalar_prefetch=2,
            grid=(t // CMB_TT,),
            in_specs=[
                pl.BlockSpec((CMB_TT, D_MODEL), lambda i, p0, p1: (i, 0)),
                pl.BlockSpec((CMB_TT, LANES), lambda i, p0, p1: (i, 0)),
                pl.BlockSpec((1, D_MODEL), lambda i, p0, p1: (0, 0)),
                pl.BlockSpec(memory_space=pl.ANY),
            ],
            out_specs=pl.BlockSpec((CMB_TT, D_MODEL), lambda i, p0, p1: (i, 0)),
            scratch_shapes=[
                pltpu.VMEM((CMB_SLOTS, EXPERT_TOPK, CMB_TT, D_MODEL // 2), jnp.uint32),
                pltpu.SemaphoreType.DMA((CMB_SLOTS,)),
            ],
        ),
        out_shape=jax.ShapeDtypeStruct((t, D_MODEL), F32),
        compiler_params=pltpu.CompilerParams(
            dimension_semantics=("arbitrary",), vmem_limit_bytes=VMEM_LIMIT),
        name="combine",
    )(pos[0], pos[1], x2, meta, norm_w, y)


def _layout(meta, counts, n_tokens):
    npad = n_tokens * EXPERT_TOPK + N_EXPERTS * MOE_TM
    counts = counts[0, :N_EXPERTS].astype(jnp.int32)
    padded = ((counts + MOE_TM - 1) // MOE_TM) * MOE_TM
    ends = jnp.cumsum(padded)
    starts = ends - padded
    ids = jnp.arange(N_EXPERTS, dtype=jnp.int32)
    pos = []
    for lane_x, lane_rank in ((META_X1, META_RANK1), (META_X2, META_RANK2)):
        hit = meta[:, lane_x].astype(jnp.int32)[:, None] == ids[None, :]
        pos.append(jnp.sum(jnp.where(hit, starts[None, :], 0), axis=-1) + meta[:, lane_rank].astype(jnp.int32))
    tile_start = jnp.arange(npad // MOE_TM, dtype=jnp.int32) * MOE_TM
    tile_expert = jnp.minimum(jnp.sum(ends[None, :] <= tile_start[:, None], axis=1), N_EXPERTS - 1)
    used = counts > 0
    k_of_expert = jnp.cumsum(used.astype(jnp.int32)) - 1
    expert_seq = jnp.sum(jnp.where(used[None, :] & (k_of_expert[None, :] == ids[:, None]), ids[None, :], 0), axis=1)
    tile_k = jnp.sum(jnp.where(tile_expert[:, None] == ids[None, :], k_of_expert[None, :], 0), axis=1)
    n_tiles = (ends[-1] // MOE_TM).astype(jnp.int32)
    n_used = jnp.stack([n_tiles, jnp.sum(used).astype(jnp.int32)])
    return (pos, starts + counts, tile_k.astype(jnp.int32), expert_seq.astype(jnp.int32), n_tiles.reshape(1),
            n_used, npad)


def kernel(x, norm_mix_w, w_in, lambda_q1, lambda_k1, lambda_q2, lambda_k2, diff_subln_w, w_out, norm_ffn_w,
           w_router_group, b_router_group, w_router_expert, b_router_expert, w_gate, w_up, w_down, norm_final_w):
    batch, seq, _ = x.shape
    assert seq % ATT_TQ == 0 and seq // MOBA_BLOCK + 8 <= LANES
    assert w_in.shape[0] == 1, "single-layer block"
    n_tokens = batch * seq
    x2d = x.reshape(n_tokens, D_MODEL)

    col = np.arange(IN_WIDTH)
    is_q = (col < MOBA_HEADS * HEAD_DIM) | ((col >= 3 * MOBA_HEADS * HEAD_DIM) & (col < 4 * MOBA_HEADS * HEAD_DIM))
    col_scale = jnp.asarray(np.where(is_q, QK_SCALE * LOG2E, 1.0).astype(np.float32))[None, :]

    proj = _in_proj(x2d, norm_mix_w[0][None, :], w_in[0], col_scale)
    out_a = _moba(proj, jnp.asarray(_alibi_slope_pieces(MOBA_HEADS)), batch, seq)
    out_b = _diff(proj, jnp.asarray(_alibi_slope_pieces(DIFF_HEADS)), lambda_q1, lambda_k1, lambda_q2, lambda_k2,
                  diff_subln_w, batch, seq)

    w_r = jnp.concatenate([w_router_group[0], w_router_expert[0]], axis=1)
    w_r = jnp.pad(w_r, ((0, 0), (0, LANES - w_r.shape[1])))
    wr_hi = w_r.astype(BF16)
    wr_lo = (w_r - wr_hi.astype(F32)).astype(BF16)
    b_r = jnp.pad(jnp.concatenate([b_router_group[0], b_router_expert[0]]), (0, LANES - N_GROUPS - N_EXPERTS))[None, :]
    x2, hp, meta, counts = _out_proj(x2d, out_a, out_b, w_out[0], norm_ffn_w[0][None, :],
                                     jnp.concatenate([wr_hi, wr_lo], axis=1), b_r)

    pos, pad_lo, tile_k, expert_seq, n_tiles, n_used, npad = _layout(meta, counts, n_tokens)
    xs = _dispatch(pos, pad_lo, n_tiles, hp, npad)
    y = _experts(tile_k, expert_seq, n_used, xs, w_gate[0], w_up[0], w_down[0])
    out = _combine(pos, x2, meta, norm_final_w[None, :], y)
    return out.reshape(batch, seq, D_MODEL)
```
